```python
import math
import jax
import jax.numpy as jnp
from jax import lax
import numpy as np


D_MODEL = 2048
BATCH = 16
SEQ = 2048
DEPTH = 2

GRID_W = 64
CTX_LEN = 256
N_MIXERS = 4
GROUP_WIDTH = D_MODEL // N_MIXERS
MIX_WIDTH = N_MIXERS * GROUP_WIDTH
N_IN_PARTS = 12
P_S5_U = 0
P_LRU_X = 1
P_LRU_GATE = 2
P_HG_Q = 3
P_HG_FF = 4
P_HG_FB = 5
P_HG_I = 6
P_HG_G = 7
P_RET_Q = 8
P_RET_K = 9
P_RET_V = 10
P_RET_G = 11
S5_CH = 16
S5_GROUPS = GROUP_WIDTH // S5_CH
S5_STATE = 64
S5_DT_MIN = 1e-3
S5_DT_MAX = 1e-1
LRU_HEADS = 8
LRU_HEAD_DIM = GROUP_WIDTH // LRU_HEADS
LRU_CONV = 4
LRU_C = 8.0
HGRN_HEADS = 4
HGRN_DIM = GROUP_WIDTH // HGRN_HEADS
RET_HEADS = 4
RET_DIM = GROUP_WIDTH // RET_HEADS
RET_DECAY_EXP_FWD = 5.0
RET_DECAY_EXP_BWD = 5.5
ROPE_BASE = 10000.0
CHUNK = 64
N_EXPERTS = 16
N_EXPERT_GROUPS = 4
EXPERTS_PER_GROUP = N_EXPERTS // N_EXPERT_GROUPS
TOP_K = 2
D_FF_EXPERT = D_MODEL // 2
EPS = 1e-6

kernel_name = 'hybrid_headgroup_flow_block'


def rms_norm(x, g):
    xf = x.astype(jnp.float32)
    y = xf * lax.rsqrt(jnp.mean(xf * xf, axis=-1, keepdims=True) + EPS)
    return (y * g.astype(jnp.float32)).astype(x.dtype)


def head_rms(x):
    return x * lax.rsqrt(jnp.mean(x * x, axis=-1, keepdims=True) + EPS)


def modulate(h, shift, scale):
    return h * (1.0 + scale) + shift


def flip_t(t):
    return jnp.flip(t, axis=1)


def gated_head_norm(o, g):
    return head_rms(o).reshape(g.shape) * jax.nn.silu(g.astype(jnp.float32))


def linear_scan(a, b, h0, reverse):
    def combine(e1, e2):
        return e1[0] * e2[0], e2[0] * e1[1] + e2[1]
    a_cum, h = lax.associative_scan(combine, (a, b), reverse=reverse, axis=1)
    return h + a_cum * h0[:, None]


def s5_states(u, h0_re, h0_im, lam_re, lam_im, log_dt, b_re, b_im, reverse):
    bsz, n, _ = u.shape
    ug = u.reshape(bsz, n, S5_GROUPS, S5_CH)
    dt = jnp.exp(log_dt.astype(jnp.float32))[:, None]
    lr, li = lam_re.astype(jnp.float32), lam_im.astype(jnp.float32)
    mag = jnp.exp(lr * dt)
    abar_re, abar_im = mag * jnp.cos(li * dt), mag * jnp.sin(li * dt)
    den = lr * lr + li * li
    zr = abar_re - 1.0
    w_re = (zr * lr + abar_im * li) / den
    w_im = (abar_im * lr - zr * li) / den
    bu_re = jnp.einsum('blgc,gcp->blgp', ug, b_re.astype(jnp.float32))
    bu_im = jnp.einsum('blgc,gcp->blgp', ug, b_im.astype(jnp.float32))
    drive_re = w_re * bu_re - w_im * bu_im
    drive_im = w_re * bu_im + w_im * bu_re
    a_re = jnp.broadcast_to(abar_re, (1, n) + abar_re.shape)
    a_im = jnp.broadcast_to(abar_im, (1, n) + abar_im.shape)

    def combine(e1, e2):
        a1r, a1i, b1r, b1i = e1
        a2r, a2i, b2r, b2i = e2
        return (a1r * a2r - a1i * a2i, a1r * a2i + a1i * a2r,
                a2r * b1r - a2i * b1i + b2r, a2r * b1i + a2i * b1r + b2i)

    p_re, p_im, h_re, h_im = lax.associative_scan(
        combine, (a_re, a_im, drive_re, drive_im), reverse=reverse, axis=1)
    h0r, h0i = h0_re[:, None], h0_im[:, None]
    return h_re + p_re * h0r - p_im * h0i, h_im + p_re * h0i + p_im * h0r


def s5_readout(h_re, h_im, c_re, c_im):
    bsz, n = h_re.shape[:2]
    y = (jnp.einsum('blgp,gpc->blgc', h_re, c_re.astype(jnp.float32))
         - jnp.einsum('blgp,gpc->blgc', h_im, c_im.astype(jnp.float32)))
    return y.reshape(bsz, n, GROUP_WIDTH)


def s5_glu(y, glu_w, glu_b):
    y = jax.nn.gelu(y)
    return y * jax.nn.sigmoid(y @ glu_w.astype(jnp.float32) + glu_b.astype(jnp.float32))


def s5_mixer(pc, pl, lam_re, lam_im, log_dt, b_re, b_im, c_re, c_im, d, glu_w, glu_b, ctx_out):
    u_c = pc[P_S5_U].astype(jnp.float32)
    u_l = pl[P_S5_U].astype(jnp.float32)
    d = d.astype(jnp.float32)
    zero = jnp.zeros((u_c.shape[0], S5_GROUPS, S5_STATE), jnp.float32)
    y_l = d * u_l
    y_c = d * u_c if ctx_out else None
    for dr in (0, 1):
        rev = dr == 1
        end = 0 if rev else -1
        prm = (lam_re[dr], lam_im[dr], log_dt[dr], b_re[dr], b_im[dr])
        hc_re, hc_im = s5_states(u_c, zero, zero, *prm, rev)
        hl_re, hl_im = s5_states(u_l, hc_re[:, end], hc_im[:, end], *prm, rev)
        y_l = y_l + s5_readout(hl_re, hl_im, c_re[dr], c_im[dr])
        if ctx_out:
            y_c = y_c + s5_readout(hc_re, hc_im, c_re[dr], c_im[dr])
    out_c = s5_glu(y_c, glu_w, glu_b) if ctx_out else None
    return out_c, s5_glu(y_l, glu_w, glu_b)


def short_conv(x, w, b):
    k = w.shape[0]
    left = k // 2
    y = lax.conv_general_dilated(x, w[:, None, :].astype(x.dtype), window_strides=(1,),
                                 padding=[(left, k - 1 - left)],
                                 dimension_numbers=('NWC', 'WIO', 'NWC'),
                                 feature_group_count=x.shape[-1])
    return y + b.astype(x.dtype)


def rglru_coefficients(x, wa, ba, wx, bx, lam):
    bsz, n, w = x.shape
    xh = x.reshape(bsz, n, LRU_HEADS, LRU_HEAD_DIM)
    r = jax.nn.sigmoid(jnp.einsum('blhi,hij->blhj', xh, wa.astype(jnp.float32)).reshape(bsz, n, w)
                       + ba.astype(jnp.float32))
    i = jax.nn.sigmoid(jnp.einsum('blhi,hij->blhj', xh, wx.astype(jnp.float32)).reshape(bsz, n, w)
                       + bx.astype(jnp.float32))
    log_a = -LRU_C * r * jax.nn.softplus(-lam.astype(jnp.float32))
    return jnp.exp(log_a), jnp.sqrt(-jnp.expm1(2.0 * log_a)) * (i * x)


def rglru_mixer(pc, pl, conv_w, conv_b, wa, ba, wx, bx, lam, ctx_out):
    x_c = short_conv(pc[P_LRU_X], conv_w, conv_b).astype(jnp.float32)
    x_l = short_conv(pl[P_LRU_X], conv_w, conv_b).astype(jnp.float32)
    zero = jnp.zeros((x_c.shape[0], GROUP_WIDTH), jnp.float32)
    h_c_sum, h_l_sum = 0.0, 0.0
    for dr in (0, 1):
        rev = dr == 1
        prm = (wa[dr], ba[dr], wx[dr], bx[dr], lam[dr])
        h_c = linear_scan(*rglru_coefficients(x_c, *prm), zero, rev)
        h_l = linear_scan(*rglru_coefficients(x_l, *prm), h_c[:, 0 if rev else -1], rev)
        h_l_sum = h_l_sum + h_l
        if ctx_out:
            h_c_sum = h_c_sum + h_c
    out_l = h_l_sum * jax.nn.gelu(pl[P_LRU_GATE].astype(jnp.float32))
    out_c = h_c_sum * jax.nn.gelu(pc[P_LRU_GATE].astype(jnp.float32)) if ctx_out else None
    return out_c, out_l


def chunk_gla(q, k, v, log_f, s0, with_out):
    bsz, n, h, _ = q.shape
    dv = v.shape[-1]
    nc = n // CHUNK

    def blocks(t):
        return t.reshape(bsz, nc, CHUNK, h, t.shape[-1]).transpose(1, 0, 3, 2, 4)

    lower = jnp.tril(jnp.ones((CHUNK, CHUNK), bool))[:, :, None]

    def step(s, blk):
        qb, kb, vb, gb = blk
        cum = jnp.cumsum(gb, axis=2)
        last = cum[:, :, -1:]
        s_new = (jnp.exp(last[:, :, 0])[..., None] * s
                 + jnp.einsum('bhsd,bhsv->bhdv', kb * jnp.exp(last - cum), vb))
        if not with_out:
            return s_new, None
        rel = jnp.where(lower, cum[:, :, :, None] - cum[:, :, None], -jnp.inf)
        scores = jnp.einsum('bhtd,bhsd,bhtsd->bhts', qb, kb, jnp.exp(rel))
        o = (jnp.einsum('bhts,bhsv->bhtv', scores, vb)
             + jnp.einsum('bhtd,bhdv->bhtv', qb * jnp.exp(cum), s))
        return s_new, o

    s_fin, o = lax.scan(step, s0, (blocks(q), blocks(k), blocks(v), blocks(log_f)))
    if not with_out:
        return None, s_fin
    return o.transpose(1, 0, 3, 2, 4).reshape(bsz, n, h, dv), s_fin


def gla_two_stream(q_c, v_c, kg_c, q_l, v_l, kg_l, ctx_out):
    s0 = jnp.zeros((q_c.shape[0], q_c.shape[2], q_c.shape[3], v_c.shape[3]), jnp.float32)
    o_c, o_l = 0.0, 0.0
    for dr in (0, 1):
        orient = flip_t if dr == 1 else (lambda t: t)
        k_c, g_c = kg_c[dr]
        k_l, g_l = kg_l[dr]
        oc, s_ctx = chunk_gla(orient(q_c), orient(k_c), orient(v_c), orient(g_c), s0, ctx_out)
        ol, _ = chunk_gla(orient(q_l), orient(k_l), orient(v_l), orient(g_l), s_ctx, True)
        o_l = o_l + orient(ol)
        if ctx_out:
            o_c = o_c + orient(oc)
    return (o_c if ctx_out else None), o_l


def hgrn_lower_bound(lb_logits, layer):
    cum = jnp.cumsum(jax.nn.softmax(lb_logits.astype(jnp.float32), axis=0), axis=0)
    return cum[layer] - cum[0]


def hgrn2_mixer(pc, pl, lbs, ctx_out):
    def prepare(p):
        bsz, n, _ = p[P_HG_Q].shape

        def heads(t):
            return t.astype(jnp.float32).reshape(bsz, n, HGRN_HEADS, HGRN_DIM)

        q = heads(p[P_HG_Q]) * HGRN_DIM ** -0.5
        v = jax.nn.silu(heads(p[P_HG_I]))
        kg = []
        for z, lb in zip((heads(p[P_HG_FF]), heads(p[P_HG_FB])), lbs):
            lb = lb.reshape(HGRN_HEADS, HGRN_DIM)
            f = lb + (1.0 - lb) * jax.nn.sigmoid(z)
            kg.append(((1.0 - lb) * jax.nn.sigmoid(-z), jnp.log(f)))
        return q, v, kg

    q_c, v_c, kg_c = prepare(pc)
    q_l, v_l, kg_l = prepare(pl)
    o_c, o_l = gla_two_stream(q_c, v_c, kg_c, q_l, v_l, kg_l, ctx_out)
    out_c = gated_head_norm(o_c, pc[P_HG_G]) if ctx_out else None
    return out_c, gated_head_norm(o_l, pl[P_HG_G])


def retention_log_decay(offset):
    return jnp.log1p(-jnp.exp2(-(offset + jnp.arange(RET_HEADS, dtype=jnp.float32))))


def axial_rotary(x):
    n = x.shape[1]
    rows = n // GRID_W
    row = jnp.repeat(jnp.arange(rows, dtype=jnp.float32), GRID_W)
    col = jnp.tile(jnp.arange(GRID_W, dtype=jnp.float32), rows)
    half = x.shape[-1] // 2
    quarter = half // 2
    inv_freq = ROPE_BASE ** (-jnp.arange(quarter, dtype=jnp.float32) / quarter)
    ang = jnp.concatenate([row[:, None] * inv_freq, col[:, None] * inv_freq], axis=-1)[None, :, None, :]
    cos, sin = jnp.cos(ang), jnp.sin(ang)
    x1, x2 = x[..., :half], x[..., half:]
    return jnp.concatenate([x1 * cos - x2 * sin, x1 * sin + x2 * cos], axis=-1)


def retention_mixer(pc, pl, ctx_out):
    def heads(t):
        bsz, n, _ = t.shape
        return t.astype(jnp.float32).reshape(bsz, n, RET_HEADS, RET_DIM)

    scale = RET_DIM ** -0.5
    q_c, k_c, v_c = heads(pc[P_RET_Q]), heads(pc[P_RET_K]) * scale, heads(pc[P_RET_V])
    q_l = axial_rotary(heads(pl[P_RET_Q]))
    k_l = axial_rotary(heads(pl[P_RET_K])) * scale
    v_l = heads(pl[P_RET_V])
    decays = (retention_log_decay(RET_DECAY_EXP_FWD), retention_log_decay(RET_DECAY_EXP_BWD))
    kg_c = [(k_c, jnp.broadcast_to(dec[:, None], k_c.shape)) for dec in decays]
    kg_l = [(k_l, jnp.broadcast_to(dec[:, None], k_l.shape)) for dec in decays]
    o_c, o_l = gla_two_stream(q_c, v_c, kg_c, q_l, v_l, kg_l, ctx_out)
    out_c = gated_head_norm(o_c, pc[P_RET_G]) if ctx_out else None
    return out_c, gated_head_norm(o_l, pl[P_RET_G])


def moe_ffn(h, router_w, router_bias, w_gate, w_up, w_down):
    t = h.shape[0]
    scores = jax.nn.sigmoid(h.astype(jnp.float32) @ router_w.astype(jnp.float32))
    biased = scores + router_bias.astype(jnp.float32)
    grouped = biased.reshape(t, N_EXPERT_GROUPS, EXPERTS_PER_GROUP)
    group_score = jnp.sum(lax.top_k(grouped, TOP_K)[0], axis=-1)
    best = jnp.argmax(group_score, axis=-1)
    in_group = (jnp.arange(N_EXPERTS) // EXPERTS_PER_GROUP)[None, :] == best[:, None]
    _, idx = lax.top_k(jnp.where(in_group, biased, -jnp.inf), TOP_K)
    w = jnp.take_along_axis(scores, idx, axis=-1)
    w = w / jnp.sum(w, axis=-1, keepdims=True)
    gates = jnp.sum(jax.nn.one_hot(idx, N_EXPERTS, dtype=jnp.float32) * w[..., None], axis=1)
    y = jnp.zeros(h.shape, jnp.float32)
    for e in range(N_EXPERTS):
        he = jax.nn.silu(h @ w_gate[e]) * (h @ w_up[e])
        y = y + gates[:, e:e + 1] * (he @ w_down[e]).astype(jnp.float32)
    return y.astype(h.dtype)


def setup_inputs(seed: int = 0) -> dict:
    key = jax.random.key(seed)
    keys = iter(jax.random.split(key, 40))

    def normal(shape, scale):
        return jax.random.normal(next(keys), shape, jnp.float32) * scale

    def uniform(shape, lo, hi):
        return jax.random.uniform(next(keys), shape, jnp.float32, lo, hi)

    d, w, nl = D_MODEL, GROUP_WIDTH, DEPTH
    inp = {}
    inp['x'] = normal((BATCH, SEQ, d), 1.0)
    inp['c'] = normal((BATCH, d), 1.0)
    inp['ctx'] = normal((BATCH, CTX_LEN, d), 1.0)
    inp['c_ctx'] = normal((d,), 1.0)
    inp['norm_mix_g'] = 1.0 + normal((nl, d), 0.01)
    inp['norm_ffn_g'] = 1.0 + normal((nl, d), 0.01)
    inp['w_mod'] = normal((nl, d, 6 * d), 0.5 * d ** -0.5)
    inp['b_mod'] = normal((nl, 6 * d), 0.02)
    inp['w_in'] = normal((nl, d, N_IN_PARTS * w), d ** -0.5)
    inp['w_out'] = normal((nl, MIX_WIDTH, d), MIX_WIDTH ** -0.5)
    inp['s5_lam_re'] = -0.5 + normal((nl, 2, S5_GROUPS, S5_STATE), 0.01)
    inp['s5_lam_im'] = jnp.pi * jnp.arange(S5_STATE, dtype=jnp.float32) + normal((nl, 2, S5_GROUPS, S5_STATE), 0.01)
    inp['s5_log_dt'] = uniform((nl, 2, S5_GROUPS), math.log(S5_DT_MIN), math.log(S5_DT_MAX))
    inp['s5_b_re'] = normal((nl, 2, S5_GROUPS, S5_CH, S5_STATE), (2 * S5_CH) ** -0.5)
    inp['s5_b_im'] = normal((nl, 2, S5_GROUPS, S5_CH, S5_STATE), (2 * S5_CH) ** -0.5)
    inp['s5_c_re'] = normal((nl, 2, S5_GROUPS, S5_STATE, S5_CH), 0.5)
    inp['s5_c_im'] = normal((nl, 2, S5_GROUPS, S5_STATE, S5_CH), 0.5)
    inp['s5_d'] = normal((nl, w), 1.0)
    inp['s5_glu_w'] = normal((nl, w, w), w ** -0.5)
    inp['s5_glu_b'] = normal((nl, w), 0.01)
    inp['lru_conv_w'] = normal((nl, LRU_CONV, w), LRU_CONV ** -0.5)
    inp['lru_conv_b'] = normal((nl, w), 0.01)
    inp['lru_wa'] = normal((nl, 2, LRU_HEADS, LRU_HEAD_DIM, LRU_HEAD_DIM), LRU_HEAD_DIM ** -0.5)
    inp['lru_ba'] = normal((nl, 2, w), 0.01)
    inp['lru_wx'] = normal((nl, 2, LRU_HEADS, LRU_HEAD_DIM, LRU_HEAD_DIM), LRU_HEAD_DIM ** -0.5)
    inp['lru_bx'] = normal((nl, 2, w), 0.01)
    sig = uniform((nl, 2, w), 0.9, 0.999) ** (1.0 / LRU_C)
    inp['lru_lam'] = jnp.log(sig) - jnp.log1p(-sig)
    inp['hgrn_lb_logits'] = normal((2, nl, w), 1.0)
    inp['router_w'] = normal((d, N_EXPERTS), d ** -0.5)
    inp['router_bias'] = normal((N_EXPERTS,), 0.01)
    inp['moe_w_gate'] = normal((nl, N_EXPERTS, d, D_FF_EXPERT), d ** -0.5)
    inp['moe_w_up'] = normal((nl, N_EXPERTS, d, D_FF_EXPERT), d ** -0.5)
    inp['moe_w_down'] = normal((nl, N_EXPERTS, D_FF_EXPERT, d), D_FF_EXPERT ** -0.5)
    inp['final_norm_g'] = 1.0 + normal((d,), 0.01)
    return inp


def reference(x, c, ctx, c_ctx, norm_mix_g, norm_ffn_g, w_mod, b_mod, w_in, w_out,
              s5_lam_re, s5_lam_im, s5_log_dt, s5_b_re, s5_b_im, s5_c_re, s5_c_im, s5_d,
              s5_glu_w, s5_glu_b, lru_conv_w, lru_conv_b, lru_wa, lru_ba, lru_wx, lru_bx, lru_lam,
              hgrn_lb_logits, router_w, router_bias, moe_w_gate, moe_w_up, moe_w_down, final_norm_g):
    bsz, n, dm = x.shape
    dtype = x.dtype
    xl = x
    xc = ctx.astype(dtype)
    for layer in range(DEPTH):
        ctx_out = layer < DEPTH - 1
        mod_l = (jax.nn.silu(c) @ w_mod[layer] + b_mod[layer]).reshape(bsz, 6, 1, dm)
        mod_c = (jax.nn.silu(c_ctx) @ w_mod[layer] + b_mod[layer]).reshape(6, dm)

        hl = modulate(rms_norm(xl, norm_mix_g[layer]), mod_l[:, 0], mod_l[:, 1])
        hc = modulate(rms_norm(xc, norm_mix_g[layer]), mod_c[0], mod_c[1])
        pl = jnp.split(hl @ w_in[layer], N_IN_PARTS, axis=-1)
        pc = jnp.split(hc @ w_in[layer], N_IN_PARTS, axis=-1)
        a_c, a_l = s5_mixer(pc, pl, s5_lam_re[layer], s5_lam_im[layer], s5_log_dt[layer],
                            s5_b_re[layer], s5_b_im[layer], s5_c_re[layer], s5_c_im[layer],
                            s5_d[layer], s5_glu_w[layer], s5_glu_b[layer], ctx_out)
        b_c, b_l = rglru_mixer(pc, pl, lru_conv_w[layer], lru_conv_b[layer], lru_wa[layer],
                               lru_ba[layer], lru_wx[layer], lru_bx[layer], lru_lam[layer], ctx_out)
        lbs = (hgrn_lower_bound(hgrn_lb_logits[0], layer), hgrn_lower_bound(hgrn_lb_logits[1], layer))
        h_c, h_l = hgrn2_mixer(pc, pl, lbs, ctx_out)
        r_c, r_l = retention_mixer(pc, pl, ctx_out)
        mix_l = jnp.concatenate([a_l, b_l, h_l, r_l], axis=-1).astype(dtype) @ w_out[layer]
        xl = xl + mod_l[:, 2] * mix_l
        if ctx_out:
            mix_c = jnp.concatenate([a_c, b_c, h_c, r_c], axis=-1).astype(dtype) @ w_out[layer]
            xc = xc + mod_c[2] * mix_c

        hl = modulate(rms_norm(xl, norm_ffn_g[layer]), mod_l[:, 3], mod_l[:, 4])
        if ctx_out:
            hc = modulate(rms_norm(xc, norm_ffn_g[layer]), mod_c[3], mod_c[4])
            tokens = jnp.concatenate([hl.reshape(-1, dm), hc.reshape(-1, dm)], axis=0)
        else:
            tokens = hl.reshape(-1, dm)
        y = moe_ffn(tokens, router_w, router_bias, moe_w_gate[layer], moe_w_up[layer], moe_w_down[layer])
        xl = xl + mod_l[:, 5] * y[: bsz * n].reshape(bsz, n, dm)
        if ctx_out:
            xc = xc + mod_c[5] * y[bsz * n:].reshape(bsz, -1, dm)
    return rms_norm(xl, final_norm_g)
```

```python
import functools
import math

import jax
import jax.numpy as jnp
from jax import lax
from jax.experimental import pallas as pl
from jax.experimental.pallas import tpu as pltpu

F32 = jnp.float32
BF16 = jnp.bfloat16
I32 = jnp.int32

GROUP_WIDTH = 512
N_IN_PARTS = 12
N_TM_PARTS = 3
N_BM_PARTS = N_IN_PARTS - N_TM_PARTS
S5_CH = 16
S5_GROUPS = 32
S5_STATE = 64
S5_LANE_BLOCKS = 4
LRU_HEADS = 8
LRU_CONV = 4
LRU_C = 8.0
GLA_HEADS = 4
GLA_DIM = 128
GLA_CHUNK = 64
GLA_SUB = 16
RET_DECAY_EXP = (5.0, 5.5)
ROPE_BASE = 10000.0
GRID_W = 64
N_EXPERTS = 16
N_EXPERT_GROUPS = 4
EXPERTS_PER_GROUP = 4
D_FF_EXPERT = 1024
EPS = 1e-6
MOD_ROWS = 24

VMEM_LIMIT_BYTES = 56 * 1024 * 1024


def _params(*semantics):
    return pltpu.CompilerParams(dimension_semantics=semantics, vmem_limit_bytes=VMEM_LIMIT_BYTES)


def _dot(a, b):
    return jnp.dot(a, b, preferred_element_type=F32)


def _dot_nt(a, b):
    return lax.dot_general(a, b, (((1,), (1,)), ((), ())), preferred_element_type=F32)


def _dot_tn(a, b):
    return lax.dot_general(a, b, (((0,), (0,)), ((), ())), preferred_element_type=F32)


def _scan_chunk_index(step, n_lat_chunks, n_chunks, rev):
    if rev:
        return n_chunks - 1 - step
    return (step + n_lat_chunks) % n_chunks


def _mod_kernel(cv_ref, w_ref, b_ref, o_ref):
    cv = cv_ref[...]
    s = cv * jax.nn.sigmoid(cv)
    o_ref[0] = _dot(s.astype(BF16), w_ref[0].astype(BF16)) + b_ref[0]


def _mod_vectors(cv, w_mod, b_mod):
    nl, d, n6 = w_mod.shape
    tn = 1024
    return pl.pallas_call(
        _mod_kernel,
        out_shape=jax.ShapeDtypeStruct((nl, MOD_ROWS, n6), F32),
        grid=(nl, n6 // tn),
        in_specs=[pl.BlockSpec((MOD_ROWS, d), lambda l, j: (0, 0)),
                  pl.BlockSpec((1, d, tn), lambda l, j: (l, 0, j)),
                  pl.BlockSpec((1, 1, tn), lambda l, j: (l, 0, j))],
        out_specs=pl.BlockSpec((1, MOD_ROWS, tn), lambda l, j: (l, 0, j)),
        compiler_params=_params("arbitrary", "arbitrary"),
        name="mod_vectors",
    )(cv, w_mod, b_mod.reshape(nl, 1, n6))


def _norm_modulate(x, g, modb_ref, modc_ref, row0, n_lat, shift_idx, scale_idx):
    tm = x.shape[0]
    ms = jnp.mean(x * x, axis=-1, keepdims=True)
    y = x * lax.rsqrt(ms + EPS) * g
    row = row0 + lax.broadcasted_iota(I32, (tm, 1), 0)
    is_ctx = row >= n_lat
    shift = jnp.where(is_ctx, modc_ref[0, shift_idx:shift_idx + 1, :], modb_ref[0, shift_idx:shift_idx + 1, :])
    scale = jnp.where(is_ctx, modc_ref[0, scale_idx:scale_idx + 1, :], modb_ref[0, scale_idx:scale_idx + 1, :])
    return y * (1.0 + scale) + shift


def _inproj_kernel(x_ref, g_ref, modb_ref, modc_ref, w_ref, otm_ref, obm_ref, h_scr, *, tm, n_lat):
    i = pl.program_id(1)
    j = pl.program_id(2)

    @pl.when(j == 0)
    def _():
        h = _norm_modulate(x_ref[0], g_ref[...], modb_ref, modc_ref, i * tm, n_lat, 0, 1)
        h_scr[...] = h.astype(BF16)

    r = _dot(h_scr[...], w_ref[...])

    @pl.when(j < N_TM_PARTS)
    def _():
        otm_ref[...] = r

    @pl.when(j >= N_TM_PARTS)
    def _():
        obm_ref[0] = r


def _in_projection(xs, gain, modv, w_bf16, n_lat, tm):
    bsz, ltot, d = xs.shape
    gw = GROUP_WIDTH
    kern = functools.partial(_inproj_kernel, tm=tm, n_lat=n_lat)
    return pl.pallas_call(
        kern,
        out_shape=(jax.ShapeDtypeStruct((ltot, bsz * N_TM_PARTS * gw), F32),
                   jax.ShapeDtypeStruct((bsz, ltot, N_BM_PARTS * gw), F32)),
        grid=(bsz, ltot // tm, N_IN_PARTS),
        in_specs=[pl.BlockSpec((1, tm, d), lambda b, i, j: (b, i, 0)),
                  pl.BlockSpec((1, d), lambda b, i, j: (0, 0)),
                  pl.BlockSpec((1, 6, d), lambda b, i, j: (b, 0, 0)),
                  pl.BlockSpec((1, 6, d), lambda b, i, j: (bsz, 0, 0)),
                  pl.BlockSpec((d, gw), lambda b, i, j: (0, j))],
        out_specs=(pl.BlockSpec((tm, gw), lambda b, i, j: (i, b * N_TM_PARTS + jnp.minimum(j, N_TM_PARTS - 1))),
                   pl.BlockSpec((1, tm, gw), lambda b, i, j: (b, i, jnp.maximum(j - N_TM_PARTS, 0)))),
        scratch_shapes=[pltpu.VMEM((tm, d), BF16)],
        compiler_params=_params("arbitrary", "arbitrary", "arbitrary"),
        name="in_projection",
    )(xs, gain, modv, modv, w_bf16)


def _s5_kernel(*refs, rev, tq, nb, has_prev):
    if has_prev:
        u_ref, bw_ref, cw_ref, a_ref, prev_ref, y_ref, z_scr, h_scr = refs
    else:
        u_ref, bw_ref, cw_ref, a_ref, y_ref, z_scr, h_scr = refs
        prev_ref = None
    rows = tq * nb
    half = S5_STATE * 8

    @pl.when(pl.program_id(0) == 0)
    def _():
        h_scr[...] = jnp.zeros_like(h_scr)

    for gb in range(S5_LANE_BLOCKS):
        lanes = slice(gb * 128, (gb + 1) * 128)
        ub = u_ref[:, :, lanes].reshape(rows, 128).astype(BF16)
        z_scr[...] = _dot(ub, bw_ref[gb])
        ar = jnp.broadcast_to(a_ref[gb, 0], (nb, half))
        ai = jnp.broadcast_to(a_ref[gb, 1], (nb, half))

        def step(i, carry):
            hr, hi = carry
            t = (tq - 1 - i) if rev else i
            r0 = pl.multiple_of(t * nb, nb)
            zr = z_scr[pl.ds(r0, nb), 0:half]
            zi = z_scr[pl.ds(r0, nb), half:2 * half]
            nr = ar * hr - ai * hi + zr
            ni = ar * hi + ai * hr + zi
            z_scr[pl.ds(r0, nb), 0:half] = nr
            z_scr[pl.ds(r0, nb), half:2 * half] = ni
            return nr, ni

        hr, hi = lax.fori_loop(0, tq, step, (h_scr[gb, 0], h_scr[gb, 1]), unroll=4)
        h_scr[gb, 0] = hr
        h_scr[gb, 1] = hi
        yb = _dot(z_scr[...].astype(BF16), cw_ref[gb]).reshape(tq, nb, 128)
        if has_prev:
            yb = yb + prev_ref[:, :, lanes]
        y_ref[:, :, lanes] = yb


def _s5_direction(tm3, bw, cw, atab, prev, rev, n_lat, tq):
    ltot, nb, _ = tm3.shape
    gw = GROUP_WIDTH
    n_chunks = ltot // tq
    n_lat_chunks = n_lat // tq
    cidx = lambda s: _scan_chunk_index(s, n_lat_chunks, n_chunks, rev)
    has_prev = prev is not None
    in_specs = [pl.BlockSpec((tq, nb, gw), lambda s: (cidx(s), 0, 0)),
                pl.BlockSpec(bw.shape, lambda s: (0, 0, 0)),
                pl.BlockSpec(cw.shape, lambda s: (0, 0, 0)),
                pl.BlockSpec(atab.shape, lambda s: (0, 0, 0, 0))]
    args = [tm3, bw, cw, atab]
    if has_prev:
        in_specs.append(pl.BlockSpec((tq, nb, gw), lambda s: (cidx(s), 0, 0)))
        args.append(prev)
    kern = functools.partial(_s5_kernel, rev=rev, tq=tq, nb=nb, has_prev=has_prev)
    return pl.pallas_call(
        kern,
        out_shape=jax.ShapeDtypeStruct((ltot, nb, gw), F32),
        grid=(n_chunks,),
        in_specs=in_specs,
        out_specs=pl.BlockSpec((tq, nb, gw), lambda s: (cidx(s), 0, 0)),
        scratch_shapes=[pltpu.VMEM((tq * nb, 2 * S5_STATE * 8), F32),
                        pltpu.VMEM((S5_LANE_BLOCKS, 2, nb, S5_STATE * 8), F32)],
        compiler_params=_params("arbitrary"),
        name="s5_rev" if rev else "s5_fwd",
    )(*args)


def _block_diag(blocks):
    n, r, c = blocks.shape
    eye = jnp.eye(n, dtype=blocks.dtype)
    return (eye[:, None, :, None] * blocks[:, :, None, :]).reshape(n * r, n * c)


def _s5_tables(lam_re, lam_im, log_dt, b_re, b_im, c_re, c_im):
    dt = jnp.exp(log_dt.astype(F32))[:, None]
    lr, li = lam_re.astype(F32), lam_im.astype(F32)
    mag = jnp.exp(lr * dt)
    abar_re, abar_im = mag * jnp.cos(li * dt), mag * jnp.sin(li * dt)
    den = lr * lr + li * li
    zr = abar_re - 1.0
    w_re = (zr * lr + abar_im * li) / den
    w_im = (abar_im * lr - zr * li) / den
    bre, bim = b_re.astype(F32), b_im.astype(F32)
    bw_re = w_re[:, None, :] * bre - w_im[:, None, :] * bim
    bw_im = w_re[:, None, :] * bim + w_im[:, None, :] * bre
    gpb = S5_GROUPS // S5_LANE_BLOCKS
    bws, cws, atabs = [], [], []
    for gb in range(S5_LANE_BLOCKS):
        sl = slice(gb * gpb, (gb + 1) * gpb)
        bws.append(jnp.concatenate([_block_diag(bw_re[sl]), _block_diag(bw_im[sl])], axis=1))
        cws.append(jnp.concatenate([_block_diag(c_re[sl].astype(F32)), -_block_diag(c_im[sl].astype(F32))], axis=0))
        atabs.append(jnp.stack([abar_re[sl].reshape(1, -1), abar_im[sl].reshape(1, -1)]))
    return jnp.stack(bws).astype(BF16), jnp.stack(cws).astype(BF16), jnp.stack(atabs)


def _lru_kernel(*refs, rev, tq, nb, n_lat_chunks, n_chunks, has_prev):
    if has_prev:
        x_ref, xp_ref, xn_ref, cw_ref, cb_ref, wa_ref, wx_ref, ba_ref, bx_ref, sp_ref, prev_ref, o_ref, a_scr, b_scr, h_scr = refs
    else:
        x_ref, xp_ref, xn_ref, cw_ref, cb_ref, wa_ref, wx_ref, ba_ref, bx_ref, sp_ref, o_ref, a_scr, b_scr, h_scr = refs
        prev_ref = None
    s = pl.program_id(0)
    gw = GROUP_WIDTH
    rows = tq * nb

    @pl.when(s == 0)
    def _():
        h_scr[...] = jnp.zeros_like(h_scr)

    c = _scan_chunk_index(s, n_lat_chunks, n_chunks, rev)
    first = jnp.logical_or(c == 0, c == n_lat_chunks)
    last = jnp.logical_or(c == n_lat_chunks - 1, c == n_chunks - 1)
    keep_prev = jnp.where(first, 0.0, 1.0)
    keep_next = jnp.where(last, 0.0, 1.0)
    xc = jnp.concatenate([xp_ref[...] * keep_prev, x_ref[...], xn_ref[...] * keep_next], axis=0)
    conv = cb_ref[...] + xc[0:tq] * cw_ref[0:1, :]
    for k in range(1, LRU_CONV):
        conv = conv + xc[k:k + tq] * cw_ref[k:k + 1, :]
    x = conv.reshape(rows, gw)
    xb = x.astype(BF16)
    r = jax.nn.sigmoid(_dot(xb, wa_ref[...]) + ba_ref[...])
    ig = jax.nn.sigmoid(_dot(xb, wx_ref[...]) + bx_ref[...])
    log_a = -LRU_C * r * sp_ref[...]
    a = jnp.exp(log_a)
    a_scr[...] = a
    b_scr[...] = jnp.sqrt(1.0 - a * a) * (ig * x)

    def step(i, h):
        t = (tq - 1 - i) if rev else i
        r0 = pl.multiple_of(t * nb, nb)
        hn = a_scr[pl.ds(r0, nb), :] * h + b_scr[pl.ds(r0, nb), :]
        b_scr[pl.ds(r0, nb), :] = hn
        return hn

    h_scr[...] = lax.fori_loop(0, tq, step, h_scr[...], unroll=4)
    out = b_scr[...].reshape(tq, nb, gw)
    if has_prev:
        out = out + prev_ref[...]
    o_ref[...] = out


def _lru_direction(tm3, conv_w, conv_b, wa_bd, wx_bd, ba, bx, sp, prev, rev, n_lat, tq):
    ltot, nb, _ = tm3.shape
    gw = GROUP_WIDTH
    n_chunks = ltot // tq
    n_lat_chunks = n_lat // tq
    cidx = lambda s: _scan_chunk_index(s, n_lat_chunks, n_chunks, rev)
    has_prev = prev is not None
    full2 = lambda s: (0, 0)
    in_specs = [pl.BlockSpec((tq, nb, gw), lambda s: (cidx(s), 0, 1)),
                pl.BlockSpec((2, nb, gw), lambda s: (jnp.maximum(cidx(s) * (tq // 2) - 1, 0), 0, 1)),
                pl.BlockSpec((1, nb, gw), lambda s: (jnp.minimum((cidx(s) + 1) * tq, ltot - 1), 0, 1)),
                pl.BlockSpec((LRU_CONV, gw), full2),
                pl.BlockSpec((1, gw), full2),
                pl.BlockSpec((gw, gw), full2),
                pl.BlockSpec((gw, gw), full2),
                pl.BlockSpec((1, gw), full2),
                pl.BlockSpec((1, gw), full2),
                pl.BlockSpec((1, gw), full2)]
    args = [tm3, tm3, tm3, conv_w, conv_b, wa_bd, wx_bd, ba, bx, sp]
    if has_prev:
        in_specs.append(pl.BlockSpec((tq, nb, gw), lambda s: (cidx(s), 0, 0)))
        args.append(prev)
    kern = functools.partial(_lru_kernel, rev=rev, tq=tq, nb=nb, n_lat_chunks=n_lat_chunks,
                             n_chunks=n_chunks, has_prev=has_prev)
    return pl.pallas_call(
        kern,
        out_shape=jax.ShapeDtypeStruct((ltot, nb, gw), F32),
        grid=(n_chunks,),
        in_specs=in_specs,
        out_specs=pl.BlockSpec((tq, nb, gw), lambda s: (cidx(s), 0, 0)),
        scratch_shapes=[pltpu.VMEM((tq * nb, gw), F32),
                        pltpu.VMEM((tq * nb, gw), F32),
                        pltpu.VMEM((nb, gw), F32)],
        compiler_params=_params("arbitrary"),
        name="lru_rev" if rev else "lru_fwd",
    )(*args)


def _tri_mask(n, rev):
    t = lax.broadcasted_iota(I32, (n, n), 0)
    s = lax.broadcasted_iota(I32, (n, n), 1)
    return (s >= t) if rev else (s <= t)


def _hgrn_kernel(*refs, rev, tq, has_prev):
    if has_prev:
        q_ref, z_ref, i_ref, lb_ref, tri_ref, ones_ref, prev_ref, o_ref, st_scr, cum_scr, k_scr, v_scr, acc_scr = refs
    else:
        q_ref, z_ref, i_ref, lb_ref, tri_ref, ones_ref, o_ref, st_scr, cum_scr, k_scr, v_scr, acc_scr = refs
        prev_ref = None
    cs, sb = GLA_CHUNK, GLA_SUB
    nsub = cs // sb
    n_ck = tq // cs

    @pl.when(pl.program_id(2) == 0)
    def _():
        st_scr[...] = jnp.zeros_like(st_scr)

    lb = lb_ref[0]
    row_id = lax.broadcasted_iota(I32, (sb, GLA_DIM), 0)

    def chunk(ci, carry):
        c = (n_ck - 1 - ci) if rev else ci
        r0 = pl.multiple_of(c * cs, cs)
        z = z_ref[0, pl.ds(r0, cs), :]
        q = q_ref[0, pl.ds(r0, cs), :] * (GLA_DIM ** -0.5)
        iv = i_ref[0, pl.ds(r0, cs), :]
        f = lb + (1.0 - lb) * jax.nn.sigmoid(z)
        k = (1.0 - lb) * jax.nn.sigmoid(-z)
        v = iv * jax.nn.sigmoid(iv)
        cum = jnp.dot(tri_ref[...], jnp.log(f), preferred_element_type=F32, precision=lax.Precision.HIGHEST)
        cum_scr[...] = cum
        k_scr[...] = k
        v_scr[...] = v
        last = cum[0:1] if rev else cum[cs - 1:cs]
        st = st_scr[...]
        acc_scr[...] = _dot_nt((q * jnp.exp(cum)).astype(BF16), st.astype(BF16))
        for i in range(nsub):
            blk = slice(i * sb, (i + 1) * sb)
            if rev:
                off = slice((i + 1) * sb, cs)
                ref_row = cum[(i + 1) * sb:(i + 1) * sb + 1] if i < nsub - 1 else None
            else:
                off = slice(0, i * sb)
                ref_row = cum[i * sb - 1:i * sb] if i > 0 else None
            q_i, cum_i = q[blk], cum[blk]
            o_i = acc_scr[blk, :]
            if ref_row is not None:
                qt = q_i * jnp.exp(cum_i - ref_row)
                kt = k[off] * jnp.exp(ref_row - cum[off])
                a_off = _dot_nt(qt.astype(BF16), kt.astype(BF16))
                o_i = o_i + _dot(a_off.astype(BF16), v[off].astype(BF16))

            def diag(s, o_acc):
                r = i * sb + s
                cum_s = cum_scr[pl.ds(r, 1), :]
                k_s = k_scr[pl.ds(r, 1), :]
                v_s = v_scr[pl.ds(r, 1), :]
                valid = (row_id <= s) if rev else (row_id >= s)
                e = jnp.exp(jnp.minimum(cum_i - cum_s, 0.0))
                p = jnp.where(valid, q_i * e * k_s, 0.0)
                return o_acc + _dot(p.astype(BF16), ones_ref[...]) * v_s

            o_i = lax.fori_loop(0, sb, diag, o_i, unroll=4)
            acc_scr[blk, :] = o_i
        out = acc_scr[...]
        if has_prev:
            out = out + prev_ref[0, pl.ds(r0, cs), :]
        o_ref[0, pl.ds(r0, cs), :] = out
        kd = k * jnp.exp(last - cum)
        st_scr[...] = st * jnp.exp(last) + _dot_tn(v.astype(BF16), kd.astype(BF16))
        return carry

    lax.fori_loop(0, n_ck, chunk, 0)


def _gla_specs(ltot, tq, n_lat, rev):
    n_chunks = ltot // tq
    n_lat_chunks = n_lat // tq
    cidx = lambda s: _scan_chunk_index(s, n_lat_chunks, n_chunks, rev)
    part = lambda p: pl.BlockSpec((1, tq, GLA_DIM), lambda b, h, s: (b, cidx(s), p * GLA_HEADS + h))
    return n_chunks, cidx, part


def _hgrn_direction(bm, lb, prev, rev, n_lat, tq):
    bsz, ltot, _ = bm.shape
    n_chunks, cidx, part = _gla_specs(ltot, tq, n_lat, rev)
    has_prev = prev is not None
    tri = _tri_mask(GLA_CHUNK, rev).astype(F32)
    ones = jnp.ones((GLA_DIM, GLA_DIM), BF16)
    z_part = 2 if rev else 1
    in_specs = [part(0), part(z_part), part(3),
                pl.BlockSpec((1, 1, GLA_DIM), lambda b, h, s: (h, 0, 0)),
                pl.BlockSpec(tri.shape, lambda b, h, s: (0, 0)),
                pl.BlockSpec(ones.shape, lambda b, h, s: (0, 0))]
    args = [bm, bm, bm, lb.reshape(GLA_HEADS, 1, GLA_DIM), tri, ones]
    out_spec = pl.BlockSpec((1, tq, GLA_DIM), lambda b, h, s: (b, cidx(s), h))
    if has_prev:
        in_specs.append(out_spec)
        args.append(prev)
    kern = functools.partial(_hgrn_kernel, rev=rev, tq=tq, has_prev=has_prev)
    tile = pltpu.VMEM((GLA_CHUNK, GLA_DIM), F32)
    return pl.pallas_call(
        kern,
        out_shape=jax.ShapeDtypeStruct((bsz, ltot, GROUP_WIDTH), F32),
        grid=(bsz, GLA_HEADS, n_chunks),
        in_specs=in_specs,
        out_specs=out_spec,
        scratch_shapes=[pltpu.VMEM((GLA_DIM, GLA_DIM), F32), tile, tile, tile, tile],
        compiler_params=_params("arbitrary", "arbitrary", "arbitrary"),
        name="hgrn_rev" if rev else "hgrn_fwd",
    )(*args)


def _ret_kernel(*refs, rev, tq, has_prev):
    if has_prev:
        q_ref, k_ref, v_ref, cos_ref, sin_ref, eq_ref, ek_ref, dm_ref, el_ref, prev_ref, o_ref, st_scr = refs
    else:
        q_ref, k_ref, v_ref, cos_ref, sin_ref, eq_ref, ek_ref, dm_ref, el_ref, o_ref, st_scr = refs
        prev_ref = None
    cs = GLA_CHUNK
    n_ck = tq // cs

    @pl.when(pl.program_id(2) == 0)
    def _():
        st_scr[...] = jnp.zeros_like(st_scr)

    def chunk(ci, carry):
        c = (n_ck - 1 - ci) if rev else ci
        r0 = pl.multiple_of(c * cs, cs)
        cos = cos_ref[pl.ds(r0, cs), :]
        sin = sin_ref[pl.ds(r0, cs), :]
        q = q_ref[0, pl.ds(r0, cs), :]
        k = k_ref[0, pl.ds(r0, cs), :]
        v = v_ref[0, pl.ds(r0, cs), :]
        q = q * cos + pltpu.roll(q, GLA_DIM // 2, 1) * sin
        k = (k * cos + pltpu.roll(k, GLA_DIM // 2, 1) * sin) * (GLA_DIM ** -0.5)
        st = st_scr[...]
        vb = v.astype(BF16)
        scores = _dot_nt(q.astype(BF16), k.astype(BF16)) * dm_ref[0]
        out = _dot_nt((q * eq_ref[0]).astype(BF16), st.astype(BF16)) + _dot(scores.astype(BF16), vb)
        if has_prev:
            out = out + prev_ref[0, pl.ds(r0, cs), :]
        o_ref[0, pl.ds(r0, cs), :] = out
        st_scr[...] = st * el_ref[0] + _dot_tn(vb, (k * ek_ref[0]).astype(BF16))
        return carry

    lax.fori_loop(0, n_ck, chunk, 0)


def _ret_tables(rev):
    cs = GLA_CHUNK
    gamma = jnp.log1p(-jnp.exp2(-(RET_DECAY_EXP[1 if rev else 0] + jnp.arange(GLA_HEADS, dtype=F32))))
    t = jnp.arange(cs, dtype=F32)
    steps = (cs - t) if rev else (t + 1.0)
    cum = gamma[:, None] * steps[None, :]
    last = gamma * cs
    eq = jnp.broadcast_to(jnp.exp(cum)[:, :, None], (GLA_HEADS, cs, GLA_DIM))
    ek = jnp.broadcast_to(jnp.exp(last[:, None] - cum)[:, :, None], (GLA_HEADS, cs, GLA_DIM))
    rel = cum[:, :, None] - cum[:, None, :]
    dm = jnp.where(_tri_mask(cs, rev)[None], jnp.exp(jnp.minimum(rel, 0.0)), 0.0)
    el = jnp.broadcast_to(jnp.exp(last)[:, None, None], (GLA_HEADS, 1, GLA_DIM))
    return eq, ek, dm, el


def _rotary_tables(n_lat, n_ctx):
    rows = n_lat // GRID_W
    row = jnp.repeat(jnp.arange(rows, dtype=F32), GRID_W)
    col = jnp.tile(jnp.arange(GRID_W, dtype=F32), rows)
    quarter = GLA_DIM // 4
    inv_freq = ROPE_BASE ** (-jnp.arange(quarter, dtype=F32) / quarter)
    ang = jnp.concatenate([row[:, None] * inv_freq, col[:, None] * inv_freq], axis=-1)
    cos, sin = jnp.cos(ang), jnp.sin(ang)
    cos_l = jnp.concatenate([cos, cos], axis=-1)
    sin_l = jnp.concatenate([-sin, sin], axis=-1)
    cos_t = jnp.concatenate([cos_l, jnp.ones((n_ctx, GLA_DIM), F32)], axis=0)
    sin_t = jnp.concatenate([sin_l, jnp.zeros((n_ctx, GLA_DIM), F32)], axis=0)
    return cos_t, sin_t


def _ret_direction(bm, cos_t, sin_t, prev, rev, n_lat, tq):
    bsz, ltot, _ = bm.shape
    n_chunks, cidx, part = _gla_specs(ltot, tq, n_lat, rev)
    has_prev = prev is not None
    eq, ek, dm, el = _ret_tables(rev)
    head3 = lambda a: pl.BlockSpec((1,) + a.shape[1:], lambda b, h, s: (h, 0, 0))
    pos = pl.BlockSpec((tq, GLA_DIM), lambda b, h, s: (cidx(s), 0))
    in_specs = [part(5), part(6), part(7), pos, pos, head3(eq), head3(ek), head3(dm), head3(el)]
    args = [bm, bm, bm, cos_t, sin_t, eq, ek, dm, el]
    out_spec = pl.BlockSpec((1, tq, GLA_DIM), lambda b, h, s: (b, cidx(s), h))
    if has_prev:
        in_specs.append(out_spec)
        args.append(prev)
    kern = functools.partial(_ret_kernel, rev=rev, tq=tq, has_prev=has_prev)
    return pl.pallas_call(
        kern,
        out_shape=jax.ShapeDtypeStruct((bsz, ltot, GROUP_WIDTH), F32),
        grid=(bsz, GLA_HEADS, n_chunks),
        in_specs=in_specs,
        out_specs=out_spec,
        scratch_shapes=[pltpu.VMEM((GLA_DIM, GLA_DIM), F32)],
        compiler_params=_params("arbitrary", "arbitrary", "arbitrary"),
        name="ret_rev" if rev else "ret_fwd",
    )(*args)


def _head_rms(o):
    parts = []
    for h in range(GLA_HEADS):
        oh = o[:, h * GLA_DIM:(h + 1) * GLA_DIM]
        parts.append(oh * lax.rsqrt(jnp.mean(oh * oh, axis=-1, keepdims=True) + EPS))
    return jnp.concatenate(parts, axis=-1)


def _mix_kernel(u_ref, lg_ref, ys_ref, hs_ref, hg_ref, rg_ref, oh_ref, or_ref, d_ref, gw_ref, gb_ref, o_ref):
    gw = GROUP_WIDTH
    y = jax.nn.gelu(d_ref[...] * u_ref[...] + ys_ref[...])
    a = y * jax.nn.sigmoid(_dot(y.astype(BF16), gw_ref[...]) + gb_ref[...])
    o_ref[0, :, 0:gw] = a.astype(BF16)
    o_ref[0, :, gw:2 * gw] = (hs_ref[...] * jax.nn.gelu(lg_ref[...])).astype(BF16)
    g = hg_ref[0]
    o_ref[0, :, 2 * gw:3 * gw] = (_head_rms(oh_ref[0]) * (g * jax.nn.sigmoid(g))).astype(BF16)
    g = rg_ref[0]
    o_ref[0, :, 3 * gw:4 * gw] = (_head_rms(or_ref[0]) * (g * jax.nn.sigmoid(g))).astype(BF16)


def _mix_epilogue(tm2, ys2, hs2, bm, o_h, o_r, s5_d, glu_w, glu_b, n_rows, tm):
    bsz, ltot, _ = bm.shape
    gw = GROUP_WIDTH
    tmaj = lambda p, k: pl.BlockSpec((tm, gw), lambda b, i: (i, b * k + p))
    bmaj = lambda p: pl.BlockSpec((1, tm, gw), lambda b, i: (b, i, p))
    full2 = lambda b, i: (0, 0)
    return pl.pallas_call(
        _mix_kernel,
        out_shape=jax.ShapeDtypeStruct((bsz, ltot, 4 * gw), BF16),
        grid=(bsz, n_rows // tm),
        in_specs=[tmaj(0, N_TM_PARTS), tmaj(2, N_TM_PARTS), tmaj(0, 1), tmaj(0, 1),
                  bmaj(4), bmaj(8), bmaj(0), bmaj(0),
                  pl.BlockSpec((1, gw), full2), pl.BlockSpec((gw, gw), full2), pl.BlockSpec((1, gw), full2)],
        out_specs=pl.BlockSpec((1, tm, 4 * gw), lambda b, i: (b, i, 0)),
        compiler_params=_params("arbitrary", "arbitrary"),
        name="mix_epilogue",
    )(tm2, tm2, ys2, hs2, bm, bm, o_h, o_r, s5_d, glu_w, glu_b)


def _gate_rows(modb_ref, modc_ref, idx, row0, tm, n_lat):
    row = row0 + lax.broadcasted_iota(I32, (tm, 1), 0)
    return jnp.where(row >= n_lat, modc_ref[0, idx:idx + 1, :], modb_ref[0, idx:idx + 1, :])


def _outproj_kernel(a_ref, w_ref, x_ref, modb_ref, modc_ref, o_ref, *, tm, n_lat):
    gate = _gate_rows(modb_ref, modc_ref, 2, pl.program_id(1) * tm, tm, n_lat)
    o_ref[0] = x_ref[0] + gate * _dot(a_ref[0], w_ref[...])


def _out_projection(mix, w_bf16, xs, modv, n_rows, n_lat, tm, tn):
    bsz, ltot, d = xs.shape
    k = mix.shape[-1]
    kern = functools.partial(_outproj_kernel, tm=tm, n_lat=n_lat)
    return pl.pallas_call(
        kern,
        out_shape=jax.ShapeDtypeStruct((bsz, ltot, d), F32),
        grid=(bsz, n_rows // tm, d // tn),
        in_specs=[pl.BlockSpec((1, tm, k), lambda b, i, j: (b, i, 0)),
                  pl.BlockSpec((k, tn), lambda b, i, j: (0, j)),
                  pl.BlockSpec((1, tm, tn), lambda b, i, j: (b, i, j)),
                  pl.BlockSpec((1, 6, tn), lambda b, i, j: (b, 0, j)),
                  pl.BlockSpec((1, 6, tn), lambda b, i, j: (bsz, 0, j))],
        out_specs=pl.BlockSpec((1, tm, tn), lambda b, i, j: (b, i, j)),
        compiler_params=_params("arbitrary", "arbitrary", "arbitrary"),
        name="out_projection",
    )(mix, w_bf16, xs, modv, modv)


def _first_max4(v):
    m1 = jnp.maximum(jnp.maximum(v[0], v[1]), jnp.maximum(v[2], v[3]))
    i1 = jnp.where(v[0] == m1, 0, jnp.where(v[1] == m1, 1, jnp.where(v[2] == m1, 2, 3)))
    rest = [jnp.where(i1 == j, -jnp.inf, v[j]) for j in range(4)]
    m2 = jnp.maximum(jnp.maximum(rest[0], rest[1]), jnp.maximum(rest[2], rest[3]))
    i2 = jnp.where(rest[0] == m2, 0, jnp.where(rest[1] == m2, 1, jnp.where(rest[2] == m2, 2, 3)))
    return m1, i1, m2, i2


def _router_kernel(x_ref, g_ref, modb_ref, modc_ref, rw_ref, rb_ref, tri_ref,
                   h_ref, ids_ref, gates_ref, ranks_ref, counts_ref, base_scr, *, tm, n_lat):
    first = jnp.logical_and(pl.program_id(0) == 0, pl.program_id(1) == 0)

    @pl.when(first)
    def _():
        base_scr[...] = jnp.zeros_like(base_scr)

    h = _norm_modulate(x_ref[0], g_ref[...], modb_ref, modc_ref, pl.program_id(1) * tm, n_lat, 3, 4)
    h_ref[0] = h.astype(BF16)
    logits = lax.dot_general(rw_ref[...], h, (((1,), (1,)), ((), ())), preferred_element_type=F32,
                             precision=lax.Precision.HIGHEST)
    scores = jax.nn.sigmoid(logits)
    biased = scores + rb_ref[...]
    tops = []
    for gidx in range(N_EXPERT_GROUPS):
        rows = [biased[gidx * EXPERTS_PER_GROUP + j:gidx * EXPERTS_PER_GROUP + j + 1, :]
                for j in range(EXPERTS_PER_GROUP)]
        tops.append(_first_max4(rows))
    gs = [t[0] + t[2] for t in tops]
    gmax = jnp.maximum(jnp.maximum(gs[0], gs[1]), jnp.maximum(gs[2], gs[3]))
    best = jnp.where(gs[0] == gmax, 0, jnp.where(gs[1] == gmax, 1, jnp.where(gs[2] == gmax, 2, 3)))
    e1 = jnp.zeros_like(best)
    e2 = jnp.zeros_like(best)
    for gidx in range(N_EXPERT_GROUPS):
        e1 = jnp.where(best == gidx, gidx * EXPERTS_PER_GROUP + tops[gidx][1], e1)
        e2 = jnp.where(best == gidx, gidx * EXPERTS_PER_GROUP + tops[gidx][3], e2)
    eid = lax.broadcasted_iota(I32, (N_EXPERTS, tm), 0)
    sel1 = eid == e1
    sel2 = eid == e2
    w1 = jnp.sum(jnp.where(sel1, scores, 0.0), axis=0, keepdims=True)
    w2 = jnp.sum(jnp.where(sel2, scores, 0.0), axis=0, keepdims=True)
    wsum = w1 + w2
    onehot = jnp.where(jnp.logical_or(sel1, sel2), 1.0, 0.0)
    pos = base_scr[...] + _dot(onehot.astype(BF16), tri_ref[...])
    r1 = jnp.sum(jnp.where(sel1, pos, 0.0), axis=0, keepdims=True)
    r2 = jnp.sum(jnp.where(sel2, pos, 0.0), axis=0, keepdims=True)
    ids_ref[0] = jnp.concatenate([e1, e2], axis=0)
    gates_ref[0] = jnp.concatenate([w1 / wsum, w2 / wsum], axis=0)
    ranks_ref[0] = jnp.concatenate([r1, r2], axis=0).astype(I32)
    base = base_scr[...] + jnp.sum(onehot, axis=1, keepdims=True)
    base_scr[...] = base
    counts_ref[...] = jnp.broadcast_to(base, counts_ref.shape)


def _router(xs, gain, modv, rw_t, rbias, n_rows, n_lat, tm):
    bsz, ltot, d = xs.shape
    tri = (lax.broadcasted_iota(I32, (tm, tm), 0) < lax.broadcasted_iota(I32, (tm, tm), 1)).astype(BF16)
    kern = functools.partial(_router_kernel, tm=tm, n_lat=n_lat)
    small = lambda dt: jax.ShapeDtypeStruct((bsz, 2, n_rows), dt)
    small_spec = pl.BlockSpec((1, 2, tm), lambda b, i: (b, 0, i))
    return pl.pallas_call(
        kern,
        out_shape=(jax.ShapeDtypeStruct((bsz, ltot, d), BF16), small(I32), small(F32), small(I32),
                   jax.ShapeDtypeStruct((N_EXPERTS, 128), F32)),
        grid=(bsz, n_rows // tm),
        in_specs=[pl.BlockSpec((1, tm, d), lambda b, i: (b, i, 0)),
                  pl.BlockSpec((1, d), lambda b, i: (0, 0)),
                  pl.BlockSpec((1, 6, d), lambda b, i: (b, 0, 0)),
                  pl.BlockSpec((1, 6, d), lambda b, i: (bsz, 0, 0)),
                  pl.BlockSpec((N_EXPERTS, d), lambda b, i: (0, 0)),
                  pl.BlockSpec((N_EXPERTS, 1), lambda b, i: (0, 0)),
                  pl.BlockSpec((tm, tm), lambda b, i: (0, 0))],
        out_specs=(pl.BlockSpec((1, tm, d), lambda b, i: (b, i, 0)), small_spec, small_spec, small_spec,
                   pl.BlockSpec((N_EXPERTS, 128), lambda b, i: (0, 0))),
        scratch_shapes=[pltpu.VMEM((N_EXPERTS, 1), F32)],
        compiler_params=_params("arbitrary", "arbitrary"),
        name="moe_router",
    )(xs, gain, modv, modv, rw_t, rbias, tri)


def _expert_ffn_kernel(te_ref, nt_ref, x_ref, wg_ref, wu_ref, wd_ref, o_ref):
    @pl.when(pl.program_id(0) < nt_ref[0])
    def _():
        x = x_ref[...]
        g = _dot(x, wg_ref[0])
        u = _dot(x, wu_ref[0])
        he = (g * jax.nn.sigmoid(g)) * u
        o_ref[...] = _dot(he.astype(BF16), wd_ref[0])


def _expert_ffn(tile_expert, n_tiles, x_sorted, wg, wu, wd, tm):
    p, d = x_sorted.shape
    f = wg.shape[-1]
    tile = lambda i, te, nt: jnp.minimum(i, nt[0] - 1)
    expert = lambda i, te, nt: te[jnp.minimum(i, nt[0] - 1)]
    grid_spec = pltpu.PrefetchScalarGridSpec(
        num_scalar_prefetch=2,
        grid=(p // tm,),
        in_specs=[pl.BlockSpec((tm, d), lambda i, te, nt: (tile(i, te, nt), 0)),
                  pl.BlockSpec((1, d, f), lambda i, te, nt: (expert(i, te, nt), 0, 0)),
                  pl.BlockSpec((1, d, f), lambda i, te, nt: (expert(i, te, nt), 0, 0)),
                  pl.BlockSpec((1, f, d), lambda i, te, nt: (expert(i, te, nt), 0, 0))],
        out_specs=pl.BlockSpec((tm, d), lambda i, te, nt: (tile(i, te, nt), 0)),
    )
    return pl.pallas_call(
        _expert_ffn_kernel,
        out_shape=jax.ShapeDtypeStruct((p, d), F32),
        grid_spec=grid_spec,
        compiler_params=_params("arbitrary"),
        name="moe_expert_ffn",
    )(tile_expert, n_tiles, x_sorted, wg, wu, wd)


def _moe_residual_kernel(x_ref, y_ref, modb_ref, modc_ref, g_ref, o_ref, *, tm, n_lat, final_norm):
    gate = _gate_rows(modb_ref, modc_ref, 5, pl.program_id(1) * tm, tm, n_lat)
    x = x_ref[0] + gate * y_ref[0]
    if final_norm:
        x = x * lax.rsqrt(jnp.mean(x * x, axis=-1, keepdims=True) + EPS) * g_ref[...]
    o_ref[0] = x


def _moe_residual(xs, y, modv, final_g, n_rows, n_lat, tm, final_norm):
    bsz, ltot, d = xs.shape
    kern = functools.partial(_moe_residual_kernel, tm=tm, n_lat=n_lat, final_norm=final_norm)
    rows_out = n_rows if final_norm else ltot
    return pl.pallas_call(
        kern,
        out_shape=jax.ShapeDtypeStruct((bsz, rows_out, d), F32),
        grid=(bsz, n_rows // tm),
        in_specs=[pl.BlockSpec((1, tm, d), lambda b, i: (b, i, 0)),
                  pl.BlockSpec((1, tm, d), lambda b, i: (b, i, 0)),
                  pl.BlockSpec((1, 6, d), lambda b, i: (b, 0, 0)),
                  pl.BlockSpec((1, 6, d), lambda b, i: (bsz, 0, 0)),
                  pl.BlockSpec((1, d), lambda b, i: (0, 0))],
        out_specs=pl.BlockSpec((1, tm, d), lambda b, i: (b, i, 0)),
        compiler_params=_params("arbitrary", "arbitrary"),
        name="moe_residual",
    )(xs, y, modv, modv, final_g)


def _moe_layer(xs, gain, modv, rw_t, rbias, wg, wu, wd, final_g, n_rows, n_lat, tm_tok, tm_res, tm_exp, final_norm):
    bsz, ltot, d = xs.shape
    h, ids, gates, ranks, counts = _router(xs, gain, modv, rw_t, rbias, n_rows, n_lat, tm_tok)
    counts = counts[:, 0].astype(I32)
    padded = ((counts + tm_exp - 1) // tm_exp) * tm_exp
    ends = jnp.cumsum(padded)
    offsets = ends - padded
    dest = offsets[ids] + ranks
    n_slots = bsz * 2 * n_rows
    p_rows = ((n_slots + N_EXPERTS * (tm_exp - 1)) + tm_exp - 1) // tm_exp * tm_exp
    tok = (jnp.arange(bsz, dtype=I32)[:, None, None] * ltot + jnp.arange(n_rows, dtype=I32)[None, None, :])
    tok = jnp.broadcast_to(tok, dest.shape)
    src = jnp.zeros((p_rows,), I32).at[dest.reshape(-1)].set(tok.reshape(-1))
    tile_start = jnp.arange(p_rows // tm_exp, dtype=I32) * tm_exp
    tile_expert = jnp.minimum(jnp.sum(tile_start[:, None] >= ends[None, :], axis=1), N_EXPERTS - 1).astype(I32)
    n_tiles = (ends[-1:] // tm_exp).astype(I32)
    x_sorted = jnp.take(h.reshape(bsz * ltot, d), src, axis=0)
    y_sorted = _expert_ffn(tile_expert, n_tiles, x_sorted, wg, wu, wd, tm_exp)
    y = (jnp.take(y_sorted, dest[:, 0].reshape(-1), axis=0) * gates[:, 0].reshape(-1, 1)
         + jnp.take(y_sorted, dest[:, 1].reshape(-1), axis=0) * gates[:, 1].reshape(-1, 1))
    y = y.reshape(bsz, n_rows, d)
    return _moe_residual(xs, y, modv, final_g, n_rows, n_lat, tm_res, final_norm)


def _row_tile(n, target):
    t = min(n, target)
    while n % t or t % 8:
        t -= 8
    return t


def kernel(x, c, ctx, c_ctx, norm_mix_g, norm_ffn_g, w_mod, b_mod, w_in, w_out, s5_lam_re, s5_lam_im, s5_log_dt, s5_b_re, s5_b_im, s5_c_re, s5_c_im, s5_d, s5_glu_w, s5_glu_b, lru_conv_w, lru_conv_b, lru_wa, lru_ba, lru_wx, lru_bx, lru_lam, hgrn_lb_logits, router_w, router_bias, moe_w_gate, moe_w_up, moe_w_down, final_norm_g):
    bsz, n_lat, d = x.shape
    n_ctx = ctx.shape[1]
    ltot = n_lat + n_ctx
    depth = w_in.shape[0]
    gw = GROUP_WIDTH
    assert bsz % 8 == 0 and bsz + 1 <= MOD_ROWS
    assert n_lat % GLA_CHUNK == 0 and n_ctx % GLA_CHUNK == 0 and n_lat % GRID_W == 0

    tq_scan = math.gcd(math.gcd(n_lat, n_ctx), 64)
    tq_gla = math.gcd(math.gcd(n_lat, n_ctx), 256)
    tm_all = _row_tile(ltot, 768)
    tm_lat = _row_tile(n_lat, 1024)
    tm_mix = math.gcd(math.gcd(n_lat, n_ctx), 256)

    xs = jnp.concatenate([x, ctx.astype(x.dtype)], axis=1)
    cv = jnp.zeros((MOD_ROWS, d), F32).at[:bsz].set(c).at[bsz].set(c_ctx)
    mod_all = _mod_vectors(cv, w_mod, b_mod).reshape(depth, MOD_ROWS, 6, d)
    cos_t, sin_t = _rotary_tables(n_lat, n_ctx)
    rw_t = router_w.astype(F32).T
    rbias = router_bias.astype(F32).reshape(N_EXPERTS, 1)
    lb_cum = [jnp.cumsum(jax.nn.softmax(hgrn_lb_logits[dr].astype(F32), axis=0), axis=0) for dr in (0, 1)]

    for layer in range(depth):
        ctx_out = layer < depth - 1
        n_rows = ltot if ctx_out else n_lat
        tm_tok = tm_all if ctx_out else tm_lat
        modv = mod_all[layer]

        tm2, bm = _in_projection(xs, norm_mix_g[layer].reshape(1, d), modv, w_in[layer].astype(BF16), n_lat, tm_all)
        tm3 = tm2.reshape(ltot, bsz, N_TM_PARTS * gw)

        ys = hs = o_h = o_r = None
        for dr in (0, 1):
            rev = dr == 1
            bw, cw, atab = _s5_tables(s5_lam_re[layer, dr], s5_lam_im[layer, dr], s5_log_dt[layer, dr],
                                      s5_b_re[layer, dr], s5_b_im[layer, dr], s5_c_re[layer, dr], s5_c_im[layer, dr])
            ys = _s5_direction(tm3, bw, cw, atab, ys, rev, n_lat, tq_scan)
            sp = jax.nn.softplus(-lru_lam[layer, dr].astype(F32)).reshape(1, gw)
            hs = _lru_direction(tm3, lru_conv_w[layer].astype(F32), lru_conv_b[layer].astype(F32).reshape(1, gw),
                                _block_diag(lru_wa[layer, dr]).astype(BF16), _block_diag(lru_wx[layer, dr]).astype(BF16),
                                lru_ba[layer, dr].astype(F32).reshape(1, gw), lru_bx[layer, dr].astype(F32).reshape(1, gw),
                                sp, hs, rev, n_lat, tq_scan)
            lb = lb_cum[dr][layer] - lb_cum[dr][0]
            o_h = _hgrn_direction(bm, lb, o_h, rev, n_lat, tq_gla)
            o_r = _ret_direction(bm, cos_t, sin_t, o_r, rev, n_lat, tq_gla)

        mix = _mix_epilogue(tm2, ys.reshape(ltot, bsz * gw), hs.reshape(ltot, bsz * gw), bm, o_h, o_r,
                            s5_d[layer].astype(F32).reshape(1, gw), s5_glu_w[layer].astype(BF16),
                            s5_glu_b[layer].astype(F32).reshape(1, gw), n_rows, tm_mix)
        xs = _out_projection(mix, w_out[layer].astype(BF16), xs, modv, n_rows, n_lat, tm_tok, 512)
        xs = _moe_layer(xs, norm_ffn_g[layer].reshape(1, d), modv, rw_t, rbias,
                        moe_w_gate[layer].astype(BF16), moe_w_up[layer].astype(BF16), moe_w_down[layer].astype(BF16),
                        final_norm_g.reshape(1, d), n_rows, n_lat, tm_tok, tm_mix, 512, not ctx_out)
    return xs
```

```python
import functools
import math

import jax
import jax.numpy as jnp
from jax import lax
from jax.experimental import pallas as pl
from jax.experimental.pallas import tpu as pltpu

F32 = jnp.float32
BF16 = jnp.bfloat16
I32 = jnp.int32

GROUP_WIDTH = 512
N_IN_PARTS = 12
N_TM_PARTS = 3
N_BM_PARTS = N_IN_PARTS - N_TM_PARTS
S5_CH = 16
S5_GROUPS = 32
S5_STATE = 64
S5_LANE_BLOCKS = 4
LRU_HEADS = 8
LRU_CONV = 4
LRU_C = 8.0
GLA_HEADS = 4
GLA_DIM = 128
GLA_CHUNK = 64
GLA_SUB = 16
RET_CHUNK = 256
RET_DECAY_EXP = (5.0, 5.5)
ROPE_BASE = 10000.0
GRID_W = 64
N_EXPERTS = 16
N_EXPERT_GROUPS = 4
EXPERTS_PER_GROUP = 4
D_FF_EXPERT = 1024
EPS = 1e-6
MOD_ROWS = 24

VMEM_LIMIT_BYTES = 56 * 1024 * 1024


def _params(*semantics):
    return pltpu.CompilerParams(dimension_semantics=semantics, vmem_limit_bytes=VMEM_LIMIT_BYTES)


def _dot(a, b):
    return jnp.dot(a, b, preferred_element_type=F32)


def _dot_nt(a, b):
    return lax.dot_general(a, b, (((1,), (1,)), ((), ())), preferred_element_type=F32)


def _dot_tn(a, b):
    return lax.dot_general(a, b, (((0,), (0,)), ((), ())), preferred_element_type=F32)


def _scan_chunk_index(step, n_lat_chunks, n_chunks, rev):
    if rev:
        return n_chunks - 1 - step
    return (step + n_lat_chunks) % n_chunks


def _mod_kernel(cv_ref, w_ref, b_ref, o_ref):
    cv = cv_ref[...]
    s = cv * jax.nn.sigmoid(cv)
    o_ref[0] = _dot(s.astype(BF16), w_ref[0].astype(BF16)) + b_ref[0]


def _mod_vectors(cv, w_mod, b_mod):
    nl, d, n6 = w_mod.shape
    tn = 1024
    return pl.pallas_call(
        _mod_kernel,
        out_shape=jax.ShapeDtypeStruct((nl, MOD_ROWS, n6), F32),
        grid=(nl, n6 // tn),
        in_specs=[pl.BlockSpec((MOD_ROWS, d), lambda l, j: (0, 0)),
                  pl.BlockSpec((1, d, tn), lambda l, j: (l, 0, j)),
                  pl.BlockSpec((1, 1, tn), lambda l, j: (l, 0, j))],
        out_specs=pl.BlockSpec((1, MOD_ROWS, tn), lambda l, j: (l, 0, j)),
        compiler_params=_params("arbitrary", "arbitrary"),
        name="mod_vectors",
    )(cv, w_mod, b_mod.reshape(nl, 1, n6))


def _norm_modulate(x, g, modb_ref, modc_ref, row0, n_lat, shift_idx, scale_idx):
    tm = x.shape[0]
    ms = jnp.mean(x * x, axis=-1, keepdims=True)
    y = x * lax.rsqrt(ms + EPS) * g
    row = row0 + lax.broadcasted_iota(I32, (tm, 1), 0)
    is_ctx = row >= n_lat
    shift = jnp.where(is_ctx, modc_ref[0, shift_idx:shift_idx + 1, :], modb_ref[0, shift_idx:shift_idx + 1, :])
    scale = jnp.where(is_ctx, modc_ref[0, scale_idx:scale_idx + 1, :], modb_ref[0, scale_idx:scale_idx + 1, :])
    return y * (1.0 + scale) + shift


def _inproj_kernel(x_ref, g_ref, modb_ref, modc_ref, w_ref, otm_ref, obm_ref, h_scr, *, tm, n_lat):
    i = pl.program_id(1)
    j = pl.program_id(2)

    @pl.when(j == 0)
    def _():
        h = _norm_modulate(x_ref[0], g_ref[...], modb_ref, modc_ref, i * tm, n_lat, 0, 1)
        h_scr[...] = h.astype(BF16)

    r = _dot(h_scr[...], w_ref[...])

    @pl.when(j < N_TM_PARTS)
    def _():
        otm_ref[...] = r

    @pl.when(j >= N_TM_PARTS)
    def _():
        obm_ref[0] = r


def _in_projection(xs, gain, modv, w_bf16, n_lat, tm):
    bsz, ltot, d = xs.shape
    gw = GROUP_WIDTH
    kern = functools.partial(_inproj_kernel, tm=tm, n_lat=n_lat)
    return pl.pallas_call(
        kern,
        out_shape=(jax.ShapeDtypeStruct((ltot, bsz * N_TM_PARTS * gw), F32),
                   jax.ShapeDtypeStruct((bsz, ltot, N_BM_PARTS * gw), F32)),
        grid=(bsz, ltot // tm, N_IN_PARTS),
        in_specs=[pl.BlockSpec((1, tm, d), lambda b, i, j: (b, i, 0)),
                  pl.BlockSpec((1, d), lambda b, i, j: (0, 0)),
                  pl.BlockSpec((1, 6, d), lambda b, i, j: (b, 0, 0)),
                  pl.BlockSpec((1, 6, d), lambda b, i, j: (bsz, 0, 0)),
                  pl.BlockSpec((d, gw), lambda b, i, j: (0, j))],
        out_specs=(pl.BlockSpec((tm, gw), lambda b, i, j: (i, b * N_TM_PARTS + jnp.minimum(j, N_TM_PARTS - 1))),
                   pl.BlockSpec((1, tm, gw), lambda b, i, j: (b, i, jnp.maximum(j - N_TM_PARTS, 0)))),
        scratch_shapes=[pltpu.VMEM((tm, d), BF16)],
        compiler_params=_params("arbitrary", "arbitrary", "arbitrary"),
        name="in_projection",
    )(xs, gain, modv, modv, w_bf16)


def _s5_kernel(*refs, rev, tq, nb, has_prev):
    if has_prev:
        u_ref, bw_ref, cw_ref, a_ref, prev_ref, y_ref, z_scr, h_scr = refs
    else:
        u_ref, bw_ref, cw_ref, a_ref, y_ref, z_scr, h_scr = refs
        prev_ref = None
    rows = tq * nb
    half = S5_STATE * 8

    @pl.when(pl.program_id(0) == 0)
    def _():
        h_scr[...] = jnp.zeros_like(h_scr)

    for gb in range(S5_LANE_BLOCKS):
        lanes = slice(gb * 128, (gb + 1) * 128)
        ub = u_ref[:, :, lanes].reshape(rows, 128).astype(BF16)
        z_scr[...] = _dot(ub, bw_ref[gb])
        ar = jnp.broadcast_to(a_ref[gb, 0], (nb, half))
        ai = jnp.broadcast_to(a_ref[gb, 1], (nb, half))

        def step(i, carry):
            hr, hi = carry
            t = (tq - 1 - i) if rev else i
            r0 = pl.multiple_of(t * nb, nb)
            zr = z_scr[pl.ds(r0, nb), 0:half]
            zi = z_scr[pl.ds(r0, nb), half:2 * half]
            nr = ar * hr - ai * hi + zr
            ni = ar * hi + ai * hr + zi
            z_scr[pl.ds(r0, nb), 0:half] = nr
            z_scr[pl.ds(r0, nb), half:2 * half] = ni
            return nr, ni

        hr, hi = lax.fori_loop(0, tq, step, (h_scr[gb, 0], h_scr[gb, 1]), unroll=4)
        h_scr[gb, 0] = hr
        h_scr[gb, 1] = hi
        yb = _dot(z_scr[...].astype(BF16), cw_ref[gb]).reshape(tq, nb, 128)
        if has_prev:
            yb = yb + prev_ref[:, :, lanes]
        y_ref[:, :, lanes] = yb


def _s5_direction(tm3, bw, cw, atab, prev, rev, n_lat, tq):
    ltot, nb, _ = tm3.shape
    gw = GROUP_WIDTH
    n_chunks = ltot // tq
    n_lat_chunks = n_lat // tq
    cidx = lambda s: _scan_chunk_index(s, n_lat_chunks, n_chunks, rev)
    has_prev = prev is not None
    in_specs = [pl.BlockSpec((tq, nb, gw), lambda s: (cidx(s), 0, 0)),
                pl.BlockSpec(bw.shape, lambda s: (0, 0, 0)),
                pl.BlockSpec(cw.shape, lambda s: (0, 0, 0)),
                pl.BlockSpec(atab.shape, lambda s: (0, 0, 0, 0))]
    args = [tm3, bw, cw, atab]
    if has_prev:
        in_specs.append(pl.BlockSpec((tq, nb, gw), lambda s: (cidx(s), 0, 0)))
        args.append(prev)
    kern = functools.partial(_s5_kernel, rev=rev, tq=tq, nb=nb, has_prev=has_prev)
    return pl.pallas_call(
        kern,
        out_shape=jax.ShapeDtypeStruct((ltot, nb, gw), F32),
        grid=(n_chunks,),
        in_specs=in_specs,
        out_specs=pl.BlockSpec((tq, nb, gw), lambda s: (cidx(s), 0, 0)),
        scratch_shapes=[pltpu.VMEM((tq * nb, 2 * S5_STATE * 8), F32),
                        pltpu.VMEM((S5_LANE_BLOCKS, 2, nb, S5_STATE * 8), F32)],
        compiler_params=_params("arbitrary"),
        name="s5_rev" if rev else "s5_fwd",
    )(*args)


def _block_diag(blocks):
    n, r, c = blocks.shape
    eye = jnp.eye(n, dtype=blocks.dtype)
    return (eye[:, None, :, None] * blocks[:, :, None, :]).reshape(n * r, n * c)


def _s5_tables(lam_re, lam_im, log_dt, b_re, b_im, c_re, c_im):
    dt = jnp.exp(log_dt.astype(F32))[:, None]
    lr, li = lam_re.astype(F32), lam_im.astype(F32)
    mag = jnp.exp(lr * dt)
    abar_re, abar_im = mag * jnp.cos(li * dt), mag * jnp.sin(li * dt)
    den = lr * lr + li * li
    zr = abar_re - 1.0
    w_re = (zr * lr + abar_im * li) / den
    w_im = (abar_im * lr - zr * li) / den
    bre, bim = b_re.astype(F32), b_im.astype(F32)
    bw_re = w_re[:, None, :] * bre - w_im[:, None, :] * bim
    bw_im = w_re[:, None, :] * bim + w_im[:, None, :] * bre
    gpb = S5_GROUPS // S5_LANE_BLOCKS
    bws, cws, atabs = [], [], []
    for gb in range(S5_LANE_BLOCKS):
        sl = slice(gb * gpb, (gb + 1) * gpb)
        bws.append(jnp.concatenate([_block_diag(bw_re[sl]), _block_diag(bw_im[sl])], axis=1))
        cws.append(jnp.concatenate([_block_diag(c_re[sl].astype(F32)), -_block_diag(c_im[sl].astype(F32))], axis=0))
        atabs.append(jnp.stack([abar_re[sl].reshape(1, -1), abar_im[sl].reshape(1, -1)]))
    return jnp.stack(bws).astype(BF16), jnp.stack(cws).astype(BF16), jnp.stack(atabs)


def _lru_kernel(*refs, rev, tq, nb, n_lat_chunks, n_chunks, has_prev):
    if has_prev:
        x_ref, xp_ref, xn_ref, cw_ref, cb_ref, wa_ref, wx_ref, ba_ref, bx_ref, sp_ref, prev_ref, o_ref, a_scr, b_scr, h_scr = refs
    else:
        x_ref, xp_ref, xn_ref, cw_ref, cb_ref, wa_ref, wx_ref, ba_ref, bx_ref, sp_ref, o_ref, a_scr, b_scr, h_scr = refs
        prev_ref = None
    s = pl.program_id(0)
    gw = GROUP_WIDTH
    rows = tq * nb

    @pl.when(s == 0)
    def _():
        h_scr[...] = jnp.zeros_like(h_scr)

    c = _scan_chunk_index(s, n_lat_chunks, n_chunks, rev)
    first = jnp.logical_or(c == 0, c == n_lat_chunks)
    last = jnp.logical_or(c == n_lat_chunks - 1, c == n_chunks - 1)
    keep_prev = jnp.where(first, 0.0, 1.0)
    keep_next = jnp.where(last, 0.0, 1.0)
    xc = jnp.concatenate([xp_ref[...] * keep_prev, x_ref[...], xn_ref[...] * keep_next], axis=0)
    conv = cb_ref[...] + xc[0:tq] * cw_ref[0:1, :]
    for k in range(1, LRU_CONV):
        conv = conv + xc[k:k + tq] * cw_ref[k:k + 1, :]
    x = conv.reshape(rows, gw)
    xb = x.astype(BF16)
    r = jax.nn.sigmoid(_dot(xb, wa_ref[...]) + ba_ref[...])
    ig = jax.nn.sigmoid(_dot(xb, wx_ref[...]) + bx_ref[...])
    log_a = -LRU_C * r * sp_ref[...]
    a = jnp.exp(log_a)
    a_scr[...] = a
    b_scr[...] = jnp.sqrt(1.0 - a * a) * (ig * x)

    def step(i, h):
        t = (tq - 1 - i) if rev else i
        r0 = pl.multiple_of(t * nb, nb)
        hn = a_scr[pl.ds(r0, nb), :] * h + b_scr[pl.ds(r0, nb), :]
        b_scr[pl.ds(r0, nb), :] = hn
        return hn

    h_scr[...] = lax.fori_loop(0, tq, step, h_scr[...], unroll=4)
    out = b_scr[...].reshape(tq, nb, gw)
    if has_prev:
        out = out + prev_ref[...]
    o_ref[...] = out


def _lru_direction(tm3, conv_w, conv_b, wa_bd, wx_bd, ba, bx, sp, prev, rev, n_lat, tq):
    ltot, nb, _ = tm3.shape
    gw = GROUP_WIDTH
    n_chunks = ltot // tq
    n_lat_chunks = n_lat // tq
    cidx = lambda s: _scan_chunk_index(s, n_lat_chunks, n_chunks, rev)
    has_prev = prev is not None
    full2 = lambda s: (0, 0)
    in_specs = [pl.BlockSpec((tq, nb, gw), lambda s: (cidx(s), 0, 1)),
                pl.BlockSpec((2, nb, gw), lambda s: (jnp.maximum(cidx(s) * (tq // 2) - 1, 0), 0, 1)),
                pl.BlockSpec((1, nb, gw), lambda s: (jnp.minimum((cidx(s) + 1) * tq, ltot - 1), 0, 1)),
                pl.BlockSpec((LRU_CONV, gw), full2),
                pl.BlockSpec((1, gw), full2),
                pl.BlockSpec((gw, gw), full2),
                pl.BlockSpec((gw, gw), full2),
                pl.BlockSpec((1, gw), full2),
                pl.BlockSpec((1, gw), full2),
                pl.BlockSpec((1, gw), full2)]
    args = [tm3, tm3, tm3, conv_w, conv_b, wa_bd, wx_bd, ba, bx, sp]
    if has_prev:
        in_specs.append(pl.BlockSpec((tq, nb, gw), lambda s: (cidx(s), 0, 0)))
        args.append(prev)
    kern = functools.partial(_lru_kernel, rev=rev, tq=tq, nb=nb, n_lat_chunks=n_lat_chunks,
                             n_chunks=n_chunks, has_prev=has_prev)
    return pl.pallas_call(
        kern,
        out_shape=jax.ShapeDtypeStruct((ltot, nb, gw), F32),
        grid=(n_chunks,),
        in_specs=in_specs,
        out_specs=pl.BlockSpec((tq, nb, gw), lambda s: (cidx(s), 0, 0)),
        scratch_shapes=[pltpu.VMEM((tq * nb, gw), F32),
                        pltpu.VMEM((tq * nb, gw), F32),
                        pltpu.VMEM((nb, gw), F32)],
        compiler_params=_params("arbitrary"),
        name="lru_rev" if rev else "lru_fwd",
    )(*args)


def _tri_mask(n, rev):
    t = lax.broadcasted_iota(I32, (n, n), 0)
    s = lax.broadcasted_iota(I32, (n, n), 1)
    return (s >= t) if rev else (s <= t)


def _hgrn_kernel(*refs, rev, tq, has_prev):
    if has_prev:
        q_ref, z_ref, i_ref, lb_ref, tri_ref, ones_ref, prev_ref, o_ref, st_scr, cum_scr, k_scr, v_scr = refs
    else:
        q_ref, z_ref, i_ref, lb_ref, tri_ref, ones_ref, o_ref, st_scr, cum_scr, k_scr, v_scr = refs
        prev_ref = None
    cs, sb = GLA_CHUNK, GLA_SUB
    nsub = cs // sb
    n_ck = tq // cs

    @pl.when(pl.program_id(2) == 0)
    def _():
        st_scr[...] = jnp.zeros_like(st_scr)

    lb = lb_ref[0]
    row_id = lax.broadcasted_iota(I32, (sb, GLA_DIM), 0)
    valid = [(row_id <= s) if rev else (row_id >= s) for s in range(sb)]

    def chunk(ci, carry):
        c = (n_ck - 1 - ci) if rev else ci
        r0 = pl.multiple_of(c * cs, cs)
        z = z_ref[0, pl.ds(r0, cs), :]
        q = q_ref[0, pl.ds(r0, cs), :] * (GLA_DIM ** -0.5)
        iv = i_ref[0, pl.ds(r0, cs), :]
        f = lb + (1.0 - lb) * jax.nn.sigmoid(z)
        k = (1.0 - lb) * jax.nn.sigmoid(-z)
        v = iv * jax.nn.sigmoid(iv)
        cum = jnp.dot(tri_ref[...], jnp.log(f), preferred_element_type=F32, precision=lax.Precision.HIGHEST)
        cum_scr[...] = cum
        k_scr[...] = k
        v_scr[...] = v
        vb = v.astype(BF16)
        last = cum[0:1] if rev else cum[cs - 1:cs]
        st = st_scr[...]
        o_inter = _dot_nt((q * jnp.exp(cum)).astype(BF16), st.astype(BF16))
        outs = []
        for i in range(nsub):
            blk = slice(i * sb, (i + 1) * sb)
            if rev:
                off = slice((i + 1) * sb, cs)
                ref_row = cum_scr[(i + 1) * sb:(i + 1) * sb + 1, :] if i < nsub - 1 else None
            else:
                off = slice(0, i * sb)
                ref_row = cum_scr[i * sb - 1:i * sb, :] if i > 0 else None
            q_i, cum_i = q[blk], cum[blk]
            o_i = o_inter[blk]
            if ref_row is not None:
                qt = q_i * jnp.exp(cum_i - ref_row)
                kt = k[off] * jnp.exp(ref_row - cum[off])
                a_off = _dot_nt(qt.astype(BF16), kt.astype(BF16))
                o_i = o_i + _dot(a_off.astype(BF16), vb[off])
            prods = []
            for s in range(sb):
                r = i * sb + s
                e = jnp.exp(jnp.minimum(cum_i - cum_scr[r:r + 1, :], 0.0))
                prods.append(jnp.where(valid[s], q_i * e * k_scr[r:r + 1, :], 0.0))
            sums = _dot(jnp.concatenate(prods, axis=0).astype(BF16), ones_ref[...])
            for s in range(sb):
                r = i * sb + s
                o_i = o_i + sums[s * sb:(s + 1) * sb] * v_scr[r:r + 1, :]
            outs.append(o_i)
        out = jnp.concatenate(outs, axis=0)
        if has_prev:
            out = out + prev_ref[0, pl.ds(r0, cs), :]
        o_ref[0, pl.ds(r0, cs), :] = out
        kd = k * jnp.exp(last - cum)
        st_scr[...] = st * jnp.exp(last) + _dot_tn(vb, kd.astype(BF16))
        return carry

    lax.fori_loop(0, n_ck, chunk, 0, unroll=2)


def _gla_specs(ltot, tq, n_lat, rev):
    n_chunks = ltot // tq
    n_lat_chunks = n_lat // tq
    cidx = lambda s: _scan_chunk_index(s, n_lat_chunks, n_chunks, rev)
    part = lambda p: pl.BlockSpec((1, tq, GLA_DIM), lambda b, h, s: (b, cidx(s), p * GLA_HEADS + h))
    return n_chunks, cidx, part


def _hgrn_direction(bm, lb, prev, rev, n_lat, tq):
    bsz, ltot, _ = bm.shape
    n_chunks, cidx, part = _gla_specs(ltot, tq, n_lat, rev)
    has_prev = prev is not None
    tri = _tri_mask(GLA_CHUNK, rev).astype(F32)
    ones = jnp.ones((GLA_DIM, GLA_DIM), BF16)
    z_part = 2 if rev else 1
    in_specs = [part(0), part(z_part), part(3),
                pl.BlockSpec((1, 1, GLA_DIM), lambda b, h, s: (h, 0, 0)),
                pl.BlockSpec(tri.shape, lambda b, h, s: (0, 0)),
                pl.BlockSpec(ones.shape, lambda b, h, s: (0, 0))]
    args = [bm, bm, bm, lb.reshape(GLA_HEADS, 1, GLA_DIM), tri, ones]
    out_spec = pl.BlockSpec((1, tq, GLA_DIM), lambda b, h, s: (b, cidx(s), h))
    if has_prev:
        in_specs.append(out_spec)
        args.append(prev)
    kern = functools.partial(_hgrn_kernel, rev=rev, tq=tq, has_prev=has_prev)
    tile = pltpu.VMEM((GLA_CHUNK, GLA_DIM), F32)
    return pl.pallas_call(
        kern,
        out_shape=jax.ShapeDtypeStruct((bsz, ltot, GROUP_WIDTH), F32),
        grid=(bsz, GLA_HEADS, n_chunks),
        in_specs=in_specs,
        out_specs=out_spec,
        scratch_shapes=[pltpu.VMEM((GLA_DIM, GLA_DIM), F32), tile, tile, tile],
        compiler_params=_params("arbitrary", "arbitrary", "arbitrary"),
        name="hgrn_rev" if rev else "hgrn_fwd",
    )(*args)


def _ret_kernel(*refs, rev, tq, has_prev):
    if has_prev:
        q_ref, k_ref, v_ref, cos_ref, sin_ref, eq_ref, ek_ref, dm_ref, el_ref, prev_ref, o_ref, st_scr = refs
    else:
        q_ref, k_ref, v_ref, cos_ref, sin_ref, eq_ref, ek_ref, dm_ref, el_ref, o_ref, st_scr = refs
        prev_ref = None
    cs = min(tq, RET_CHUNK)
    n_ck = tq // cs

    @pl.when(pl.program_id(2) == 0)
    def _():
        st_scr[...] = jnp.zeros_like(st_scr)

    def chunk(ci, carry):
        c = (n_ck - 1 - ci) if rev else ci
        r0 = pl.multiple_of(c * cs, cs)
        cos = cos_ref[pl.ds(r0, cs), :]
        sin = sin_ref[pl.ds(r0, cs), :]
        q = q_ref[0, pl.ds(r0, cs), :]
        k = k_ref[0, pl.ds(r0, cs), :]
        v = v_ref[0, pl.ds(r0, cs), :]
        q = q * cos + pltpu.roll(q, GLA_DIM // 2, 1) * sin
        k = (k * cos + pltpu.roll(k, GLA_DIM // 2, 1) * sin) * (GLA_DIM ** -0.5)
        st = st_scr[...]
        vb = v.astype(BF16)
        scores = _dot_nt(q.astype(BF16), k.astype(BF16)) * dm_ref[0]
        out = _dot_nt((q * eq_ref[0]).astype(BF16), st.astype(BF16)) + _dot(scores.astype(BF16), vb)
        if has_prev:
            out = out + prev_ref[0, pl.ds(r0, cs), :]
        o_ref[0, pl.ds(r0, cs), :] = out
        st_scr[...] = st * el_ref[0] + _dot_tn(vb, (k * ek_ref[0]).astype(BF16))
        return carry

    lax.fori_loop(0, n_ck, chunk, 0)


def _ret_tables(rev, cs):
    gamma = jnp.log1p(-jnp.exp2(-(RET_DECAY_EXP[1 if rev else 0] + jnp.arange(GLA_HEADS, dtype=F32))))
    t = jnp.arange(cs, dtype=F32)
    steps = (cs - t) if rev else (t + 1.0)
    cum = gamma[:, None] * steps[None, :]
    last = gamma * cs
    eq = jnp.broadcast_to(jnp.exp(cum)[:, :, None], (GLA_HEADS, cs, GLA_DIM))
    ek = jnp.broadcast_to(jnp.exp(last[:, None] - cum)[:, :, None], (GLA_HEADS, cs, GLA_DIM))
    rel = cum[:, :, None] - cum[:, None, :]
    dm = jnp.where(_tri_mask(cs, rev)[None], jnp.exp(jnp.minimum(rel, 0.0)), 0.0)
    el = jnp.broadcast_to(jnp.exp(last)[:, None, None], (GLA_HEADS, 1, GLA_DIM))
    return eq, ek, dm, el


def _rotary_tables(n_lat, n_ctx):
    rows = n_lat // GRID_W
    row = jnp.repeat(jnp.arange(rows, dtype=F32), GRID_W)
    col = jnp.tile(jnp.arange(GRID_W, dtype=F32), rows)
    quarter = GLA_DIM // 4
    inv_freq = ROPE_BASE ** (-jnp.arange(quarter, dtype=F32) / quarter)
    ang = jnp.concatenate([row[:, None] * inv_freq, col[:, None] * inv_freq], axis=-1)
    cos, sin = jnp.cos(ang), jnp.sin(ang)
    cos_l = jnp.concatenate([cos, cos], axis=-1)
    sin_l = jnp.concatenate([-sin, sin], axis=-1)
    cos_t = jnp.concatenate([cos_l, jnp.ones((n_ctx, GLA_DIM), F32)], axis=0)
    sin_t = jnp.concatenate([sin_l, jnp.zeros((n_ctx, GLA_DIM), F32)], axis=0)
    return cos_t, sin_t


def _ret_direction(bm, cos_t, sin_t, prev, rev, n_lat, tq):
    bsz, ltot, _ = bm.shape
    n_chunks, cidx, part = _gla_specs(ltot, tq, n_lat, rev)
    has_prev = prev is not None
    eq, ek, dm, el = _ret_tables(rev, min(tq, RET_CHUNK))
    head3 = lambda a: pl.BlockSpec((1,) + a.shape[1:], lambda b, h, s: (h, 0, 0))
    pos = pl.BlockSpec((tq, GLA_DIM), lambda b, h, s: (cidx(s), 0))
    in_specs = [part(5), part(6), part(7), pos, pos, head3(eq), head3(ek), head3(dm), head3(el)]
    args = [bm, bm, bm, cos_t, sin_t, eq, ek, dm, el]
    out_spec = pl.BlockSpec((1, tq, GLA_DIM), lambda b, h, s: (b, cidx(s), h))
    if has_prev:
        in_specs.append(out_spec)
        args.append(prev)
    kern = functools.partial(_ret_kernel, rev=rev, tq=tq, has_prev=has_prev)
    return pl.pallas_call(
        kern,
        out_shape=jax.ShapeDtypeStruct((bsz, ltot, GROUP_WIDTH), F32),
        grid=(bsz, GLA_HEADS, n_chunks),
        in_specs=in_specs,
        out_specs=out_spec,
        scratch_shapes=[pltpu.VMEM((GLA_DIM, GLA_DIM), F32)],
        compiler_params=_params("arbitrary", "arbitrary", "arbitrary"),
        name="ret_rev" if rev else "ret_fwd",
    )(*args)


def _head_rms(o):
    parts = []
    for h in range(GLA_HEADS):
        oh = o[:, h * GLA_DIM:(h + 1) * GLA_DIM]
        parts.append(oh * lax.rsqrt(jnp.mean(oh * oh, axis=-1, keepdims=True) + EPS))
    return jnp.concatenate(parts, axis=-1)


def _mix_kernel(u_ref, lg_ref, ys_ref, hs_ref, hg_ref, rg_ref, oh_ref, or_ref, d_ref, gw_ref, gb_ref, o_ref):
    gw = GROUP_WIDTH
    y = jax.nn.gelu(d_ref[...] * u_ref[...] + ys_ref[...])
    a = y * jax.nn.sigmoid(_dot(y.astype(BF16), gw_ref[...]) + gb_ref[...])
    o_ref[0, :, 0:gw] = a.astype(BF16)
    o_ref[0, :, gw:2 * gw] = (hs_ref[...] * jax.nn.gelu(lg_ref[...])).astype(BF16)
    g = hg_ref[0]
    o_ref[0, :, 2 * gw:3 * gw] = (_head_rms(oh_ref[0]) * (g * jax.nn.sigmoid(g))).astype(BF16)
    g = rg_ref[0]
    o_ref[0, :, 3 * gw:4 * gw] = (_head_rms(or_ref[0]) * (g * jax.nn.sigmoid(g))).astype(BF16)


def _mix_epilogue(tm2, ys2, hs2, bm, o_h, o_r, s5_d, glu_w, glu_b, n_rows, tm):
    bsz, ltot, _ = bm.shape
    gw = GROUP_WIDTH
    tmaj = lambda p, k: pl.BlockSpec((tm, gw), lambda b, i: (i, b * k + p))
    bmaj = lambda p: pl.BlockSpec((1, tm, gw), lambda b, i: (b, i, p))
    full2 = lambda b, i: (0, 0)
    return pl.pallas_call(
        _mix_kernel,
        out_shape=jax.ShapeDtypeStruct((bsz, ltot, 4 * gw), BF16),
        grid=(bsz, n_rows // tm),
        in_specs=[tmaj(0, N_TM_PARTS), tmaj(2, N_TM_PARTS), tmaj(0, 1), tmaj(0, 1),
                  bmaj(4), bmaj(8), bmaj(0), bmaj(0),
                  pl.BlockSpec((1, gw), full2), pl.BlockSpec((gw, gw), full2), pl.BlockSpec((1, gw), full2)],
        out_specs=pl.BlockSpec((1, tm, 4 * gw), lambda b, i: (b, i, 0)),
        compiler_params=_params("arbitrary", "arbitrary"),
        name="mix_epilogue",
    )(tm2, tm2, ys2, hs2, bm, bm, o_h, o_r, s5_d, glu_w, glu_b)


def _gate_rows(modb_ref, modc_ref, idx, row0, tm, n_lat):
    row = row0 + lax.broadcasted_iota(I32, (tm, 1), 0)
    return jnp.where(row >= n_lat, modc_ref[0, idx:idx + 1, :], modb_ref[0, idx:idx + 1, :])


def _outproj_kernel(a_ref, w_ref, x_ref, modb_ref, modc_ref, o_ref, *, tm, n_lat):
    gate = _gate_rows(modb_ref, modc_ref, 2, pl.program_id(1) * tm, tm, n_lat)
    o_ref[0] = x_ref[0] + gate * _dot(a_ref[0], w_ref[...])


def _out_projection(mix, w_bf16, xs, modv, n_rows, n_lat, tm, tn):
    bsz, ltot, d = xs.shape
    k = mix.shape[-1]
    kern = functools.partial(_outproj_kernel, tm=tm, n_lat=n_lat)
    return pl.pallas_call(
        kern,
        out_shape=jax.ShapeDtypeStruct((bsz, ltot, d), F32),
        grid=(bsz, n_rows // tm, d // tn),
        in_specs=[pl.BlockSpec((1, tm, k), lambda b, i, j: (b, i, 0)),
                  pl.BlockSpec((k, tn), lambda b, i, j: (0, j)),
                  pl.BlockSpec((1, tm, tn), lambda b, i, j: (b, i, j)),
                  pl.BlockSpec((1, 6, tn), lambda b, i, j: (b, 0, j)),
                  pl.BlockSpec((1, 6, tn), lambda b, i, j: (bsz, 0, j))],
        out_specs=pl.BlockSpec((1, tm, tn), lambda b, i, j: (b, i, j)),
        compiler_params=_params("arbitrary", "arbitrary", "arbitrary"),
        name="out_projection",
    )(mix, w_bf16, xs, modv, modv)


def _first_max4(v):
    m1 = jnp.maximum(jnp.maximum(v[0], v[1]), jnp.maximum(v[2], v[3]))
    i1 = jnp.where(v[0] == m1, 0, jnp.where(v[1] == m1, 1, jnp.where(v[2] == m1, 2, 3)))
    rest = [jnp.where(i1 == j, -jnp.inf, v[j]) for j in range(4)]
    m2 = jnp.maximum(jnp.maximum(rest[0], rest[1]), jnp.maximum(rest[2], rest[3]))
    i2 = jnp.where(rest[0] == m2, 0, jnp.where(rest[1] == m2, 1, jnp.where(rest[2] == m2, 2, 3)))
    return m1, i1, m2, i2


def _router_kernel(x_ref, g_ref, modb_ref, modc_ref, rw_ref, rb_ref, tri_ref,
                   h_ref, ids_ref, gates_ref, ranks_ref, counts_ref, base_scr, *, tm, n_lat):
    first = jnp.logical_and(pl.program_id(0) == 0, pl.program_id(1) == 0)

    @pl.when(first)
    def _():
        base_scr[...] = jnp.zeros_like(base_scr)

    h = _norm_modulate(x_ref[0], g_ref[...], modb_ref, modc_ref, pl.program_id(1) * tm, n_lat, 3, 4)
    h_ref[0] = h.astype(BF16)
    logits = lax.dot_general(rw_ref[...], h, (((1,), (1,)), ((), ())), preferred_element_type=F32,
                             precision=lax.Precision.HIGHEST)
    scores = jax.nn.sigmoid(logits)
    biased = scores + rb_ref[...]
    tops = []
    for gidx in range(N_EXPERT_GROUPS):
        rows = [biased[gidx * EXPERTS_PER_GROUP + j:gidx * EXPERTS_PER_GROUP + j + 1, :]
                for j in range(EXPERTS_PER_GROUP)]
        tops.append(_first_max4(rows))
    gs = [t[0] + t[2] for t in tops]
    gmax = jnp.maximum(jnp.maximum(gs[0], gs[1]), jnp.maximum(gs[2], gs[3]))
    best = jnp.where(gs[0] == gmax, 0, jnp.where(gs[1] == gmax, 1, jnp.where(gs[2] == gmax, 2, 3)))
    e1 = jnp.zeros_like(best)
    e2 = jnp.zeros_like(best)
    for gidx in range(N_EXPERT_GROUPS):
        e1 = jnp.where(best == gidx, gidx * EXPERTS_PER_GROUP + tops[gidx][1], e1)
        e2 = jnp.where(best == gidx, gidx * EXPERTS_PER_GROUP + tops[gidx][3], e2)
    eid = lax.broadcasted_iota(I32, (N_EXPERTS, tm), 0)
    sel1 = eid == e1
    sel2 = eid == e2
    w1 = jnp.sum(jnp.where(sel1, scores, 0.0), axis=0, keepdims=True)
    w2 = jnp.sum(jnp.where(sel2, scores, 0.0), axis=0, keepdims=True)
    wsum = w1 + w2
    onehot = jnp.where(jnp.logical_or(sel1, sel2), 1.0, 0.0)
    pos = base_scr[...] + _dot(onehot.astype(BF16), tri_ref[...])
    r1 = jnp.sum(jnp.where(sel1, pos, 0.0), axis=0, keepdims=True)
    r2 = jnp.sum(jnp.where(sel2, pos, 0.0), axis=0, keepdims=True)
    ids_ref[0] = jnp.concatenate([e1, e2], axis=0)
    gates_ref[0] = jnp.concatenate([w1 / wsum, w2 / wsum], axis=0)
    ranks_ref[0] = jnp.concatenate([r1, r2], axis=0).astype(I32)
    base = base_scr[...] + jnp.sum(onehot, axis=1, keepdims=True)
    base_scr[...] = base
    counts_ref[...] = jnp.broadcast_to(base, counts_ref.shape)


def _router(xs, gain, modv, rw_t, rbias, n_rows, n_lat, tm):
    bsz, ltot, d = xs.shape
    tri = (lax.broadcasted_iota(I32, (tm, tm), 0) < lax.broadcasted_iota(I32, (tm, tm), 1)).astype(BF16)
    kern = functools.partial(_router_kernel, tm=tm, n_lat=n_lat)
    small = lambda dt: jax.ShapeDtypeStruct((bsz, 2, n_rows), dt)
    small_spec = pl.BlockSpec((1, 2, tm), lambda b, i: (b, 0, i))
    return pl.pallas_call(
        kern,
        out_shape=(jax.ShapeDtypeStruct((bsz, ltot, d), BF16), small(I32), small(F32), small(I32),
                   jax.ShapeDtypeStruct((N_EXPERTS, 128), F32)),
        grid=(bsz, n_rows // tm),
        in_specs=[pl.BlockSpec((1, tm, d), lambda b, i: (b, i, 0)),
                  pl.BlockSpec((1, d), lambda b, i: (0, 0)),
                  pl.BlockSpec((1, 6, d), lambda b, i: (b, 0, 0)),
                  pl.BlockSpec((1, 6, d), lambda b, i: (bsz, 0, 0)),
                  pl.BlockSpec((N_EXPERTS, d), lambda b, i: (0, 0)),
                  pl.BlockSpec((N_EXPERTS, 1), lambda b, i: (0, 0)),
                  pl.BlockSpec((tm, tm), lambda b, i: (0, 0))],
        out_specs=(pl.BlockSpec((1, tm, d), lambda b, i: (b, i, 0)), small_spec, small_spec, small_spec,
                   pl.BlockSpec((N_EXPERTS, 128), lambda b, i: (0, 0))),
        scratch_shapes=[pltpu.VMEM((N_EXPERTS, 1), F32)],
        compiler_params=_params("arbitrary", "arbitrary"),
        name="moe_router",
    )(xs, gain, modv, modv, rw_t, rbias, tri)


def _expert_ffn_kernel(te_ref, nt_ref, x_ref, wg_ref, wu_ref, wd_ref, o_ref):
    @pl.when(pl.program_id(0) < nt_ref[0])
    def _():
        x = x_ref[...]
        g = _dot(x, wg_ref[0])
        u = _dot(x, wu_ref[0])
        he = (g * jax.nn.sigmoid(g)) * u
        o_ref[...] = _dot(he.astype(BF16), wd_ref[0])


def _expert_ffn(tile_expert, n_tiles, x_sorted, wg, wu, wd, tm):
    p, d = x_sorted.shape
    f = wg.shape[-1]
    tile = lambda i, te, nt: jnp.minimum(i, nt[0] - 1)
    expert = lambda i, te, nt: te[jnp.minimum(i, nt[0] - 1)]
    grid_spec = pltpu.PrefetchScalarGridSpec(
        num_scalar_prefetch=2,
        grid=(p // tm,),
        in_specs=[pl.BlockSpec((tm, d), lambda i, te, nt: (tile(i, te, nt), 0)),
                  pl.BlockSpec((1, d, f), lambda i, te, nt: (expert(i, te, nt), 0, 0)),
                  pl.BlockSpec((1, d, f), lambda i, te, nt: (expert(i, te, nt), 0, 0)),
                  pl.BlockSpec((1, f, d), lambda i, te, nt: (expert(i, te, nt), 0, 0))],
        out_specs=pl.BlockSpec((tm, d), lambda i, te, nt: (tile(i, te, nt), 0)),
    )
    return pl.pallas_call(
        _expert_ffn_kernel,
        out_shape=jax.ShapeDtypeStruct((p, d), F32),
        grid_spec=grid_spec,
        compiler_params=_params("arbitrary"),
        name="moe_expert_ffn",
    )(tile_expert, n_tiles, x_sorted, wg, wu, wd)


def _moe_residual_kernel(x_ref, y_ref, modb_ref, modc_ref, g_ref, o_ref, *, tm, n_lat, final_norm):
    gate = _gate_rows(modb_ref, modc_ref, 5, pl.program_id(1) * tm, tm, n_lat)
    x = x_ref[0] + gate * y_ref[0]
    if final_norm:
        x = x * lax.rsqrt(jnp.mean(x * x, axis=-1, keepdims=True) + EPS) * g_ref[...]
    o_ref[0] = x


def _moe_residual(xs, y, modv, final_g, n_rows, n_lat, tm, final_norm):
    bsz, ltot, d = xs.shape
    kern = functools.partial(_moe_residual_kernel, tm=tm, n_lat=n_lat, final_norm=final_norm)
    rows_out = n_rows if final_norm else ltot
    return pl.pallas_call(
        kern,
        out_shape=jax.ShapeDtypeStruct((bsz, rows_out, d), F32),
        grid=(bsz, n_rows // tm),
        in_specs=[pl.BlockSpec((1, tm, d), lambda b, i: (b, i, 0)),
                  pl.BlockSpec((1, tm, d), lambda b, i: (b, i, 0)),
                  pl.BlockSpec((1, 6, d), lambda b, i: (b, 0, 0)),
                  pl.BlockSpec((1, 6, d), lambda b, i: (bsz, 0, 0)),
                  pl.BlockSpec((1, d), lambda b, i: (0, 0))],
        out_specs=pl.BlockSpec((1, tm, d), lambda b, i: (b, i, 0)),
        compiler_params=_params("arbitrary", "arbitrary"),
        name="moe_residual",
    )(xs, y, modv, modv, final_g)


def _moe_layer(xs, gain, modv, rw_t, rbias, wg, wu, wd, final_g, n_rows, n_lat, tm_tok, tm_res, tm_exp, final_norm):
    bsz, ltot, d = xs.shape
    h, ids, gates, ranks, counts = _router(xs, gain, modv, rw_t, rbias, n_rows, n_lat, tm_tok)
    counts = counts[:, 0].astype(I32)
    padded = ((counts + tm_exp - 1) // tm_exp) * tm_exp
    ends = jnp.cumsum(padded)
    offsets = ends - padded
    dest = offsets[ids] + ranks
    n_slots = bsz * 2 * n_rows
    p_rows = ((n_slots + N_EXPERTS * (tm_exp - 1)) + tm_exp - 1) // tm_exp * tm_exp
    tok = (jnp.arange(bsz, dtype=I32)[:, None, None] * ltot + jnp.arange(n_rows, dtype=I32)[None, None, :])
    tok = jnp.broadcast_to(tok, dest.shape)
    src = jnp.zeros((p_rows,), I32).at[dest.reshape(-1)].set(tok.reshape(-1))
    tile_start = jnp.arange(p_rows // tm_exp, dtype=I32) * tm_exp
    tile_expert = jnp.minimum(jnp.sum(tile_start[:, None] >= ends[None, :], axis=1), N_EXPERTS - 1).astype(I32)
    n_tiles = (ends[-1:] // tm_exp).astype(I32)
    x_sorted = jnp.take(h.reshape(bsz * ltot, d), src, axis=0)
    y_sorted = _expert_ffn(tile_expert, n_tiles, x_sorted, wg, wu, wd, tm_exp)
    y = (jnp.take(y_sorted, dest[:, 0].reshape(-1), axis=0) * gates[:, 0].reshape(-1, 1)
         + jnp.take(y_sorted, dest[:, 1].reshape(-1), axis=0) * gates[:, 1].reshape(-1, 1))
    y = y.reshape(bsz, n_rows, d)
    return _moe_residual(xs, y, modv, final_g, n_rows, n_lat, tm_res, final_norm)


def _row_tile(n, target):
    t = min(n, target)
    while n % t or t % 8:
        t -= 8
    return t


def kernel(x, c, ctx, c_ctx, norm_mix_g, norm_ffn_g, w_mod, b_mod, w_in, w_out, s5_lam_re, s5_lam_im, s5_log_dt, s5_b_re, s5_b_im, s5_c_re, s5_c_im, s5_d, s5_glu_w, s5_glu_b, lru_conv_w, lru_conv_b, lru_wa, lru_ba, lru_wx, lru_bx, lru_lam, hgrn_lb_logits, router_w, router_bias, moe_w_gate, moe_w_up, moe_w_down, final_norm_g):
    bsz, n_lat, d = x.shape
    n_ctx = ctx.shape[1]
    ltot = n_lat + n_ctx
    depth = w_in.shape[0]
    gw = GROUP_WIDTH
    assert bsz % 8 == 0 and bsz + 1 <= MOD_ROWS
    assert n_lat % GLA_CHUNK == 0 and n_ctx % GLA_CHUNK == 0 and n_lat % GRID_W == 0

    tq_scan = math.gcd(math.gcd(n_lat, n_ctx), 64)
    tq_gla = math.gcd(math.gcd(n_lat, n_ctx), 256)
    tm_all = _row_tile(ltot, 768)
    tm_lat = _row_tile(n_lat, 1024)
    tm_mix = math.gcd(math.gcd(n_lat, n_ctx), 256)

    xs = jnp.concatenate([x, ctx.astype(x.dtype)], axis=1)
    cv = jnp.zeros((MOD_ROWS, d), F32).at[:bsz].set(c).at[bsz].set(c_ctx)
    mod_all = _mod_vectors(cv, w_mod, b_mod).reshape(depth, MOD_ROWS, 6, d)
    cos_t, sin_t = _rotary_tables(n_lat, n_ctx)
    rw_t = router_w.astype(F32).T
    rbias = router_bias.astype(F32).reshape(N_EXPERTS, 1)
    lb_cum = [jnp.cumsum(jax.nn.softmax(hgrn_lb_logits[dr].astype(F32), axis=0), axis=0) for dr in (0, 1)]

    for layer in range(depth):
        ctx_out = layer < depth - 1
        n_rows = ltot if ctx_out else n_lat
        tm_tok = tm_all if ctx_out else tm_lat
        modv = mod_all[layer]

        tm2, bm = _in_projection(xs, norm_mix_g[layer].reshape(1, d), modv, w_in[layer].astype(BF16), n_lat, tm_all)
        tm3 = tm2.reshape(ltot, bsz, N_TM_PARTS * gw)

        ys = hs = o_h = o_r = None
        for dr in (0, 1):
            rev = dr == 1
            bw, cw, atab = _s5_tables(s5_lam_re[layer, dr], s5_lam_im[layer, dr], s5_log_dt[layer, dr],
                                      s5_b_re[layer, dr], s5_b_im[layer, dr], s5_c_re[layer, dr], s5_c_im[layer, dr])
            ys = _s5_direction(tm3, bw, cw, atab, ys, rev, n_lat, tq_scan)
            sp = jax.nn.softplus(-lru_lam[layer, dr].astype(F32)).reshape(1, gw)
            hs = _lru_direction(tm3, lru_conv_w[layer].astype(F32), lru_conv_b[layer].astype(F32).reshape(1, gw),
                                _block_diag(lru_wa[layer, dr]).astype(BF16), _block_diag(lru_wx[layer, dr]).astype(BF16),
                                lru_ba[layer, dr].astype(F32).reshape(1, gw), lru_bx[layer, dr].astype(F32).reshape(1, gw),
                                sp, hs, rev, n_lat, tq_scan)
            lb = lb_cum[dr][layer] - lb_cum[dr][0]
            o_h = _hgrn_direction(bm, lb, o_h, rev, n_lat, tq_gla)
            o_r = _ret_direction(bm, cos_t, sin_t, o_r, rev, n_lat, tq_gla)

        mix = _mix_epilogue(tm2, ys.reshape(ltot, bsz * gw), hs.reshape(ltot, bsz * gw), bm, o_h, o_r,
                            s5_d[layer].astype(F32).reshape(1, gw), s5_glu_w[layer].astype(BF16),
                            s5_glu_b[layer].astype(F32).reshape(1, gw), n_rows, tm_mix)
        xs = _out_projection(mix, w_out[layer].astype(BF16), xs, modv, n_rows, n_lat, tm_tok, 512)
        xs = _moe_layer(xs, norm_ffn_g[layer].reshape(1, d), modv, rw_t, rbias,
                        moe_w_gate[layer].astype(BF16), moe_w_up[layer].astype(BF16), moe_w_down[layer].astype(BF16),
                        final_norm_g.reshape(1, d), n_rows, n_lat, tm_tok, tm_mix, 512, not ctx_out)
    return xs
```

```python
import functools
import math

import jax
import jax.numpy as jnp
from jax import lax
from jax.experimental import pallas as pl
from jax.experimental.pallas import tpu as pltpu

F32 = jnp.float32
BF16 = jnp.bfloat16
I32 = jnp.int32

GROUP_WIDTH = 512
N_IN_PARTS = 12
N_TM_PARTS = 3
N_BM_PARTS = N_IN_PARTS - N_TM_PARTS
S5_CH = 16
S5_GROUPS = 32
S5_STATE = 64
S5_LANE_BLOCKS = 4
LRU_HEADS = 8
LRU_CONV = 4
LRU_C = 8.0
GLA_HEADS = 4
GLA_DIM = 128
GLA_CHUNK = 64
GLA_SUB = 16
HGRN_FAST_WINDOW = 32
HGRN_SAFE_LOG_DECAY = 60.0
HGRN_HEADS_PER_STEP = 2
RET_CHUNK = 256
RET_DECAY_EXP = (5.0, 5.5)
ROPE_BASE = 10000.0
GRID_W = 64
N_EXPERTS = 16
N_EXPERT_GROUPS = 4
EXPERTS_PER_GROUP = 4
D_FF_EXPERT = 1024
EPS = 1e-6
NORM_SLAB = 16
MOD_ROWS = 24

VMEM_LIMIT_BYTES = 56 * 1024 * 1024


def _params(*semantics):
    return pltpu.CompilerParams(dimension_semantics=semantics, vmem_limit_bytes=VMEM_LIMIT_BYTES)


def _dot(a, b):
    return jnp.dot(a, b, preferred_element_type=F32)


def _dot_nt(a, b):
    return lax.dot_general(a, b, (((1,), (1,)), ((), ())), preferred_element_type=F32)


def _dot_tn(a, b):
    return lax.dot_general(a, b, (((0,), (0,)), ((), ())), preferred_element_type=F32)


def _scan_chunk_index(step, n_lat_chunks, n_chunks, rev):
    if rev:
        return n_chunks - 1 - step
    return (step + n_lat_chunks) % n_chunks


def _mod_kernel(cv_ref, w_ref, b_ref, o_ref):
    cv = cv_ref[...]
    s = cv * jax.nn.sigmoid(cv)
    o_ref[0] = _dot(s.astype(BF16), w_ref[0].astype(BF16)) + b_ref[0]


def _mod_vectors(cv, w_mod, b_mod):
    nl, d, n6 = w_mod.shape
    tn = 1024
    return pl.pallas_call(
        _mod_kernel,
        out_shape=jax.ShapeDtypeStruct((nl, MOD_ROWS, n6), F32),
        grid=(nl, n6 // tn),
        in_specs=[pl.BlockSpec((MOD_ROWS, d), lambda l, j: (0, 0)),
                  pl.BlockSpec((1, d, tn), lambda l, j: (l, 0, j)),
                  pl.BlockSpec((1, 1, tn), lambda l, j: (l, 0, j))],
        out_specs=pl.BlockSpec((1, MOD_ROWS, tn), lambda l, j: (l, 0, j)),
        compiler_params=_params("arbitrary", "arbitrary"),
        name="mod_vectors",
    )(cv, w_mod, b_mod.reshape(nl, 1, n6))


def _norm_modulate(x, g, modb_ref, modc_ref, row0, n_lat, shift_idx, scale_idx):
    tm = x.shape[0]
    ms = jnp.mean(x * x, axis=-1, keepdims=True)
    y = x * lax.rsqrt(ms + EPS) * g
    row = row0 + lax.broadcasted_iota(I32, (tm, 1), 0)
    is_ctx = row >= n_lat
    shift = jnp.where(is_ctx, modc_ref[0, shift_idx:shift_idx + 1, :], modb_ref[0, shift_idx:shift_idx + 1, :])
    scale = jnp.where(is_ctx, modc_ref[0, scale_idx:scale_idx + 1, :], modb_ref[0, scale_idx:scale_idx + 1, :])
    return y * (1.0 + scale) + shift


def _inproj_kernel(x_ref, g_ref, modb_ref, modc_ref, w_ref, otm_ref, obm_ref, h_scr, *, tm, n_lat):
    i = pl.program_id(1)
    j = pl.program_id(2)

    @pl.when(j == 0)
    def _():
        def slab(s, carry):
            r0 = pl.multiple_of(s * NORM_SLAB, NORM_SLAB)
            h = _norm_modulate(x_ref[0, pl.ds(r0, NORM_SLAB), :], g_ref[...], modb_ref, modc_ref,
                               i * tm + r0, n_lat, 0, 1)
            h_scr[pl.ds(r0, NORM_SLAB), :] = h.astype(BF16)
            return carry

        lax.fori_loop(0, tm // NORM_SLAB, slab, 0, unroll=2)

    r = _dot(h_scr[...], w_ref[...])

    @pl.when(j == 0)
    def _():
        otm_ref[...] = r

    @pl.when(j > 0)
    def _():
        obm_ref[0] = r


def _in_projection(xs, gain, modv, w_bf16, n_lat, tm):
    bsz, ltot, d = xs.shape
    gw = GROUP_WIDTH
    kern = functools.partial(_inproj_kernel, tm=tm, n_lat=n_lat)
    tn = N_TM_PARTS * gw
    return pl.pallas_call(
        kern,
        out_shape=(jax.ShapeDtypeStruct((ltot, bsz * tn), F32),
                   jax.ShapeDtypeStruct((bsz, ltot, N_BM_PARTS * gw), F32)),
        grid=(bsz, ltot // tm, N_IN_PARTS * gw // tn),
        in_specs=[pl.BlockSpec((1, tm, d), lambda b, i, j: (b, i, 0)),
                  pl.BlockSpec((1, d), lambda b, i, j: (0, 0)),
                  pl.BlockSpec((1, 6, d), lambda b, i, j: (b, 0, 0)),
                  pl.BlockSpec((1, 6, d), lambda b, i, j: (bsz, 0, 0)),
                  pl.BlockSpec((d, tn), lambda b, i, j: (0, j))],
        out_specs=(pl.BlockSpec((tm, tn), lambda b, i, j: (i, b)),
                   pl.BlockSpec((1, tm, tn), lambda b, i, j: (b, i, jnp.maximum(j - 1, 0)))),
        scratch_shapes=[pltpu.VMEM((tm, d), BF16)],
        compiler_params=_params("arbitrary", "arbitrary", "arbitrary"),
        name="in_projection",
    )(xs, gain, modv, modv, w_bf16)


def _s5_kernel(*refs, rev, tq, nb, has_prev):
    if has_prev:
        u_ref, bw_ref, cw_ref, a_ref, prev_ref, y_ref, z_scr, h_scr = refs
    else:
        u_ref, bw_ref, cw_ref, a_ref, y_ref, z_scr, h_scr = refs
        prev_ref = None
    rows = tq * nb
    half = S5_STATE * 8

    @pl.when(pl.program_id(0) == 0)
    def _():
        h_scr[...] = jnp.zeros_like(h_scr)

    for gb in range(S5_LANE_BLOCKS):
        lanes = slice(gb * 128, (gb + 1) * 128)
        ub = u_ref[:, :, lanes].reshape(rows, 128).astype(BF16)
        z_scr[...] = _dot(ub, bw_ref[gb])
        ar = jnp.broadcast_to(a_ref[gb, 0], (nb, half))
        ai = jnp.broadcast_to(a_ref[gb, 1], (nb, half))

        def step(i, carry):
            hr, hi = carry
            t = (tq - 1 - i) if rev else i
            r0 = pl.multiple_of(t * nb, nb)
            zr = z_scr[pl.ds(r0, nb), 0:half]
            zi = z_scr[pl.ds(r0, nb), half:2 * half]
            nr = ar * hr - ai * hi + zr
            ni = ar * hi + ai * hr + zi
            z_scr[pl.ds(r0, nb), 0:half] = nr
            z_scr[pl.ds(r0, nb), half:2 * half] = ni
            return nr, ni

        hr, hi = lax.fori_loop(0, tq, step, (h_scr[gb, 0], h_scr[gb, 1]), unroll=4)
        h_scr[gb, 0] = hr
        h_scr[gb, 1] = hi
        yb = _dot(z_scr[...].astype(BF16), cw_ref[gb]).reshape(tq, nb, 128)
        if has_prev:
            yb = yb + prev_ref[:, :, lanes]
        y_ref[:, :, lanes] = yb


def _s5_direction(tm3, bw, cw, atab, prev, rev, n_lat, tq):
    ltot, nb, _ = tm3.shape
    gw = GROUP_WIDTH
    n_chunks = ltot // tq
    n_lat_chunks = n_lat // tq
    cidx = lambda s: _scan_chunk_index(s, n_lat_chunks, n_chunks, rev)
    has_prev = prev is not None
    in_specs = [pl.BlockSpec((tq, nb, gw), lambda s: (cidx(s), 0, 0)),
                pl.BlockSpec(bw.shape, lambda s: (0, 0, 0)),
                pl.BlockSpec(cw.shape, lambda s: (0, 0, 0)),
                pl.BlockSpec(atab.shape, lambda s: (0, 0, 0, 0))]
    args = [tm3, bw, cw, atab]
    if has_prev:
        in_specs.append(pl.BlockSpec((tq, nb, gw), lambda s: (cidx(s), 0, 0)))
        args.append(prev)
    kern = functools.partial(_s5_kernel, rev=rev, tq=tq, nb=nb, has_prev=has_prev)
    return pl.pallas_call(
        kern,
        out_shape=jax.ShapeDtypeStruct((ltot, nb, gw), F32),
        grid=(n_chunks,),
        in_specs=in_specs,
        out_specs=pl.BlockSpec((tq, nb, gw), lambda s: (cidx(s), 0, 0)),
        scratch_shapes=[pltpu.VMEM((tq * nb, 2 * S5_STATE * 8), F32),
                        pltpu.VMEM((S5_LANE_BLOCKS, 2, nb, S5_STATE * 8), F32)],
        compiler_params=_params("arbitrary"),
        name="s5_rev" if rev else "s5_fwd",
    )(*args)


def _block_diag(blocks):
    n, r, c = blocks.shape
    eye = jnp.eye(n, dtype=blocks.dtype)
    return (eye[:, None, :, None] * blocks[:, :, None, :]).reshape(n * r, n * c)


def _s5_tables(lam_re, lam_im, log_dt, b_re, b_im, c_re, c_im):
    dt = jnp.exp(log_dt.astype(F32))[:, None]
    lr, li = lam_re.astype(F32), lam_im.astype(F32)
    mag = jnp.exp(lr * dt)
    abar_re, abar_im = mag * jnp.cos(li * dt), mag * jnp.sin(li * dt)
    den = lr * lr + li * li
    zr = abar_re - 1.0
    w_re = (zr * lr + abar_im * li) / den
    w_im = (abar_im * lr - zr * li) / den
    bre, bim = b_re.astype(F32), b_im.astype(F32)
    bw_re = w_re[:, None, :] * bre - w_im[:, None, :] * bim
    bw_im = w_re[:, None, :] * bim + w_im[:, None, :] * bre
    gpb = S5_GROUPS // S5_LANE_BLOCKS
    bws, cws, atabs = [], [], []
    for gb in range(S5_LANE_BLOCKS):
        sl = slice(gb * gpb, (gb + 1) * gpb)
        bws.append(jnp.concatenate([_block_diag(bw_re[sl]), _block_diag(bw_im[sl])], axis=1))
        cws.append(jnp.concatenate([_block_diag(c_re[sl].astype(F32)), -_block_diag(c_im[sl].astype(F32))], axis=0))
        atabs.append(jnp.stack([abar_re[sl].reshape(1, -1), abar_im[sl].reshape(1, -1)]))
    return jnp.stack(bws).astype(BF16), jnp.stack(cws).astype(BF16), jnp.stack(atabs)


def _lru_kernel(*refs, rev, tq, nb, n_lat_chunks, n_chunks, has_prev):
    if has_prev:
        x_ref, xp_ref, xn_ref, cw_ref, cb_ref, wa_ref, wx_ref, ba_ref, bx_ref, sp_ref, prev_ref, o_ref, a_scr, b_scr, h_scr = refs
    else:
        x_ref, xp_ref, xn_ref, cw_ref, cb_ref, wa_ref, wx_ref, ba_ref, bx_ref, sp_ref, o_ref, a_scr, b_scr, h_scr = refs
        prev_ref = None
    s = pl.program_id(0)
    gw = GROUP_WIDTH
    rows = tq * nb

    @pl.when(s == 0)
    def _():
        h_scr[...] = jnp.zeros_like(h_scr)

    c = _scan_chunk_index(s, n_lat_chunks, n_chunks, rev)
    first = jnp.logical_or(c == 0, c == n_lat_chunks)
    last = jnp.logical_or(c == n_lat_chunks - 1, c == n_chunks - 1)
    keep_prev = jnp.where(first, 0.0, 1.0)
    keep_next = jnp.where(last, 0.0, 1.0)
    xc = jnp.concatenate([xp_ref[...] * keep_prev, x_ref[...], xn_ref[...] * keep_next], axis=0)
    conv = cb_ref[...] + xc[0:tq] * cw_ref[0:1, :]
    for k in range(1, LRU_CONV):
        conv = conv + xc[k:k + tq] * cw_ref[k:k + 1, :]
    x = conv.reshape(rows, gw)
    xb = x.astype(BF16)
    r = jax.nn.sigmoid(_dot(xb, wa_ref[...]) + ba_ref[...])
    ig = jax.nn.sigmoid(_dot(xb, wx_ref[...]) + bx_ref[...])
    log_a = -LRU_C * r * sp_ref[...]
    a = jnp.exp(log_a)
    a_scr[...] = a
    b_scr[...] = jnp.sqrt(1.0 - a * a) * (ig * x)

    def step(i, h):
        t = (tq - 1 - i) if rev else i
        r0 = pl.multiple_of(t * nb, nb)
        hn = a_scr[pl.ds(r0, nb), :] * h + b_scr[pl.ds(r0, nb), :]
        b_scr[pl.ds(r0, nb), :] = hn
        return hn

    h_scr[...] = lax.fori_loop(0, tq, step, h_scr[...], unroll=4)
    out = b_scr[...].reshape(tq, nb, gw)
    if has_prev:
        out = out + prev_ref[...]
    o_ref[...] = out


def _lru_direction(tm3, conv_w, conv_b, wa_bd, wx_bd, ba, bx, sp, prev, rev, n_lat, tq):
    ltot, nb, _ = tm3.shape
    gw = GROUP_WIDTH
    n_chunks = ltot // tq
    n_lat_chunks = n_lat // tq
    cidx = lambda s: _scan_chunk_index(s, n_lat_chunks, n_chunks, rev)
    has_prev = prev is not None
    full2 = lambda s: (0, 0)
    in_specs = [pl.BlockSpec((tq, nb, gw), lambda s: (cidx(s), 0, 1)),
                pl.BlockSpec((2, nb, gw), lambda s: (jnp.maximum(cidx(s) * (tq // 2) - 1, 0), 0, 1)),
                pl.BlockSpec((1, nb, gw), lambda s: (jnp.minimum((cidx(s) + 1) * tq, ltot - 1), 0, 1)),
                pl.BlockSpec((LRU_CONV, gw), full2),
                pl.BlockSpec((1, gw), full2),
                pl.BlockSpec((gw, gw), full2),
                pl.BlockSpec((gw, gw), full2),
                pl.BlockSpec((1, gw), full2),
                pl.BlockSpec((1, gw), full2),
                pl.BlockSpec((1, gw), full2)]
    args = [tm3, tm3, tm3, conv_w, conv_b, wa_bd, wx_bd, ba, bx, sp]
    if has_prev:
        in_specs.append(pl.BlockSpec((tq, nb, gw), lambda s: (cidx(s), 0, 0)))
        args.append(prev)
    kern = functools.partial(_lru_kernel, rev=rev, tq=tq, nb=nb, n_lat_chunks=n_lat_chunks,
                             n_chunks=n_chunks, has_prev=has_prev)
    return pl.pallas_call(
        kern,
        out_shape=jax.ShapeDtypeStruct((ltot, nb, gw), F32),
        grid=(n_chunks,),
        in_specs=in_specs,
        out_specs=pl.BlockSpec((tq, nb, gw), lambda s: (cidx(s), 0, 0)),
        scratch_shapes=[pltpu.VMEM((tq * nb, gw), F32),
                        pltpu.VMEM((tq * nb, gw), F32),
                        pltpu.VMEM((nb, gw), F32)],
        compiler_params=_params("arbitrary"),
        name="lru_rev" if rev else "lru_fwd",
    )(*args)


def _tri_mask(n, rev):
    t = lax.broadcasted_iota(I32, (n, n), 0)
    s = lax.broadcasted_iota(I32, (n, n), 1)
    return (s >= t) if rev else (s <= t)


def _hgrn_kernel(*refs, rev, tq, has_prev):
    if has_prev:
        q_ref, z_ref, i_ref, lb_ref, tri_ref, ones_ref, prev_ref, o_ref, st_scr, cum_scr, q_scr, k_scr, v_scr = refs
    else:
        q_ref, z_ref, i_ref, lb_ref, tri_ref, ones_ref, o_ref, st_scr, cum_scr, q_scr, k_scr, v_scr = refs
        prev_ref = None
    hps = HGRN_HEADS_PER_STEP
    cs, sb, w = GLA_CHUNK, GLA_SUB, min(HGRN_FAST_WINDOW, tq)
    nsub = cs // sb
    n_ck = tq // cs
    n_w = tq // w

    @pl.when(pl.program_id(2) == 0)
    def _():
        st_scr[...] = jnp.zeros_like(st_scr)

    lb = lb_ref[0]
    z = z_ref[0]
    iv = i_ref[0]
    f = lb + (1.0 - lb) * jax.nn.sigmoid(z)
    k_all = (1.0 - lb) * jax.nn.sigmoid(-z)
    v_all = iv * jax.nn.sigmoid(iv)
    q_all = q_ref[0] * (GLA_DIM ** -0.5)
    cum_all = jnp.dot(tri_ref[...], jnp.log(f), preferred_element_type=F32, precision=lax.Precision.HIGHEST)
    for hh in range(hps):
        lanes = slice(hh * GLA_DIM, (hh + 1) * GLA_DIM)
        cum_scr[hh] = cum_all[:, lanes]
        q_scr[hh] = q_all[:, lanes]
        k_scr[hh] = k_all[:, lanes]
        v_scr[hh] = v_all[:, lanes]

    def far_row(c, width):
        return c * width if rev else (c + 1) * width - 1

    def outside_row(c, width, n):
        if rev:
            return (c + 1) * width if c < n - 1 else None
        return c * width - 1 if c > 0 else None

    def store_out(hh, rows, out):
        lanes = slice(hh * GLA_DIM, (hh + 1) * GLA_DIM)
        if has_prev:
            out = out + prev_ref[0, rows, lanes]
        o_ref[0, rows, lanes] = out

    def block_factored(_):
        lane_id = lax.broadcasted_iota(I32, (1, tq), 1)
        for hh in range(hps):
            cumg, q, k = cum_scr[hh], q_scr[hh], k_scr[hh]
            vb = v_scr[hh].astype(BF16)
            st = st_scr[hh]
            ends = [cum_scr[hh, far_row(c, w):far_row(c, w) + 1, :] for c in range(n_w)]
            end_full = jnp.concatenate([jnp.broadcast_to(e, (w, GLA_DIM)) for e in ends], axis=0)
            ks = (k * jnp.exp(end_full - cumg)).astype(BF16)
            pieces, starts, row_lo = [], [], []
            total = 0
            for c in range(n_w):
                rows = slice(0, (c + 1) * w) if rev else slice(c * w, tq)
                scale = jnp.exp(jnp.minimum(cumg[rows] - ends[c], HGRN_SAFE_LOG_DECAY))
                pieces.append((q[rows] * scale).astype(BF16))
                starts.append(total)
                row_lo.append(rows.start)
                total += rows.stop - rows.start
            a_stack = _dot_nt(jnp.concatenate(pieces, axis=0), ks)
            a_rows = []
            for r in range(n_w):
                acc = jnp.zeros((w, tq), F32)
                for c in (range(r, n_w) if rev else range(0, r + 1)):
                    lo = starts[c] + r * w - row_lo[c]
                    keep = jnp.logical_and(lane_id >= c * w, lane_id < (c + 1) * w)
                    if c == r:
                        t_id = r * w + lax.broadcasted_iota(I32, (w, tq), 0)
                        keep = jnp.logical_and(keep, (lane_id >= t_id) if rev else (lane_id <= t_id))
                    acc = jnp.where(keep, a_stack[lo:lo + w, :], acc)
                a_rows.append(acc)
            a = jnp.concatenate(a_rows, axis=0).astype(BF16)
            out = _dot(a, vb) + _dot_nt((q * jnp.exp(cumg)).astype(BF16), st.astype(BF16))
            store_out(hh, slice(None), out)
            last = ends[0] if rev else ends[n_w - 1]
            kd = k * jnp.exp(last - cumg)
            st_scr[hh] = st * jnp.exp(last) + _dot_tn(vb, kd.astype(BF16))
        return 0

    row_id = lax.broadcasted_iota(I32, (sb, GLA_DIM), 0)
    valid = [(row_id <= s) if rev else (row_id >= s) for s in range(sb)]

    def chunk_direct(hh, r0):
        rows = pl.ds(r0, cs)
        q, k, cumg = q_scr[hh, rows, :], k_scr[hh, rows, :], cum_scr[hh, rows, :]
        vb = v_scr[hh, rows, :].astype(BF16)
        if rev:
            base = cum_scr[hh, pl.ds(jnp.minimum(r0 + cs, tq - 1), 1), :]
            base = jnp.where(r0 + cs >= tq, 0.0, base)
        else:
            base = cum_scr[hh, pl.ds(jnp.maximum(r0 - 1, 0), 1), :]
            base = jnp.where(r0 == 0, 0.0, base)
        cum = cumg - base
        last = cum[0:1] if rev else cum[cs - 1:cs]
        st = st_scr[hh]
        o_inter = _dot_nt((q * jnp.exp(cum)).astype(BF16), st.astype(BF16))
        outs = []
        for i in range(nsub):
            blk = slice(i * sb, (i + 1) * sb)
            off = slice((i + 1) * sb, cs) if rev else slice(0, i * sb)
            o_row = outside_row(i, sb, nsub)
            q_i, cum_i = q[blk], cum[blk]
            o_i = o_inter[blk]
            if o_row is not None:
                ref_row = cum[o_row:o_row + 1]
                qt = q_i * jnp.exp(cum_i - ref_row)
                kt = k[off] * jnp.exp(ref_row - cum[off])
                a_off = _dot_nt(qt.astype(BF16), kt.astype(BF16))
                o_i = o_i + _dot(a_off.astype(BF16), vb[off])
            prods = []
            for s in range(sb):
                r = r0 + i * sb + s
                e = jnp.exp(jnp.minimum(cumg[blk] - cum_scr[hh, pl.ds(r, 1), :], 0.0))
                prods.append(jnp.where(valid[s], q_i * e * k_scr[hh, pl.ds(r, 1), :], 0.0))
            sums = _dot(jnp.concatenate(prods, axis=0).astype(BF16), ones_ref[...])
            for s in range(sb):
                r = r0 + i * sb + s
                o_i = o_i + sums[s * sb:(s + 1) * sb] * v_scr[hh, pl.ds(r, 1), :]
            outs.append(o_i)
        store_out(hh, rows, jnp.concatenate(outs, axis=0))
        kd = k * jnp.exp(last - cum)
        st_scr[hh] = st * jnp.exp(last) + _dot_tn(vb, kd.astype(BF16))

    def block_direct(_):
        def chunk(ci, carry):
            c = (n_ck - 1 - ci) if rev else ci
            r0 = pl.multiple_of(c * cs, cs)
            for hh in range(hps):
                chunk_direct(hh, r0)
            return carry

        return lax.fori_loop(0, n_ck, chunk, 0)

    decays = []
    for c in range(n_w):
        far = cum_all[far_row(c, w):far_row(c, w) + 1]
        o_row = outside_row(c, w, n_w)
        decays.append(far if o_row is None else far - cum_all[o_row:o_row + 1])
    safe = jnp.min(jnp.concatenate(decays, axis=0)) > -HGRN_SAFE_LOG_DECAY
    lax.cond(safe, block_factored, block_direct, 0)


def _gla_specs(ltot, tq, n_lat, rev):
    n_chunks = ltot // tq
    n_lat_chunks = n_lat // tq
    cidx = lambda s: _scan_chunk_index(s, n_lat_chunks, n_chunks, rev)
    part = lambda p: pl.BlockSpec((1, tq, GLA_DIM), lambda b, h, s: (b, cidx(s), p * GLA_HEADS + h))
    return n_chunks, cidx, part


def _hgrn_direction(bm, lb, prev, rev, n_lat, tq):
    bsz, ltot, _ = bm.shape
    n_chunks, cidx, _ = _gla_specs(ltot, tq, n_lat, rev)
    has_prev = prev is not None
    hps = HGRN_HEADS_PER_STEP
    width = hps * GLA_DIM
    n_hsteps = GLA_HEADS // hps
    part = lambda p: pl.BlockSpec((1, tq, width), lambda b, h, s: (b, cidx(s), p * n_hsteps + h))
    tri = _tri_mask(tq, rev).astype(F32)
    ones = jnp.ones((GLA_DIM, GLA_DIM), BF16)
    z_part = 2 if rev else 1
    in_specs = [part(0), part(z_part), part(3),
                pl.BlockSpec((1, 1, width), lambda b, h, s: (h, 0, 0)),
                pl.BlockSpec(tri.shape, lambda b, h, s: (0, 0)),
                pl.BlockSpec(ones.shape, lambda b, h, s: (0, 0))]
    args = [bm, bm, bm, lb.reshape(n_hsteps, 1, width), tri, ones]
    out_spec = pl.BlockSpec((1, tq, width), lambda b, h, s: (b, cidx(s), h))
    if has_prev:
        in_specs.append(out_spec)
        args.append(prev)
    kern = functools.partial(_hgrn_kernel, rev=rev, tq=tq, has_prev=has_prev)
    tile = pltpu.VMEM((hps, tq, GLA_DIM), F32)
    return pl.pallas_call(
        kern,
        out_shape=jax.ShapeDtypeStruct((bsz, ltot, GROUP_WIDTH), F32),
        grid=(bsz, n_hsteps, n_chunks),
        in_specs=in_specs,
        out_specs=out_spec,
        scratch_shapes=[pltpu.VMEM((hps, GLA_DIM, GLA_DIM), F32), tile, tile, tile, tile],
        compiler_params=_params("arbitrary", "arbitrary", "arbitrary"),
        name="hgrn_rev" if rev else "hgrn_fwd",
    )(*args)


def _ret_kernel(*refs, rev, tq, has_prev):
    if has_prev:
        q_ref, k_ref, v_ref, cos_ref, sin_ref, eq_ref, ek_ref, dm_ref, el_ref, prev_ref, o_ref, st_scr = refs
    else:
        q_ref, k_ref, v_ref, cos_ref, sin_ref, eq_ref, ek_ref, dm_ref, el_ref, o_ref, st_scr = refs
        prev_ref = None
    cs = min(tq, RET_CHUNK)
    n_ck = tq // cs

    @pl.when(pl.program_id(1) == 0)
    def _():
        st_scr[...] = jnp.zeros_like(st_scr)

    def chunk(ci, carry):
        c = (n_ck - 1 - ci) if rev else ci
        r0 = pl.multiple_of(c * cs, cs)
        rows = pl.ds(r0, cs)
        cos = cos_ref[rows, :]
        sin = sin_ref[rows, :]
        for h in range(GLA_HEADS):
            lanes = slice(h * GLA_DIM, (h + 1) * GLA_DIM)
            q = q_ref[0, rows, lanes]
            k = k_ref[0, rows, lanes]
            vb = v_ref[0, rows, lanes].astype(BF16)
            q = q * cos + pltpu.roll(q, GLA_DIM // 2, 1) * sin
            k = (k * cos + pltpu.roll(k, GLA_DIM // 2, 1) * sin) * (GLA_DIM ** -0.5)
            st = st_scr[h]
            scores = _dot_nt(q.astype(BF16), k.astype(BF16)) * dm_ref[h]
            out = _dot_nt((q * eq_ref[h]).astype(BF16), st.astype(BF16)) + _dot(scores.astype(BF16), vb)
            if has_prev:
                out = out + prev_ref[0, rows, lanes]
            o_ref[0, rows, lanes] = out
            st_scr[h] = st * el_ref[h] + _dot_tn(vb, (k * ek_ref[h]).astype(BF16))
        return carry

    lax.fori_loop(0, n_ck, chunk, 0)


def _ret_tables(rev, cs):
    gamma = jnp.log1p(-jnp.exp2(-(RET_DECAY_EXP[1 if rev else 0] + jnp.arange(GLA_HEADS, dtype=F32))))
    t = jnp.arange(cs, dtype=F32)
    steps = (cs - t) if rev else (t + 1.0)
    cum = gamma[:, None] * steps[None, :]
    last = gamma * cs
    eq = jnp.broadcast_to(jnp.exp(cum)[:, :, None], (GLA_HEADS, cs, GLA_DIM))
    ek = jnp.broadcast_to(jnp.exp(last[:, None] - cum)[:, :, None], (GLA_HEADS, cs, GLA_DIM))
    rel = cum[:, :, None] - cum[:, None, :]
    dm = jnp.where(_tri_mask(cs, rev)[None], jnp.exp(jnp.minimum(rel, 0.0)), 0.0)
    el = jnp.broadcast_to(jnp.exp(last)[:, None, None], (GLA_HEADS, 1, GLA_DIM))
    return eq, ek, dm, el


def _rotary_tables(n_lat, n_ctx):
    rows = n_lat // GRID_W
    row = jnp.repeat(jnp.arange(rows, dtype=F32), GRID_W)
    col = jnp.tile(jnp.arange(GRID_W, dtype=F32), rows)
    quarter = GLA_DIM // 4
    inv_freq = ROPE_BASE ** (-jnp.arange(quarter, dtype=F32) / quarter)
    ang = jnp.concatenate([row[:, None] * inv_freq, col[:, None] * inv_freq], axis=-1)
    cos, sin = jnp.cos(ang), jnp.sin(ang)
    cos_l = jnp.concatenate([cos, cos], axis=-1)
    sin_l = jnp.concatenate([-sin, sin], axis=-1)
    cos_t = jnp.concatenate([cos_l, jnp.ones((n_ctx, GLA_DIM), F32)], axis=0)
    sin_t = jnp.concatenate([sin_l, jnp.zeros((n_ctx, GLA_DIM), F32)], axis=0)
    return cos_t, sin_t


def _ret_direction(bm, cos_t, sin_t, prev, rev, n_lat, tq):
    bsz, ltot, _ = bm.shape
    n_chunks, cidx, _ = _gla_specs(ltot, tq, n_lat, rev)
    has_prev = prev is not None
    eq, ek, dm, el = _ret_tables(rev, min(tq, RET_CHUNK))
    gw = GROUP_WIDTH
    part = lambda p: pl.BlockSpec((1, tq, gw), lambda b, s: (b, cidx(s), p))
    whole = lambda a: pl.BlockSpec(a.shape, lambda b, s: (0, 0, 0))
    pos = pl.BlockSpec((tq, GLA_DIM), lambda b, s: (cidx(s), 0))
    in_specs = [part(5), part(6), part(7), pos, pos, whole(eq), whole(ek), whole(dm), whole(el)]
    args = [bm, bm, bm, cos_t, sin_t, eq, ek, dm, el]
    out_spec = pl.BlockSpec((1, tq, gw), lambda b, s: (b, cidx(s), 0))
    if has_prev:
        in_specs.append(out_spec)
        args.append(prev)
    kern = functools.partial(_ret_kernel, rev=rev, tq=tq, has_prev=has_prev)
    return pl.pallas_call(
        kern,
        out_shape=jax.ShapeDtypeStruct((bsz, ltot, gw), F32),
        grid=(bsz, n_chunks),
        in_specs=in_specs,
        out_specs=out_spec,
        scratch_shapes=[pltpu.VMEM((GLA_HEADS, GLA_DIM, GLA_DIM), F32)],
        compiler_params=_params("arbitrary", "arbitrary"),
        name="ret_rev" if rev else "ret_fwd",
    )(*args)


def _head_rms(o):
    parts = []
    for h in range(GLA_HEADS):
        oh = o[:, h * GLA_DIM:(h + 1) * GLA_DIM]
        parts.append(oh * lax.rsqrt(jnp.mean(oh * oh, axis=-1, keepdims=True) + EPS))
    return jnp.concatenate(parts, axis=-1)


def _mix_kernel(u_ref, lg_ref, ys_ref, hs_ref, hg_ref, rg_ref, oh_ref, or_ref, d_ref, gw_ref, gb_ref, o_ref):
    gw = GROUP_WIDTH
    y = jax.nn.gelu(d_ref[...] * u_ref[...] + ys_ref[...])
    a = y * jax.nn.sigmoid(_dot(y.astype(BF16), gw_ref[...]) + gb_ref[...])
    o_ref[0, :, 0:gw] = a.astype(BF16)
    o_ref[0, :, gw:2 * gw] = (hs_ref[...] * jax.nn.gelu(lg_ref[...])).astype(BF16)
    g = hg_ref[0]
    o_ref[0, :, 2 * gw:3 * gw] = (_head_rms(oh_ref[0]) * (g * jax.nn.sigmoid(g))).astype(BF16)
    g = rg_ref[0]
    o_ref[0, :, 3 * gw:4 * gw] = (_head_rms(or_ref[0]) * (g * jax.nn.sigmoid(g))).astype(BF16)


def _mix_epilogue(tm2, ys2, hs2, bm, o_h, o_r, s5_d, glu_w, glu_b, n_rows, tm):
    bsz, ltot, _ = bm.shape
    gw = GROUP_WIDTH
    tmaj = lambda p, k: pl.BlockSpec((tm, gw), lambda b, i: (i, b * k + p))
    bmaj = lambda p: pl.BlockSpec((1, tm, gw), lambda b, i: (b, i, p))
    full2 = lambda b, i: (0, 0)
    return pl.pallas_call(
        _mix_kernel,
        out_shape=jax.ShapeDtypeStruct((bsz, ltot, 4 * gw), BF16),
        grid=(bsz, n_rows // tm),
        in_specs=[tmaj(0, N_TM_PARTS), tmaj(2, N_TM_PARTS), tmaj(0, 1), tmaj(0, 1),
                  bmaj(4), bmaj(8), bmaj(0), bmaj(0),
                  pl.BlockSpec((1, gw), full2), pl.BlockSpec((gw, gw), full2), pl.BlockSpec((1, gw), full2)],
        out_specs=pl.BlockSpec((1, tm, 4 * gw), lambda b, i: (b, i, 0)),
        compiler_params=_params("arbitrary", "arbitrary"),
        name="mix_epilogue",
    )(tm2, tm2, ys2, hs2, bm, bm, o_h, o_r, s5_d, glu_w, glu_b)


def _gate_rows(modb_ref, modc_ref, idx, row0, tm, n_lat):
    row = row0 + lax.broadcasted_iota(I32, (tm, 1), 0)
    return jnp.where(row >= n_lat, modc_ref[0, idx:idx + 1, :], modb_ref[0, idx:idx + 1, :])


def _outproj_kernel(a_ref, w_ref, x_ref, modb_ref, modc_ref, o_ref, *, tm, n_lat):
    gate = _gate_rows(modb_ref, modc_ref, 2, pl.program_id(1) * tm, tm, n_lat)
    o_ref[0] = x_ref[0] + gate * _dot(a_ref[0], w_ref[...])


def _out_projection(mix, w_bf16, xs, modv, n_rows, n_lat, tm, tn):
    bsz, ltot, d = xs.shape
    k = mix.shape[-1]
    kern = functools.partial(_outproj_kernel, tm=tm, n_lat=n_lat)
    return pl.pallas_call(
        kern,
        out_shape=jax.ShapeDtypeStruct((bsz, ltot, d), F32),
        grid=(bsz, n_rows // tm, d // tn),
        in_specs=[pl.BlockSpec((1, tm, k), lambda b, i, j: (b, i, 0)),
                  pl.BlockSpec((k, tn), lambda b, i, j: (0, j)),
                  pl.BlockSpec((1, tm, tn), lambda b, i, j: (b, i, j)),
                  pl.BlockSpec((1, 6, tn), lambda b, i, j: (b, 0, j)),
                  pl.BlockSpec((1, 6, tn), lambda b, i, j: (bsz, 0, j))],
        out_specs=pl.BlockSpec((1, tm, tn), lambda b, i, j: (b, i, j)),
        compiler_params=_params("arbitrary", "arbitrary", "arbitrary"),
        name="out_projection",
    )(mix, w_bf16, xs, modv, modv)


def _first_max4(v):
    m1 = jnp.maximum(jnp.maximum(v[0], v[1]), jnp.maximum(v[2], v[3]))
    i1 = jnp.where(v[0] == m1, 0, jnp.where(v[1] == m1, 1, jnp.where(v[2] == m1, 2, 3)))
    rest = [jnp.where(i1 == j, -jnp.inf, v[j]) for j in range(4)]
    m2 = jnp.maximum(jnp.maximum(rest[0], rest[1]), jnp.maximum(rest[2], rest[3]))
    i2 = jnp.where(rest[0] == m2, 0, jnp.where(rest[1] == m2, 1, jnp.where(rest[2] == m2, 2, 3)))
    return m1, i1, m2, i2


def _router_kernel(x_ref, g_ref, modb_ref, modc_ref, rw_ref, rb_ref, tri_ref,
                   h_ref, ids_ref, gates_ref, ranks_ref, counts_ref, base_scr, *, tm, n_lat):
    first = jnp.logical_and(pl.program_id(0) == 0, pl.program_id(1) == 0)

    @pl.when(first)
    def _():
        base_scr[...] = jnp.zeros_like(base_scr)

    h = _norm_modulate(x_ref[0], g_ref[...], modb_ref, modc_ref, pl.program_id(1) * tm, n_lat, 3, 4)
    h_ref[0] = h.astype(BF16)
    logits = lax.dot_general(rw_ref[...], h, (((1,), (1,)), ((), ())), preferred_element_type=F32,
                             precision=lax.Precision.HIGHEST)
    scores = jax.nn.sigmoid(logits)
    biased = scores + rb_ref[...]
    tops = []
    for gidx in range(N_EXPERT_GROUPS):
        rows = [biased[gidx * EXPERTS_PER_GROUP + j:gidx * EXPERTS_PER_GROUP + j + 1, :]
                for j in range(EXPERTS_PER_GROUP)]
        tops.append(_first_max4(rows))
    gs = [t[0] + t[2] for t in tops]
    gmax = jnp.maximum(jnp.maximum(gs[0], gs[1]), jnp.maximum(gs[2], gs[3]))
    best = jnp.where(gs[0] == gmax, 0, jnp.where(gs[1] == gmax, 1, jnp.where(gs[2] == gmax, 2, 3)))
    e1 = jnp.zeros_like(best)
    e2 = jnp.zeros_like(best)
    for gidx in range(N_EXPERT_GROUPS):
        e1 = jnp.where(best == gidx, gidx * EXPERTS_PER_GROUP + tops[gidx][1], e1)
        e2 = jnp.where(best == gidx, gidx * EXPERTS_PER_GROUP + tops[gidx][3], e2)
    eid = lax.broadcasted_iota(I32, (N_EXPERTS, tm), 0)
    sel1 = eid == e1
    sel2 = eid == e2
    w1 = jnp.sum(jnp.where(sel1, scores, 0.0), axis=0, keepdims=True)
    w2 = jnp.sum(jnp.where(sel2, scores, 0.0), axis=0, keepdims=True)
    wsum = w1 + w2
    onehot = jnp.where(jnp.logical_or(sel1, sel2), 1.0, 0.0)
    pos = base_scr[...] + _dot(onehot.astype(BF16), tri_ref[...])
    r1 = jnp.sum(jnp.where(sel1, pos, 0.0), axis=0, keepdims=True)
    r2 = jnp.sum(jnp.where(sel2, pos, 0.0), axis=0, keepdims=True)
    ids_ref[0] = jnp.concatenate([e1, e2], axis=0)
    gates_ref[0] = jnp.concatenate([w1 / wsum, w2 / wsum], axis=0)
    ranks_ref[0] = jnp.concatenate([r1, r2], axis=0).astype(I32)
    base = base_scr[...] + jnp.sum(onehot, axis=1, keepdims=True)
    base_scr[...] = base
    counts_ref[...] = jnp.broadcast_to(base, counts_ref.shape)


def _router(xs, gain, modv, rw_t, rbias, n_rows, n_lat, tm):
    bsz, ltot, d = xs.shape
    tri = (lax.broadcasted_iota(I32, (tm, tm), 0) < lax.broadcasted_iota(I32, (tm, tm), 1)).astype(BF16)
    kern = functools.partial(_router_kernel, tm=tm, n_lat=n_lat)
    small = lambda dt: jax.ShapeDtypeStruct((bsz, 2, n_rows), dt)
    small_spec = pl.BlockSpec((1, 2, tm), lambda b, i: (b, 0, i))
    return pl.pallas_call(
        kern,
        out_shape=(jax.ShapeDtypeStruct((bsz, ltot, d), BF16), small(I32), small(F32), small(I32),
                   jax.ShapeDtypeStruct((N_EXPERTS, 128), F32)),
        grid=(bsz, n_rows // tm),
        in_specs=[pl.BlockSpec((1, tm, d), lambda b, i: (b, i, 0)),
                  pl.BlockSpec((1, d), lambda b, i: (0, 0)),
                  pl.BlockSpec((1, 6, d), lambda b, i: (b, 0, 0)),
                  pl.BlockSpec((1, 6, d), lambda b, i: (bsz, 0, 0)),
                  pl.BlockSpec((N_EXPERTS, d), lambda b, i: (0, 0)),
                  pl.BlockSpec((N_EXPERTS, 1), lambda b, i: (0, 0)),
                  pl.BlockSpec((tm, tm), lambda b, i: (0, 0))],
        out_specs=(pl.BlockSpec((1, tm, d), lambda b, i: (b, i, 0)), small_spec, small_spec, small_spec,
                   pl.BlockSpec((N_EXPERTS, 128), lambda b, i: (0, 0))),
        scratch_shapes=[pltpu.VMEM((N_EXPERTS, 1), F32)],
        compiler_params=_params("arbitrary", "arbitrary"),
        name="moe_router",
    )(xs, gain, modv, modv, rw_t, rbias, tri)


def _expert_ffn_kernel(te_ref, nt_ref, x_ref, wg_ref, wu_ref, wd_ref, o_ref):
    @pl.when(pl.program_id(0) < nt_ref[0])
    def _():
        x = x_ref[...]
        g = _dot(x, wg_ref[0])
        u = _dot(x, wu_ref[0])
        he = (g * jax.nn.sigmoid(g)) * u
        o_ref[...] = _dot(he.astype(BF16), wd_ref[0])


def _expert_ffn(tile_expert, n_tiles, x_sorted, wg, wu, wd, tm):
    p, d = x_sorted.shape
    f = wg.shape[-1]
    tile = lambda i, te, nt: jnp.minimum(i, nt[0] - 1)
    expert = lambda i, te, nt: te[jnp.minimum(i, nt[0] - 1)]
    grid_spec = pltpu.PrefetchScalarGridSpec(
        num_scalar_prefetch=2,
        grid=(p // tm,),
        in_specs=[pl.BlockSpec((tm, d), lambda i, te, nt: (tile(i, te, nt), 0)),
                  pl.BlockSpec((1, d, f), lambda i, te, nt: (expert(i, te, nt), 0, 0)),
                  pl.BlockSpec((1, d, f), lambda i, te, nt: (expert(i, te, nt), 0, 0)),
                  pl.BlockSpec((1, f, d), lambda i, te, nt: (expert(i, te, nt), 0, 0))],
        out_specs=pl.BlockSpec((tm, d), lambda i, te, nt: (tile(i, te, nt), 0)),
    )
    return pl.pallas_call(
        _expert_ffn_kernel,
        out_shape=jax.ShapeDtypeStruct((p, d), F32),
        grid_spec=grid_spec,
        compiler_params=_params("arbitrary"),
        name="moe_expert_ffn",
    )(tile_expert, n_tiles, x_sorted, wg, wu, wd)


def _moe_residual_kernel(x_ref, y_ref, modb_ref, modc_ref, g_ref, o_ref, *, tm, n_lat, final_norm):
    gate = _gate_rows(modb_ref, modc_ref, 5, pl.program_id(1) * tm, tm, n_lat)
    x = x_ref[0] + gate * y_ref[0]
    if final_norm:
        x = x * lax.rsqrt(jnp.mean(x * x, axis=-1, keepdims=True) + EPS) * g_ref[...]
    o_ref[0] = x


def _moe_residual(xs, y, modv, final_g, n_rows, n_lat, tm, final_norm):
    bsz, ltot, d = xs.shape
    kern = functools.partial(_moe_residual_kernel, tm=tm, n_lat=n_lat, final_norm=final_norm)
    rows_out = n_rows if final_norm else ltot
    return pl.pallas_call(
        kern,
        out_shape=jax.ShapeDtypeStruct((bsz, rows_out, d), F32),
        grid=(bsz, n_rows // tm),
        in_specs=[pl.BlockSpec((1, tm, d), lambda b, i: (b, i, 0)),
                  pl.BlockSpec((1, tm, d), lambda b, i: (b, i, 0)),
                  pl.BlockSpec((1, 6, d), lambda b, i: (b, 0, 0)),
                  pl.BlockSpec((1, 6, d), lambda b, i: (bsz, 0, 0)),
                  pl.BlockSpec((1, d), lambda b, i: (0, 0))],
        out_specs=pl.BlockSpec((1, tm, d), lambda b, i: (b, i, 0)),
        compiler_params=_params("arbitrary", "arbitrary"),
        name="moe_residual",
    )(xs, y, modv, modv, final_g)


def _moe_layer(xs, gain, modv, rw_t, rbias, wg, wu, wd, final_g, n_rows, n_lat, tm_tok, tm_res, tm_exp, final_norm):
    bsz, ltot, d = xs.shape
    h, ids, gates, ranks, counts = _router(xs, gain, modv, rw_t, rbias, n_rows, n_lat, tm_tok)
    counts = counts[:, 0].astype(I32)
    padded = ((counts + tm_exp - 1) // tm_exp) * tm_exp
    ends = jnp.cumsum(padded)
    offsets = ends - padded
    dest = offsets[ids] + ranks
    n_slots = bsz * 2 * n_rows
    p_rows = ((n_slots + N_EXPERTS * (tm_exp - 1)) + tm_exp - 1) // tm_exp * tm_exp
    tok = (jnp.arange(bsz, dtype=I32)[:, None, None] * ltot + jnp.arange(n_rows, dtype=I32)[None, None, :])
    tok = jnp.broadcast_to(tok, dest.shape)
    src = jnp.zeros((p_rows,), I32).at[dest.reshape(-1)].set(tok.reshape(-1))
    tile_start = jnp.arange(p_rows // tm_exp, dtype=I32) * tm_exp
    tile_expert = jnp.minimum(jnp.sum(tile_start[:, None] >= ends[None, :], axis=1), N_EXPERTS - 1).astype(I32)
    n_tiles = (ends[-1:] // tm_exp).astype(I32)
    x_sorted = jnp.take(h.reshape(bsz * ltot, d), src, axis=0)
    y_sorted = _expert_ffn(tile_expert, n_tiles, x_sorted, wg, wu, wd, tm_exp)
    y = (jnp.take(y_sorted, dest[:, 0].reshape(-1), axis=0) * gates[:, 0].reshape(-1, 1)
         + jnp.take(y_sorted, dest[:, 1].reshape(-1), axis=0) * gates[:, 1].reshape(-1, 1))
    y = y.reshape(bsz, n_rows, d)
    return _moe_residual(xs, y, modv, final_g, n_rows, n_lat, tm_res, final_norm)


def _row_tile(n, target):
    t = min(n, target)
    while n % t or t % 8:
        t -= 8
    return t


def kernel(x, c, ctx, c_ctx, norm_mix_g, norm_ffn_g, w_mod, b_mod, w_in, w_out, s5_lam_re, s5_lam_im, s5_log_dt, s5_b_re, s5_b_im, s5_c_re, s5_c_im, s5_d, s5_glu_w, s5_glu_b, lru_conv_w, lru_conv_b, lru_wa, lru_ba, lru_wx, lru_bx, lru_lam, hgrn_lb_logits, router_w, router_bias, moe_w_gate, moe_w_up, moe_w_down, final_norm_g):
    bsz, n_lat, d = x.shape
    n_ctx = ctx.shape[1]
    ltot = n_lat + n_ctx
    depth = w_in.shape[0]
    gw = GROUP_WIDTH
    assert bsz % 8 == 0 and bsz + 1 <= MOD_ROWS
    assert n_lat % GLA_CHUNK == 0 and n_ctx % GLA_CHUNK == 0 and n_lat % GRID_W == 0

    tq_scan = math.gcd(math.gcd(n_lat, n_ctx), 64)
    tq_gla = math.gcd(math.gcd(n_lat, n_ctx), 256)
    tm_all = _row_tile(ltot, 768)
    tm_lat = _row_tile(n_lat, 1024)
    tm_mix = math.gcd(math.gcd(n_lat, n_ctx), 256)

    xs = jnp.concatenate([x, ctx.astype(x.dtype)], axis=1)
    cv = jnp.zeros((MOD_ROWS, d), F32).at[:bsz].set(c).at[bsz].set(c_ctx)
    mod_all = _mod_vectors(cv, w_mod, b_mod).reshape(depth, MOD_ROWS, 6, d)
    cos_t, sin_t = _rotary_tables(n_lat, n_ctx)
    rw_t = router_w.astype(F32).T
    rbias = router_bias.astype(F32).reshape(N_EXPERTS, 1)
    lb_cum = [jnp.cumsum(jax.nn.softmax(hgrn_lb_logits[dr].astype(F32), axis=0), axis=0) for dr in (0, 1)]

    for layer in range(depth):
        ctx_out = layer < depth - 1
        n_rows = ltot if ctx_out else n_lat
        tm_tok = tm_all if ctx_out else tm_lat
        modv = mod_all[layer]

        tm2, bm = _in_projection(xs, norm_mix_g[layer].reshape(1, d), modv, w_in[layer].astype(BF16), n_lat, tm_all)
        tm3 = tm2.reshape(ltot, bsz, N_TM_PARTS * gw)

        ys = hs = o_h = o_r = None
        for dr in (0, 1):
            rev = dr == 1
            bw, cw, atab = _s5_tables(s5_lam_re[layer, dr], s5_lam_im[layer, dr], s5_log_dt[layer, dr],
                                      s5_b_re[layer, dr], s5_b_im[layer, dr], s5_c_re[layer, dr], s5_c_im[layer, dr])
            ys = _s5_direction(tm3, bw, cw, atab, ys, rev, n_lat, tq_scan)
            sp = jax.nn.softplus(-lru_lam[layer, dr].astype(F32)).reshape(1, gw)
            hs = _lru_direction(tm3, lru_conv_w[layer].astype(F32), lru_conv_b[layer].astype(F32).reshape(1, gw),
                                _block_diag(lru_wa[layer, dr]).astype(BF16), _block_diag(lru_wx[layer, dr]).astype(BF16),
                                lru_ba[layer, dr].astype(F32).reshape(1, gw), lru_bx[layer, dr].astype(F32).reshape(1, gw),
                                sp, hs, rev, n_lat, tq_scan)
            lb = lb_cum[dr][layer] - lb_cum[dr][0]
            o_h = _hgrn_direction(bm, lb, o_h, rev, n_lat, tq_gla)
            o_r = _ret_direction(bm, cos_t, sin_t, o_r, rev, n_lat, tq_gla)

        mix = _mix_epilogue(tm2, ys.reshape(ltot, bsz * gw), hs.reshape(ltot, bsz * gw), bm, o_h, o_r,
                            s5_d[layer].astype(F32).reshape(1, gw), s5_glu_w[layer].astype(BF16),
                            s5_glu_b[layer].astype(F32).reshape(1, gw), n_rows, tm_mix)
        xs = _out_projection(mix, w_out[layer].astype(BF16), xs, modv, n_rows, n_lat, tm_tok, 1024)
        xs = _moe_layer(xs, norm_ffn_g[layer].reshape(1, d), modv, rw_t, rbias,
                        moe_w_gate[layer].astype(BF16), moe_w_up[layer].astype(BF16), moe_w_down[layer].astype(BF16),
                        final_norm_g.reshape(1, d), n_rows, n_lat, tm_tok, tm_mix, 512, not ctx_out)
    return xs
```

```python
import functools
import math

import jax
import jax.numpy as jnp
from jax import lax
from jax.experimental import pallas as pl
from jax.experimental.pallas import tpu as pltpu

F32 = jnp.float32
BF16 = jnp.bfloat16
I32 = jnp.int32

GROUP_WIDTH = 512
N_IN_PARTS = 12
N_TM_PARTS = 3
N_BM_PARTS = N_IN_PARTS - N_TM_PARTS
S5_CH = 16
S5_GROUPS = 32
S5_STATE = 64
S5_LANE_BLOCKS = 4
LRU_HEADS = 8
LRU_CONV = 4
LRU_C = 8.0
GLA_HEADS = 4
GLA_DIM = 128
GLA_CHUNK = 64
GLA_SUB = 16
HGRN_FAST_WINDOW = 32
HGRN_SAFE_LOG_DECAY = 60.0
HGRN_HEADS_PER_STEP = 2
RET_CHUNK = 256
RET_DECAY_EXP = (5.0, 5.5)
ROPE_BASE = 10000.0
GRID_W = 64
N_EXPERTS = 16
N_EXPERT_GROUPS = 4
EXPERTS_PER_GROUP = 4
D_FF_EXPERT = 1024
EPS = 1e-6
NORM_SLAB = 16
MOD_ROWS = 24

VMEM_LIMIT_BYTES = 56 * 1024 * 1024


def _params(*semantics):
    return pltpu.CompilerParams(dimension_semantics=semantics, vmem_limit_bytes=VMEM_LIMIT_BYTES)


def _dot(a, b):
    return jnp.dot(a, b, preferred_element_type=F32)


def _dot_nt(a, b):
    return lax.dot_general(a, b, (((1,), (1,)), ((), ())), preferred_element_type=F32)


def _dot_tn(a, b):
    return lax.dot_general(a, b, (((0,), (0,)), ((), ())), preferred_element_type=F32)


def _scan_chunk_index(step, n_lat_chunks, n_chunks, rev):
    if rev:
        return n_chunks - 1 - step
    return (step + n_lat_chunks) % n_chunks


def _mod_kernel(cv_ref, w_ref, b_ref, o_ref):
    cv = cv_ref[...]
    s = cv * jax.nn.sigmoid(cv)
    o_ref[0] = _dot(s.astype(BF16), w_ref[0].astype(BF16)) + b_ref[0]


def _mod_vectors(cv, w_mod, b_mod):
    nl, d, n6 = w_mod.shape
    tn = 1024
    return pl.pallas_call(
        _mod_kernel,
        out_shape=jax.ShapeDtypeStruct((nl, MOD_ROWS, n6), F32),
        grid=(nl, n6 // tn),
        in_specs=[pl.BlockSpec((MOD_ROWS, d), lambda l, j: (0, 0)),
                  pl.BlockSpec((1, d, tn), lambda l, j: (l, 0, j)),
                  pl.BlockSpec((1, 1, tn), lambda l, j: (l, 0, j))],
        out_specs=pl.BlockSpec((1, MOD_ROWS, tn), lambda l, j: (l, 0, j)),
        compiler_params=_params("arbitrary", "arbitrary"),
        name="mod_vectors",
    )(cv, w_mod, b_mod.reshape(nl, 1, n6))


def _norm_modulate(x, g, modb_ref, modc_ref, row0, n_lat, shift_idx, scale_idx):
    tm = x.shape[0]
    ms = jnp.mean(x * x, axis=-1, keepdims=True)
    y = x * lax.rsqrt(ms + EPS) * g
    row = row0 + lax.broadcasted_iota(I32, (tm, 1), 0)
    is_ctx = row >= n_lat
    shift = jnp.where(is_ctx, modc_ref[0, shift_idx:shift_idx + 1, :], modb_ref[0, shift_idx:shift_idx + 1, :])
    scale = jnp.where(is_ctx, modc_ref[0, scale_idx:scale_idx + 1, :], modb_ref[0, scale_idx:scale_idx + 1, :])
    return y * (1.0 + scale) + shift


def _inproj_kernel(x_ref, g_ref, modb_ref, modc_ref, w_ref, otm_ref, obm_ref, h_scr, *, tm, n_lat):
    i = pl.program_id(1)
    j = pl.program_id(2)

    @pl.when(j == 0)
    def _():
        def slab(s, carry):
            r0 = pl.multiple_of(s * NORM_SLAB, NORM_SLAB)
            h = _norm_modulate(x_ref[0, pl.ds(r0, NORM_SLAB), :], g_ref[...], modb_ref, modc_ref,
                               i * tm + r0, n_lat, 0, 1)
            h_scr[pl.ds(r0, NORM_SLAB), :] = h.astype(BF16)
            return carry

        lax.fori_loop(0, tm // NORM_SLAB, slab, 0, unroll=2)

    r = _dot(h_scr[...], w_ref[0])

    @pl.when(j == 0)
    def _():
        otm_ref[...] = r

    @pl.when(j > 0)
    def _():
        obm_ref[0] = r


def _in_projection(xs, gain, modv, w_bf16, layer, n_lat, tm):
    bsz, ltot, d = xs.shape
    gw = GROUP_WIDTH
    kern = functools.partial(_inproj_kernel, tm=tm, n_lat=n_lat)
    tn = N_TM_PARTS * gw
    return pl.pallas_call(
        kern,
        out_shape=(jax.ShapeDtypeStruct((ltot, bsz * tn), F32),
                   jax.ShapeDtypeStruct((bsz, ltot, N_BM_PARTS * gw), F32)),
        grid=(bsz, ltot // tm, N_IN_PARTS * gw // tn),
        in_specs=[pl.BlockSpec((1, tm, d), lambda b, i, j: (b, i, 0)),
                  pl.BlockSpec((1, d), lambda b, i, j: (0, 0)),
                  pl.BlockSpec((1, 6, d), lambda b, i, j: (b, 0, 0)),
                  pl.BlockSpec((1, 6, d), lambda b, i, j: (bsz, 0, 0)),
                  pl.BlockSpec((1, d, tn), lambda b, i, j: (layer, 0, j))],
        out_specs=(pl.BlockSpec((tm, tn), lambda b, i, j: (i, b)),
                   pl.BlockSpec((1, tm, tn), lambda b, i, j: (b, i, jnp.maximum(j - 1, 0)))),
        scratch_shapes=[pltpu.VMEM((tm, d), BF16)],
        compiler_params=_params("arbitrary", "arbitrary", "arbitrary"),
        name="in_projection",
    )(xs, gain, modv, modv, w_bf16)


def _s5_kernel(*refs, rev, tq, nb, has_prev):
    if has_prev:
        u_ref, bw_ref, cw_ref, a_ref, prev_ref, y_ref, z_scr, h_scr = refs
    else:
        u_ref, bw_ref, cw_ref, a_ref, y_ref, z_scr, h_scr = refs
        prev_ref = None
    rows = tq * nb
    half = S5_STATE * 8

    @pl.when(pl.program_id(0) == 0)
    def _():
        h_scr[...] = jnp.zeros_like(h_scr)

    for gb in range(S5_LANE_BLOCKS):
        lanes = slice(gb * 128, (gb + 1) * 128)
        ub = u_ref[:, :, lanes].reshape(rows, 128).astype(BF16)
        z_scr[...] = _dot(ub, bw_ref[gb])
        ar = jnp.broadcast_to(a_ref[gb, 0], (nb, half))
        ai = jnp.broadcast_to(a_ref[gb, 1], (nb, half))

        def step(i, carry):
            hr, hi = carry
            t = (tq - 1 - i) if rev else i
            r0 = pl.multiple_of(t * nb, nb)
            zr = z_scr[pl.ds(r0, nb), 0:half]
            zi = z_scr[pl.ds(r0, nb), half:2 * half]
            nr = ar * hr - ai * hi + zr
            ni = ar * hi + ai * hr + zi
            z_scr[pl.ds(r0, nb), 0:half] = nr
            z_scr[pl.ds(r0, nb), half:2 * half] = ni
            return nr, ni

        hr, hi = lax.fori_loop(0, tq, step, (h_scr[gb, 0], h_scr[gb, 1]), unroll=4)
        h_scr[gb, 0] = hr
        h_scr[gb, 1] = hi
        yb = _dot(z_scr[...].astype(BF16), cw_ref[gb]).reshape(tq, nb, 128)
        if has_prev:
            yb = yb + prev_ref[:, :, lanes]
        y_ref[:, :, lanes] = yb


def _s5_direction(tm3, bw, cw, atab, prev, rev, n_lat, tq):
    ltot, nb, _ = tm3.shape
    gw = GROUP_WIDTH
    n_chunks = ltot // tq
    n_lat_chunks = n_lat // tq
    cidx = lambda s: _scan_chunk_index(s, n_lat_chunks, n_chunks, rev)
    has_prev = prev is not None
    in_specs = [pl.BlockSpec((tq, nb, gw), lambda s: (cidx(s), 0, 0)),
                pl.BlockSpec(bw.shape, lambda s: (0, 0, 0)),
                pl.BlockSpec(cw.shape, lambda s: (0, 0, 0)),
                pl.BlockSpec(atab.shape, lambda s: (0, 0, 0, 0))]
    args = [tm3, bw, cw, atab]
    if has_prev:
        in_specs.append(pl.BlockSpec((tq, nb, gw), lambda s: (cidx(s), 0, 0)))
        args.append(prev)
    kern = functools.partial(_s5_kernel, rev=rev, tq=tq, nb=nb, has_prev=has_prev)
    return pl.pallas_call(
        kern,
        out_shape=jax.ShapeDtypeStruct((ltot, nb, gw), F32),
        grid=(n_chunks,),
        in_specs=in_specs,
        out_specs=pl.BlockSpec((tq, nb, gw), lambda s: (cidx(s), 0, 0)),
        scratch_shapes=[pltpu.VMEM((tq * nb, 2 * S5_STATE * 8), F32),
                        pltpu.VMEM((S5_LANE_BLOCKS, 2, nb, S5_STATE * 8), F32)],
        compiler_params=_params("arbitrary"),
        name="s5_rev" if rev else "s5_fwd",
    )(*args)


def _block_diag(blocks):
    n, r, c = blocks.shape
    eye = jnp.eye(n, dtype=blocks.dtype)
    return (eye[:, None, :, None] * blocks[:, :, None, :]).reshape(n * r, n * c)


def _s5_tables(lam_re, lam_im, log_dt, b_re, b_im, c_re, c_im):
    dt = jnp.exp(log_dt.astype(F32))[:, None]
    lr, li = lam_re.astype(F32), lam_im.astype(F32)
    mag = jnp.exp(lr * dt)
    abar_re, abar_im = mag * jnp.cos(li * dt), mag * jnp.sin(li * dt)
    den = lr * lr + li * li
    zr = abar_re - 1.0
    w_re = (zr * lr + abar_im * li) / den
    w_im = (abar_im * lr - zr * li) / den
    bre, bim = b_re.astype(F32), b_im.astype(F32)
    bw_re = w_re[:, None, :] * bre - w_im[:, None, :] * bim
    bw_im = w_re[:, None, :] * bim + w_im[:, None, :] * bre
    gpb = S5_GROUPS // S5_LANE_BLOCKS
    bws, cws, atabs = [], [], []
    for gb in range(S5_LANE_BLOCKS):
        sl = slice(gb * gpb, (gb + 1) * gpb)
        bws.append(jnp.concatenate([_block_diag(bw_re[sl]), _block_diag(bw_im[sl])], axis=1))
        cws.append(jnp.concatenate([_block_diag(c_re[sl].astype(F32)), -_block_diag(c_im[sl].astype(F32))], axis=0))
        atabs.append(jnp.stack([abar_re[sl].reshape(1, -1), abar_im[sl].reshape(1, -1)]))
    return jnp.stack(bws).astype(BF16), jnp.stack(cws).astype(BF16), jnp.stack(atabs)


def _lru_kernel(*refs, rev, tq, nb, n_lat_chunks, n_chunks, has_prev):
    if has_prev:
        x_ref, xp_ref, xn_ref, cw_ref, cb_ref, wa_ref, wx_ref, ba_ref, bx_ref, sp_ref, prev_ref, o_ref, a_scr, b_scr, h_scr = refs
    else:
        x_ref, xp_ref, xn_ref, cw_ref, cb_ref, wa_ref, wx_ref, ba_ref, bx_ref, sp_ref, o_ref, a_scr, b_scr, h_scr = refs
        prev_ref = None
    s = pl.program_id(0)
    gw = GROUP_WIDTH
    rows = tq * nb

    @pl.when(s == 0)
    def _():
        h_scr[...] = jnp.zeros_like(h_scr)

    c = _scan_chunk_index(s, n_lat_chunks, n_chunks, rev)
    first = jnp.logical_or(c == 0, c == n_lat_chunks)
    last = jnp.logical_or(c == n_lat_chunks - 1, c == n_chunks - 1)
    keep_prev = jnp.where(first, 0.0, 1.0)
    keep_next = jnp.where(last, 0.0, 1.0)
    xc = jnp.concatenate([xp_ref[...] * keep_prev, x_ref[...], xn_ref[...] * keep_next], axis=0)
    conv = cb_ref[...] + xc[0:tq] * cw_ref[0:1, :]
    for k in range(1, LRU_CONV):
        conv = conv + xc[k:k + tq] * cw_ref[k:k + 1, :]
    x = conv.reshape(rows, gw)
    xb = x.astype(BF16)
    r = jax.nn.sigmoid(_dot(xb, wa_ref[...]) + ba_ref[...])
    ig = jax.nn.sigmoid(_dot(xb, wx_ref[...]) + bx_ref[...])
    log_a = -LRU_C * r * sp_ref[...]
    a = jnp.exp(log_a)
    a_scr[...] = a
    b_scr[...] = jnp.sqrt(1.0 - a * a) * (ig * x)

    def step(i, h):
        t = (tq - 1 - i) if rev else i
        r0 = pl.multiple_of(t * nb, nb)
        hn = a_scr[pl.ds(r0, nb), :] * h + b_scr[pl.ds(r0, nb), :]
        b_scr[pl.ds(r0, nb), :] = hn
        return hn

    h_scr[...] = lax.fori_loop(0, tq, step, h_scr[...], unroll=4)
    out = b_scr[...].reshape(tq, nb, gw)
    if has_prev:
        out = out + prev_ref[...]
    o_ref[...] = out


def _lru_direction(tm3, conv_w, conv_b, wa_bd, wx_bd, ba, bx, sp, prev, rev, n_lat, tq):
    ltot, nb, _ = tm3.shape
    gw = GROUP_WIDTH
    n_chunks = ltot // tq
    n_lat_chunks = n_lat // tq
    cidx = lambda s: _scan_chunk_index(s, n_lat_chunks, n_chunks, rev)
    has_prev = prev is not None
    full2 = lambda s: (0, 0)
    in_specs = [pl.BlockSpec((tq, nb, gw), lambda s: (cidx(s), 0, 1)),
                pl.BlockSpec((2, nb, gw), lambda s: (jnp.maximum(cidx(s) * (tq // 2) - 1, 0), 0, 1)),
                pl.BlockSpec((1, nb, gw), lambda s: (jnp.minimum((cidx(s) + 1) * tq, ltot - 1), 0, 1)),
                pl.BlockSpec((LRU_CONV, gw), full2),
                pl.BlockSpec((1, gw), full2),
                pl.BlockSpec((gw, gw), full2),
                pl.BlockSpec((gw, gw), full2),
                pl.BlockSpec((1, gw), full2),
                pl.BlockSpec((1, gw), full2),
                pl.BlockSpec((1, gw), full2)]
    args = [tm3, tm3, tm3, conv_w, conv_b, wa_bd, wx_bd, ba, bx, sp]
    if has_prev:
        in_specs.append(pl.BlockSpec((tq, nb, gw), lambda s: (cidx(s), 0, 0)))
        args.append(prev)
    kern = functools.partial(_lru_kernel, rev=rev, tq=tq, nb=nb, n_lat_chunks=n_lat_chunks,
                             n_chunks=n_chunks, has_prev=has_prev)
    return pl.pallas_call(
        kern,
        out_shape=jax.ShapeDtypeStruct((ltot, nb, gw), F32),
        grid=(n_chunks,),
        in_specs=in_specs,
        out_specs=pl.BlockSpec((tq, nb, gw), lambda s: (cidx(s), 0, 0)),
        scratch_shapes=[pltpu.VMEM((tq * nb, gw), F32),
                        pltpu.VMEM((tq * nb, gw), F32),
                        pltpu.VMEM((nb, gw), F32)],
        compiler_params=_params("arbitrary"),
        name="lru_rev" if rev else "lru_fwd",
    )(*args)


def _tri_mask(n, rev):
    t = lax.broadcasted_iota(I32, (n, n), 0)
    s = lax.broadcasted_iota(I32, (n, n), 1)
    return (s >= t) if rev else (s <= t)


def _hgrn_kernel(*refs, rev, tq, has_prev):
    if has_prev:
        q_ref, z_ref, i_ref, lb_ref, tri_ref, ones_ref, prev_ref, o_ref, st_scr, cum_scr, q_scr, k_scr, v_scr = refs
    else:
        q_ref, z_ref, i_ref, lb_ref, tri_ref, ones_ref, o_ref, st_scr, cum_scr, q_scr, k_scr, v_scr = refs
        prev_ref = None
    hps = HGRN_HEADS_PER_STEP
    cs, sb, w = GLA_CHUNK, GLA_SUB, min(HGRN_FAST_WINDOW, tq)
    nsub = cs // sb
    n_ck = tq // cs
    n_w = tq // w

    @pl.when(pl.program_id(2) == 0)
    def _():
        st_scr[...] = jnp.zeros_like(st_scr)

    lb = lb_ref[0]
    z = z_ref[0]
    iv = i_ref[0]
    f = lb + (1.0 - lb) * jax.nn.sigmoid(z)
    k_all = 1.0 - f
    v_all = iv * jax.nn.sigmoid(iv)
    q_all = q_ref[0] * (GLA_DIM ** -0.5)
    logf = jnp.log(f)
    lf_hi = logf.astype(BF16)
    rest = logf - lf_hi.astype(F32)
    lf_mid = rest.astype(BF16)
    lf_lo = (rest - lf_mid.astype(F32)).astype(BF16)
    tri = tri_ref[...]
    cum_all = _dot(tri, lf_hi) + _dot(tri, lf_mid) + _dot(tri, lf_lo)
    for hh in range(hps):
        lanes = slice(hh * GLA_DIM, (hh + 1) * GLA_DIM)
        cum_scr[hh] = cum_all[:, lanes]
        q_scr[hh] = q_all[:, lanes]
        k_scr[hh] = k_all[:, lanes]
        v_scr[hh] = v_all[:, lanes]

    def far_row(c, width):
        return c * width if rev else (c + 1) * width - 1

    def outside_row(c, width, n):
        if rev:
            return (c + 1) * width if c < n - 1 else None
        return c * width - 1 if c > 0 else None

    def store_out(hh, rows, out):
        lanes = slice(hh * GLA_DIM, (hh + 1) * GLA_DIM)
        if has_prev:
            out = out + prev_ref[0, rows, lanes]
        o_ref[0, rows, lanes] = out

    def block_factored(_):
        lane_id = lax.broadcasted_iota(I32, (1, tq), 1)
        for hh in range(hps):
            cumg, q, k = cum_scr[hh], q_scr[hh], k_scr[hh]
            vb = v_scr[hh].astype(BF16)
            st = st_scr[hh]
            ends = [cum_scr[hh, far_row(c, w):far_row(c, w) + 1, :] for c in range(n_w)]
            end_full = jnp.concatenate([jnp.broadcast_to(e, (w, GLA_DIM)) for e in ends], axis=0)
            ks = (k * jnp.exp(end_full - cumg)).astype(BF16)
            pieces, starts, row_lo = [], [], []
            total = 0
            for c in range(n_w):
                rows = slice(0, (c + 1) * w) if rev else slice(c * w, tq)
                scale = jnp.exp(jnp.minimum(cumg[rows] - ends[c], HGRN_SAFE_LOG_DECAY))
                pieces.append((q[rows] * scale).astype(BF16))
                starts.append(total)
                row_lo.append(rows.start)
                total += rows.stop - rows.start
            a_stack = _dot_nt(jnp.concatenate(pieces, axis=0), ks)
            a_rows = []
            for r in range(n_w):
                acc = jnp.zeros((w, tq), F32)
                for c in (range(r, n_w) if rev else range(0, r + 1)):
                    lo = starts[c] + r * w - row_lo[c]
                    keep = jnp.logical_and(lane_id >= c * w, lane_id < (c + 1) * w)
                    if c == r:
                        t_id = r * w + lax.broadcasted_iota(I32, (w, tq), 0)
                        keep = jnp.logical_and(keep, (lane_id >= t_id) if rev else (lane_id <= t_id))
                    acc = jnp.where(keep, a_stack[lo:lo + w, :], acc)
                a_rows.append(acc)
            a = jnp.concatenate(a_rows, axis=0).astype(BF16)
            out = _dot(a, vb) + _dot_nt((q * jnp.exp(cumg)).astype(BF16), st.astype(BF16))
            store_out(hh, slice(None), out)
            last = ends[0] if rev else ends[n_w - 1]
            kd = k * jnp.exp(last - cumg)
            st_scr[hh] = st * jnp.exp(last) + _dot_tn(vb, kd.astype(BF16))
        return 0

    row_id = lax.broadcasted_iota(I32, (sb, GLA_DIM), 0)
    valid = [(row_id <= s) if rev else (row_id >= s) for s in range(sb)]

    def chunk_direct(hh, r0):
        rows = pl.ds(r0, cs)
        q, k, cumg = q_scr[hh, rows, :], k_scr[hh, rows, :], cum_scr[hh, rows, :]
        vb = v_scr[hh, rows, :].astype(BF16)
        if rev:
            base = cum_scr[hh, pl.ds(jnp.minimum(r0 + cs, tq - 1), 1), :]
            base = jnp.where(r0 + cs >= tq, 0.0, base)
        else:
            base = cum_scr[hh, pl.ds(jnp.maximum(r0 - 1, 0), 1), :]
            base = jnp.where(r0 == 0, 0.0, base)
        cum = cumg - base
        last = cum[0:1] if rev else cum[cs - 1:cs]
        st = st_scr[hh]
        o_inter = _dot_nt((q * jnp.exp(cum)).astype(BF16), st.astype(BF16))
        outs = []
        for i in range(nsub):
            blk = slice(i * sb, (i + 1) * sb)
            off = slice((i + 1) * sb, cs) if rev else slice(0, i * sb)
            o_row = outside_row(i, sb, nsub)
            q_i, cum_i = q[blk], cum[blk]
            o_i = o_inter[blk]
            if o_row is not None:
                ref_row = cum[o_row:o_row + 1]
                qt = q_i * jnp.exp(cum_i - ref_row)
                kt = k[off] * jnp.exp(ref_row - cum[off])
                a_off = _dot_nt(qt.astype(BF16), kt.astype(BF16))
                o_i = o_i + _dot(a_off.astype(BF16), vb[off])
            prods = []
            for s in range(sb):
                r = r0 + i * sb + s
                e = jnp.exp(jnp.minimum(cumg[blk] - cum_scr[hh, pl.ds(r, 1), :], 0.0))
                prods.append(jnp.where(valid[s], q_i * e * k_scr[hh, pl.ds(r, 1), :], 0.0))
            sums = _dot(jnp.concatenate(prods, axis=0).astype(BF16), ones_ref[...])
            for s in range(sb):
                r = r0 + i * sb + s
                o_i = o_i + sums[s * sb:(s + 1) * sb] * v_scr[hh, pl.ds(r, 1), :]
            outs.append(o_i)
        store_out(hh, rows, jnp.concatenate(outs, axis=0))
        kd = k * jnp.exp(last - cum)
        st_scr[hh] = st * jnp.exp(last) + _dot_tn(vb, kd.astype(BF16))

    def block_direct(_):
        def chunk(ci, carry):
            c = (n_ck - 1 - ci) if rev else ci
            r0 = pl.multiple_of(c * cs, cs)
            for hh in range(hps):
                chunk_direct(hh, r0)
            return carry

        return lax.fori_loop(0, n_ck, chunk, 0)

    decays = []
    for c in range(n_w):
        far = cum_all[far_row(c, w):far_row(c, w) + 1]
        o_row = outside_row(c, w, n_w)
        decays.append(far if o_row is None else far - cum_all[o_row:o_row + 1])
    safe = jnp.min(jnp.concatenate(decays, axis=0)) > -HGRN_SAFE_LOG_DECAY
    lax.cond(safe, block_factored, block_direct, 0)


def _gla_specs(ltot, tq, n_lat, rev):
    n_chunks = ltot // tq
    n_lat_chunks = n_lat // tq
    cidx = lambda s: _scan_chunk_index(s, n_lat_chunks, n_chunks, rev)
    part = lambda p: pl.BlockSpec((1, tq, GLA_DIM), lambda b, h, s: (b, cidx(s), p * GLA_HEADS + h))
    return n_chunks, cidx, part


def _hgrn_direction(bm, lb, prev, rev, n_lat, tq):
    bsz, ltot, _ = bm.shape
    n_chunks, cidx, _ = _gla_specs(ltot, tq, n_lat, rev)
    has_prev = prev is not None
    hps = HGRN_HEADS_PER_STEP
    width = hps * GLA_DIM
    n_hsteps = GLA_HEADS // hps
    part = lambda p: pl.BlockSpec((1, tq, width), lambda b, h, s: (b, cidx(s), p * n_hsteps + h))
    tri = _tri_mask(tq, rev).astype(BF16)
    ones = jnp.ones((GLA_DIM, GLA_DIM), BF16)
    z_part = 2 if rev else 1
    in_specs = [part(0), part(z_part), part(3),
                pl.BlockSpec((1, 1, width), lambda b, h, s: (h, 0, 0)),
                pl.BlockSpec(tri.shape, lambda b, h, s: (0, 0)),
                pl.BlockSpec(ones.shape, lambda b, h, s: (0, 0))]
    args = [bm, bm, bm, lb.reshape(n_hsteps, 1, width), tri, ones]
    out_spec = pl.BlockSpec((1, tq, width), lambda b, h, s: (b, cidx(s), h))
    if has_prev:
        in_specs.append(out_spec)
        args.append(prev)
    kern = functools.partial(_hgrn_kernel, rev=rev, tq=tq, has_prev=has_prev)
    tile = pltpu.VMEM((hps, tq, GLA_DIM), F32)
    return pl.pallas_call(
        kern,
        out_shape=jax.ShapeDtypeStruct((bsz, ltot, GROUP_WIDTH), F32),
        grid=(bsz, n_hsteps, n_chunks),
        in_specs=in_specs,
        out_specs=out_spec,
        scratch_shapes=[pltpu.VMEM((hps, GLA_DIM, GLA_DIM), F32), tile, tile, tile, tile],
        compiler_params=_params("arbitrary", "arbitrary", "arbitrary"),
        name="hgrn_rev" if rev else "hgrn_fwd",
    )(*args)


def _ret_kernel(*refs, rev, tq, has_prev):
    if has_prev:
        q_ref, k_ref, v_ref, cos_ref, sin_ref, eq_ref, ek_ref, dm_ref, el_ref, prev_ref, o_ref, st_scr = refs
    else:
        q_ref, k_ref, v_ref, cos_ref, sin_ref, eq_ref, ek_ref, dm_ref, el_ref, o_ref, st_scr = refs
        prev_ref = None
    cs = min(tq, RET_CHUNK)
    n_ck = tq // cs

    @pl.when(pl.program_id(1) == 0)
    def _():
        st_scr[...] = jnp.zeros_like(st_scr)

    def chunk(ci, carry):
        c = (n_ck - 1 - ci) if rev else ci
        r0 = pl.multiple_of(c * cs, cs)
        rows = pl.ds(r0, cs)
        cos = cos_ref[rows, :]
        sin = sin_ref[rows, :]
        for h in range(GLA_HEADS):
            lanes = slice(h * GLA_DIM, (h + 1) * GLA_DIM)
            q = q_ref[0, rows, lanes]
            k = k_ref[0, rows, lanes]
            vb = v_ref[0, rows, lanes].astype(BF16)
            q = q * cos + pltpu.roll(q, GLA_DIM // 2, 1) * sin
            k = (k * cos + pltpu.roll(k, GLA_DIM // 2, 1) * sin) * (GLA_DIM ** -0.5)
            st = st_scr[h]
            scores = _dot_nt(q.astype(BF16), k.astype(BF16)) * dm_ref[h]
            out = _dot_nt((q * eq_ref[h]).astype(BF16), st.astype(BF16)) + _dot(scores.astype(BF16), vb)
            if has_prev:
                out = out + prev_ref[0, rows, lanes]
            o_ref[0, rows, lanes] = out
            st_scr[h] = st * el_ref[h] + _dot_tn(vb, (k * ek_ref[h]).astype(BF16))
        return carry

    lax.fori_loop(0, n_ck, chunk, 0)


def _ret_tables(rev, cs):
    gamma = jnp.log1p(-jnp.exp2(-(RET_DECAY_EXP[1 if rev else 0] + jnp.arange(GLA_HEADS, dtype=F32))))
    t = jnp.arange(cs, dtype=F32)
    steps = (cs - t) if rev else (t + 1.0)
    cum = gamma[:, None] * steps[None, :]
    last = gamma * cs
    eq = jnp.broadcast_to(jnp.exp(cum)[:, :, None], (GLA_HEADS, cs, GLA_DIM))
    ek = jnp.broadcast_to(jnp.exp(last[:, None] - cum)[:, :, None], (GLA_HEADS, cs, GLA_DIM))
    rel = cum[:, :, None] - cum[:, None, :]
    dm = jnp.where(_tri_mask(cs, rev)[None], jnp.exp(jnp.minimum(rel, 0.0)), 0.0)
    el = jnp.broadcast_to(jnp.exp(last)[:, None, None], (GLA_HEADS, 1, GLA_DIM))
    return eq, ek, dm, el


def _rotary_tables(n_lat, n_ctx):
    rows = n_lat // GRID_W
    row = jnp.repeat(jnp.arange(rows, dtype=F32), GRID_W)
    col = jnp.tile(jnp.arange(GRID_W, dtype=F32), rows)
    quarter = GLA_DIM // 4
    inv_freq = ROPE_BASE ** (-jnp.arange(quarter, dtype=F32) / quarter)
    ang = jnp.concatenate([row[:, None] * inv_freq, col[:, None] * inv_freq], axis=-1)
    cos, sin = jnp.cos(ang), jnp.sin(ang)
    cos_l = jnp.concatenate([cos, cos], axis=-1)
    sin_l = jnp.concatenate([-sin, sin], axis=-1)
    cos_t = jnp.concatenate([cos_l, jnp.ones((n_ctx, GLA_DIM), F32)], axis=0)
    sin_t = jnp.concatenate([sin_l, jnp.zeros((n_ctx, GLA_DIM), F32)], axis=0)
    return cos_t, sin_t


def _ret_direction(bm, cos_t, sin_t, prev, rev, n_lat, tq):
    bsz, ltot, _ = bm.shape
    n_chunks, cidx, _ = _gla_specs(ltot, tq, n_lat, rev)
    has_prev = prev is not None
    eq, ek, dm, el = _ret_tables(rev, min(tq, RET_CHUNK))
    gw = GROUP_WIDTH
    part = lambda p: pl.BlockSpec((1, tq, gw), lambda b, s: (b, cidx(s), p))
    whole = lambda a: pl.BlockSpec(a.shape, lambda b, s: (0, 0, 0))
    pos = pl.BlockSpec((tq, GLA_DIM), lambda b, s: (cidx(s), 0))
    in_specs = [part(5), part(6), part(7), pos, pos, whole(eq), whole(ek), whole(dm), whole(el)]
    args = [bm, bm, bm, cos_t, sin_t, eq, ek, dm, el]
    out_spec = pl.BlockSpec((1, tq, gw), lambda b, s: (b, cidx(s), 0))
    if has_prev:
        in_specs.append(out_spec)
        args.append(prev)
    kern = functools.partial(_ret_kernel, rev=rev, tq=tq, has_prev=has_prev)
    return pl.pallas_call(
        kern,
        out_shape=jax.ShapeDtypeStruct((bsz, ltot, gw), F32),
        grid=(bsz, n_chunks),
        in_specs=in_specs,
        out_specs=out_spec,
        scratch_shapes=[pltpu.VMEM((GLA_HEADS, GLA_DIM, GLA_DIM), F32)],
        compiler_params=_params("arbitrary", "arbitrary"),
        name="ret_rev" if rev else "ret_fwd",
    )(*args)


def _head_rms(o):
    parts = []
    for h in range(GLA_HEADS):
        oh = o[:, h * GLA_DIM:(h + 1) * GLA_DIM]
        parts.append(oh * lax.rsqrt(jnp.mean(oh * oh, axis=-1, keepdims=True) + EPS))
    return jnp.concatenate(parts, axis=-1)


def _mix_kernel(u_ref, lg_ref, ys_ref, hs_ref, hg_ref, rg_ref, oh_ref, or_ref, d_ref, gw_ref, gb_ref, o_ref):
    gw = GROUP_WIDTH
    y = jax.nn.gelu(d_ref[...] * u_ref[...] + ys_ref[...])
    a = y * jax.nn.sigmoid(_dot(y.astype(BF16), gw_ref[...]) + gb_ref[...])
    o_ref[0, :, 0:gw] = a.astype(BF16)
    o_ref[0, :, gw:2 * gw] = (hs_ref[...] * jax.nn.gelu(lg_ref[...])).astype(BF16)
    g = hg_ref[0]
    o_ref[0, :, 2 * gw:3 * gw] = (_head_rms(oh_ref[0]) * (g * jax.nn.sigmoid(g))).astype(BF16)
    g = rg_ref[0]
    o_ref[0, :, 3 * gw:4 * gw] = (_head_rms(or_ref[0]) * (g * jax.nn.sigmoid(g))).astype(BF16)


def _mix_epilogue(tm2, ys2, hs2, bm, o_h, o_r, s5_d, glu_w, glu_b, n_rows, tm):
    bsz, ltot, _ = bm.shape
    gw = GROUP_WIDTH
    tmaj = lambda p, k: pl.BlockSpec((tm, gw), lambda b, i: (i, b * k + p))
    bmaj = lambda p: pl.BlockSpec((1, tm, gw), lambda b, i: (b, i, p))
    full2 = lambda b, i: (0, 0)
    return pl.pallas_call(
        _mix_kernel,
        out_shape=jax.ShapeDtypeStruct((bsz, ltot, 4 * gw), BF16),
        grid=(bsz, n_rows // tm),
        in_specs=[tmaj(0, N_TM_PARTS), tmaj(2, N_TM_PARTS), tmaj(0, 1), tmaj(0, 1),
                  bmaj(4), bmaj(8), bmaj(0), bmaj(0),
                  pl.BlockSpec((1, gw), full2), pl.BlockSpec((gw, gw), full2), pl.BlockSpec((1, gw), full2)],
        out_specs=pl.BlockSpec((1, tm, 4 * gw), lambda b, i: (b, i, 0)),
        compiler_params=_params("arbitrary", "arbitrary"),
        name="mix_epilogue",
    )(tm2, tm2, ys2, hs2, bm, bm, o_h, o_r, s5_d, glu_w, glu_b)


def _gate_rows(modb_ref, modc_ref, idx, row0, tm, n_lat):
    row = row0 + lax.broadcasted_iota(I32, (tm, 1), 0)
    return jnp.where(row >= n_lat, modc_ref[0, idx:idx + 1, :], modb_ref[0, idx:idx + 1, :])


def _outproj_kernel(a_ref, w_ref, x_ref, modb_ref, modc_ref, o_ref, *, tm, n_lat):
    gate = _gate_rows(modb_ref, modc_ref, 2, pl.program_id(1) * tm, tm, n_lat)
    o_ref[0] = x_ref[0] + gate * _dot(a_ref[0], w_ref[0])


def _out_projection(mix, w_bf16, layer, xs, modv, n_rows, n_lat, tm, tn):
    bsz, ltot, d = xs.shape
    k = mix.shape[-1]
    kern = functools.partial(_outproj_kernel, tm=tm, n_lat=n_lat)
    return pl.pallas_call(
        kern,
        out_shape=jax.ShapeDtypeStruct((bsz, ltot, d), F32),
        grid=(bsz, n_rows // tm, d // tn),
        in_specs=[pl.BlockSpec((1, tm, k), lambda b, i, j: (b, i, 0)),
                  pl.BlockSpec((1, k, tn), lambda b, i, j: (layer, 0, j)),
                  pl.BlockSpec((1, tm, tn), lambda b, i, j: (b, i, j)),
                  pl.BlockSpec((1, 6, tn), lambda b, i, j: (b, 0, j)),
                  pl.BlockSpec((1, 6, tn), lambda b, i, j: (bsz, 0, j))],
        out_specs=pl.BlockSpec((1, tm, tn), lambda b, i, j: (b, i, j)),
        compiler_params=_params("arbitrary", "arbitrary", "arbitrary"),
        name="out_projection",
    )(mix, w_bf16, xs, modv, modv)


def _first_max4(v):
    m1 = jnp.maximum(jnp.maximum(v[0], v[1]), jnp.maximum(v[2], v[3]))
    i1 = jnp.where(v[0] == m1, 0, jnp.where(v[1] == m1, 1, jnp.where(v[2] == m1, 2, 3)))
    rest = [jnp.where(i1 == j, -jnp.inf, v[j]) for j in range(4)]
    m2 = jnp.maximum(jnp.maximum(rest[0], rest[1]), jnp.maximum(rest[2], rest[3]))
    i2 = jnp.where(rest[0] == m2, 0, jnp.where(rest[1] == m2, 1, jnp.where(rest[2] == m2, 2, 3)))
    return m1, i1, m2, i2


def _router_kernel(x_ref, g_ref, modb_ref, modc_ref, rw_ref, rb_ref, tri_ref,
                   h_ref, ids_ref, gates_ref, ranks_ref, counts_ref, base_scr, *, tm, n_lat):
    first = jnp.logical_and(pl.program_id(0) == 0, pl.program_id(1) == 0)

    @pl.when(first)
    def _():
        base_scr[...] = jnp.zeros_like(base_scr)

    h = _norm_modulate(x_ref[0], g_ref[...], modb_ref, modc_ref, pl.program_id(1) * tm, n_lat, 3, 4)
    h_ref[0] = h.astype(BF16)
    logits = lax.dot_general(rw_ref[...], h, (((1,), (1,)), ((), ())), preferred_element_type=F32,
                             precision=lax.Precision.HIGHEST)
    scores = jax.nn.sigmoid(logits)
    biased = scores + rb_ref[...]
    tops = []
    for gidx in range(N_EXPERT_GROUPS):
        rows = [biased[gidx * EXPERTS_PER_GROUP + j:gidx * EXPERTS_PER_GROUP + j + 1, :]
                for j in range(EXPERTS_PER_GROUP)]
        tops.append(_first_max4(rows))
    gs = [t[0] + t[2] for t in tops]
    gmax = jnp.maximum(jnp.maximum(gs[0], gs[1]), jnp.maximum(gs[2], gs[3]))
    best = jnp.where(gs[0] == gmax, 0, jnp.where(gs[1] == gmax, 1, jnp.where(gs[2] == gmax, 2, 3)))
    e1 = jnp.zeros_like(best)
    e2 = jnp.zeros_like(best)
    for gidx in range(N_EXPERT_GROUPS):
        e1 = jnp.where(best == gidx, gidx * EXPERTS_PER_GROUP + tops[gidx][1], e1)
        e2 = jnp.where(best == gidx, gidx * EXPERTS_PER_GROUP + tops[gidx][3], e2)
    eid = lax.broadcasted_iota(I32, (N_EXPERTS, tm), 0)
    sel1 = eid == e1
    sel2 = eid == e2
    w1 = jnp.sum(jnp.where(sel1, scores, 0.0), axis=0, keepdims=True)
    w2 = jnp.sum(jnp.where(sel2, scores, 0.0), axis=0, keepdims=True)
    wsum = w1 + w2
    onehot = jnp.where(jnp.logical_or(sel1, sel2), 1.0, 0.0)
    pos = base_scr[...] + _dot(onehot.astype(BF16), tri_ref[...])
    r1 = jnp.sum(jnp.where(sel1, pos, 0.0), axis=0, keepdims=True)
    r2 = jnp.sum(jnp.where(sel2, pos, 0.0), axis=0, keepdims=True)
    ids_ref[0] = jnp.concatenate([e1, e2], axis=0)
    gates_ref[0] = jnp.concatenate([w1 / wsum, w2 / wsum], axis=0)
    ranks_ref[0] = jnp.concatenate([r1, r2], axis=0).astype(I32)
    base = base_scr[...] + jnp.sum(onehot, axis=1, keepdims=True)
    base_scr[...] = base
    counts_ref[...] = jnp.broadcast_to(base, counts_ref.shape)


def _router(xs, gain, modv, rw_t, rbias, n_rows, n_lat, tm):
    bsz, ltot, d = xs.shape
    tri = (lax.broadcasted_iota(I32, (tm, tm), 0) < lax.broadcasted_iota(I32, (tm, tm), 1)).astype(BF16)
    kern = functools.partial(_router_kernel, tm=tm, n_lat=n_lat)
    small = lambda dt: jax.ShapeDtypeStruct((bsz, 2, n_rows), dt)
    small_spec = pl.BlockSpec((1, 2, tm), lambda b, i: (b, 0, i))
    return pl.pallas_call(
        kern,
        out_shape=(jax.ShapeDtypeStruct((bsz, ltot, d), BF16), small(I32), small(F32), small(I32),
                   jax.ShapeDtypeStruct((N_EXPERTS, 128), F32)),
        grid=(bsz, n_rows // tm),
        in_specs=[pl.BlockSpec((1, tm, d), lambda b, i: (b, i, 0)),
                  pl.BlockSpec((1, d), lambda b, i: (0, 0)),
                  pl.BlockSpec((1, 6, d), lambda b, i: (b, 0, 0)),
                  pl.BlockSpec((1, 6, d), lambda b, i: (bsz, 0, 0)),
                  pl.BlockSpec((N_EXPERTS, d), lambda b, i: (0, 0)),
                  pl.BlockSpec((N_EXPERTS, 1), lambda b, i: (0, 0)),
                  pl.BlockSpec((tm, tm), lambda b, i: (0, 0))],
        out_specs=(pl.BlockSpec((1, tm, d), lambda b, i: (b, i, 0)), small_spec, small_spec, small_spec,
                   pl.BlockSpec((N_EXPERTS, 128), lambda b, i: (0, 0))),
        scratch_shapes=[pltpu.VMEM((N_EXPERTS, 1), F32)],
        compiler_params=_params("arbitrary", "arbitrary"),
        name="moe_router",
    )(xs, gain, modv, modv, rw_t, rbias, tri)


def _expert_ffn_kernel(te_ref, nt_ref, x_ref, wg_ref, wu_ref, wd_ref, o_ref):
    live = pl.program_id(0) < nt_ref[0]

    @pl.when(live)
    def _():
        x = x_ref[...]
        g = _dot(x, wg_ref[0])
        u = _dot(x, wu_ref[0])
        he = (g * jax.nn.sigmoid(g)) * u
        o_ref[...] = _dot(he.astype(BF16), wd_ref[0])

    @pl.when(jnp.logical_not(live))
    def _():
        o_ref[...] = jnp.zeros_like(o_ref)


def _expert_ffn(tile_expert, n_tiles, x_sorted, wg, wu, wd, tm):
    p, d = x_sorted.shape
    f = wg.shape[-1]
    tile = lambda i, te, nt: i
    expert = lambda i, te, nt: te[jnp.minimum(i, nt[0] - 1)]
    grid_spec = pltpu.PrefetchScalarGridSpec(
        num_scalar_prefetch=2,
        grid=(p // tm,),
        in_specs=[pl.BlockSpec((tm, d), lambda i, te, nt: (tile(i, te, nt), 0)),
                  pl.BlockSpec((1, d, f), lambda i, te, nt: (expert(i, te, nt), 0, 0)),
                  pl.BlockSpec((1, d, f), lambda i, te, nt: (expert(i, te, nt), 0, 0)),
                  pl.BlockSpec((1, f, d), lambda i, te, nt: (expert(i, te, nt), 0, 0))],
        out_specs=pl.BlockSpec((tm, d), lambda i, te, nt: (tile(i, te, nt), 0)),
    )
    return pl.pallas_call(
        _expert_ffn_kernel,
        out_shape=jax.ShapeDtypeStruct((p, d), F32),
        grid_spec=grid_spec,
        compiler_params=_params("arbitrary"),
        name="moe_expert_ffn",
    )(tile_expert, n_tiles, x_sorted, wg, wu, wd)


def _moe_residual_kernel(x_ref, y_ref, gates_ref, modb_ref, modc_ref, g_ref, o_ref, *, tm, n_lat, final_norm):
    d = x_ref.shape[-1]
    gates = gates_ref[0]
    y = gates[:, 0:1] * y_ref[0, :, 0:d] + gates[:, 1:2] * y_ref[0, :, d:2 * d]
    gate = _gate_rows(modb_ref, modc_ref, 5, pl.program_id(1) * tm, tm, n_lat)
    x = x_ref[0] + gate * y
    if final_norm:
        x = x * lax.rsqrt(jnp.mean(x * x, axis=-1, keepdims=True) + EPS) * g_ref[...]
    o_ref[0] = x


def _moe_residual(xs, y_pair, gates_t, modv, final_g, n_rows, n_lat, tm, final_norm):
    bsz, ltot, d = xs.shape
    kern = functools.partial(_moe_residual_kernel, tm=tm, n_lat=n_lat, final_norm=final_norm)
    rows_out = n_rows if final_norm else ltot
    return pl.pallas_call(
        kern,
        out_shape=jax.ShapeDtypeStruct((bsz, rows_out, d), F32),
        grid=(bsz, n_rows // tm),
        in_specs=[pl.BlockSpec((1, tm, d), lambda b, i: (b, i, 0)),
                  pl.BlockSpec((1, tm, 2 * d), lambda b, i: (b, i, 0)),
                  pl.BlockSpec((1, tm, 2), lambda b, i: (b, i, 0)),
                  pl.BlockSpec((1, 6, d), lambda b, i: (b, 0, 0)),
                  pl.BlockSpec((1, 6, d), lambda b, i: (bsz, 0, 0)),
                  pl.BlockSpec((1, d), lambda b, i: (0, 0))],
        out_specs=pl.BlockSpec((1, tm, d), lambda b, i: (b, i, 0)),
        compiler_params=_params("arbitrary", "arbitrary"),
        name="moe_residual",
    )(xs, y_pair, gates_t, modv, modv, final_g)


def _moe_layer(xs, gain, modv, rw_t, rbias, wg, wu, wd, expert_base, final_g, n_rows, n_lat, tm_tok, tm_res, tm_exp,
               final_norm):
    bsz, ltot, d = xs.shape
    h, ids, gates, ranks, counts = _router(xs, gain, modv, rw_t, rbias, n_rows, n_lat, tm_tok)
    counts = counts[:, 0].astype(I32)
    padded = ((counts + tm_exp - 1) // tm_exp) * tm_exp
    ends = jnp.cumsum(padded)
    offsets = ends - padded
    dest = offsets[ids] + ranks
    n_slots = bsz * 2 * n_rows
    p_rows = ((n_slots + N_EXPERTS * (tm_exp - 1)) + tm_exp - 1) // tm_exp * tm_exp
    tok = (jnp.arange(bsz, dtype=I32)[:, None, None] * ltot + jnp.arange(n_rows, dtype=I32)[None, None, :])
    tok = jnp.broadcast_to(tok, dest.shape)
    src = jnp.zeros((p_rows,), I32).at[dest.reshape(-1)].set(tok.reshape(-1))
    tile_start = jnp.arange(p_rows // tm_exp, dtype=I32) * tm_exp
    tile_expert = jnp.minimum(jnp.sum(tile_start[:, None] >= ends[None, :], axis=1), N_EXPERTS - 1).astype(I32)
    n_tiles = (ends[-1:] // tm_exp).astype(I32)
    x_sorted = jnp.take(h.reshape(bsz * ltot, d), src, axis=0)
    y_sorted = _expert_ffn(tile_expert + expert_base, n_tiles, x_sorted, wg, wu, wd, tm_exp)
    y_pair = jnp.take(y_sorted, dest.transpose(0, 2, 1).reshape(-1), axis=0).reshape(bsz, n_rows, 2 * d)
    return _moe_residual(xs, y_pair, gates.transpose(0, 2, 1), modv, final_g, n_rows, n_lat, tm_res, final_norm)


def _row_tile(n, target):
    t = min(n, target)
    while n % t or t % 8:
        t -= 8
    return t


def kernel(x, c, ctx, c_ctx, norm_mix_g, norm_ffn_g, w_mod, b_mod, w_in, w_out, s5_lam_re, s5_lam_im, s5_log_dt, s5_b_re, s5_b_im, s5_c_re, s5_c_im, s5_d, s5_glu_w, s5_glu_b, lru_conv_w, lru_conv_b, lru_wa, lru_ba, lru_wx, lru_bx, lru_lam, hgrn_lb_logits, router_w, router_bias, moe_w_gate, moe_w_up, moe_w_down, final_norm_g):
    bsz, n_lat, d = x.shape
    n_ctx = ctx.shape[1]
    ltot = n_lat + n_ctx
    depth = w_in.shape[0]
    gw = GROUP_WIDTH
    assert bsz % 8 == 0 and bsz + 1 <= MOD_ROWS
    assert n_lat % GLA_CHUNK == 0 and n_ctx % GLA_CHUNK == 0 and n_lat % GRID_W == 0

    tq_scan = math.gcd(math.gcd(n_lat, n_ctx), 64)
    tq_gla = math.gcd(math.gcd(n_lat, n_ctx), 256)
    tm_all = _row_tile(ltot, 768)
    tm_lat = _row_tile(n_lat, 1024)
    tm_mix = math.gcd(math.gcd(n_lat, n_ctx), 256)

    xs = jnp.concatenate([x, ctx.astype(x.dtype)], axis=1)
    cv = jnp.zeros((MOD_ROWS, d), F32).at[:bsz].set(c).at[bsz].set(c_ctx)
    mod_all = _mod_vectors(cv, w_mod, b_mod).reshape(depth, MOD_ROWS, 6, d)
    cos_t, sin_t = _rotary_tables(n_lat, n_ctx)
    rw_t = router_w.astype(F32).T
    rbias = router_bias.astype(F32).reshape(N_EXPERTS, 1)
    lb_cum = [jnp.cumsum(jax.nn.softmax(hgrn_lb_logits[dr].astype(F32), axis=0), axis=0) for dr in (0, 1)]
    w_in_b, w_out_b = w_in.astype(BF16), w_out.astype(BF16)
    n_exp, _, d_ff = moe_w_gate.shape[1:]
    wg_b = moe_w_gate.astype(BF16).reshape(depth * n_exp, d, d_ff)
    wu_b = moe_w_up.astype(BF16).reshape(depth * n_exp, d, d_ff)
    wd_b = moe_w_down.astype(BF16).reshape(depth * n_exp, d_ff, d)

    for layer in range(depth):
        ctx_out = layer < depth - 1
        n_rows = ltot if ctx_out else n_lat
        tm_tok = tm_all if ctx_out else tm_lat
        modv = mod_all[layer]

        tm2, bm = _in_projection(xs, norm_mix_g[layer].reshape(1, d), modv, w_in_b, layer, n_lat, tm_all)
        tm3 = tm2.reshape(ltot, bsz, N_TM_PARTS * gw)

        ys = hs = o_h = o_r = None
        for dr in (0, 1):
            rev = dr == 1
            bw, cw, atab = _s5_tables(s5_lam_re[layer, dr], s5_lam_im[layer, dr], s5_log_dt[layer, dr],
                                      s5_b_re[layer, dr], s5_b_im[layer, dr], s5_c_re[layer, dr], s5_c_im[layer, dr])
            ys = _s5_direction(tm3, bw, cw, atab, ys, rev, n_lat, tq_scan)
            sp = jax.nn.softplus(-lru_lam[layer, dr].astype(F32)).reshape(1, gw)
            hs = _lru_direction(tm3, lru_conv_w[layer].astype(F32), lru_conv_b[layer].astype(F32).reshape(1, gw),
                                _block_diag(lru_wa[layer, dr]).astype(BF16), _block_diag(lru_wx[layer, dr]).astype(BF16),
                                lru_ba[layer, dr].astype(F32).reshape(1, gw), lru_bx[layer, dr].astype(F32).reshape(1, gw),
                                sp, hs, rev, n_lat, tq_scan)
            lb = lb_cum[dr][layer] - lb_cum[dr][0]
            o_h = _hgrn_direction(bm, lb, o_h, rev, n_lat, tq_gla)
            o_r = _ret_direction(bm, cos_t, sin_t, o_r, rev, n_lat, tq_gla)

        mix = _mix_epilogue(tm2, ys.reshape(ltot, bsz * gw), hs.reshape(ltot, bsz * gw), bm, o_h, o_r,
                            s5_d[layer].astype(F32).reshape(1, gw), s5_glu_w[layer].astype(BF16),
                            s5_glu_b[layer].astype(F32).reshape(1, gw), n_rows, tm_mix)
        xs = _out_projection(mix, w_out_b, layer, xs, modv, n_rows, n_lat, tm_tok, 1024)
        xs = _moe_layer(xs, norm_ffn_g[layer].reshape(1, d), modv, rw_t, rbias, wg_b, wu_b, wd_b, layer * n_exp,
                        final_norm_g.reshape(1, d), n_rows, n_lat, tm_tok, tm_mix, 512, not ctx_out)
    return xs
```

```python
import functools
import math

import jax
import jax.numpy as jnp
from jax import lax
from jax.experimental import pallas as pl
from jax.experimental.pallas import tpu as pltpu

F32 = jnp.float32
BF16 = jnp.bfloat16
I32 = jnp.int32

GROUP_WIDTH = 512
N_IN_PARTS = 12
N_TM_PARTS = 3
N_BM_PARTS = N_IN_PARTS - N_TM_PARTS
S5_CH = 16
S5_GROUPS = 32
S5_STATE = 64
S5_LANE_BLOCKS = 4
LRU_HEADS = 8
LRU_CONV = 4
LRU_C = 8.0
GLA_HEADS = 4
GLA_DIM = 128
GLA_CHUNK = 64
GLA_SUB = 16
HGRN_FAST_WINDOW = 32
HGRN_SAFE_LOG_DECAY = 60.0
HGRN_HEADS_PER_STEP = 2
RET_CHUNK = 256
RET_DECAY_EXP = (5.0, 5.5)
ROPE_BASE = 10000.0
GRID_W = 64
N_EXPERTS = 16
N_EXPERT_GROUPS = 4
EXPERTS_PER_GROUP = 4
D_FF_EXPERT = 1024
EPS = 1e-6
NORM_SLAB = 16
MOD_ROWS = 24

VMEM_LIMIT_BYTES = 56 * 1024 * 1024


def _params(*semantics):
    return pltpu.CompilerParams(dimension_semantics=semantics, vmem_limit_bytes=VMEM_LIMIT_BYTES)


def _dot(a, b):
    return jnp.dot(a, b, preferred_element_type=F32)


def _dot_nt(a, b):
    return lax.dot_general(a, b, (((1,), (1,)), ((), ())), preferred_element_type=F32)


def _dot_tn(a, b):
    return lax.dot_general(a, b, (((0,), (0,)), ((), ())), preferred_element_type=F32)


def _scan_chunk_index(step, n_lat_chunks, n_chunks, rev):
    if rev:
        return n_chunks - 1 - step
    return (step + n_lat_chunks) % n_chunks


def _mod_kernel(cv_ref, w_ref, b_ref, o_ref):
    cv = cv_ref[...]
    s = cv * jax.nn.sigmoid(cv)
    o_ref[0] = _dot(s.astype(BF16), w_ref[0].astype(BF16)) + b_ref[0]


def _mod_vectors(cv, w_mod, b_mod):
    nl, d, n6 = w_mod.shape
    tn = 1024
    return pl.pallas_call(
        _mod_kernel,
        out_shape=jax.ShapeDtypeStruct((nl, MOD_ROWS, n6), F32),
        grid=(nl, n6 // tn),
        in_specs=[pl.BlockSpec((MOD_ROWS, d), lambda l, j: (0, 0)),
                  pl.BlockSpec((1, d, tn), lambda l, j: (l, 0, j)),
                  pl.BlockSpec((1, 1, tn), lambda l, j: (l, 0, j))],
        out_specs=pl.BlockSpec((1, MOD_ROWS, tn), lambda l, j: (l, 0, j)),
        compiler_params=_params("arbitrary", "arbitrary"),
        name="mod_vectors",
    )(cv, w_mod, b_mod.reshape(nl, 1, n6))


def _norm_modulate(x, g, modb_ref, modc_ref, row0, n_lat, shift_idx, scale_idx):
    tm = x.shape[0]
    ms = jnp.mean(x * x, axis=-1, keepdims=True)
    y = x * lax.rsqrt(ms + EPS) * g
    row = row0 + lax.broadcasted_iota(I32, (tm, 1), 0)
    is_ctx = row >= n_lat
    shift = jnp.where(is_ctx, modc_ref[0, shift_idx:shift_idx + 1, :], modb_ref[0, shift_idx:shift_idx + 1, :])
    scale = jnp.where(is_ctx, modc_ref[0, scale_idx:scale_idx + 1, :], modb_ref[0, scale_idx:scale_idx + 1, :])
    return y * (1.0 + scale) + shift


def _inproj_kernel(x_ref, g_ref, modb_ref, modc_ref, w_ref, otm_ref, obm_ref, h_scr, *, tm, n_lat):
    i = pl.program_id(1)
    j = pl.program_id(2)

    @pl.when(j == 0)
    def _():
        def slab(s, carry):
            r0 = pl.multiple_of(s * NORM_SLAB, NORM_SLAB)
            h = _norm_modulate(x_ref[0, pl.ds(r0, NORM_SLAB), :], g_ref[...], modb_ref, modc_ref,
                               i * tm + r0, n_lat, 0, 1)
            h_scr[pl.ds(r0, NORM_SLAB), :] = h.astype(BF16)
            return carry

        lax.fori_loop(0, tm // NORM_SLAB, slab, 0, unroll=2)

    r = _dot(h_scr[...], w_ref[0])

    @pl.when(j == 0)
    def _():
        otm_ref[...] = r

    @pl.when(j > 0)
    def _():
        obm_ref[0] = r


def _in_projection(xs, gain, modv, w_bf16, layer, n_lat, tm):
    bsz, ltot, d = xs.shape
    gw = GROUP_WIDTH
    kern = functools.partial(_inproj_kernel, tm=tm, n_lat=n_lat)
    tn = N_TM_PARTS * gw
    return pl.pallas_call(
        kern,
        out_shape=(jax.ShapeDtypeStruct((ltot, bsz * tn), F32),
                   jax.ShapeDtypeStruct((bsz, ltot, N_BM_PARTS * gw), F32)),
        grid=(bsz, ltot // tm, N_IN_PARTS * gw // tn),
        in_specs=[pl.BlockSpec((1, tm, d), lambda b, i, j: (b, i, 0)),
                  pl.BlockSpec((1, d), lambda b, i, j: (0, 0)),
                  pl.BlockSpec((1, 6, d), lambda b, i, j: (b, 0, 0)),
                  pl.BlockSpec((1, 6, d), lambda b, i, j: (bsz, 0, 0)),
                  pl.BlockSpec((1, d, tn), lambda b, i, j: (layer, 0, j))],
        out_specs=(pl.BlockSpec((tm, tn), lambda b, i, j: (i, b)),
                   pl.BlockSpec((1, tm, tn), lambda b, i, j: (b, i, jnp.maximum(j - 1, 0)))),
        scratch_shapes=[pltpu.VMEM((tm, d), BF16)],
        compiler_params=_params("arbitrary", "arbitrary", "arbitrary"),
        name="in_projection",
    )(xs, gain, modv, modv, w_bf16)


def _s5_kernel(*refs, rev, tq, nb, has_prev):
    if has_prev:
        u_ref, bw_ref, cw_ref, a_ref, prev_ref, y_ref, z_scr, h_scr = refs
    else:
        u_ref, bw_ref, cw_ref, a_ref, y_ref, z_scr, h_scr = refs
        prev_ref = None
    rows = tq * nb
    half = S5_STATE * 8

    @pl.when(pl.program_id(0) == 0)
    def _():
        h_scr[...] = jnp.zeros_like(h_scr)

    for gb in range(S5_LANE_BLOCKS):
        lanes = slice(gb * 128, (gb + 1) * 128)
        ub = u_ref[:, :, lanes].reshape(rows, 128).astype(BF16)
        z_scr[...] = _dot(ub, bw_ref[gb])
        ar = jnp.broadcast_to(a_ref[gb, 0], (nb, half))
        ai = jnp.broadcast_to(a_ref[gb, 1], (nb, half))

        def step(i, carry):
            hr, hi = carry
            t = (tq - 1 - i) if rev else i
            r0 = pl.multiple_of(t * nb, nb)
            zr = z_scr[pl.ds(r0, nb), 0:half]
            zi = z_scr[pl.ds(r0, nb), half:2 * half]
            nr = ar * hr - ai * hi + zr
            ni = ar * hi + ai * hr + zi
            z_scr[pl.ds(r0, nb), 0:half] = nr
            z_scr[pl.ds(r0, nb), half:2 * half] = ni
            return nr, ni

        hr, hi = lax.fori_loop(0, tq, step, (h_scr[gb, 0], h_scr[gb, 1]), unroll=4)
        h_scr[gb, 0] = hr
        h_scr[gb, 1] = hi
        yb = _dot(z_scr[...].astype(BF16), cw_ref[gb]).reshape(tq, nb, 128)
        if has_prev:
            yb = yb + prev_ref[:, :, lanes]
        y_ref[:, :, lanes] = yb


def _s5_direction(tm3, bw, cw, atab, prev, rev, n_lat, tq):
    ltot, nb, _ = tm3.shape
    gw = GROUP_WIDTH
    n_chunks = ltot // tq
    n_lat_chunks = n_lat // tq
    cidx = lambda s: _scan_chunk_index(s, n_lat_chunks, n_chunks, rev)
    has_prev = prev is not None
    in_specs = [pl.BlockSpec((tq, nb, gw), lambda s: (cidx(s), 0, 0)),
                pl.BlockSpec(bw.shape, lambda s: (0, 0, 0)),
                pl.BlockSpec(cw.shape, lambda s: (0, 0, 0)),
                pl.BlockSpec(atab.shape, lambda s: (0, 0, 0, 0))]
    args = [tm3, bw, cw, atab]
    if has_prev:
        in_specs.append(pl.BlockSpec((tq, nb, gw), lambda s: (cidx(s), 0, 0)))
        args.append(prev)
    kern = functools.partial(_s5_kernel, rev=rev, tq=tq, nb=nb, has_prev=has_prev)
    return pl.pallas_call(
        kern,
        out_shape=jax.ShapeDtypeStruct((ltot, nb, gw), F32),
        grid=(n_chunks,),
        in_specs=in_specs,
        out_specs=pl.BlockSpec((tq, nb, gw), lambda s: (cidx(s), 0, 0)),
        scratch_shapes=[pltpu.VMEM((tq * nb, 2 * S5_STATE * 8), F32),
                        pltpu.VMEM((S5_LANE_BLOCKS, 2, nb, S5_STATE * 8), F32)],
        compiler_params=_params("arbitrary"),
        name="s5_rev" if rev else "s5_fwd",
    )(*args)


def _block_diag(blocks):
    n, r, c = blocks.shape
    eye = jnp.eye(n, dtype=blocks.dtype)
    return (eye[:, None, :, None] * blocks[:, :, None, :]).reshape(n * r, n * c)


def _s5_tables(lam_re, lam_im, log_dt, b_re, b_im, c_re, c_im):
    dt = jnp.exp(log_dt.astype(F32))[:, None]
    lr, li = lam_re.astype(F32), lam_im.astype(F32)
    mag = jnp.exp(lr * dt)
    abar_re, abar_im = mag * jnp.cos(li * dt), mag * jnp.sin(li * dt)
    den = lr * lr + li * li
    zr = abar_re - 1.0
    w_re = (zr * lr + abar_im * li) / den
    w_im = (abar_im * lr - zr * li) / den
    bre, bim = b_re.astype(F32), b_im.astype(F32)
    bw_re = w_re[:, None, :] * bre - w_im[:, None, :] * bim
    bw_im = w_re[:, None, :] * bim + w_im[:, None, :] * bre
    gpb = S5_GROUPS // S5_LANE_BLOCKS
    bws, cws, atabs = [], [], []
    for gb in range(S5_LANE_BLOCKS):
        sl = slice(gb * gpb, (gb + 1) * gpb)
        bws.append(jnp.concatenate([_block_diag(bw_re[sl]), _block_diag(bw_im[sl])], axis=1))
        cws.append(jnp.concatenate([_block_diag(c_re[sl].astype(F32)), -_block_diag(c_im[sl].astype(F32))], axis=0))
        atabs.append(jnp.stack([abar_re[sl].reshape(1, -1), abar_im[sl].reshape(1, -1)]))
    return jnp.stack(bws).astype(BF16), jnp.stack(cws).astype(BF16), jnp.stack(atabs)


def _lru_kernel(*refs, rev, tq, nb, n_lat_chunks, n_chunks, has_prev):
    if has_prev:
        x_ref, xp_ref, xn_ref, cw_ref, cb_ref, wa_ref, wx_ref, ba_ref, bx_ref, sp_ref, prev_ref, o_ref, a_scr, b_scr, h_scr = refs
    else:
        x_ref, xp_ref, xn_ref, cw_ref, cb_ref, wa_ref, wx_ref, ba_ref, bx_ref, sp_ref, o_ref, a_scr, b_scr, h_scr = refs
        prev_ref = None
    s = pl.program_id(0)
    gw = GROUP_WIDTH
    rows = tq * nb

    @pl.when(s == 0)
    def _():
        h_scr[...] = jnp.zeros_like(h_scr)

    c = _scan_chunk_index(s, n_lat_chunks, n_chunks, rev)
    first = jnp.logical_or(c == 0, c == n_lat_chunks)
    last = jnp.logical_or(c == n_lat_chunks - 1, c == n_chunks - 1)
    keep_prev = jnp.where(first, 0.0, 1.0)
    keep_next = jnp.where(last, 0.0, 1.0)
    xc = jnp.concatenate([xp_ref[...] * keep_prev, x_ref[...], xn_ref[...] * keep_next], axis=0)
    conv = cb_ref[...] + xc[0:tq] * cw_ref[0:1, :]
    for k in range(1, LRU_CONV):
        conv = conv + xc[k:k + tq] * cw_ref[k:k + 1, :]
    x = conv.reshape(rows, gw)
    xb = x.astype(BF16)
    r = jax.nn.sigmoid(_dot(xb, wa_ref[...]) + ba_ref[...])
    ig = jax.nn.sigmoid(_dot(xb, wx_ref[...]) + bx_ref[...])
    log_a = -LRU_C * r * sp_ref[...]
    a = jnp.exp(log_a)
    a_scr[...] = a
    b_scr[...] = jnp.sqrt(1.0 - a * a) * (ig * x)

    def step(i, h):
        t = (tq - 1 - i) if rev else i
        r0 = pl.multiple_of(t * nb, nb)
        hn = a_scr[pl.ds(r0, nb), :] * h + b_scr[pl.ds(r0, nb), :]
        b_scr[pl.ds(r0, nb), :] = hn
        return hn

    h_scr[...] = lax.fori_loop(0, tq, step, h_scr[...], unroll=4)
    out = b_scr[...].reshape(tq, nb, gw)
    if has_prev:
        out = out + prev_ref[...]
    o_ref[...] = out


def _lru_direction(tm3, conv_w, conv_b, wa_bd, wx_bd, ba, bx, sp, prev, rev, n_lat, tq):
    ltot, nb, _ = tm3.shape
    gw = GROUP_WIDTH
    n_chunks = ltot // tq
    n_lat_chunks = n_lat // tq
    cidx = lambda s: _scan_chunk_index(s, n_lat_chunks, n_chunks, rev)
    has_prev = prev is not None
    full2 = lambda s: (0, 0)
    in_specs = [pl.BlockSpec((tq, nb, gw), lambda s: (cidx(s), 0, 1)),
                pl.BlockSpec((2, nb, gw), lambda s: (jnp.maximum(cidx(s) * (tq // 2) - 1, 0), 0, 1)),
                pl.BlockSpec((1, nb, gw), lambda s: (jnp.minimum((cidx(s) + 1) * tq, ltot - 1), 0, 1)),
                pl.BlockSpec((LRU_CONV, gw), full2),
                pl.BlockSpec((1, gw), full2),
                pl.BlockSpec((gw, gw), full2),
                pl.BlockSpec((gw, gw), full2),
                pl.BlockSpec((1, gw), full2),
                pl.BlockSpec((1, gw), full2),
                pl.BlockSpec((1, gw), full2)]
    args = [tm3, tm3, tm3, conv_w, conv_b, wa_bd, wx_bd, ba, bx, sp]
    if has_prev:
        in_specs.append(pl.BlockSpec((tq, nb, gw), lambda s: (cidx(s), 0, 0)))
        args.append(prev)
    kern = functools.partial(_lru_kernel, rev=rev, tq=tq, nb=nb, n_lat_chunks=n_lat_chunks,
                             n_chunks=n_chunks, has_prev=has_prev)
    return pl.pallas_call(
        kern,
        out_shape=jax.ShapeDtypeStruct((ltot, nb, gw), F32),
        grid=(n_chunks,),
        in_specs=in_specs,
        out_specs=pl.BlockSpec((tq, nb, gw), lambda s: (cidx(s), 0, 0)),
        scratch_shapes=[pltpu.VMEM((tq * nb, gw), F32),
                        pltpu.VMEM((tq * nb, gw), F32),
                        pltpu.VMEM((nb, gw), F32)],
        compiler_params=_params("arbitrary"),
        name="lru_rev" if rev else "lru_fwd",
    )(*args)


def _tri_mask(n, rev):
    t = lax.broadcasted_iota(I32, (n, n), 0)
    s = lax.broadcasted_iota(I32, (n, n), 1)
    return (s >= t) if rev else (s <= t)


def _hgrn_kernel(*refs, rev, tq, has_prev):
    if has_prev:
        q_ref, z_ref, i_ref, lb_ref, tri_ref, ones_ref, prev_ref, o_ref, st_scr, cum_scr, q_scr, k_scr, v_scr = refs
    else:
        q_ref, z_ref, i_ref, lb_ref, tri_ref, ones_ref, o_ref, st_scr, cum_scr, q_scr, k_scr, v_scr = refs
        prev_ref = None
    hps = HGRN_HEADS_PER_STEP
    cs, sb, w = GLA_CHUNK, GLA_SUB, min(HGRN_FAST_WINDOW, tq)
    nsub = cs // sb
    n_ck = tq // cs
    n_w = tq // w

    @pl.when(pl.program_id(2) == 0)
    def _():
        st_scr[...] = jnp.zeros_like(st_scr)

    lb = lb_ref[0]
    z = z_ref[0]
    iv = i_ref[0]
    f = lb + (1.0 - lb) * jax.nn.sigmoid(z)
    k_all = 1.0 - f
    v_all = iv * jax.nn.sigmoid(iv)
    q_all = q_ref[0] * (GLA_DIM ** -0.5)
    logf = jnp.log(f)
    lf_hi = logf.astype(BF16)
    rest = logf - lf_hi.astype(F32)
    lf_mid = rest.astype(BF16)
    lf_lo = (rest - lf_mid.astype(F32)).astype(BF16)
    tri = tri_ref[...]
    cum_all = _dot(tri, lf_hi) + _dot(tri, lf_mid) + _dot(tri, lf_lo)
    for hh in range(hps):
        lanes = slice(hh * GLA_DIM, (hh + 1) * GLA_DIM)
        cum_scr[hh] = cum_all[:, lanes]
        q_scr[hh] = q_all[:, lanes]
        k_scr[hh] = k_all[:, lanes]
        v_scr[hh] = v_all[:, lanes]

    def far_row(c, width):
        return c * width if rev else (c + 1) * width - 1

    def outside_row(c, width, n):
        if rev:
            return (c + 1) * width if c < n - 1 else None
        return c * width - 1 if c > 0 else None

    def store_out(hh, rows, out):
        lanes = slice(hh * GLA_DIM, (hh + 1) * GLA_DIM)
        if has_prev:
            out = out + prev_ref[0, rows, lanes]
        o_ref[0, rows, lanes] = out

    def block_factored(_):
        lane_id = lax.broadcasted_iota(I32, (1, tq), 1)
        for hh in range(hps):
            cumg, q, k = cum_scr[hh], q_scr[hh], k_scr[hh]
            vb = v_scr[hh].astype(BF16)
            st = st_scr[hh]
            ends = [cum_scr[hh, far_row(c, w):far_row(c, w) + 1, :] for c in range(n_w)]
            end_full = jnp.concatenate([jnp.broadcast_to(e, (w, GLA_DIM)) for e in ends], axis=0)
            ks = (k * jnp.exp(end_full - cumg)).astype(BF16)
            pieces, starts, row_lo = [], [], []
            total = 0
            for c in range(n_w):
                rows = slice(0, (c + 1) * w) if rev else slice(c * w, tq)
                scale = jnp.exp(jnp.minimum(cumg[rows] - ends[c], HGRN_SAFE_LOG_DECAY))
                pieces.append((q[rows] * scale).astype(BF16))
                starts.append(total)
                row_lo.append(rows.start)
                total += rows.stop - rows.start
            a_stack = _dot_nt(jnp.concatenate(pieces, axis=0), ks)
            a_rows = []
            for r in range(n_w):
                acc = jnp.zeros((w, tq), F32)
                for c in (range(r, n_w) if rev else range(0, r + 1)):
                    lo = starts[c] + r * w - row_lo[c]
                    keep = jnp.logical_and(lane_id >= c * w, lane_id < (c + 1) * w)
                    if c == r:
                        t_id = r * w + lax.broadcasted_iota(I32, (w, tq), 0)
                        keep = jnp.logical_and(keep, (lane_id >= t_id) if rev else (lane_id <= t_id))
                    acc = jnp.where(keep, a_stack[lo:lo + w, :], acc)
                a_rows.append(acc)
            a = jnp.concatenate(a_rows, axis=0).astype(BF16)
            out = _dot(a, vb) + _dot_nt((q * jnp.exp(cumg)).astype(BF16), st.astype(BF16))
            store_out(hh, slice(None), out)
            last = ends[0] if rev else ends[n_w - 1]
            kd = k * jnp.exp(last - cumg)
            st_scr[hh] = st * jnp.exp(last) + _dot_tn(vb, kd.astype(BF16))
        return 0

    row_id = lax.broadcasted_iota(I32, (sb, GLA_DIM), 0)
    valid = [(row_id <= s) if rev else (row_id >= s) for s in range(sb)]

    def chunk_direct(hh, r0):
        rows = pl.ds(r0, cs)
        q, k, cumg = q_scr[hh, rows, :], k_scr[hh, rows, :], cum_scr[hh, rows, :]
        vb = v_scr[hh, rows, :].astype(BF16)
        if rev:
            base = cum_scr[hh, pl.ds(jnp.minimum(r0 + cs, tq - 1), 1), :]
            base = jnp.where(r0 + cs >= tq, 0.0, base)
        else:
            base = cum_scr[hh, pl.ds(jnp.maximum(r0 - 1, 0), 1), :]
            base = jnp.where(r0 == 0, 0.0, base)
        cum = cumg - base
        last = cum[0:1] if rev else cum[cs - 1:cs]
        st = st_scr[hh]
        o_inter = _dot_nt((q * jnp.exp(cum)).astype(BF16), st.astype(BF16))
        outs = []
        for i in range(nsub):
            blk = slice(i * sb, (i + 1) * sb)
            off = slice((i + 1) * sb, cs) if rev else slice(0, i * sb)
            o_row = outside_row(i, sb, nsub)
            q_i, cum_i = q[blk], cum[blk]
            o_i = o_inter[blk]
            if o_row is not None:
                ref_row = cum[o_row:o_row + 1]
                qt = q_i * jnp.exp(cum_i - ref_row)
                kt = k[off] * jnp.exp(ref_row - cum[off])
                a_off = _dot_nt(qt.astype(BF16), kt.astype(BF16))
                o_i = o_i + _dot(a_off.astype(BF16), vb[off])
            prods = []
            for s in range(sb):
                r = r0 + i * sb + s
                e = jnp.exp(jnp.minimum(cumg[blk] - cum_scr[hh, pl.ds(r, 1), :], 0.0))
                prods.append(jnp.where(valid[s], q_i * e * k_scr[hh, pl.ds(r, 1), :], 0.0))
            sums = _dot(jnp.concatenate(prods, axis=0).astype(BF16), ones_ref[...])
            for s in range(sb):
                r = r0 + i * sb + s
                o_i = o_i + sums[s * sb:(s + 1) * sb] * v_scr[hh, pl.ds(r, 1), :]
            outs.append(o_i)
        store_out(hh, rows, jnp.concatenate(outs, axis=0))
        kd = k * jnp.exp(last - cum)
        st_scr[hh] = st * jnp.exp(last) + _dot_tn(vb, kd.astype(BF16))

    def block_direct(_):
        def chunk(ci, carry):
            c = (n_ck - 1 - ci) if rev else ci
            r0 = pl.multiple_of(c * cs, cs)
            for hh in range(hps):
                chunk_direct(hh, r0)
            return carry

        return lax.fori_loop(0, n_ck, chunk, 0)

    decays = []
    for c in range(n_w):
        far = cum_all[far_row(c, w):far_row(c, w) + 1]
        o_row = outside_row(c, w, n_w)
        decays.append(far if o_row is None else far - cum_all[o_row:o_row + 1])
    safe = jnp.min(jnp.concatenate(decays, axis=0)) > -HGRN_SAFE_LOG_DECAY
    lax.cond(safe, block_factored, block_direct, 0)


def _gla_specs(ltot, tq, n_lat, rev):
    n_chunks = ltot // tq
    n_lat_chunks = n_lat // tq
    cidx = lambda s: _scan_chunk_index(s, n_lat_chunks, n_chunks, rev)
    part = lambda p: pl.BlockSpec((1, tq, GLA_DIM), lambda b, h, s: (b, cidx(s), p * GLA_HEADS + h))
    return n_chunks, cidx, part


def _hgrn_direction(bm, lb, prev, rev, n_lat, tq):
    bsz, ltot, _ = bm.shape
    n_chunks, cidx, _ = _gla_specs(ltot, tq, n_lat, rev)
    has_prev = prev is not None
    hps = HGRN_HEADS_PER_STEP
    width = hps * GLA_DIM
    n_hsteps = GLA_HEADS // hps
    part = lambda p: pl.BlockSpec((1, tq, width), lambda b, h, s: (b, cidx(s), p * n_hsteps + h))
    tri = _tri_mask(tq, rev).astype(BF16)
    ones = jnp.ones((GLA_DIM, GLA_DIM), BF16)
    z_part = 2 if rev else 1
    in_specs = [part(0), part(z_part), part(3),
                pl.BlockSpec((1, 1, width), lambda b, h, s: (h, 0, 0)),
                pl.BlockSpec(tri.shape, lambda b, h, s: (0, 0)),
                pl.BlockSpec(ones.shape, lambda b, h, s: (0, 0))]
    args = [bm, bm, bm, lb.reshape(n_hsteps, 1, width), tri, ones]
    out_spec = pl.BlockSpec((1, tq, width), lambda b, h, s: (b, cidx(s), h))
    if has_prev:
        in_specs.append(out_spec)
        args.append(prev)
    kern = functools.partial(_hgrn_kernel, rev=rev, tq=tq, has_prev=has_prev)
    tile = pltpu.VMEM((hps, tq, GLA_DIM), F32)
    return pl.pallas_call(
        kern,
        out_shape=jax.ShapeDtypeStruct((bsz, ltot, GROUP_WIDTH), F32),
        grid=(bsz, n_hsteps, n_chunks),
        in_specs=in_specs,
        out_specs=out_spec,
        scratch_shapes=[pltpu.VMEM((hps, GLA_DIM, GLA_DIM), F32), tile, tile, tile, tile],
        compiler_params=_params("arbitrary", "arbitrary", "arbitrary"),
        name="hgrn_rev" if rev else "hgrn_fwd",
    )(*args)


def _ret_kernel(*refs, rev, tq, has_prev):
    if has_prev:
        q_ref, k_ref, v_ref, cos_ref, sin_ref, eq_ref, ek_ref, dm_ref, el_ref, prev_ref, o_ref, st_scr = refs
    else:
        q_ref, k_ref, v_ref, cos_ref, sin_ref, eq_ref, ek_ref, dm_ref, el_ref, o_ref, st_scr = refs
        prev_ref = None
    cs = min(tq, RET_CHUNK)
    n_ck = tq // cs

    @pl.when(pl.program_id(1) == 0)
    def _():
        st_scr[...] = jnp.zeros_like(st_scr)

    def chunk(ci, carry):
        c = (n_ck - 1 - ci) if rev else ci
        r0 = pl.multiple_of(c * cs, cs)
        rows = pl.ds(r0, cs)
        cos = cos_ref[rows, :]
        sin = sin_ref[rows, :]
        for h in range(GLA_HEADS):
            lanes = slice(h * GLA_DIM, (h + 1) * GLA_DIM)
            q = q_ref[0, rows, lanes]
            k = k_ref[0, rows, lanes]
            vb = v_ref[0, rows, lanes].astype(BF16)
            q = q * cos + pltpu.roll(q, GLA_DIM // 2, 1) * sin
            k = (k * cos + pltpu.roll(k, GLA_DIM // 2, 1) * sin) * (GLA_DIM ** -0.5)
            st = st_scr[h]
            scores = _dot_nt(q.astype(BF16), k.astype(BF16)) * dm_ref[h]
            out = _dot_nt((q * eq_ref[h]).astype(BF16), st.astype(BF16)) + _dot(scores.astype(BF16), vb)
            if has_prev:
                out = out + prev_ref[0, rows, lanes]
            o_ref[0, rows, lanes] = out
            st_scr[h] = st * el_ref[h] + _dot_tn(vb, (k * ek_ref[h]).astype(BF16))
        return carry

    lax.fori_loop(0, n_ck, chunk, 0)


def _ret_tables(rev, cs):
    gamma = jnp.log1p(-jnp.exp2(-(RET_DECAY_EXP[1 if rev else 0] + jnp.arange(GLA_HEADS, dtype=F32))))
    t = jnp.arange(cs, dtype=F32)
    steps = (cs - t) if rev else (t + 1.0)
    cum = gamma[:, None] * steps[None, :]
    last = gamma * cs
    eq = jnp.broadcast_to(jnp.exp(cum)[:, :, None], (GLA_HEADS, cs, GLA_DIM))
    ek = jnp.broadcast_to(jnp.exp(last[:, None] - cum)[:, :, None], (GLA_HEADS, cs, GLA_DIM))
    rel = cum[:, :, None] - cum[:, None, :]
    dm = jnp.where(_tri_mask(cs, rev)[None], jnp.exp(jnp.minimum(rel, 0.0)), 0.0)
    el = jnp.broadcast_to(jnp.exp(last)[:, None, None], (GLA_HEADS, 1, GLA_DIM))
    return eq, ek, dm, el


def _rotary_tables(n_lat, n_ctx):
    rows = n_lat // GRID_W
    row = jnp.repeat(jnp.arange(rows, dtype=F32), GRID_W)
    col = jnp.tile(jnp.arange(GRID_W, dtype=F32), rows)
    quarter = GLA_DIM // 4
    inv_freq = ROPE_BASE ** (-jnp.arange(quarter, dtype=F32) / quarter)
    ang = jnp.concatenate([row[:, None] * inv_freq, col[:, None] * inv_freq], axis=-1)
    cos, sin = jnp.cos(ang), jnp.sin(ang)
    cos_l = jnp.concatenate([cos, cos], axis=-1)
    sin_l = jnp.concatenate([-sin, sin], axis=-1)
    cos_t = jnp.concatenate([cos_l, jnp.ones((n_ctx, GLA_DIM), F32)], axis=0)
    sin_t = jnp.concatenate([sin_l, jnp.zeros((n_ctx, GLA_DIM), F32)], axis=0)
    return cos_t, sin_t


def _ret_direction(bm, cos_t, sin_t, prev, rev, n_lat, tq):
    bsz, ltot, _ = bm.shape
    n_chunks, cidx, _ = _gla_specs(ltot, tq, n_lat, rev)
    has_prev = prev is not None
    eq, ek, dm, el = _ret_tables(rev, min(tq, RET_CHUNK))
    gw = GROUP_WIDTH
    part = lambda p: pl.BlockSpec((1, tq, gw), lambda b, s: (b, cidx(s), p))
    whole = lambda a: pl.BlockSpec(a.shape, lambda b, s: (0, 0, 0))
    pos = pl.BlockSpec((tq, GLA_DIM), lambda b, s: (cidx(s), 0))
    in_specs = [part(5), part(6), part(7), pos, pos, whole(eq), whole(ek), whole(dm), whole(el)]
    args = [bm, bm, bm, cos_t, sin_t, eq, ek, dm, el]
    out_spec = pl.BlockSpec((1, tq, gw), lambda b, s: (b, cidx(s), 0))
    if has_prev:
        in_specs.append(out_spec)
        args.append(prev)
    kern = functools.partial(_ret_kernel, rev=rev, tq=tq, has_prev=has_prev)
    return pl.pallas_call(
        kern,
        out_shape=jax.ShapeDtypeStruct((bsz, ltot, gw), F32),
        grid=(bsz, n_chunks),
        in_specs=in_specs,
        out_specs=out_spec,
        scratch_shapes=[pltpu.VMEM((GLA_HEADS, GLA_DIM, GLA_DIM), F32)],
        compiler_params=_params("arbitrary", "arbitrary"),
        name="ret_rev" if rev else "ret_fwd",
    )(*args)


def _head_rms(o):
    parts = []
    for h in range(GLA_HEADS):
        oh = o[:, h * GLA_DIM:(h + 1) * GLA_DIM]
        parts.append(oh * lax.rsqrt(jnp.mean(oh * oh, axis=-1, keepdims=True) + EPS))
    return jnp.concatenate(parts, axis=-1)


def _mix_kernel(u_ref, lg_ref, ys_ref, hs_ref, hg_ref, rg_ref, oh_ref, or_ref, d_ref, gw_ref, gb_ref, o_ref):
    gw = GROUP_WIDTH
    y = jax.nn.gelu(d_ref[...] * u_ref[...] + ys_ref[...])
    a = y * jax.nn.sigmoid(_dot(y.astype(BF16), gw_ref[...]) + gb_ref[...])
    o_ref[0, :, 0:gw] = a.astype(BF16)
    o_ref[0, :, gw:2 * gw] = (hs_ref[...] * jax.nn.gelu(lg_ref[...])).astype(BF16)
    g = hg_ref[0]
    o_ref[0, :, 2 * gw:3 * gw] = (_head_rms(oh_ref[0]) * (g * jax.nn.sigmoid(g))).astype(BF16)
    g = rg_ref[0]
    o_ref[0, :, 3 * gw:4 * gw] = (_head_rms(or_ref[0]) * (g * jax.nn.sigmoid(g))).astype(BF16)


def _mix_epilogue(tm2, ys2, hs2, bm, o_h, o_r, s5_d, glu_w, glu_b, n_rows, tm):
    bsz, ltot, _ = bm.shape
    gw = GROUP_WIDTH
    tmaj = lambda p, k: pl.BlockSpec((tm, gw), lambda b, i: (i, b * k + p))
    bmaj = lambda p: pl.BlockSpec((1, tm, gw), lambda b, i: (b, i, p))
    full2 = lambda b, i: (0, 0)
    return pl.pallas_call(
        _mix_kernel,
        out_shape=jax.ShapeDtypeStruct((bsz, ltot, 4 * gw), BF16),
        grid=(bsz, n_rows // tm),
        in_specs=[tmaj(0, N_TM_PARTS), tmaj(2, N_TM_PARTS), tmaj(0, 1), tmaj(0, 1),
                  bmaj(4), bmaj(8), bmaj(0), bmaj(0),
                  pl.BlockSpec((1, gw), full2), pl.BlockSpec((gw, gw), full2), pl.BlockSpec((1, gw), full2)],
        out_specs=pl.BlockSpec((1, tm, 4 * gw), lambda b, i: (b, i, 0)),
        compiler_params=_params("arbitrary", "arbitrary"),
        name="mix_epilogue",
    )(tm2, tm2, ys2, hs2, bm, bm, o_h, o_r, s5_d, glu_w, glu_b)


def _gate_rows(modb_ref, modc_ref, idx, row0, tm, n_lat):
    row = row0 + lax.broadcasted_iota(I32, (tm, 1), 0)
    return jnp.where(row >= n_lat, modc_ref[0, idx:idx + 1, :], modb_ref[0, idx:idx + 1, :])


def _outproj_kernel(a_ref, w_ref, x_ref, modb_ref, modc_ref, o_ref, *, tm, n_lat):
    gate = _gate_rows(modb_ref, modc_ref, 2, pl.program_id(1) * tm, tm, n_lat)
    o_ref[0] = x_ref[0] + gate * _dot(a_ref[0], w_ref[0])


def _out_projection(mix, w_bf16, layer, xs, modv, n_rows, n_lat, tm, tn):
    bsz, ltot, d = xs.shape
    k = mix.shape[-1]
    kern = functools.partial(_outproj_kernel, tm=tm, n_lat=n_lat)
    return pl.pallas_call(
        kern,
        out_shape=jax.ShapeDtypeStruct((bsz, ltot, d), F32),
        grid=(bsz, n_rows // tm, d // tn),
        in_specs=[pl.BlockSpec((1, tm, k), lambda b, i, j: (b, i, 0)),
                  pl.BlockSpec((1, k, tn), lambda b, i, j: (layer, 0, j)),
                  pl.BlockSpec((1, tm, tn), lambda b, i, j: (b, i, j)),
                  pl.BlockSpec((1, 6, tn), lambda b, i, j: (b, 0, j)),
                  pl.BlockSpec((1, 6, tn), lambda b, i, j: (bsz, 0, j))],
        out_specs=pl.BlockSpec((1, tm, tn), lambda b, i, j: (b, i, j)),
        compiler_params=_params("arbitrary", "arbitrary", "arbitrary"),
        name="out_projection",
    )(mix, w_bf16, xs, modv, modv)


def _first_max4(v):
    m1 = jnp.maximum(jnp.maximum(v[0], v[1]), jnp.maximum(v[2], v[3]))
    i1 = jnp.where(v[0] == m1, 0, jnp.where(v[1] == m1, 1, jnp.where(v[2] == m1, 2, 3)))
    rest = [jnp.where(i1 == j, -jnp.inf, v[j]) for j in range(4)]
    m2 = jnp.maximum(jnp.maximum(rest[0], rest[1]), jnp.maximum(rest[2], rest[3]))
    i2 = jnp.where(rest[0] == m2, 0, jnp.where(rest[1] == m2, 1, jnp.where(rest[2] == m2, 2, 3)))
    return m1, i1, m2, i2


def _router_kernel(x_ref, g_ref, modb_ref, modc_ref, rw_ref, rb_ref, tri_ref,
                   h_ref, ids_ref, gates_ref, ranks_ref, counts_ref, base_scr, *, tm, n_lat):
    first = jnp.logical_and(pl.program_id(0) == 0, pl.program_id(1) == 0)

    @pl.when(first)
    def _():
        base_scr[...] = jnp.zeros_like(base_scr)

    h = _norm_modulate(x_ref[0], g_ref[...], modb_ref, modc_ref, pl.program_id(1) * tm, n_lat, 3, 4)
    h_ref[0] = h.astype(BF16)
    logits = lax.dot_general(rw_ref[...], h, (((1,), (1,)), ((), ())), preferred_element_type=F32,
                             precision=lax.Precision.HIGHEST)
    scores = jax.nn.sigmoid(logits)
    biased = scores + rb_ref[...]
    tops = []
    for gidx in range(N_EXPERT_GROUPS):
        rows = [biased[gidx * EXPERTS_PER_GROUP + j:gidx * EXPERTS_PER_GROUP + j + 1, :]
                for j in range(EXPERTS_PER_GROUP)]
        tops.append(_first_max4(rows))
    gs = [t[0] + t[2] for t in tops]
    gmax = jnp.maximum(jnp.maximum(gs[0], gs[1]), jnp.maximum(gs[2], gs[3]))
    best = jnp.where(gs[0] == gmax, 0, jnp.where(gs[1] == gmax, 1, jnp.where(gs[2] == gmax, 2, 3)))
    e1 = jnp.zeros_like(best)
    e2 = jnp.zeros_like(best)
    for gidx in range(N_EXPERT_GROUPS):
        e1 = jnp.where(best == gidx, gidx * EXPERTS_PER_GROUP + tops[gidx][1], e1)
        e2 = jnp.where(best == gidx, gidx * EXPERTS_PER_GROUP + tops[gidx][3], e2)
    eid = lax.broadcasted_iota(I32, (N_EXPERTS, tm), 0)
    sel1 = eid == e1
    sel2 = eid == e2
    w1 = jnp.sum(jnp.where(sel1, scores, 0.0), axis=0, keepdims=True)
    w2 = jnp.sum(jnp.where(sel2, scores, 0.0), axis=0, keepdims=True)
    wsum = w1 + w2
    onehot = jnp.where(jnp.logical_or(sel1, sel2), 1.0, 0.0)
    pos = base_scr[...] + _dot(onehot.astype(BF16), tri_ref[...])
    r1 = jnp.sum(jnp.where(sel1, pos, 0.0), axis=0, keepdims=True)
    r2 = jnp.sum(jnp.where(sel2, pos, 0.0), axis=0, keepdims=True)
    ids_ref[0] = jnp.concatenate([e1, e2], axis=0)
    gates_ref[0] = jnp.concatenate([w1 / wsum, w2 / wsum], axis=0)
    ranks_ref[0] = jnp.concatenate([r1, r2], axis=0).astype(I32)
    base = base_scr[...] + jnp.sum(onehot, axis=1, keepdims=True)
    base_scr[...] = base
    counts_ref[...] = jnp.broadcast_to(base, counts_ref.shape)


def _router(xs, gain, modv, rw_t, rbias, n_rows, n_lat, tm):
    bsz, ltot, d = xs.shape
    tri = (lax.broadcasted_iota(I32, (tm, tm), 0) < lax.broadcasted_iota(I32, (tm, tm), 1)).astype(BF16)
    kern = functools.partial(_router_kernel, tm=tm, n_lat=n_lat)
    small = lambda dt: jax.ShapeDtypeStruct((bsz, 2, n_rows), dt)
    small_spec = pl.BlockSpec((1, 2, tm), lambda b, i: (b, 0, i))
    return pl.pallas_call(
        kern,
        out_shape=(jax.ShapeDtypeStruct((bsz, ltot, d), BF16), small(I32), small(F32), small(I32),
                   jax.ShapeDtypeStruct((N_EXPERTS, 128), F32)),
        grid=(bsz, n_rows // tm),
        in_specs=[pl.BlockSpec((1, tm, d), lambda b, i: (b, i, 0)),
                  pl.BlockSpec((1, d), lambda b, i: (0, 0)),
                  pl.BlockSpec((1, 6, d), lambda b, i: (b, 0, 0)),
                  pl.BlockSpec((1, 6, d), lambda b, i: (bsz, 0, 0)),
                  pl.BlockSpec((N_EXPERTS, d), lambda b, i: (0, 0)),
                  pl.BlockSpec((N_EXPERTS, 1), lambda b, i: (0, 0)),
                  pl.BlockSpec((tm, tm), lambda b, i: (0, 0))],
        out_specs=(pl.BlockSpec((1, tm, d), lambda b, i: (b, i, 0)), small_spec, small_spec, small_spec,
                   pl.BlockSpec((N_EXPERTS, 128), lambda b, i: (0, 0))),
        scratch_shapes=[pltpu.VMEM((N_EXPERTS, 1), F32)],
        compiler_params=_params("arbitrary", "arbitrary"),
        name="moe_router",
    )(xs, gain, modv, modv, rw_t, rbias, tri)


def _expert_ffn_kernel(te_ref, nt_ref, x_ref, wg_ref, wu_ref, wd_ref, o_ref):
    live = pl.program_id(0) < nt_ref[0]

    @pl.when(live)
    def _():
        x = x_ref[...]
        g = _dot(x, wg_ref[0])
        u = _dot(x, wu_ref[0])
        he = (g * jax.nn.sigmoid(g)) * u
        o_ref[...] = _dot(he.astype(BF16), wd_ref[0])

    @pl.when(jnp.logical_not(live))
    def _():
        o_ref[...] = jnp.zeros_like(o_ref)


def _expert_ffn(tile_expert, n_tiles, x_sorted, wg, wu, wd, tm):
    p, d = x_sorted.shape
    f = wg.shape[-1]
    tile = lambda i, te, nt: i
    expert = lambda i, te, nt: te[jnp.minimum(i, nt[0] - 1)]
    grid_spec = pltpu.PrefetchScalarGridSpec(
        num_scalar_prefetch=2,
        grid=(p // tm,),
        in_specs=[pl.BlockSpec((tm, d), lambda i, te, nt: (tile(i, te, nt), 0)),
                  pl.BlockSpec((1, d, f), lambda i, te, nt: (expert(i, te, nt), 0, 0)),
                  pl.BlockSpec((1, d, f), lambda i, te, nt: (expert(i, te, nt), 0, 0)),
                  pl.BlockSpec((1, f, d), lambda i, te, nt: (expert(i, te, nt), 0, 0))],
        out_specs=pl.BlockSpec((tm, d), lambda i, te, nt: (tile(i, te, nt), 0)),
    )
    return pl.pallas_call(
        _expert_ffn_kernel,
        out_shape=jax.ShapeDtypeStruct((p, d), F32),
        grid_spec=grid_spec,
        compiler_params=_params("arbitrary"),
        name="moe_expert_ffn",
    )(tile_expert, n_tiles, x_sorted, wg, wu, wd)


def _moe_residual_kernel(x_ref, y0_ref, y1_ref, gates_ref, modb_ref, modc_ref, g_ref, o_ref,
                         *, tm, n_lat, final_norm):
    gates = gates_ref[0]
    y = gates[:, 0:1] * y0_ref[0, 0] + gates[:, 1:2] * y1_ref[0, 0]
    gate = _gate_rows(modb_ref, modc_ref, 5, pl.program_id(1) * tm, tm, n_lat)
    x = x_ref[0] + gate * y
    if final_norm:
        x = x * lax.rsqrt(jnp.mean(x * x, axis=-1, keepdims=True) + EPS) * g_ref[...]
    o_ref[0] = x


def _moe_residual(xs, y_pair, gates_t, modv, final_g, n_rows, n_lat, tm, final_norm):
    bsz, ltot, d = xs.shape
    kern = functools.partial(_moe_residual_kernel, tm=tm, n_lat=n_lat, final_norm=final_norm)
    rows_out = n_rows if final_norm else ltot
    return pl.pallas_call(
        kern,
        out_shape=jax.ShapeDtypeStruct((bsz, rows_out, d), F32),
        grid=(bsz, n_rows // tm),
        in_specs=[pl.BlockSpec((1, tm, d), lambda b, i: (b, i, 0)),
                  pl.BlockSpec((1, 1, tm, d), lambda b, i: (0, b, i, 0)),
                  pl.BlockSpec((1, 1, tm, d), lambda b, i: (1, b, i, 0)),
                  pl.BlockSpec((1, tm, 2), lambda b, i: (b, i, 0)),
                  pl.BlockSpec((1, 6, d), lambda b, i: (b, 0, 0)),
                  pl.BlockSpec((1, 6, d), lambda b, i: (bsz, 0, 0)),
                  pl.BlockSpec((1, d), lambda b, i: (0, 0))],
        out_specs=pl.BlockSpec((1, tm, d), lambda b, i: (b, i, 0)),
        compiler_params=_params("arbitrary", "arbitrary"),
        name="moe_residual",
    )(xs, y_pair, y_pair, gates_t, modv, modv, final_g)


def _moe_layer(xs, gain, modv, rw_t, rbias, wg, wu, wd, expert_base, final_g, n_rows, n_lat, tm_tok, tm_res, tm_exp,
               final_norm):
    bsz, ltot, d = xs.shape
    h, ids, gates, ranks, counts = _router(xs, gain, modv, rw_t, rbias, n_rows, n_lat, tm_tok)
    counts = counts[:, 0].astype(I32)
    padded = ((counts + tm_exp - 1) // tm_exp) * tm_exp
    ends = jnp.cumsum(padded)
    offsets = ends - padded
    dest = offsets[ids] + ranks
    n_slots = bsz * 2 * n_rows
    p_rows = ((n_slots + N_EXPERTS * (tm_exp - 1)) + tm_exp - 1) // tm_exp * tm_exp
    tok = (jnp.arange(bsz, dtype=I32)[:, None, None] * ltot + jnp.arange(n_rows, dtype=I32)[None, None, :])
    tok = jnp.broadcast_to(tok, dest.shape)
    src = jnp.zeros((p_rows,), I32).at[dest.reshape(-1)].set(tok.reshape(-1), mode="promise_in_bounds",
                                                            unique_indices=True)
    tile_start = jnp.arange(p_rows // tm_exp, dtype=I32) * tm_exp
    tile_expert = jnp.minimum(jnp.sum(tile_start[:, None] >= ends[None, :], axis=1), N_EXPERTS - 1).astype(I32)
    n_tiles = (ends[-1:] // tm_exp).astype(I32)
    x_sorted = h.reshape(bsz * ltot, d).at[src].get(mode="promise_in_bounds")
    y_sorted = _expert_ffn(tile_expert + expert_base, n_tiles, x_sorted, wg, wu, wd, tm_exp)
    y_pair = y_sorted.at[dest.transpose(1, 0, 2).reshape(-1)].get(mode="promise_in_bounds")
    y_pair = y_pair.reshape(2, bsz, n_rows, d)
    return _moe_residual(xs, y_pair, gates.transpose(0, 2, 1), modv, final_g, n_rows, n_lat, tm_res, final_norm)


def _row_tile(n, target):
    t = min(n, target)
    while n % t or t % 8:
        t -= 8
    return t


def kernel(x, c, ctx, c_ctx, norm_mix_g, norm_ffn_g, w_mod, b_mod, w_in, w_out, s5_lam_re, s5_lam_im, s5_log_dt, s5_b_re, s5_b_im, s5_c_re, s5_c_im, s5_d, s5_glu_w, s5_glu_b, lru_conv_w, lru_conv_b, lru_wa, lru_ba, lru_wx, lru_bx, lru_lam, hgrn_lb_logits, router_w, router_bias, moe_w_gate, moe_w_up, moe_w_down, final_norm_g):
    bsz, n_lat, d = x.shape
    n_ctx = ctx.shape[1]
    ltot = n_lat + n_ctx
    depth = w_in.shape[0]
    gw = GROUP_WIDTH
    assert bsz % 8 == 0 and bsz + 1 <= MOD_ROWS
    assert n_lat % GLA_CHUNK == 0 and n_ctx % GLA_CHUNK == 0 and n_lat % GRID_W == 0

    tq_scan = math.gcd(math.gcd(n_lat, n_ctx), 64)
    tq_gla = math.gcd(math.gcd(n_lat, n_ctx), 256)
    tm_all = _row_tile(ltot, 768)
    tm_lat = _row_tile(n_lat, 1024)
    tm_mix = math.gcd(math.gcd(n_lat, n_ctx), 256)

    xs = jnp.concatenate([x, ctx.astype(x.dtype)], axis=1)
    cv = jnp.zeros((MOD_ROWS, d), F32).at[:bsz].set(c).at[bsz].set(c_ctx)
    mod_all = _mod_vectors(cv, w_mod, b_mod).reshape(depth, MOD_ROWS, 6, d)
    cos_t, sin_t = _rotary_tables(n_lat, n_ctx)
    rw_t = router_w.astype(F32).T
    rbias = router_bias.astype(F32).reshape(N_EXPERTS, 1)
    lb_cum = [jnp.cumsum(jax.nn.softmax(hgrn_lb_logits[dr].astype(F32), axis=0), axis=0) for dr in (0, 1)]
    w_in_b, w_out_b = w_in.astype(BF16), w_out.astype(BF16)
    n_exp, _, d_ff = moe_w_gate.shape[1:]
    wg_b = moe_w_gate.astype(BF16).reshape(depth * n_exp, d, d_ff)
    wu_b = moe_w_up.astype(BF16).reshape(depth * n_exp, d, d_ff)
    wd_b = moe_w_down.astype(BF16).reshape(depth * n_exp, d_ff, d)

    for layer in range(depth):
        ctx_out = layer < depth - 1
        n_rows = ltot if ctx_out else n_lat
        tm_tok = tm_all if ctx_out else tm_lat
        modv = mod_all[layer]

        tm2, bm = _in_projection(xs, norm_mix_g[layer].reshape(1, d), modv, w_in_b, layer, n_lat, tm_all)
        tm3 = tm2.reshape(ltot, bsz, N_TM_PARTS * gw)

        ys = hs = o_h = o_r = None
        for dr in (0, 1):
            rev = dr == 1
            bw, cw, atab = _s5_tables(s5_lam_re[layer, dr], s5_lam_im[layer, dr], s5_log_dt[layer, dr],
                                      s5_b_re[layer, dr], s5_b_im[layer, dr], s5_c_re[layer, dr], s5_c_im[layer, dr])
            ys = _s5_direction(tm3, bw, cw, atab, ys, rev, n_lat, tq_scan)
            sp = jax.nn.softplus(-lru_lam[layer, dr].astype(F32)).reshape(1, gw)
            hs = _lru_direction(tm3, lru_conv_w[layer].astype(F32), lru_conv_b[layer].astype(F32).reshape(1, gw),
                                _block_diag(lru_wa[layer, dr]).astype(BF16), _block_diag(lru_wx[layer, dr]).astype(BF16),
                                lru_ba[layer, dr].astype(F32).reshape(1, gw), lru_bx[layer, dr].astype(F32).reshape(1, gw),
                                sp, hs, rev, n_lat, tq_scan)
            lb = lb_cum[dr][layer] - lb_cum[dr][0]
            o_h = _hgrn_direction(bm, lb, o_h, rev, n_lat, tq_gla)
            o_r = _ret_direction(bm, cos_t, sin_t, o_r, rev, n_lat, tq_gla)

        mix = _mix_epilogue(tm2, ys.reshape(ltot, bsz * gw), hs.reshape(ltot, bsz * gw), bm, o_h, o_r,
                            s5_d[layer].astype(F32).reshape(1, gw), s5_glu_w[layer].astype(BF16),
                            s5_glu_b[layer].astype(F32).reshape(1, gw), n_rows, tm_mix)
        xs = _out_projection(mix, w_out_b, layer, xs, modv, n_rows, n_lat, tm_tok, 1024)
        xs = _moe_layer(xs, norm_ffn_g[layer].reshape(1, d), modv, rw_t, rbias, wg_b, wu_b, wd_b, layer * n_exp,
                        final_norm_g.reshape(1, d), n_rows, n_lat, tm_tok, tm_mix, 512, not ctx_out)
    return xs
```

```python
import functools
import math

import jax
import jax.numpy as jnp
from jax import lax
from jax.experimental import pallas as pl
from jax.experimental.pallas import tpu as pltpu

F32 = jnp.float32
BF16 = jnp.bfloat16
I32 = jnp.int32

GROUP_WIDTH = 512
N_IN_PARTS = 12
N_TM_PARTS = 3
N_BM_PARTS = N_IN_PARTS - N_TM_PARTS
S5_CH = 16
S5_GROUPS = 32
S5_STATE = 64
S5_LANE_BLOCKS = 4
LRU_HEADS = 8
LRU_CONV = 4
LRU_C = 8.0
GLA_HEADS = 4
GLA_DIM = 128
GLA_CHUNK = 64
GLA_SUB = 16
HGRN_FAST_WINDOW = 32
HGRN_SAFE_LOG_DECAY = 60.0
HGRN_HEADS_PER_STEP = 2
RET_CHUNK = 256
RET_DECAY_EXP = (5.0, 5.5)
ROPE_BASE = 10000.0
GRID_W = 64
N_EXPERTS = 16
N_EXPERT_GROUPS = 4
EXPERTS_PER_GROUP = 4
D_FF_EXPERT = 1024
EPS = 1e-6
NORM_SLAB = 16
MOD_ROWS = 24

VMEM_LIMIT_BYTES = 56 * 1024 * 1024


def _params(*semantics):
    return pltpu.CompilerParams(dimension_semantics=semantics, vmem_limit_bytes=VMEM_LIMIT_BYTES)


def _dot(a, b):
    return jnp.dot(a, b, preferred_element_type=F32)


def _dot_nt(a, b):
    return lax.dot_general(a, b, (((1,), (1,)), ((), ())), preferred_element_type=F32)


def _dot_tn(a, b):
    return lax.dot_general(a, b, (((0,), (0,)), ((), ())), preferred_element_type=F32)


def _scan_chunk_index(step, n_lat_chunks, n_chunks, rev):
    if rev:
        return n_chunks - 1 - step
    return (step + n_lat_chunks) % n_chunks


def _cast_kernel(x_ref, o_ref):
    o_ref[...] = x_ref[...].astype(o_ref.dtype)


def _cast_bf16(w):
    n, r, c = w.shape
    tc = min(c, 1024)
    spec = pl.BlockSpec((1, r, tc), lambda i, j: (i, 0, j))
    return pl.pallas_call(
        _cast_kernel,
        out_shape=jax.ShapeDtypeStruct(w.shape, BF16),
        grid=(n, c // tc),
        in_specs=[spec],
        out_specs=spec,
        compiler_params=_params("arbitrary", "arbitrary"),
        name="cast_bf16",
    )(w)


def _mod_kernel(cv_ref, w_ref, b_ref, o_ref):
    cv = cv_ref[...]
    s = cv * jax.nn.sigmoid(cv)
    o_ref[0] = _dot(s.astype(BF16), w_ref[0].astype(BF16)) + b_ref[0]


def _mod_vectors(cv, w_mod, b_mod):
    nl, d, n6 = w_mod.shape
    tn = 1024
    return pl.pallas_call(
        _mod_kernel,
        out_shape=jax.ShapeDtypeStruct((nl, MOD_ROWS, n6), F32),
        grid=(nl, n6 // tn),
        in_specs=[pl.BlockSpec((MOD_ROWS, d), lambda l, j: (0, 0)),
                  pl.BlockSpec((1, d, tn), lambda l, j: (l, 0, j)),
                  pl.BlockSpec((1, 1, tn), lambda l, j: (l, 0, j))],
        out_specs=pl.BlockSpec((1, MOD_ROWS, tn), lambda l, j: (l, 0, j)),
        compiler_params=_params("arbitrary", "arbitrary"),
        name="mod_vectors",
    )(cv, w_mod, b_mod.reshape(nl, 1, n6))


def _norm_modulate(x, g, modb_ref, modc_ref, row0, n_lat, shift_idx, scale_idx):
    tm = x.shape[0]
    ms = jnp.mean(x * x, axis=-1, keepdims=True)
    y = x * lax.rsqrt(ms + EPS) * g
    row = row0 + lax.broadcasted_iota(I32, (tm, 1), 0)
    is_ctx = row >= n_lat
    shift = jnp.where(is_ctx, modc_ref[0, shift_idx:shift_idx + 1, :], modb_ref[0, shift_idx:shift_idx + 1, :])
    scale = jnp.where(is_ctx, modc_ref[0, scale_idx:scale_idx + 1, :], modb_ref[0, scale_idx:scale_idx + 1, :])
    return y * (1.0 + scale) + shift


def _inproj_kernel(x_ref, g_ref, modb_ref, modc_ref, w_ref, otm_ref, obm_ref, h_scr, *, tm, n_lat):
    i = pl.program_id(1)
    j = pl.program_id(2)

    @pl.when(j == 0)
    def _():
        def slab(s, carry):
            r0 = pl.multiple_of(s * NORM_SLAB, NORM_SLAB)
            is_ctx = i * tm + r0 >= n_lat
            shift = jnp.where(is_ctx, modc_ref[0, 0:1, :], modb_ref[0, 0:1, :])
            gain = g_ref[...] * (1.0 + jnp.where(is_ctx, modc_ref[0, 1:2, :], modb_ref[0, 1:2, :]))
            x = x_ref[0, pl.ds(r0, NORM_SLAB), :]
            inv = lax.rsqrt(jnp.mean(x * x, axis=-1, keepdims=True) + EPS)
            h_scr[pl.ds(r0, NORM_SLAB), :] = (x * inv * gain + shift).astype(BF16)
            return carry

        lax.fori_loop(0, tm // NORM_SLAB, slab, 0, unroll=2)

    r = _dot(h_scr[...], w_ref[0])

    @pl.when(j == 0)
    def _():
        otm_ref[...] = r

    @pl.when(j > 0)
    def _():
        obm_ref[0] = r


def _in_projection(xs, gain, modv, w_bf16, layer, n_lat, tm):
    bsz, ltot, d = xs.shape
    gw = GROUP_WIDTH
    kern = functools.partial(_inproj_kernel, tm=tm, n_lat=n_lat)
    tn = N_TM_PARTS * gw
    return pl.pallas_call(
        kern,
        out_shape=(jax.ShapeDtypeStruct((ltot, bsz * tn), F32),
                   jax.ShapeDtypeStruct((bsz, ltot, N_BM_PARTS * gw), F32)),
        grid=(bsz, ltot // tm, N_IN_PARTS * gw // tn),
        in_specs=[pl.BlockSpec((1, tm, d), lambda b, i, j: (b, i, 0)),
                  pl.BlockSpec((1, d), lambda b, i, j: (0, 0)),
                  pl.BlockSpec((1, 6, d), lambda b, i, j: (b, 0, 0)),
                  pl.BlockSpec((1, 6, d), lambda b, i, j: (bsz, 0, 0)),
                  pl.BlockSpec((1, d, tn), lambda b, i, j: (layer, 0, j))],
        out_specs=(pl.BlockSpec((tm, tn), lambda b, i, j: (i, b)),
                   pl.BlockSpec((1, tm, tn), lambda b, i, j: (b, i, jnp.maximum(j - 1, 0)))),
        scratch_shapes=[pltpu.VMEM((tm, d), BF16)],
        compiler_params=_params("arbitrary", "arbitrary", "arbitrary"),
        name="in_projection",
    )(xs, gain, modv, modv, w_bf16)


def _s5_kernel(*refs, rev, tq, nb, has_prev):
    if has_prev:
        u_ref, bw_ref, cw_ref, a_ref, prev_ref, y_ref, z_scr, h_scr = refs
    else:
        u_ref, bw_ref, cw_ref, a_ref, y_ref, z_scr, h_scr = refs
        prev_ref = None
    rows = tq * nb
    half = S5_STATE * 8

    @pl.when(pl.program_id(0) == 0)
    def _():
        h_scr[...] = jnp.zeros_like(h_scr)

    for gb in range(S5_LANE_BLOCKS):
        lanes = slice(gb * 128, (gb + 1) * 128)
        ub = u_ref[:, :, lanes].reshape(rows, 128).astype(BF16)
        z_scr[...] = _dot(ub, bw_ref[gb])
        ar = jnp.broadcast_to(a_ref[gb, 0], (nb, half))
        ai = jnp.broadcast_to(a_ref[gb, 1], (nb, half))

        def step(i, carry):
            hr, hi = carry
            t = (tq - 1 - i) if rev else i
            r0 = pl.multiple_of(t * nb, nb)
            zr = z_scr[pl.ds(r0, nb), 0:half]
            zi = z_scr[pl.ds(r0, nb), half:2 * half]
            nr = ar * hr - ai * hi + zr
            ni = ar * hi + ai * hr + zi
            z_scr[pl.ds(r0, nb), 0:half] = nr
            z_scr[pl.ds(r0, nb), half:2 * half] = ni
            return nr, ni

        hr, hi = lax.fori_loop(0, tq, step, (h_scr[gb, 0], h_scr[gb, 1]), unroll=4)
        h_scr[gb, 0] = hr
        h_scr[gb, 1] = hi
        yb = _dot(z_scr[...].astype(BF16), cw_ref[gb]).reshape(tq, nb, 128)
        if has_prev:
            yb = yb + prev_ref[:, :, lanes]
        y_ref[:, :, lanes] = yb


def _s5_direction(tm3, bw, cw, atab, prev, rev, n_lat, tq):
    ltot, nb, _ = tm3.shape
    gw = GROUP_WIDTH
    n_chunks = ltot // tq
    n_lat_chunks = n_lat // tq
    cidx = lambda s: _scan_chunk_index(s, n_lat_chunks, n_chunks, rev)
    has_prev = prev is not None
    in_specs = [pl.BlockSpec((tq, nb, gw), lambda s: (cidx(s), 0, 0)),
                pl.BlockSpec(bw.shape, lambda s: (0, 0, 0)),
                pl.BlockSpec(cw.shape, lambda s: (0, 0, 0)),
                pl.BlockSpec(atab.shape, lambda s: (0, 0, 0, 0))]
    args = [tm3, bw, cw, atab]
    if has_prev:
        in_specs.append(pl.BlockSpec((tq, nb, gw), lambda s: (cidx(s), 0, 0)))
        args.append(prev)
    kern = functools.partial(_s5_kernel, rev=rev, tq=tq, nb=nb, has_prev=has_prev)
    return pl.pallas_call(
        kern,
        out_shape=jax.ShapeDtypeStruct((ltot, nb, gw), F32),
        grid=(n_chunks,),
        in_specs=in_specs,
        out_specs=pl.BlockSpec((tq, nb, gw), lambda s: (cidx(s), 0, 0)),
        scratch_shapes=[pltpu.VMEM((tq * nb, 2 * S5_STATE * 8), F32),
                        pltpu.VMEM((S5_LANE_BLOCKS, 2, nb, S5_STATE * 8), F32)],
        compiler_params=_params("arbitrary"),
        name="s5_rev" if rev else "s5_fwd",
    )(*args)


def _block_diag(blocks):
    n, r, c = blocks.shape
    eye = jnp.eye(n, dtype=blocks.dtype)
    return (eye[:, None, :, None] * blocks[:, :, None, :]).reshape(n * r, n * c)


def _s5_tables(lam_re, lam_im, log_dt, b_re, b_im, c_re, c_im):
    dt = jnp.exp(log_dt.astype(F32))[:, None]
    lr, li = lam_re.astype(F32), lam_im.astype(F32)
    mag = jnp.exp(lr * dt)
    abar_re, abar_im = mag * jnp.cos(li * dt), mag * jnp.sin(li * dt)
    den = lr * lr + li * li
    zr = abar_re - 1.0
    w_re = (zr * lr + abar_im * li) / den
    w_im = (abar_im * lr - zr * li) / den
    bre, bim = b_re.astype(F32), b_im.astype(F32)
    bw_re = w_re[:, None, :] * bre - w_im[:, None, :] * bim
    bw_im = w_re[:, None, :] * bim + w_im[:, None, :] * bre
    gpb = S5_GROUPS // S5_LANE_BLOCKS
    bws, cws, atabs = [], [], []
    for gb in range(S5_LANE_BLOCKS):
        sl = slice(gb * gpb, (gb + 1) * gpb)
        bws.append(jnp.concatenate([_block_diag(bw_re[sl]), _block_diag(bw_im[sl])], axis=1))
        cws.append(jnp.concatenate([_block_diag(c_re[sl].astype(F32)), -_block_diag(c_im[sl].astype(F32))], axis=0))
        atabs.append(jnp.stack([abar_re[sl].reshape(1, -1), abar_im[sl].reshape(1, -1)]))
    return jnp.stack(bws).astype(BF16), jnp.stack(cws).astype(BF16), jnp.stack(atabs)


def _lru_kernel(*refs, rev, tq, nb, n_lat_chunks, n_chunks, has_prev):
    if has_prev:
        x_ref, xp_ref, xn_ref, cw_ref, cb_ref, wa_ref, wx_ref, ba_ref, bx_ref, sp_ref, prev_ref, o_ref, a_scr, b_scr, h_scr = refs
    else:
        x_ref, xp_ref, xn_ref, cw_ref, cb_ref, wa_ref, wx_ref, ba_ref, bx_ref, sp_ref, o_ref, a_scr, b_scr, h_scr = refs
        prev_ref = None
    s = pl.program_id(0)
    gw = GROUP_WIDTH
    rows = tq * nb

    @pl.when(s == 0)
    def _():
        h_scr[...] = jnp.zeros_like(h_scr)

    c = _scan_chunk_index(s, n_lat_chunks, n_chunks, rev)
    first = jnp.logical_or(c == 0, c == n_lat_chunks)
    last = jnp.logical_or(c == n_lat_chunks - 1, c == n_chunks - 1)
    keep_prev = jnp.where(first, 0.0, 1.0)
    keep_next = jnp.where(last, 0.0, 1.0)
    xc = jnp.concatenate([xp_ref[...] * keep_prev, x_ref[...], xn_ref[...] * keep_next], axis=0)
    conv = cb_ref[...] + xc[0:tq] * cw_ref[0:1, :]
    for k in range(1, LRU_CONV):
        conv = conv + xc[k:k + tq] * cw_ref[k:k + 1, :]
    x = conv.reshape(rows, gw)
    xb = x.astype(BF16)
    r = jax.nn.sigmoid(_dot(xb, wa_ref[...]) + ba_ref[...])
    ig = jax.nn.sigmoid(_dot(xb, wx_ref[...]) + bx_ref[...])
    log_a = -LRU_C * r * sp_ref[...]
    a = jnp.exp(log_a)
    a_scr[...] = a
    b_scr[...] = jnp.sqrt(1.0 - a * a) * (ig * x)

    def step(i, h):
        t = (tq - 1 - i) if rev else i
        r0 = pl.multiple_of(t * nb, nb)
        hn = a_scr[pl.ds(r0, nb), :] * h + b_scr[pl.ds(r0, nb), :]
        b_scr[pl.ds(r0, nb), :] = hn
        return hn

    h_scr[...] = lax.fori_loop(0, tq, step, h_scr[...], unroll=4)
    out = b_scr[...].reshape(tq, nb, gw)
    if has_prev:
        out = out + prev_ref[...]
    o_ref[...] = out


def _lru_direction(tm3, conv_w, conv_b, wa_bd, wx_bd, ba, bx, sp, prev, rev, n_lat, tq):
    ltot, nb, _ = tm3.shape
    gw = GROUP_WIDTH
    n_chunks = ltot // tq
    n_lat_chunks = n_lat // tq
    cidx = lambda s: _scan_chunk_index(s, n_lat_chunks, n_chunks, rev)
    has_prev = prev is not None
    full2 = lambda s: (0, 0)
    in_specs = [pl.BlockSpec((tq, nb, gw), lambda s: (cidx(s), 0, 1)),
                pl.BlockSpec((2, nb, gw), lambda s: (jnp.maximum(cidx(s) * (tq // 2) - 1, 0), 0, 1)),
                pl.BlockSpec((1, nb, gw), lambda s: (jnp.minimum((cidx(s) + 1) * tq, ltot - 1), 0, 1)),
                pl.BlockSpec((LRU_CONV, gw), full2),
                pl.BlockSpec((1, gw), full2),
                pl.BlockSpec((gw, gw), full2),
                pl.BlockSpec((gw, gw), full2),
                pl.BlockSpec((1, gw), full2),
                pl.BlockSpec((1, gw), full2),
                pl.BlockSpec((1, gw), full2)]
    args = [tm3, tm3, tm3, conv_w, conv_b, wa_bd, wx_bd, ba, bx, sp]
    if has_prev:
        in_specs.append(pl.BlockSpec((tq, nb, gw), lambda s: (cidx(s), 0, 0)))
        args.append(prev)
    kern = functools.partial(_lru_kernel, rev=rev, tq=tq, nb=nb, n_lat_chunks=n_lat_chunks,
                             n_chunks=n_chunks, has_prev=has_prev)
    return pl.pallas_call(
        kern,
        out_shape=jax.ShapeDtypeStruct((ltot, nb, gw), F32),
        grid=(n_chunks,),
        in_specs=in_specs,
        out_specs=pl.BlockSpec((tq, nb, gw), lambda s: (cidx(s), 0, 0)),
        scratch_shapes=[pltpu.VMEM((tq * nb, gw), F32),
                        pltpu.VMEM((tq * nb, gw), F32),
                        pltpu.VMEM((nb, gw), F32)],
        compiler_params=_params("arbitrary"),
        name="lru_rev" if rev else "lru_fwd",
    )(*args)


def _tri_mask(n, rev):
    t = lax.broadcasted_iota(I32, (n, n), 0)
    s = lax.broadcasted_iota(I32, (n, n), 1)
    return (s >= t) if rev else (s <= t)


def _hgrn_kernel(*refs, rev, tq, has_prev):
    if has_prev:
        q_ref, z_ref, i_ref, lb_ref, tri_ref, ones_ref, prev_ref, o_ref, st_scr, cum_scr, q_scr, k_scr, v_scr = refs
    else:
        q_ref, z_ref, i_ref, lb_ref, tri_ref, ones_ref, o_ref, st_scr, cum_scr, q_scr, k_scr, v_scr = refs
        prev_ref = None
    hps = HGRN_HEADS_PER_STEP
    cs, sb, w = GLA_CHUNK, GLA_SUB, min(HGRN_FAST_WINDOW, tq)
    nsub = cs // sb
    n_ck = tq // cs
    n_w = tq // w

    @pl.when(pl.program_id(2) == 0)
    def _():
        st_scr[...] = jnp.zeros_like(st_scr)

    lb = lb_ref[0]
    z = z_ref[0]
    iv = i_ref[0]
    f = lb + (1.0 - lb) * jax.nn.sigmoid(z)
    k_all = 1.0 - f
    v_all = iv * jax.nn.sigmoid(iv)
    q_all = q_ref[0] * (GLA_DIM ** -0.5)
    logf = jnp.log(f)
    lf_hi = logf.astype(BF16)
    rest = logf - lf_hi.astype(F32)
    lf_mid = rest.astype(BF16)
    lf_lo = (rest - lf_mid.astype(F32)).astype(BF16)
    tri = tri_ref[...]
    cum_all = _dot(tri, lf_hi) + _dot(tri, lf_mid) + _dot(tri, lf_lo)
    for hh in range(hps):
        lanes = slice(hh * GLA_DIM, (hh + 1) * GLA_DIM)
        cum_scr[hh] = cum_all[:, lanes]
        q_scr[hh] = q_all[:, lanes]
        k_scr[hh] = k_all[:, lanes]
        v_scr[hh] = v_all[:, lanes]

    def far_row(c, width):
        return c * width if rev else (c + 1) * width - 1

    def outside_row(c, width, n):
        if rev:
            return (c + 1) * width if c < n - 1 else None
        return c * width - 1 if c > 0 else None

    def store_out(hh, rows, out):
        lanes = slice(hh * GLA_DIM, (hh + 1) * GLA_DIM)
        if has_prev:
            out = out + prev_ref[0, rows, lanes]
        o_ref[0, rows, lanes] = out

    def block_factored(_):
        lane_id = lax.broadcasted_iota(I32, (1, tq), 1)
        for hh in range(hps):
            cumg, q, k = cum_scr[hh], q_scr[hh], k_scr[hh]
            vb = v_scr[hh].astype(BF16)
            st = st_scr[hh]
            ends = [cum_scr[hh, far_row(c, w):far_row(c, w) + 1, :] for c in range(n_w)]
            end_full = jnp.concatenate([jnp.broadcast_to(e, (w, GLA_DIM)) for e in ends], axis=0)
            ks = (k * jnp.exp(end_full - cumg)).astype(BF16)
            pieces, starts, row_lo = [], [], []
            total = 0
            for c in range(n_w):
                rows = slice(0, (c + 1) * w) if rev else slice(c * w, tq)
                scale = jnp.exp(jnp.minimum(cumg[rows] - ends[c], HGRN_SAFE_LOG_DECAY))
                pieces.append((q[rows] * scale).astype(BF16))
                starts.append(total)
                row_lo.append(rows.start)
                total += rows.stop - rows.start
            a_stack = _dot_nt(jnp.concatenate(pieces, axis=0), ks)
            a_rows = []
            for r in range(n_w):
                acc = jnp.zeros((w, tq), F32)
                for c in (range(r, n_w) if rev else range(0, r + 1)):
                    lo = starts[c] + r * w - row_lo[c]
                    keep = jnp.logical_and(lane_id >= c * w, lane_id < (c + 1) * w)
                    if c == r:
                        t_id = r * w + lax.broadcasted_iota(I32, (w, tq), 0)
                        keep = jnp.logical_and(keep, (lane_id >= t_id) if rev else (lane_id <= t_id))
                    acc = jnp.where(keep, a_stack[lo:lo + w, :], acc)
                a_rows.append(acc)
            a = jnp.concatenate(a_rows, axis=0).astype(BF16)
            out = _dot(a, vb) + _dot_nt((q * jnp.exp(cumg)).astype(BF16), st.astype(BF16))
            store_out(hh, slice(None), out)
            last = ends[0] if rev else ends[n_w - 1]
            kd = k * jnp.exp(last - cumg)
            st_scr[hh] = st * jnp.exp(last) + _dot_tn(vb, kd.astype(BF16))
        return 0

    row_id = lax.broadcasted_iota(I32, (sb, GLA_DIM), 0)
    valid = [(row_id <= s) if rev else (row_id >= s) for s in range(sb)]

    def chunk_direct(hh, r0):
        rows = pl.ds(r0, cs)
        q, k, cumg = q_scr[hh, rows, :], k_scr[hh, rows, :], cum_scr[hh, rows, :]
        vb = v_scr[hh, rows, :].astype(BF16)
        if rev:
            base = cum_scr[hh, pl.ds(jnp.minimum(r0 + cs, tq - 1), 1), :]
            base = jnp.where(r0 + cs >= tq, 0.0, base)
        else:
            base = cum_scr[hh, pl.ds(jnp.maximum(r0 - 1, 0), 1), :]
            base = jnp.where(r0 == 0, 0.0, base)
        cum = cumg - base
        last = cum[0:1] if rev else cum[cs - 1:cs]
        st = st_scr[hh]
        o_inter = _dot_nt((q * jnp.exp(cum)).astype(BF16), st.astype(BF16))
        outs = []
        for i in range(nsub):
            blk = slice(i * sb, (i + 1) * sb)
            off = slice((i + 1) * sb, cs) if rev else slice(0, i * sb)
            o_row = outside_row(i, sb, nsub)
            q_i, cum_i = q[blk], cum[blk]
            o_i = o_inter[blk]
            if o_row is not None:
                ref_row = cum[o_row:o_row + 1]
                qt = q_i * jnp.exp(cum_i - ref_row)
                kt = k[off] * jnp.exp(ref_row - cum[off])
                a_off = _dot_nt(qt.astype(BF16), kt.astype(BF16))
                o_i = o_i + _dot(a_off.astype(BF16), vb[off])
            prods = []
            for s in range(sb):
                r = r0 + i * sb + s
                e = jnp.exp(jnp.minimum(cumg[blk] - cum_scr[hh, pl.ds(r, 1), :], 0.0))
                prods.append(jnp.where(valid[s], q_i * e * k_scr[hh, pl.ds(r, 1), :], 0.0))
            sums = _dot(jnp.concatenate(prods, axis=0).astype(BF16), ones_ref[...])
            for s in range(sb):
                r = r0 + i * sb + s
                o_i = o_i + sums[s * sb:(s + 1) * sb] * v_scr[hh, pl.ds(r, 1), :]
            outs.append(o_i)
        store_out(hh, rows, jnp.concatenate(outs, axis=0))
        kd = k * jnp.exp(last - cum)
        st_scr[hh] = st * jnp.exp(last) + _dot_tn(vb, kd.astype(BF16))

    def block_direct(_):
        def chunk(ci, carry):
            c = (n_ck - 1 - ci) if rev else ci
            r0 = pl.multiple_of(c * cs, cs)
            for hh in range(hps):
                chunk_direct(hh, r0)
            return carry

        return lax.fori_loop(0, n_ck, chunk, 0)

    decays = []
    for c in range(n_w):
        far = cum_all[far_row(c, w):far_row(c, w) + 1]
        o_row = outside_row(c, w, n_w)
        decays.append(far if o_row is None else far - cum_all[o_row:o_row + 1])
    safe = jnp.min(jnp.concatenate(decays, axis=0)) > -HGRN_SAFE_LOG_DECAY
    lax.cond(safe, block_factored, block_direct, 0)


def _gla_specs(ltot, tq, n_lat, rev):
    n_chunks = ltot // tq
    n_lat_chunks = n_lat // tq
    cidx = lambda s: _scan_chunk_index(s, n_lat_chunks, n_chunks, rev)
    part = lambda p: pl.BlockSpec((1, tq, GLA_DIM), lambda b, h, s: (b, cidx(s), p * GLA_HEADS + h))
    return n_chunks, cidx, part


def _hgrn_direction(bm, lb, prev, rev, n_lat, tq):
    bsz, ltot, _ = bm.shape
    n_chunks, cidx, _ = _gla_specs(ltot, tq, n_lat, rev)
    has_prev = prev is not None
    hps = HGRN_HEADS_PER_STEP
    width = hps * GLA_DIM
    n_hsteps = GLA_HEADS // hps
    part = lambda p: pl.BlockSpec((1, tq, width), lambda b, h, s: (b, cidx(s), p * n_hsteps + h))
    tri = _tri_mask(tq, rev).astype(BF16)
    ones = jnp.ones((GLA_DIM, GLA_DIM), BF16)
    z_part = 2 if rev else 1
    in_specs = [part(0), part(z_part), part(3),
                pl.BlockSpec((1, 1, width), lambda b, h, s: (h, 0, 0)),
                pl.BlockSpec(tri.shape, lambda b, h, s: (0, 0)),
                pl.BlockSpec(ones.shape, lambda b, h, s: (0, 0))]
    args = [bm, bm, bm, lb.reshape(n_hsteps, 1, width), tri, ones]
    out_spec = pl.BlockSpec((1, tq, width), lambda b, h, s: (b, cidx(s), h))
    if has_prev:
        in_specs.append(out_spec)
        args.append(prev)
    kern = functools.partial(_hgrn_kernel, rev=rev, tq=tq, has_prev=has_prev)
    tile = pltpu.VMEM((hps, tq, GLA_DIM), F32)
    return pl.pallas_call(
        kern,
        out_shape=jax.ShapeDtypeStruct((bsz, ltot, GROUP_WIDTH), F32),
        grid=(bsz, n_hsteps, n_chunks),
        in_specs=in_specs,
        out_specs=out_spec,
        scratch_shapes=[pltpu.VMEM((hps, GLA_DIM, GLA_DIM), F32), tile, tile, tile, tile],
        compiler_params=_params("arbitrary", "arbitrary", "arbitrary"),
        name="hgrn_rev" if rev else "hgrn_fwd",
    )(*args)


def _ret_kernel(*refs, rev, tq, has_prev):
    if has_prev:
        q_ref, k_ref, v_ref, cos_ref, sin_ref, eq_ref, ek_ref, dm_ref, el_ref, prev_ref, o_ref, st_scr = refs
    else:
        q_ref, k_ref, v_ref, cos_ref, sin_ref, eq_ref, ek_ref, dm_ref, el_ref, o_ref, st_scr = refs
        prev_ref = None
    cs = min(tq, RET_CHUNK)
    n_ck = tq // cs

    @pl.when(pl.program_id(1) == 0)
    def _():
        st_scr[...] = jnp.zeros_like(st_scr)

    def chunk(ci, carry):
        c = (n_ck - 1 - ci) if rev else ci
        r0 = pl.multiple_of(c * cs, cs)
        rows = pl.ds(r0, cs)
        cos = cos_ref[rows, :]
        sin = sin_ref[rows, :]
        for h in range(GLA_HEADS):
            lanes = slice(h * GLA_DIM, (h + 1) * GLA_DIM)
            q = q_ref[0, rows, lanes]
            k = k_ref[0, rows, lanes]
            vb = v_ref[0, rows, lanes].astype(BF16)
            q = q * cos + pltpu.roll(q, GLA_DIM // 2, 1) * sin
            k = (k * cos + pltpu.roll(k, GLA_DIM // 2, 1) * sin) * (GLA_DIM ** -0.5)
            st = st_scr[h]
            scores = _dot_nt(q.astype(BF16), k.astype(BF16)) * dm_ref[h]
            out = _dot_nt((q * eq_ref[h]).astype(BF16), st.astype(BF16)) + _dot(scores.astype(BF16), vb)
            if has_prev:
                out = out + prev_ref[0, rows, lanes]
            o_ref[0, rows, lanes] = out
            st_scr[h] = st * el_ref[h] + _dot_tn(vb, (k * ek_ref[h]).astype(BF16))
        return carry

    lax.fori_loop(0, n_ck, chunk, 0)


def _ret_tables(rev, cs):
    gamma = jnp.log1p(-jnp.exp2(-(RET_DECAY_EXP[1 if rev else 0] + jnp.arange(GLA_HEADS, dtype=F32))))
    t = jnp.arange(cs, dtype=F32)
    steps = (cs - t) if rev else (t + 1.0)
    cum = gamma[:, None] * steps[None, :]
    last = gamma * cs
    eq = jnp.broadcast_to(jnp.exp(cum)[:, :, None], (GLA_HEADS, cs, GLA_DIM))
    ek = jnp.broadcast_to(jnp.exp(last[:, None] - cum)[:, :, None], (GLA_HEADS, cs, GLA_DIM))
    rel = cum[:, :, None] - cum[:, None, :]
    dm = jnp.where(_tri_mask(cs, rev)[None], jnp.exp(jnp.minimum(rel, 0.0)), 0.0)
    el = jnp.broadcast_to(jnp.exp(last)[:, None, None], (GLA_HEADS, 1, GLA_DIM))
    return eq, ek, dm, el


def _rotary_tables(n_lat, n_ctx):
    rows = n_lat // GRID_W
    row = jnp.repeat(jnp.arange(rows, dtype=F32), GRID_W)
    col = jnp.tile(jnp.arange(GRID_W, dtype=F32), rows)
    quarter = GLA_DIM // 4
    inv_freq = ROPE_BASE ** (-jnp.arange(quarter, dtype=F32) / quarter)
    ang = jnp.concatenate([row[:, None] * inv_freq, col[:, None] * inv_freq], axis=-1)
    cos, sin = jnp.cos(ang), jnp.sin(ang)
    cos_l = jnp.concatenate([cos, cos], axis=-1)
    sin_l = jnp.concatenate([-sin, sin], axis=-1)
    cos_t = jnp.concatenate([cos_l, jnp.ones((n_ctx, GLA_DIM), F32)], axis=0)
    sin_t = jnp.concatenate([sin_l, jnp.zeros((n_ctx, GLA_DIM), F32)], axis=0)
    return cos_t, sin_t


def _ret_direction(bm, cos_t, sin_t, prev, rev, n_lat, tq):
    bsz, ltot, _ = bm.shape
    n_chunks, cidx, _ = _gla_specs(ltot, tq, n_lat, rev)
    has_prev = prev is not None
    eq, ek, dm, el = _ret_tables(rev, min(tq, RET_CHUNK))
    gw = GROUP_WIDTH
    part = lambda p: pl.BlockSpec((1, tq, gw), lambda b, s: (b, cidx(s), p))
    whole = lambda a: pl.BlockSpec(a.shape, lambda b, s: (0, 0, 0))
    pos = pl.BlockSpec((tq, GLA_DIM), lambda b, s: (cidx(s), 0))
    in_specs = [part(5), part(6), part(7), pos, pos, whole(eq), whole(ek), whole(dm), whole(el)]
    args = [bm, bm, bm, cos_t, sin_t, eq, ek, dm, el]
    out_spec = pl.BlockSpec((1, tq, gw), lambda b, s: (b, cidx(s), 0))
    if has_prev:
        in_specs.append(out_spec)
        args.append(prev)
    kern = functools.partial(_ret_kernel, rev=rev, tq=tq, has_prev=has_prev)
    return pl.pallas_call(
        kern,
        out_shape=jax.ShapeDtypeStruct((bsz, ltot, gw), F32),
        grid=(bsz, n_chunks),
        in_specs=in_specs,
        out_specs=out_spec,
        scratch_shapes=[pltpu.VMEM((GLA_HEADS, GLA_DIM, GLA_DIM), F32)],
        compiler_params=_params("arbitrary", "arbitrary"),
        name="ret_rev" if rev else "ret_fwd",
    )(*args)


def _head_rms(o):
    parts = []
    for h in range(GLA_HEADS):
        oh = o[:, h * GLA_DIM:(h + 1) * GLA_DIM]
        parts.append(oh * lax.rsqrt(jnp.mean(oh * oh, axis=-1, keepdims=True) + EPS))
    return jnp.concatenate(parts, axis=-1)


def _mix_kernel(u_ref, lg_ref, ys_ref, hs_ref, hg_ref, rg_ref, oh_ref, or_ref, d_ref, gw_ref, gb_ref, o_ref):
    gw = GROUP_WIDTH
    y = jax.nn.gelu(d_ref[...] * u_ref[...] + ys_ref[...])
    a = y * jax.nn.sigmoid(_dot(y.astype(BF16), gw_ref[...]) + gb_ref[...])
    o_ref[0, :, 0:gw] = a.astype(BF16)
    o_ref[0, :, gw:2 * gw] = (hs_ref[...] * jax.nn.gelu(lg_ref[...])).astype(BF16)
    g = hg_ref[0]
    o_ref[0, :, 2 * gw:3 * gw] = (_head_rms(oh_ref[0]) * (g * jax.nn.sigmoid(g))).astype(BF16)
    g = rg_ref[0]
    o_ref[0, :, 3 * gw:4 * gw] = (_head_rms(or_ref[0]) * (g * jax.nn.sigmoid(g))).astype(BF16)


def _mix_epilogue(tm2, ys2, hs2, bm, o_h, o_r, s5_d, glu_w, glu_b, n_rows, tm):
    bsz, ltot, _ = bm.shape
    gw = GROUP_WIDTH
    tmaj = lambda p, k: pl.BlockSpec((tm, gw), lambda b, i: (i, b * k + p))
    bmaj = lambda p: pl.BlockSpec((1, tm, gw), lambda b, i: (b, i, p))
    full2 = lambda b, i: (0, 0)
    return pl.pallas_call(
        _mix_kernel,
        out_shape=jax.ShapeDtypeStruct((bsz, ltot, 4 * gw), BF16),
        grid=(bsz, n_rows // tm),
        in_specs=[tmaj(0, N_TM_PARTS), tmaj(2, N_TM_PARTS), tmaj(0, 1), tmaj(0, 1),
                  bmaj(4), bmaj(8), bmaj(0), bmaj(0),
                  pl.BlockSpec((1, gw), full2), pl.BlockSpec((gw, gw), full2), pl.BlockSpec((1, gw), full2)],
        out_specs=pl.BlockSpec((1, tm, 4 * gw), lambda b, i: (b, i, 0)),
        compiler_params=_params("arbitrary", "arbitrary"),
        name="mix_epilogue",
    )(tm2, tm2, ys2, hs2, bm, bm, o_h, o_r, s5_d, glu_w, glu_b)


def _gate_rows(modb_ref, modc_ref, idx, row0, tm, n_lat):
    row = row0 + lax.broadcasted_iota(I32, (tm, 1), 0)
    return jnp.where(row >= n_lat, modc_ref[0, idx:idx + 1, :], modb_ref[0, idx:idx + 1, :])


def _outproj_kernel(a_ref, w_ref, x_ref, modb_ref, modc_ref, o_ref, *, tm, n_lat):
    gate = _gate_rows(modb_ref, modc_ref, 2, pl.program_id(1) * tm, tm, n_lat)
    o_ref[0] = x_ref[0] + gate * _dot(a_ref[0], w_ref[0])


def _out_projection(mix, w_bf16, layer, xs, modv, n_rows, n_lat, tm, tn):
    bsz, ltot, d = xs.shape
    k = mix.shape[-1]
    kern = functools.partial(_outproj_kernel, tm=tm, n_lat=n_lat)
    return pl.pallas_call(
        kern,
        out_shape=jax.ShapeDtypeStruct((bsz, ltot, d), F32),
        grid=(bsz, n_rows // tm, d // tn),
        in_specs=[pl.BlockSpec((1, tm, k), lambda b, i, j: (b, i, 0)),
                  pl.BlockSpec((1, k, tn), lambda b, i, j: (layer, 0, j)),
                  pl.BlockSpec((1, tm, tn), lambda b, i, j: (b, i, j)),
                  pl.BlockSpec((1, 6, tn), lambda b, i, j: (b, 0, j)),
                  pl.BlockSpec((1, 6, tn), lambda b, i, j: (bsz, 0, j))],
        out_specs=pl.BlockSpec((1, tm, tn), lambda b, i, j: (b, i, j)),
        compiler_params=_params("arbitrary", "arbitrary", "arbitrary"),
        name="out_projection",
    )(mix, w_bf16, xs, modv, modv)


def _first_max4(v):
    m1 = jnp.maximum(jnp.maximum(v[0], v[1]), jnp.maximum(v[2], v[3]))
    i1 = jnp.where(v[0] == m1, 0, jnp.where(v[1] == m1, 1, jnp.where(v[2] == m1, 2, 3)))
    rest = [jnp.where(i1 == j, -jnp.inf, v[j]) for j in range(4)]
    m2 = jnp.maximum(jnp.maximum(rest[0], rest[1]), jnp.maximum(rest[2], rest[3]))
    i2 = jnp.where(rest[0] == m2, 0, jnp.where(rest[1] == m2, 1, jnp.where(rest[2] == m2, 2, 3)))
    return m1, i1, m2, i2


def _router_kernel(x_ref, g_ref, modb_ref, modc_ref, rw_ref, rb_ref, tri_ref,
                   h_ref, ids_ref, gates_ref, ranks_ref, counts_ref, base_scr, *, tm, n_lat):
    first = jnp.logical_and(pl.program_id(0) == 0, pl.program_id(1) == 0)

    @pl.when(first)
    def _():
        base_scr[...] = jnp.zeros_like(base_scr)

    h = _norm_modulate(x_ref[0], g_ref[...], modb_ref, modc_ref, pl.program_id(1) * tm, n_lat, 3, 4)
    h_ref[0] = h.astype(BF16)
    logits = lax.dot_general(rw_ref[...], h, (((1,), (1,)), ((), ())), preferred_element_type=F32,
                             precision=lax.Precision.HIGHEST)
    scores = jax.nn.sigmoid(logits)
    biased = scores + rb_ref[...]
    tops = []
    for gidx in range(N_EXPERT_GROUPS):
        rows = [biased[gidx * EXPERTS_PER_GROUP + j:gidx * EXPERTS_PER_GROUP + j + 1, :]
                for j in range(EXPERTS_PER_GROUP)]
        tops.append(_first_max4(rows))
    gs = [t[0] + t[2] for t in tops]
    gmax = jnp.maximum(jnp.maximum(gs[0], gs[1]), jnp.maximum(gs[2], gs[3]))
    best = jnp.where(gs[0] == gmax, 0, jnp.where(gs[1] == gmax, 1, jnp.where(gs[2] == gmax, 2, 3)))
    e1 = jnp.zeros_like(best)
    e2 = jnp.zeros_like(best)
    for gidx in range(N_EXPERT_GROUPS):
        e1 = jnp.where(best == gidx, gidx * EXPERTS_PER_GROUP + tops[gidx][1], e1)
        e2 = jnp.where(best == gidx, gidx * EXPERTS_PER_GROUP + tops[gidx][3], e2)
    eid = lax.broadcasted_iota(I32, (N_EXPERTS, tm), 0)
    sel1 = eid == e1
    sel2 = eid == e2
    w1 = jnp.sum(jnp.where(sel1, scores, 0.0), axis=0, keepdims=True)
    w2 = jnp.sum(jnp.where(sel2, scores, 0.0), axis=0, keepdims=True)
    wsum = w1 + w2
    onehot = jnp.where(jnp.logical_or(sel1, sel2), 1.0, 0.0)
    pos = base_scr[...] + _dot(onehot.astype(BF16), tri_ref[...])
    r1 = jnp.sum(jnp.where(sel1, pos, 0.0), axis=0, keepdims=True)
    r2 = jnp.sum(jnp.where(sel2, pos, 0.0), axis=0, keepdims=True)
    ids_ref[0] = jnp.concatenate([e1, e2], axis=0)
    gates_ref[0] = jnp.concatenate([w1 / wsum, w2 / wsum], axis=0)
    ranks_ref[0] = jnp.concatenate([r1, r2], axis=0).astype(I32)
    base = base_scr[...] + jnp.sum(onehot, axis=1, keepdims=True)
    base_scr[...] = base
    counts_ref[...] = jnp.broadcast_to(base, counts_ref.shape)


def _router(xs, gain, modv, rw_t, rbias, n_rows, n_lat, tm):
    bsz, ltot, d = xs.shape
    tri = (lax.broadcasted_iota(I32, (tm, tm), 0) < lax.broadcasted_iota(I32, (tm, tm), 1)).astype(BF16)
    kern = functools.partial(_router_kernel, tm=tm, n_lat=n_lat)
    small = lambda dt: jax.ShapeDtypeStruct((bsz, 2, n_rows), dt)
    small_spec = pl.BlockSpec((1, 2, tm), lambda b, i: (b, 0, i))
    return pl.pallas_call(
        kern,
        out_shape=(jax.ShapeDtypeStruct((bsz, ltot, d), BF16), small(I32), small(F32), small(I32),
                   jax.ShapeDtypeStruct((N_EXPERTS, 128), F32)),
        grid=(bsz, n_rows // tm),
        in_specs=[pl.BlockSpec((1, tm, d), lambda b, i: (b, i, 0)),
                  pl.BlockSpec((1, d), lambda b, i: (0, 0)),
                  pl.BlockSpec((1, 6, d), lambda b, i: (b, 0, 0)),
                  pl.BlockSpec((1, 6, d), lambda b, i: (bsz, 0, 0)),
                  pl.BlockSpec((N_EXPERTS, d), lambda b, i: (0, 0)),
                  pl.BlockSpec((N_EXPERTS, 1), lambda b, i: (0, 0)),
                  pl.BlockSpec((tm, tm), lambda b, i: (0, 0))],
        out_specs=(pl.BlockSpec((1, tm, d), lambda b, i: (b, i, 0)), small_spec, small_spec, small_spec,
                   pl.BlockSpec((N_EXPERTS, 128), lambda b, i: (0, 0))),
        scratch_shapes=[pltpu.VMEM((N_EXPERTS, 1), F32)],
        compiler_params=_params("arbitrary", "arbitrary"),
        name="moe_router",
    )(xs, gain, modv, modv, rw_t, rbias, tri)


def _expert_ffn_kernel(te_ref, nt_ref, x_ref, wg_ref, wu_ref, wd_ref, o_ref):
    live = pl.program_id(0) < nt_ref[0]

    @pl.when(live)
    def _():
        x = x_ref[...]
        g = _dot(x, wg_ref[0])
        u = _dot(x, wu_ref[0])
        he = (g * jax.nn.sigmoid(g)) * u
        o_ref[...] = _dot(he.astype(BF16), wd_ref[0]).astype(o_ref.dtype)

    @pl.when(jnp.logical_not(live))
    def _():
        o_ref[...] = jnp.zeros_like(o_ref)


def _expert_ffn(tile_expert, n_tiles, x_sorted, wg, wu, wd, tm):
    p, d = x_sorted.shape
    f = wg.shape[-1]
    tile = lambda i, te, nt: i
    expert = lambda i, te, nt: te[jnp.minimum(i, nt[0] - 1)]
    grid_spec = pltpu.PrefetchScalarGridSpec(
        num_scalar_prefetch=2,
        grid=(p // tm,),
        in_specs=[pl.BlockSpec((tm, d), lambda i, te, nt: (tile(i, te, nt), 0)),
                  pl.BlockSpec((1, d, f), lambda i, te, nt: (expert(i, te, nt), 0, 0)),
                  pl.BlockSpec((1, d, f), lambda i, te, nt: (expert(i, te, nt), 0, 0)),
                  pl.BlockSpec((1, f, d), lambda i, te, nt: (expert(i, te, nt), 0, 0))],
        out_specs=pl.BlockSpec((tm, d), lambda i, te, nt: (tile(i, te, nt), 0)),
    )
    return pl.pallas_call(
        _expert_ffn_kernel,
        out_shape=jax.ShapeDtypeStruct((p, d), BF16),
        grid_spec=grid_spec,
        compiler_params=_params("arbitrary"),
        name="moe_expert_ffn",
    )(tile_expert, n_tiles, x_sorted, wg, wu, wd)


def _moe_residual_kernel(x_ref, y0_ref, y1_ref, gates_ref, modb_ref, modc_ref, g_ref, o_ref,
                         *, tm, n_lat, final_norm):
    gates = gates_ref[0]
    y = gates[:, 0:1] * y0_ref[0, 0].astype(F32) + gates[:, 1:2] * y1_ref[0, 0].astype(F32)
    gate = _gate_rows(modb_ref, modc_ref, 5, pl.program_id(1) * tm, tm, n_lat)
    x = x_ref[0] + gate * y
    if final_norm:
        x = x * lax.rsqrt(jnp.mean(x * x, axis=-1, keepdims=True) + EPS) * g_ref[...]
    o_ref[0] = x


def _moe_residual(xs, y_pair, gates_t, modv, final_g, n_rows, n_lat, tm, final_norm):
    bsz, ltot, d = xs.shape
    kern = functools.partial(_moe_residual_kernel, tm=tm, n_lat=n_lat, final_norm=final_norm)
    rows_out = n_rows if final_norm else ltot
    return pl.pallas_call(
        kern,
        out_shape=jax.ShapeDtypeStruct((bsz, rows_out, d), F32),
        grid=(bsz, n_rows // tm),
        in_specs=[pl.BlockSpec((1, tm, d), lambda b, i: (b, i, 0)),
                  pl.BlockSpec((1, 1, tm, d), lambda b, i: (0, b, i, 0)),
                  pl.BlockSpec((1, 1, tm, d), lambda b, i: (1, b, i, 0)),
                  pl.BlockSpec((1, tm, 2), lambda b, i: (b, i, 0)),
                  pl.BlockSpec((1, 6, d), lambda b, i: (b, 0, 0)),
                  pl.BlockSpec((1, 6, d), lambda b, i: (bsz, 0, 0)),
                  pl.BlockSpec((1, d), lambda b, i: (0, 0))],
        out_specs=pl.BlockSpec((1, tm, d), lambda b, i: (b, i, 0)),
        compiler_params=_params("arbitrary", "arbitrary"),
        name="moe_residual",
    )(xs, y_pair, y_pair, gates_t, modv, modv, final_g)


def _moe_layer(xs, gain, modv, rw_t, rbias, wg, wu, wd, expert_base, final_g, n_rows, n_lat, tm_tok, tm_res, tm_exp,
               final_norm):
    bsz, ltot, d = xs.shape
    h, ids, gates, ranks, counts = _router(xs, gain, modv, rw_t, rbias, n_rows, n_lat, tm_tok)
    counts = counts[:, 0].astype(I32)
    padded = ((counts + tm_exp - 1) // tm_exp) * tm_exp
    ends = jnp.cumsum(padded)
    offsets = ends - padded
    onehot = ids[..., None] == jnp.arange(N_EXPERTS, dtype=I32)
    dest = jnp.sum(jnp.where(onehot, offsets, 0), axis=-1) + ranks
    n_slots = bsz * 2 * n_rows
    p_rows = ((n_slots + N_EXPERTS * (tm_exp - 1)) + tm_exp - 1) // tm_exp * tm_exp
    tok = (jnp.arange(bsz, dtype=I32)[:, None, None] * ltot + jnp.arange(n_rows, dtype=I32)[None, None, :])
    tok = jnp.broadcast_to(tok, dest.shape)
    src = jnp.zeros((p_rows,), I32).at[dest.reshape(-1)].set(tok.reshape(-1), mode="promise_in_bounds",
                                                            unique_indices=True)
    tile_start = jnp.arange(p_rows // tm_exp, dtype=I32) * tm_exp
    tile_expert = jnp.minimum(jnp.sum(tile_start[:, None] >= ends[None, :], axis=1), N_EXPERTS - 1).astype(I32)
    n_tiles = (ends[-1:] // tm_exp).astype(I32)
    x_sorted = h.reshape(bsz * ltot, d).at[src].get(mode="promise_in_bounds")
    y_sorted = _expert_ffn(tile_expert + expert_base, n_tiles, x_sorted, wg, wu, wd, tm_exp)
    y_pair = y_sorted.at[dest.transpose(1, 0, 2).reshape(-1)].get(mode="promise_in_bounds")
    y_pair = y_pair.reshape(2, bsz, n_rows, d)
    return _moe_residual(xs, y_pair, gates.transpose(0, 2, 1), modv, final_g, n_rows, n_lat, tm_res, final_norm)


def _row_tile(n, target):
    t = min(n, target)
    while n % t or t % 8:
        t -= 8
    return t


def kernel(x, c, ctx, c_ctx, norm_mix_g, norm_ffn_g, w_mod, b_mod, w_in, w_out, s5_lam_re, s5_lam_im, s5_log_dt, s5_b_re, s5_b_im, s5_c_re, s5_c_im, s5_d, s5_glu_w, s5_glu_b, lru_conv_w, lru_conv_b, lru_wa, lru_ba, lru_wx, lru_bx, lru_lam, hgrn_lb_logits, router_w, router_bias, moe_w_gate, moe_w_up, moe_w_down, final_norm_g):
    bsz, n_lat, d = x.shape
    n_ctx = ctx.shape[1]
    ltot = n_lat + n_ctx
    depth = w_in.shape[0]
    gw = GROUP_WIDTH
    assert bsz % 8 == 0 and bsz + 1 <= MOD_ROWS
    assert n_lat % GLA_CHUNK == 0 and n_ctx % GLA_CHUNK == 0 and n_lat % GRID_W == 0 and n_lat % NORM_SLAB == 0

    tq_scan = math.gcd(math.gcd(n_lat, n_ctx), 64)
    tq_gla = math.gcd(math.gcd(n_lat, n_ctx), 256)
    tm_all = _row_tile(ltot, 768)
    tm_lat = _row_tile(n_lat, 1024)
    tm_mix = math.gcd(math.gcd(n_lat, n_ctx), 256)

    xs = jnp.concatenate([x, ctx.astype(x.dtype)], axis=1)
    cv = jnp.zeros((MOD_ROWS, d), F32).at[:bsz].set(c).at[bsz].set(c_ctx)
    mod_all = _mod_vectors(cv, w_mod, b_mod).reshape(depth, MOD_ROWS, 6, d)
    cos_t, sin_t = _rotary_tables(n_lat, n_ctx)
    rw_t = router_w.astype(F32).T
    rbias = router_bias.astype(F32).reshape(N_EXPERTS, 1)
    lb_cum = [jnp.cumsum(jax.nn.softmax(hgrn_lb_logits[dr].astype(F32), axis=0), axis=0) for dr in (0, 1)]
    w_in_b, w_out_b = _cast_bf16(w_in), _cast_bf16(w_out)
    n_exp, _, d_ff = moe_w_gate.shape[1:]
    wg_b = _cast_bf16(moe_w_gate.reshape(depth * n_exp, d, d_ff))
    wu_b = _cast_bf16(moe_w_up.reshape(depth * n_exp, d, d_ff))
    wd_b = _cast_bf16(moe_w_down.reshape(depth * n_exp, d_ff, d))

    for layer in range(depth):
        ctx_out = layer < depth - 1
        n_rows = ltot if ctx_out else n_lat
        tm_tok = tm_all if ctx_out else tm_lat
        modv = mod_all[layer]

        tm2, bm = _in_projection(xs, norm_mix_g[layer].reshape(1, d), modv, w_in_b, layer, n_lat, tm_all)
        tm3 = tm2.reshape(ltot, bsz, N_TM_PARTS * gw)

        ys = hs = o_h = o_r = None
        for dr in (0, 1):
            rev = dr == 1
            bw, cw, atab = _s5_tables(s5_lam_re[layer, dr], s5_lam_im[layer, dr], s5_log_dt[layer, dr],
                                      s5_b_re[layer, dr], s5_b_im[layer, dr], s5_c_re[layer, dr], s5_c_im[layer, dr])
            ys = _s5_direction(tm3, bw, cw, atab, ys, rev, n_lat, tq_scan)
            sp = jax.nn.softplus(-lru_lam[layer, dr].astype(F32)).reshape(1, gw)
            hs = _lru_direction(tm3, lru_conv_w[layer].astype(F32), lru_conv_b[layer].astype(F32).reshape(1, gw),
                                _block_diag(lru_wa[layer, dr]).astype(BF16), _block_diag(lru_wx[layer, dr]).astype(BF16),
                                lru_ba[layer, dr].astype(F32).reshape(1, gw), lru_bx[layer, dr].astype(F32).reshape(1, gw),
                                sp, hs, rev, n_lat, tq_scan)
            lb = lb_cum[dr][layer] - lb_cum[dr][0]
            o_h = _hgrn_direction(bm, lb, o_h, rev, n_lat, tq_gla)
            o_r = _ret_direction(bm, cos_t, sin_t, o_r, rev, n_lat, tq_gla)

        mix = _mix_epilogue(tm2, ys.reshape(ltot, bsz * gw), hs.reshape(ltot, bsz * gw), bm, o_h, o_r,
                            s5_d[layer].astype(F32).reshape(1, gw), s5_glu_w[layer].astype(BF16),
                            s5_glu_b[layer].astype(F32).reshape(1, gw), n_rows, tm_mix)
        xs = _out_projection(mix, w_out_b, layer, xs, modv, n_rows, n_lat, tm_tok, 1024)
        xs = _moe_layer(xs, norm_ffn_g[layer].reshape(1, d), modv, rw_t, rbias, wg_b, wu_b, wd_b, layer * n_exp,
                        final_norm_g.reshape(1, d), n_rows, n_lat, tm_tok, tm_mix, 512, not ctx_out)
    return xs
```

```python
import functools
import math

import jax
import jax.numpy as jnp
from jax import lax
from jax.experimental import pallas as pl
from jax.experimental.pallas import tpu as pltpu

F32 = jnp.float32
BF16 = jnp.bfloat16
I32 = jnp.int32

GROUP_WIDTH = 512
N_IN_PARTS = 12
N_TM_PARTS = 3
N_BM_PARTS = N_IN_PARTS - N_TM_PARTS
S5_CH = 16
S5_GROUPS = 32
S5_STATE = 64
S5_LANE_BLOCKS = 4
LRU_HEADS = 8
LRU_CONV = 4
LRU_C = 8.0
GLA_HEADS = 4
GLA_DIM = 128
GLA_CHUNK = 64
GLA_SUB = 16
HGRN_FAST_WINDOW = 64
HGRN_SAFE_LOG_DECAY = 80.0
HGRN_SAFE_QUERY = 1.0e3
HGRN_HEADS_PER_STEP = 2
RET_CHUNK = 256
RET_DECAY_EXP = (5.0, 5.5)
ROPE_BASE = 10000.0
GRID_W = 64
N_EXPERTS = 16
N_EXPERT_GROUPS = 4
EXPERTS_PER_GROUP = 4
D_FF_EXPERT = 1024
EPS = 1e-6
NORM_SLAB = 16
MOD_ROWS = 24

VMEM_LIMIT_BYTES = 56 * 1024 * 1024


def _params(*semantics):
    return pltpu.CompilerParams(dimension_semantics=semantics, vmem_limit_bytes=VMEM_LIMIT_BYTES)


def _dot(a, b):
    return jnp.dot(a, b, preferred_element_type=F32)


def _dot_nt(a, b):
    return lax.dot_general(a, b, (((1,), (1,)), ((), ())), preferred_element_type=F32)


def _dot_tn(a, b):
    return lax.dot_general(a, b, (((0,), (0,)), ((), ())), preferred_element_type=F32)


def _scan_chunk_index(step, n_lat_chunks, n_chunks, rev):
    if rev:
        return n_chunks - 1 - step
    return (step + n_lat_chunks) % n_chunks


def _cast_kernel(x_ref, o_ref):
    o_ref[...] = x_ref[...].astype(o_ref.dtype)


def _cast_bf16(w):
    n, r, c = w.shape
    tc = min(c, 1024)
    spec = pl.BlockSpec((1, r, tc), lambda i, j: (i, 0, j))
    return pl.pallas_call(
        _cast_kernel,
        out_shape=jax.ShapeDtypeStruct(w.shape, BF16),
        grid=(n, c // tc),
        in_specs=[spec],
        out_specs=spec,
        compiler_params=_params("arbitrary", "arbitrary"),
        name="cast_bf16",
    )(w)


def _mod_kernel(cv_ref, w_ref, b_ref, o_ref):
    cv = cv_ref[...]
    s = cv * jax.nn.sigmoid(cv)
    o_ref[0] = _dot(s.astype(BF16), w_ref[0].astype(BF16)) + b_ref[0]


def _mod_vectors(cv, w_mod, b_mod):
    nl, d, n6 = w_mod.shape
    tn = 1024
    return pl.pallas_call(
        _mod_kernel,
        out_shape=jax.ShapeDtypeStruct((nl, MOD_ROWS, n6), F32),
        grid=(nl, n6 // tn),
        in_specs=[pl.BlockSpec((MOD_ROWS, d), lambda l, j: (0, 0)),
                  pl.BlockSpec((1, d, tn), lambda l, j: (l, 0, j)),
                  pl.BlockSpec((1, 1, tn), lambda l, j: (l, 0, j))],
        out_specs=pl.BlockSpec((1, MOD_ROWS, tn), lambda l, j: (l, 0, j)),
        compiler_params=_params("arbitrary", "arbitrary"),
        name="mod_vectors",
    )(cv, w_mod, b_mod.reshape(nl, 1, n6))


def _norm_modulate(x, g, modb_ref, modc_ref, row0, n_lat, shift_idx, scale_idx):
    tm = x.shape[0]
    ms = jnp.mean(x * x, axis=-1, keepdims=True)
    y = x * lax.rsqrt(ms + EPS) * g
    row = row0 + lax.broadcasted_iota(I32, (tm, 1), 0)
    is_ctx = row >= n_lat
    shift = jnp.where(is_ctx, modc_ref[0, shift_idx:shift_idx + 1, :], modb_ref[0, shift_idx:shift_idx + 1, :])
    scale = jnp.where(is_ctx, modc_ref[0, scale_idx:scale_idx + 1, :], modb_ref[0, scale_idx:scale_idx + 1, :])
    return y * (1.0 + scale) + shift


def _inproj_kernel(x_ref, g_ref, modb_ref, modc_ref, w_ref, otm_ref, obm_ref, h_scr, *, tm, n_lat):
    i = pl.program_id(1)
    j = pl.program_id(2)

    @pl.when(j == 0)
    def _():
        def slab(s, carry):
            r0 = pl.multiple_of(s * NORM_SLAB, NORM_SLAB)
            is_ctx = i * tm + r0 >= n_lat
            shift = jnp.where(is_ctx, modc_ref[0, 0:1, :], modb_ref[0, 0:1, :])
            gain = g_ref[...] * (1.0 + jnp.where(is_ctx, modc_ref[0, 1:2, :], modb_ref[0, 1:2, :]))
            x = x_ref[0, pl.ds(r0, NORM_SLAB), :]
            inv = lax.rsqrt(jnp.mean(x * x, axis=-1, keepdims=True) + EPS)
            h_scr[pl.ds(r0, NORM_SLAB), :] = (x * inv * gain + shift).astype(BF16)
            return carry

        lax.fori_loop(0, tm // NORM_SLAB, slab, 0, unroll=2)

    r = _dot(h_scr[...], w_ref[0])

    @pl.when(j == 0)
    def _():
        otm_ref[...] = r

    @pl.when(j > 0)
    def _():
        obm_ref[0] = r


def _in_projection(xs, gain, modv, w_bf16, layer, n_lat, tm):
    bsz, ltot, d = xs.shape
    gw = GROUP_WIDTH
    kern = functools.partial(_inproj_kernel, tm=tm, n_lat=n_lat)
    tn = N_TM_PARTS * gw
    return pl.pallas_call(
        kern,
        out_shape=(jax.ShapeDtypeStruct((ltot, bsz * tn), F32),
                   jax.ShapeDtypeStruct((bsz, ltot, N_BM_PARTS * gw), F32)),
        grid=(bsz, ltot // tm, N_IN_PARTS * gw // tn),
        in_specs=[pl.BlockSpec((1, tm, d), lambda b, i, j: (b, i, 0)),
                  pl.BlockSpec((1, d), lambda b, i, j: (0, 0)),
                  pl.BlockSpec((1, 6, d), lambda b, i, j: (b, 0, 0)),
                  pl.BlockSpec((1, 6, d), lambda b, i, j: (bsz, 0, 0)),
                  pl.BlockSpec((1, d, tn), lambda b, i, j: (layer, 0, j))],
        out_specs=(pl.BlockSpec((tm, tn), lambda b, i, j: (i, b)),
                   pl.BlockSpec((1, tm, tn), lambda b, i, j: (b, i, jnp.maximum(j - 1, 0)))),
        scratch_shapes=[pltpu.VMEM((tm, d), BF16)],
        compiler_params=_params("arbitrary", "arbitrary", "arbitrary"),
        name="in_projection",
    )(xs, gain, modv, modv, w_bf16)


def _s5_kernel(*refs, rev, tq, nb, has_prev):
    if has_prev:
        u_ref, bw_ref, cw_ref, a_ref, prev_ref, y_ref, z_scr, h_scr = refs
    else:
        u_ref, bw_ref, cw_ref, a_ref, y_ref, z_scr, h_scr = refs
        prev_ref = None
    rows = tq * nb
    half = S5_STATE * 8

    @pl.when(pl.program_id(0) == 0)
    def _():
        h_scr[...] = jnp.zeros_like(h_scr)

    for gb in range(S5_LANE_BLOCKS):
        lanes = slice(gb * 128, (gb + 1) * 128)
        ub = u_ref[:, :, lanes].reshape(rows, 128).astype(BF16)
        z_scr[...] = _dot(ub, bw_ref[gb])
        ar = jnp.broadcast_to(a_ref[gb, 0], (nb, half))
        ai = jnp.broadcast_to(a_ref[gb, 1], (nb, half))

        def step(i, carry):
            hr, hi = carry
            t = (tq - 1 - i) if rev else i
            r0 = pl.multiple_of(t * nb, nb)
            zr = z_scr[pl.ds(r0, nb), 0:half]
            zi = z_scr[pl.ds(r0, nb), half:2 * half]
            nr = ar * hr - ai * hi + zr
            ni = ar * hi + ai * hr + zi
            z_scr[pl.ds(r0, nb), 0:half] = nr
            z_scr[pl.ds(r0, nb), half:2 * half] = ni
            return nr, ni

        hr, hi = lax.fori_loop(0, tq, step, (h_scr[gb, 0], h_scr[gb, 1]), unroll=4)
        h_scr[gb, 0] = hr
        h_scr[gb, 1] = hi
        yb = _dot(z_scr[...].astype(BF16), cw_ref[gb]).reshape(tq, nb, 128)
        if has_prev:
            yb = yb + prev_ref[:, :, lanes]
        y_ref[:, :, lanes] = yb


def _s5_direction(tm3, bw, cw, atab, prev, rev, n_lat, tq):
    ltot, nb, _ = tm3.shape
    gw = GROUP_WIDTH
    n_chunks = ltot // tq
    n_lat_chunks = n_lat // tq
    cidx = lambda s: _scan_chunk_index(s, n_lat_chunks, n_chunks, rev)
    has_prev = prev is not None
    in_specs = [pl.BlockSpec((tq, nb, gw), lambda s: (cidx(s), 0, 0)),
                pl.BlockSpec(bw.shape, lambda s: (0, 0, 0)),
                pl.BlockSpec(cw.shape, lambda s: (0, 0, 0)),
                pl.BlockSpec(atab.shape, lambda s: (0, 0, 0, 0))]
    args = [tm3, bw, cw, atab]
    if has_prev:
        in_specs.append(pl.BlockSpec((tq, nb, gw), lambda s: (cidx(s), 0, 0)))
        args.append(prev)
    kern = functools.partial(_s5_kernel, rev=rev, tq=tq, nb=nb, has_prev=has_prev)
    return pl.pallas_call(
        kern,
        out_shape=jax.ShapeDtypeStruct((ltot, nb, gw), F32),
        grid=(n_chunks,),
        in_specs=in_specs,
        out_specs=pl.BlockSpec((tq, nb, gw), lambda s: (cidx(s), 0, 0)),
        scratch_shapes=[pltpu.VMEM((tq * nb, 2 * S5_STATE * 8), F32),
                        pltpu.VMEM((S5_LANE_BLOCKS, 2, nb, S5_STATE * 8), F32)],
        compiler_params=_params("arbitrary"),
        name="s5_rev" if rev else "s5_fwd",
    )(*args)


def _block_diag(blocks):
    n, r, c = blocks.shape
    eye = jnp.eye(n, dtype=blocks.dtype)
    return (eye[:, None, :, None] * blocks[:, :, None, :]).reshape(n * r, n * c)


def _s5_tables(lam_re, lam_im, log_dt, b_re, b_im, c_re, c_im):
    dt = jnp.exp(log_dt.astype(F32))[:, None]
    lr, li = lam_re.astype(F32), lam_im.astype(F32)
    mag = jnp.exp(lr * dt)
    abar_re, abar_im = mag * jnp.cos(li * dt), mag * jnp.sin(li * dt)
    den = lr * lr + li * li
    zr = abar_re - 1.0
    w_re = (zr * lr + abar_im * li) / den
    w_im = (abar_im * lr - zr * li) / den
    bre, bim = b_re.astype(F32), b_im.astype(F32)
    bw_re = w_re[:, None, :] * bre - w_im[:, None, :] * bim
    bw_im = w_re[:, None, :] * bim + w_im[:, None, :] * bre
    gpb = S5_GROUPS // S5_LANE_BLOCKS
    bws, cws, atabs = [], [], []
    for gb in range(S5_LANE_BLOCKS):
        sl = slice(gb * gpb, (gb + 1) * gpb)
        bws.append(jnp.concatenate([_block_diag(bw_re[sl]), _block_diag(bw_im[sl])], axis=1))
        cws.append(jnp.concatenate([_block_diag(c_re[sl].astype(F32)), -_block_diag(c_im[sl].astype(F32))], axis=0))
        atabs.append(jnp.stack([abar_re[sl].reshape(1, -1), abar_im[sl].reshape(1, -1)]))
    return jnp.stack(bws).astype(BF16), jnp.stack(cws).astype(BF16), jnp.stack(atabs)


def _lru_kernel(*refs, rev, tq, nb, n_lat_chunks, n_chunks, has_prev):
    if has_prev:
        x_ref, xp_ref, xn_ref, cw_ref, cb_ref, wa_ref, wx_ref, ba_ref, bx_ref, sp_ref, prev_ref, o_ref, a_scr, b_scr, h_scr = refs
    else:
        x_ref, xp_ref, xn_ref, cw_ref, cb_ref, wa_ref, wx_ref, ba_ref, bx_ref, sp_ref, o_ref, a_scr, b_scr, h_scr = refs
        prev_ref = None
    s = pl.program_id(0)
    gw = GROUP_WIDTH
    rows = tq * nb

    @pl.when(s == 0)
    def _():
        h_scr[...] = jnp.zeros_like(h_scr)

    c = _scan_chunk_index(s, n_lat_chunks, n_chunks, rev)
    first = jnp.logical_or(c == 0, c == n_lat_chunks)
    last = jnp.logical_or(c == n_lat_chunks - 1, c == n_chunks - 1)
    keep_prev = jnp.where(first, 0.0, 1.0)
    keep_next = jnp.where(last, 0.0, 1.0)
    xc = jnp.concatenate([xp_ref[...] * keep_prev, x_ref[...], xn_ref[...] * keep_next], axis=0)
    conv = cb_ref[...] + xc[0:tq] * cw_ref[0:1, :]
    for k in range(1, LRU_CONV):
        conv = conv + xc[k:k + tq] * cw_ref[k:k + 1, :]
    x = conv.reshape(rows, gw)
    xb = x.astype(BF16)
    r = jax.nn.sigmoid(_dot(xb, wa_ref[...]) + ba_ref[...])
    ig = jax.nn.sigmoid(_dot(xb, wx_ref[...]) + bx_ref[...])
    log_a = -LRU_C * r * sp_ref[...]
    a = jnp.exp(log_a)
    a_scr[...] = a
    b_scr[...] = jnp.sqrt(1.0 - a * a) * (ig * x)

    def step(i, h):
        t = (tq - 1 - i) if rev else i
        r0 = pl.multiple_of(t * nb, nb)
        hn = a_scr[pl.ds(r0, nb), :] * h + b_scr[pl.ds(r0, nb), :]
        b_scr[pl.ds(r0, nb), :] = hn
        return hn

    h_scr[...] = lax.fori_loop(0, tq, step, h_scr[...], unroll=4)
    out = b_scr[...].reshape(tq, nb, gw)
    if has_prev:
        out = out + prev_ref[...]
    o_ref[...] = out


def _lru_direction(tm3, conv_w, conv_b, wa_bd, wx_bd, ba, bx, sp, prev, rev, n_lat, tq):
    ltot, nb, _ = tm3.shape
    gw = GROUP_WIDTH
    n_chunks = ltot // tq
    n_lat_chunks = n_lat // tq
    cidx = lambda s: _scan_chunk_index(s, n_lat_chunks, n_chunks, rev)
    has_prev = prev is not None
    full2 = lambda s: (0, 0)
    in_specs = [pl.BlockSpec((tq, nb, gw), lambda s: (cidx(s), 0, 1)),
                pl.BlockSpec((2, nb, gw), lambda s: (jnp.maximum(cidx(s) * (tq // 2) - 1, 0), 0, 1)),
                pl.BlockSpec((1, nb, gw), lambda s: (jnp.minimum((cidx(s) + 1) * tq, ltot - 1), 0, 1)),
                pl.BlockSpec((LRU_CONV, gw), full2),
                pl.BlockSpec((1, gw), full2),
                pl.BlockSpec((gw, gw), full2),
                pl.BlockSpec((gw, gw), full2),
                pl.BlockSpec((1, gw), full2),
                pl.BlockSpec((1, gw), full2),
                pl.BlockSpec((1, gw), full2)]
    args = [tm3, tm3, tm3, conv_w, conv_b, wa_bd, wx_bd, ba, bx, sp]
    if has_prev:
        in_specs.append(pl.BlockSpec((tq, nb, gw), lambda s: (cidx(s), 0, 0)))
        args.append(prev)
    kern = functools.partial(_lru_kernel, rev=rev, tq=tq, nb=nb, n_lat_chunks=n_lat_chunks,
                             n_chunks=n_chunks, has_prev=has_prev)
    return pl.pallas_call(
        kern,
        out_shape=jax.ShapeDtypeStruct((ltot, nb, gw), F32),
        grid=(n_chunks,),
        in_specs=in_specs,
        out_specs=pl.BlockSpec((tq, nb, gw), lambda s: (cidx(s), 0, 0)),
        scratch_shapes=[pltpu.VMEM((tq * nb, gw), F32),
                        pltpu.VMEM((tq * nb, gw), F32),
                        pltpu.VMEM((nb, gw), F32)],
        compiler_params=_params("arbitrary"),
        name="lru_rev" if rev else "lru_fwd",
    )(*args)


def _tri_mask(n, rev):
    t = lax.broadcasted_iota(I32, (n, n), 0)
    s = lax.broadcasted_iota(I32, (n, n), 1)
    return (s >= t) if rev else (s <= t)


def _hgrn_kernel(*refs, rev, tq, has_prev):
    if has_prev:
        q_ref, z_ref, i_ref, lb_ref, tri_ref, ones_ref, prev_ref, o_ref, st_scr, cum_scr, q_scr, k_scr, v_scr = refs
    else:
        q_ref, z_ref, i_ref, lb_ref, tri_ref, ones_ref, o_ref, st_scr, cum_scr, q_scr, k_scr, v_scr = refs
        prev_ref = None
    hps = HGRN_HEADS_PER_STEP
    cs, sb, w = GLA_CHUNK, GLA_SUB, min(HGRN_FAST_WINDOW, tq)
    nsub = cs // sb
    n_ck = tq // cs
    n_w = tq // w

    @pl.when(pl.program_id(2) == 0)
    def _():
        st_scr[...] = jnp.zeros_like(st_scr)

    lb = lb_ref[0]
    z = z_ref[0]
    iv = i_ref[0]
    f = lb + (1.0 - lb) * jax.nn.sigmoid(z)
    k_all = 1.0 - f
    v_all = iv * jax.nn.sigmoid(iv)
    q_all = q_ref[0] * (GLA_DIM ** -0.5)
    logf = jnp.log(f)
    lf_hi = logf.astype(BF16)
    rest = logf - lf_hi.astype(F32)
    lf_mid = rest.astype(BF16)
    lf_lo = (rest - lf_mid.astype(F32)).astype(BF16)
    tri = tri_ref[...]
    cum_all = _dot(tri, lf_hi) + _dot(tri, lf_mid) + _dot(tri, lf_lo)
    for hh in range(hps):
        lanes = slice(hh * GLA_DIM, (hh + 1) * GLA_DIM)
        cum_scr[hh] = cum_all[:, lanes]
        q_scr[hh] = q_all[:, lanes]
        k_scr[hh] = k_all[:, lanes]
        v_scr[hh] = v_all[:, lanes]

    def far_row(c, width):
        return c * width if rev else (c + 1) * width - 1

    def outside_row(c, width, n):
        if rev:
            return (c + 1) * width if c < n - 1 else None
        return c * width - 1 if c > 0 else None

    def store_out(hh, rows, out):
        lanes = slice(hh * GLA_DIM, (hh + 1) * GLA_DIM)
        if has_prev:
            out = out + prev_ref[0, rows, lanes]
        o_ref[0, rows, lanes] = out

    def block_factored(_):
        lane_id = lax.broadcasted_iota(I32, (1, tq), 1)
        for hh in range(hps):
            cumg, q, k = cum_scr[hh], q_scr[hh], k_scr[hh]
            vb = v_scr[hh].astype(BF16)
            st = st_scr[hh]
            ends = [cum_scr[hh, far_row(c, w):far_row(c, w) + 1, :] for c in range(n_w)]
            end_full = jnp.concatenate([jnp.broadcast_to(e, (w, GLA_DIM)) for e in ends], axis=0)
            ks = (k * jnp.exp(end_full - cumg)).astype(BF16)
            pieces, starts, row_lo = [], [], []
            total = 0
            for c in range(n_w):
                rows = slice(0, (c + 1) * w) if rev else slice(c * w, tq)
                scale = jnp.exp(jnp.minimum(cumg[rows] - ends[c], HGRN_SAFE_LOG_DECAY))
                pieces.append((q[rows] * scale).astype(BF16))
                starts.append(total)
                row_lo.append(rows.start)
                total += rows.stop - rows.start
            a_stack = _dot_nt(jnp.concatenate(pieces, axis=0), ks)
            a_rows = []
            for r in range(n_w):
                acc = jnp.zeros((w, tq), F32)
                for c in (range(r, n_w) if rev else range(0, r + 1)):
                    lo = starts[c] + r * w - row_lo[c]
                    keep = jnp.logical_and(lane_id >= c * w, lane_id < (c + 1) * w)
                    if c == r:
                        t_id = r * w + lax.broadcasted_iota(I32, (w, tq), 0)
                        keep = jnp.logical_and(keep, (lane_id >= t_id) if rev else (lane_id <= t_id))
                    acc = jnp.where(keep, a_stack[lo:lo + w, :], acc)
                a_rows.append(acc)
            a = jnp.concatenate(a_rows, axis=0).astype(BF16)
            out = _dot(a, vb) + _dot_nt((q * jnp.exp(cumg)).astype(BF16), st.astype(BF16))
            store_out(hh, slice(None), out)
            last = ends[0] if rev else ends[n_w - 1]
            kd = k * jnp.exp(last - cumg)
            st_scr[hh] = st * jnp.exp(last) + _dot_tn(vb, kd.astype(BF16))
        return 0

    row_id = lax.broadcasted_iota(I32, (sb, GLA_DIM), 0)
    valid = [(row_id <= s) if rev else (row_id >= s) for s in range(sb)]

    def chunk_direct(hh, r0):
        rows = pl.ds(r0, cs)
        q, k, cumg = q_scr[hh, rows, :], k_scr[hh, rows, :], cum_scr[hh, rows, :]
        vb = v_scr[hh, rows, :].astype(BF16)
        if rev:
            base = cum_scr[hh, pl.ds(jnp.minimum(r0 + cs, tq - 1), 1), :]
            base = jnp.where(r0 + cs >= tq, 0.0, base)
        else:
            base = cum_scr[hh, pl.ds(jnp.maximum(r0 - 1, 0), 1), :]
            base = jnp.where(r0 == 0, 0.0, base)
        cum = cumg - base
        last = cum[0:1] if rev else cum[cs - 1:cs]
        st = st_scr[hh]
        o_inter = _dot_nt((q * jnp.exp(cum)).astype(BF16), st.astype(BF16))
        outs = []
        for i in range(nsub):
            blk = slice(i * sb, (i + 1) * sb)
            off = slice((i + 1) * sb, cs) if rev else slice(0, i * sb)
            o_row = outside_row(i, sb, nsub)
            q_i, cum_i = q[blk], cum[blk]
            o_i = o_inter[blk]
            if o_row is not None:
                ref_row = cum[o_row:o_row + 1]
                qt = q_i * jnp.exp(cum_i - ref_row)
                kt = k[off] * jnp.exp(ref_row - cum[off])
                a_off = _dot_nt(qt.astype(BF16), kt.astype(BF16))
                o_i = o_i + _dot(a_off.astype(BF16), vb[off])
            prods = []
            for s in range(sb):
                r = r0 + i * sb + s
                e = jnp.exp(jnp.minimum(cumg[blk] - cum_scr[hh, pl.ds(r, 1), :], 0.0))
                prods.append(jnp.where(valid[s], q_i * e * k_scr[hh, pl.ds(r, 1), :], 0.0))
            sums = _dot(jnp.concatenate(prods, axis=0).astype(BF16), ones_ref[...])
            for s in range(sb):
                r = r0 + i * sb + s
                o_i = o_i + sums[s * sb:(s + 1) * sb] * v_scr[hh, pl.ds(r, 1), :]
            outs.append(o_i)
        store_out(hh, rows, jnp.concatenate(outs, axis=0))
        kd = k * jnp.exp(last - cum)
        st_scr[hh] = st * jnp.exp(last) + _dot_tn(vb, kd.astype(BF16))

    def block_direct(_):
        def chunk(ci, carry):
            c = (n_ck - 1 - ci) if rev else ci
            r0 = pl.multiple_of(c * cs, cs)
            for hh in range(hps):
                chunk_direct(hh, r0)
            return carry

        return lax.fori_loop(0, n_ck, chunk, 0)

    decays = []
    for c in range(n_w):
        far = cum_all[far_row(c, w):far_row(c, w) + 1]
        o_row = outside_row(c, w, n_w)
        decays.append(far if o_row is None else far - cum_all[o_row:o_row + 1])
    safe = jnp.logical_and(jnp.min(jnp.concatenate(decays, axis=0)) > -HGRN_SAFE_LOG_DECAY,
                           jnp.max(jnp.abs(q_all)) < HGRN_SAFE_QUERY)
    lax.cond(safe, block_factored, block_direct, 0)


def _gla_specs(ltot, tq, n_lat, rev):
    n_chunks = ltot // tq
    n_lat_chunks = n_lat // tq
    cidx = lambda s: _scan_chunk_index(s, n_lat_chunks, n_chunks, rev)
    part = lambda p: pl.BlockSpec((1, tq, GLA_DIM), lambda b, h, s: (b, cidx(s), p * GLA_HEADS + h))
    return n_chunks, cidx, part


def _hgrn_direction(bm, lb, prev, rev, n_lat, tq):
    bsz, ltot, _ = bm.shape
    n_chunks, cidx, _ = _gla_specs(ltot, tq, n_lat, rev)
    has_prev = prev is not None
    hps = HGRN_HEADS_PER_STEP
    width = hps * GLA_DIM
    n_hsteps = GLA_HEADS // hps
    part = lambda p: pl.BlockSpec((1, tq, width), lambda b, h, s: (b, cidx(s), p * n_hsteps + h))
    tri = _tri_mask(tq, rev).astype(BF16)
    ones = jnp.ones((GLA_DIM, GLA_DIM), BF16)
    z_part = 2 if rev else 1
    in_specs = [part(0), part(z_part), part(3),
                pl.BlockSpec((1, 1, width), lambda b, h, s: (h, 0, 0)),
                pl.BlockSpec(tri.shape, lambda b, h, s: (0, 0)),
                pl.BlockSpec(ones.shape, lambda b, h, s: (0, 0))]
    args = [bm, bm, bm, lb.reshape(n_hsteps, 1, width), tri, ones]
    out_spec = pl.BlockSpec((1, tq, width), lambda b, h, s: (b, cidx(s), h))
    if has_prev:
        in_specs.append(out_spec)
        args.append(prev)
    kern = functools.partial(_hgrn_kernel, rev=rev, tq=tq, has_prev=has_prev)
    tile = pltpu.VMEM((hps, tq, GLA_DIM), F32)
    return pl.pallas_call(
        kern,
        out_shape=jax.ShapeDtypeStruct((bsz, ltot, GROUP_WIDTH), F32),
        grid=(bsz, n_hsteps, n_chunks),
        in_specs=in_specs,
        out_specs=out_spec,
        scratch_shapes=[pltpu.VMEM((hps, GLA_DIM, GLA_DIM), F32), tile, tile, tile, tile],
        compiler_params=_params("arbitrary", "arbitrary", "arbitrary"),
        name="hgrn_rev" if rev else "hgrn_fwd",
    )(*args)


def _ret_kernel(*refs, rev, tq, has_prev):
    if has_prev:
        q_ref, k_ref, v_ref, cos_ref, sin_ref, eq_ref, ek_ref, dm_ref, el_ref, prev_ref, o_ref, st_scr = refs
    else:
        q_ref, k_ref, v_ref, cos_ref, sin_ref, eq_ref, ek_ref, dm_ref, el_ref, o_ref, st_scr = refs
        prev_ref = None
    cs = min(tq, RET_CHUNK)
    n_ck = tq // cs

    @pl.when(pl.program_id(1) == 0)
    def _():
        st_scr[...] = jnp.zeros_like(st_scr)

    def chunk(ci, carry):
        c = (n_ck - 1 - ci) if rev else ci
        r0 = pl.multiple_of(c * cs, cs)
        rows = pl.ds(r0, cs)
        cos = cos_ref[rows, :]
        sin = sin_ref[rows, :]
        for h in range(GLA_HEADS):
            lanes = slice(h * GLA_DIM, (h + 1) * GLA_DIM)
            q = q_ref[0, rows, lanes]
            k = k_ref[0, rows, lanes]
            vb = v_ref[0, rows, lanes].astype(BF16)
            q = q * cos + pltpu.roll(q, GLA_DIM // 2, 1) * sin
            k = (k * cos + pltpu.roll(k, GLA_DIM // 2, 1) * sin) * (GLA_DIM ** -0.5)
            st = st_scr[h]
            scores = _dot_nt(q.astype(BF16), k.astype(BF16)) * dm_ref[h]
            out = _dot_nt((q * eq_ref[h]).astype(BF16), st.astype(BF16)) + _dot(scores.astype(BF16), vb)
            if has_prev:
                out = out + prev_ref[0, rows, lanes]
            o_ref[0, rows, lanes] = out
            st_scr[h] = st * el_ref[h] + _dot_tn(vb, (k * ek_ref[h]).astype(BF16))
        return carry

    lax.fori_loop(0, n_ck, chunk, 0)


def _ret_tables(rev, cs):
    gamma = jnp.log1p(-jnp.exp2(-(RET_DECAY_EXP[1 if rev else 0] + jnp.arange(GLA_HEADS, dtype=F32))))
    t = jnp.arange(cs, dtype=F32)
    steps = (cs - t) if rev else (t + 1.0)
    cum = gamma[:, None] * steps[None, :]
    last = gamma * cs
    eq = jnp.broadcast_to(jnp.exp(cum)[:, :, None], (GLA_HEADS, cs, GLA_DIM))
    ek = jnp.broadcast_to(jnp.exp(last[:, None] - cum)[:, :, None], (GLA_HEADS, cs, GLA_DIM))
    rel = cum[:, :, None] - cum[:, None, :]
    dm = jnp.where(_tri_mask(cs, rev)[None], jnp.exp(jnp.minimum(rel, 0.0)), 0.0)
    el = jnp.broadcast_to(jnp.exp(last)[:, None, None], (GLA_HEADS, 1, GLA_DIM))
    return eq, ek, dm, el


def _rotary_tables(n_lat, n_ctx):
    rows = n_lat // GRID_W
    row = jnp.repeat(jnp.arange(rows, dtype=F32), GRID_W)
    col = jnp.tile(jnp.arange(GRID_W, dtype=F32), rows)
    quarter = GLA_DIM // 4
    inv_freq = ROPE_BASE ** (-jnp.arange(quarter, dtype=F32) / quarter)
    ang = jnp.concatenate([row[:, None] * inv_freq, col[:, None] * inv_freq], axis=-1)
    cos, sin = jnp.cos(ang), jnp.sin(ang)
    cos_l = jnp.concatenate([cos, cos], axis=-1)
    sin_l = jnp.concatenate([-sin, sin], axis=-1)
    cos_t = jnp.concatenate([cos_l, jnp.ones((n_ctx, GLA_DIM), F32)], axis=0)
    sin_t = jnp.concatenate([sin_l, jnp.zeros((n_ctx, GLA_DIM), F32)], axis=0)
    return cos_t, sin_t


def _ret_direction(bm, cos_t, sin_t, prev, rev, n_lat, tq):
    bsz, ltot, _ = bm.shape
    n_chunks, cidx, _ = _gla_specs(ltot, tq, n_lat, rev)
    has_prev = prev is not None
    eq, ek, dm, el = _ret_tables(rev, min(tq, RET_CHUNK))
    gw = GROUP_WIDTH
    part = lambda p: pl.BlockSpec((1, tq, gw), lambda b, s: (b, cidx(s), p))
    whole = lambda a: pl.BlockSpec(a.shape, lambda b, s: (0, 0, 0))
    pos = pl.BlockSpec((tq, GLA_DIM), lambda b, s: (cidx(s), 0))
    in_specs = [part(5), part(6), part(7), pos, pos, whole(eq), whole(ek), whole(dm), whole(el)]
    args = [bm, bm, bm, cos_t, sin_t, eq, ek, dm, el]
    out_spec = pl.BlockSpec((1, tq, gw), lambda b, s: (b, cidx(s), 0))
    if has_prev:
        in_specs.append(out_spec)
        args.append(prev)
    kern = functools.partial(_ret_kernel, rev=rev, tq=tq, has_prev=has_prev)
    return pl.pallas_call(
        kern,
        out_shape=jax.ShapeDtypeStruct((bsz, ltot, gw), F32),
        grid=(bsz, n_chunks),
        in_specs=in_specs,
        out_specs=out_spec,
        scratch_shapes=[pltpu.VMEM((GLA_HEADS, GLA_DIM, GLA_DIM), F32)],
        compiler_params=_params("arbitrary", "arbitrary"),
        name="ret_rev" if rev else "ret_fwd",
    )(*args)


def _head_rms(o):
    parts = []
    for h in range(GLA_HEADS):
        oh = o[:, h * GLA_DIM:(h + 1) * GLA_DIM]
        parts.append(oh * lax.rsqrt(jnp.mean(oh * oh, axis=-1, keepdims=True) + EPS))
    return jnp.concatenate(parts, axis=-1)


def _mix_kernel(u_ref, lg_ref, ys_ref, hs_ref, hg_ref, rg_ref, oh_ref, or_ref, d_ref, gw_ref, gb_ref, o_ref):
    gw = GROUP_WIDTH
    y = jax.nn.gelu(d_ref[...] * u_ref[...] + ys_ref[...])
    a = y * jax.nn.sigmoid(_dot(y.astype(BF16), gw_ref[...]) + gb_ref[...])
    o_ref[0, :, 0:gw] = a.astype(BF16)
    o_ref[0, :, gw:2 * gw] = (hs_ref[...] * jax.nn.gelu(lg_ref[...])).astype(BF16)
    g = hg_ref[0]
    o_ref[0, :, 2 * gw:3 * gw] = (_head_rms(oh_ref[0]) * (g * jax.nn.sigmoid(g))).astype(BF16)
    g = rg_ref[0]
    o_ref[0, :, 3 * gw:4 * gw] = (_head_rms(or_ref[0]) * (g * jax.nn.sigmoid(g))).astype(BF16)


def _mix_epilogue(tm2, ys2, hs2, bm, o_h, o_r, s5_d, glu_w, glu_b, n_rows, tm):
    bsz, ltot, _ = bm.shape
    gw = GROUP_WIDTH
    tmaj = lambda p, k: pl.BlockSpec((tm, gw), lambda b, i: (i, b * k + p))
    bmaj = lambda p: pl.BlockSpec((1, tm, gw), lambda b, i: (b, i, p))
    full2 = lambda b, i: (0, 0)
    return pl.pallas_call(
        _mix_kernel,
        out_shape=jax.ShapeDtypeStruct((bsz, ltot, 4 * gw), BF16),
        grid=(bsz, n_rows // tm),
        in_specs=[tmaj(0, N_TM_PARTS), tmaj(2, N_TM_PARTS), tmaj(0, 1), tmaj(0, 1),
                  bmaj(4), bmaj(8), bmaj(0), bmaj(0),
                  pl.BlockSpec((1, gw), full2), pl.BlockSpec((gw, gw), full2), pl.BlockSpec((1, gw), full2)],
        out_specs=pl.BlockSpec((1, tm, 4 * gw), lambda b, i: (b, i, 0)),
        compiler_params=_params("arbitrary", "arbitrary"),
        name="mix_epilogue",
    )(tm2, tm2, ys2, hs2, bm, bm, o_h, o_r, s5_d, glu_w, glu_b)


def _gate_rows(modb_ref, modc_ref, idx, row0, tm, n_lat):
    row = row0 + lax.broadcasted_iota(I32, (tm, 1), 0)
    return jnp.where(row >= n_lat, modc_ref[0, idx:idx + 1, :], modb_ref[0, idx:idx + 1, :])


def _outproj_kernel(a_ref, w_ref, x_ref, modb_ref, modc_ref, o_ref, *, tm, n_lat):
    gate = _gate_rows(modb_ref, modc_ref, 2, pl.program_id(1) * tm, tm, n_lat)
    o_ref[0] = x_ref[0] + gate * _dot(a_ref[0], w_ref[0])


def _out_projection(mix, w_bf16, layer, xs, modv, n_rows, n_lat, tm, tn):
    bsz, ltot, d = xs.shape
    k = mix.shape[-1]
    kern = functools.partial(_outproj_kernel, tm=tm, n_lat=n_lat)
    return pl.pallas_call(
        kern,
        out_shape=jax.ShapeDtypeStruct((bsz, ltot, d), F32),
        grid=(bsz, n_rows // tm, d // tn),
        in_specs=[pl.BlockSpec((1, tm, k), lambda b, i, j: (b, i, 0)),
                  pl.BlockSpec((1, k, tn), lambda b, i, j: (layer, 0, j)),
                  pl.BlockSpec((1, tm, tn), lambda b, i, j: (b, i, j)),
                  pl.BlockSpec((1, 6, tn), lambda b, i, j: (b, 0, j)),
                  pl.BlockSpec((1, 6, tn), lambda b, i, j: (bsz, 0, j))],
        out_specs=pl.BlockSpec((1, tm, tn), lambda b, i, j: (b, i, j)),
        compiler_params=_params("arbitrary", "arbitrary", "arbitrary"),
        name="out_projection",
    )(mix, w_bf16, xs, modv, modv)


def _first_max4(v):
    m1 = jnp.maximum(jnp.maximum(v[0], v[1]), jnp.maximum(v[2], v[3]))
    i1 = jnp.where(v[0] == m1, 0, jnp.where(v[1] == m1, 1, jnp.where(v[2] == m1, 2, 3)))
    rest = [jnp.where(i1 == j, -jnp.inf, v[j]) for j in range(4)]
    m2 = jnp.maximum(jnp.maximum(rest[0], rest[1]), jnp.maximum(rest[2], rest[3]))
    i2 = jnp.where(rest[0] == m2, 0, jnp.where(rest[1] == m2, 1, jnp.where(rest[2] == m2, 2, 3)))
    return m1, i1, m2, i2


def _router_kernel(x_ref, g_ref, modb_ref, modc_ref, rw_ref, rb_ref, tri_ref,
                   h_ref, ids_ref, gates_ref, ranks_ref, counts_ref, base_scr, *, tm, n_lat):
    first = jnp.logical_and(pl.program_id(0) == 0, pl.program_id(1) == 0)

    @pl.when(first)
    def _():
        base_scr[...] = jnp.zeros_like(base_scr)

    h = _norm_modulate(x_ref[0], g_ref[...], modb_ref, modc_ref, pl.program_id(1) * tm, n_lat, 3, 4)
    h_ref[0] = h.astype(BF16)
    logits = lax.dot_general(rw_ref[...], h, (((1,), (1,)), ((), ())), preferred_element_type=F32,
                             precision=lax.Precision.HIGHEST)
    scores = jax.nn.sigmoid(logits)
    biased = scores + rb_ref[...]
    tops = []
    for gidx in range(N_EXPERT_GROUPS):
        rows = [biased[gidx * EXPERTS_PER_GROUP + j:gidx * EXPERTS_PER_GROUP + j + 1, :]
                for j in range(EXPERTS_PER_GROUP)]
        tops.append(_first_max4(rows))
    gs = [t[0] + t[2] for t in tops]
    gmax = jnp.maximum(jnp.maximum(gs[0], gs[1]), jnp.maximum(gs[2], gs[3]))
    best = jnp.where(gs[0] == gmax, 0, jnp.where(gs[1] == gmax, 1, jnp.where(gs[2] == gmax, 2, 3)))
    e1 = jnp.zeros_like(best)
    e2 = jnp.zeros_like(best)
    for gidx in range(N_EXPERT_GROUPS):
        e1 = jnp.where(best == gidx, gidx * EXPERTS_PER_GROUP + tops[gidx][1], e1)
        e2 = jnp.where(best == gidx, gidx * EXPERTS_PER_GROUP + tops[gidx][3], e2)
    eid = lax.broadcasted_iota(I32, (N_EXPERTS, tm), 0)
    sel1 = eid == e1
    sel2 = eid == e2
    w1 = jnp.sum(jnp.where(sel1, scores, 0.0), axis=0, keepdims=True)
    w2 = jnp.sum(jnp.where(sel2, scores, 0.0), axis=0, keepdims=True)
    wsum = w1 + w2
    onehot = jnp.where(jnp.logical_or(sel1, sel2), 1.0, 0.0)
    pos = base_scr[...] + _dot(onehot.astype(BF16), tri_ref[...])
    r1 = jnp.sum(jnp.where(sel1, pos, 0.0), axis=0, keepdims=True)
    r2 = jnp.sum(jnp.where(sel2, pos, 0.0), axis=0, keepdims=True)
    ids_ref[0] = jnp.concatenate([e1, e2], axis=0)
    gates_ref[0] = jnp.concatenate([w1 / wsum, w2 / wsum], axis=0)
    ranks_ref[0] = jnp.concatenate([r1, r2], axis=0).astype(I32)
    base = base_scr[...] + jnp.sum(onehot, axis=1, keepdims=True)
    base_scr[...] = base
    counts_ref[...] = jnp.broadcast_to(base, counts_ref.shape)


def _router(xs, gain, modv, rw_t, rbias, n_rows, n_lat, tm):
    bsz, ltot, d = xs.shape
    tri = (lax.broadcasted_iota(I32, (tm, tm), 0) < lax.broadcasted_iota(I32, (tm, tm), 1)).astype(BF16)
    kern = functools.partial(_router_kernel, tm=tm, n_lat=n_lat)
    small = lambda dt: jax.ShapeDtypeStruct((bsz, 2, n_rows), dt)
    small_spec = pl.BlockSpec((1, 2, tm), lambda b, i: (b, 0, i))
    return pl.pallas_call(
        kern,
        out_shape=(jax.ShapeDtypeStruct((bsz, ltot, d), BF16), small(I32), small(F32), small(I32),
                   jax.ShapeDtypeStruct((N_EXPERTS, 128), F32)),
        grid=(bsz, n_rows // tm),
        in_specs=[pl.BlockSpec((1, tm, d), lambda b, i: (b, i, 0)),
                  pl.BlockSpec((1, d), lambda b, i: (0, 0)),
                  pl.BlockSpec((1, 6, d), lambda b, i: (b, 0, 0)),
                  pl.BlockSpec((1, 6, d), lambda b, i: (bsz, 0, 0)),
                  pl.BlockSpec((N_EXPERTS, d), lambda b, i: (0, 0)),
                  pl.BlockSpec((N_EXPERTS, 1), lambda b, i: (0, 0)),
                  pl.BlockSpec((tm, tm), lambda b, i: (0, 0))],
        out_specs=(pl.BlockSpec((1, tm, d), lambda b, i: (b, i, 0)), small_spec, small_spec, small_spec,
                   pl.BlockSpec((N_EXPERTS, 128), lambda b, i: (0, 0))),
        scratch_shapes=[pltpu.VMEM((N_EXPERTS, 1), F32)],
        compiler_params=_params("arbitrary", "arbitrary"),
        name="moe_router",
    )(xs, gain, modv, modv, rw_t, rbias, tri)


def _expert_ffn_kernel(te_ref, nt_ref, x_ref, wg_ref, wu_ref, wd_ref, o_ref):
    live = pl.program_id(0) < nt_ref[0]

    @pl.when(live)
    def _():
        x = x_ref[...]
        g = _dot(x, wg_ref[0])
        u = _dot(x, wu_ref[0])
        he = (g * jax.nn.sigmoid(g)) * u
        o_ref[...] = _dot(he.astype(BF16), wd_ref[0]).astype(o_ref.dtype)

    @pl.when(jnp.logical_not(live))
    def _():
        o_ref[...] = jnp.zeros_like(o_ref)


def _expert_ffn(tile_expert, n_tiles, x_sorted, wg, wu, wd, tm):
    p, d = x_sorted.shape
    f = wg.shape[-1]
    tile = lambda i, te, nt: i
    expert = lambda i, te, nt: te[jnp.minimum(i, nt[0] - 1)]
    grid_spec = pltpu.PrefetchScalarGridSpec(
        num_scalar_prefetch=2,
        grid=(p // tm,),
        in_specs=[pl.BlockSpec((tm, d), lambda i, te, nt: (tile(i, te, nt), 0)),
                  pl.BlockSpec((1, d, f), lambda i, te, nt: (expert(i, te, nt), 0, 0)),
                  pl.BlockSpec((1, d, f), lambda i, te, nt: (expert(i, te, nt), 0, 0)),
                  pl.BlockSpec((1, f, d), lambda i, te, nt: (expert(i, te, nt), 0, 0))],
        out_specs=pl.BlockSpec((tm, d), lambda i, te, nt: (tile(i, te, nt), 0)),
    )
    return pl.pallas_call(
        _expert_ffn_kernel,
        out_shape=jax.ShapeDtypeStruct((p, d), BF16),
        grid_spec=grid_spec,
        compiler_params=_params("arbitrary"),
        name="moe_expert_ffn",
    )(tile_expert, n_tiles, x_sorted, wg, wu, wd)


def _moe_residual_kernel(x_ref, y0_ref, y1_ref, gates_ref, modb_ref, modc_ref, g_ref, o_ref,
                         *, tm, n_lat, final_norm):
    gates = gates_ref[0]
    y = gates[:, 0:1] * y0_ref[0, 0].astype(F32) + gates[:, 1:2] * y1_ref[0, 0].astype(F32)
    gate = _gate_rows(modb_ref, modc_ref, 5, pl.program_id(1) * tm, tm, n_lat)
    x = x_ref[0] + gate * y
    if final_norm:
        x = x * lax.rsqrt(jnp.mean(x * x, axis=-1, keepdims=True) + EPS) * g_ref[...]
    o_ref[0] = x


def _moe_residual(xs, y_pair, gates_t, modv, final_g, n_rows, n_lat, tm, final_norm):
    bsz, ltot, d = xs.shape
    kern = functools.partial(_moe_residual_kernel, tm=tm, n_lat=n_lat, final_norm=final_norm)
    rows_out = n_rows if final_norm else ltot
    return pl.pallas_call(
        kern,
        out_shape=jax.ShapeDtypeStruct((bsz, rows_out, d), F32),
        grid=(bsz, n_rows // tm),
        in_specs=[pl.BlockSpec((1, tm, d), lambda b, i: (b, i, 0)),
                  pl.BlockSpec((1, 1, tm, d), lambda b, i: (0, b, i, 0)),
                  pl.BlockSpec((1, 1, tm, d), lambda b, i: (1, b, i, 0)),
                  pl.BlockSpec((1, tm, 2), lambda b, i: (b, i, 0)),
                  pl.BlockSpec((1, 6, d), lambda b, i: (b, 0, 0)),
                  pl.BlockSpec((1, 6, d), lambda b, i: (bsz, 0, 0)),
                  pl.BlockSpec((1, d), lambda b, i: (0, 0))],
        out_specs=pl.BlockSpec((1, tm, d), lambda b, i: (b, i, 0)),
        compiler_params=_params("arbitrary", "arbitrary"),
        name="moe_residual",
    )(xs, y_pair, y_pair, gates_t, modv, modv, final_g)


def _moe_layer(xs, gain, modv, rw_t, rbias, wg, wu, wd, expert_base, final_g, n_rows, n_lat, tm_tok, tm_res, tm_exp,
               final_norm):
    bsz, ltot, d = xs.shape
    h, ids, gates, ranks, counts = _router(xs, gain, modv, rw_t, rbias, n_rows, n_lat, tm_tok)
    counts = counts[:, 0].astype(I32)
    padded = ((counts + tm_exp - 1) // tm_exp) * tm_exp
    ends = jnp.cumsum(padded)
    offsets = ends - padded
    onehot = ids[..., None] == jnp.arange(N_EXPERTS, dtype=I32)
    dest = jnp.sum(jnp.where(onehot, offsets, 0), axis=-1) + ranks
    n_slots = bsz * 2 * n_rows
    p_rows = ((n_slots + N_EXPERTS * (tm_exp - 1)) + tm_exp - 1) // tm_exp * tm_exp
    tok = (jnp.arange(bsz, dtype=I32)[:, None, None] * ltot + jnp.arange(n_rows, dtype=I32)[None, None, :])
    tok = jnp.broadcast_to(tok, dest.shape)
    fill_slot = jnp.arange(p_rows, dtype=I32) % n_slots
    filler = (fill_slot // (2 * n_rows)) * ltot + fill_slot % n_rows
    src = filler.at[dest.reshape(-1)].set(tok.reshape(-1), mode="promise_in_bounds", unique_indices=True)
    tile_start = jnp.arange(p_rows // tm_exp, dtype=I32) * tm_exp
    tile_expert = jnp.minimum(jnp.sum(tile_start[:, None] >= ends[None, :], axis=1), N_EXPERTS - 1).astype(I32)
    n_tiles = (ends[-1:] // tm_exp).astype(I32)
    x_sorted = h.reshape(bsz * ltot, d).at[src].get(mode="promise_in_bounds")
    y_sorted = _expert_ffn(tile_expert + expert_base, n_tiles, x_sorted, wg, wu, wd, tm_exp)
    y_pair = y_sorted.at[dest.transpose(1, 0, 2).reshape(-1)].get(mode="promise_in_bounds")
    y_pair = y_pair.reshape(2, bsz, n_rows, d)
    return _moe_residual(xs, y_pair, gates.transpose(0, 2, 1), modv, final_g, n_rows, n_lat, tm_res, final_norm)


def _row_tile(n, target):
    t = min(n, target)
    while n % t or t % 8:
        t -= 8
    return t


def kernel(x, c, ctx, c_ctx, norm_mix_g, norm_ffn_g, w_mod, b_mod, w_in, w_out, s5_lam_re, s5_lam_im, s5_log_dt, s5_b_re, s5_b_im, s5_c_re, s5_c_im, s5_d, s5_glu_w, s5_glu_b, lru_conv_w, lru_conv_b, lru_wa, lru_ba, lru_wx, lru_bx, lru_lam, hgrn_lb_logits, router_w, router_bias, moe_w_gate, moe_w_up, moe_w_down, final_norm_g):
    bsz, n_lat, d = x.shape
    n_ctx = ctx.shape[1]
    ltot = n_lat + n_ctx
    depth = w_in.shape[0]
    gw = GROUP_WIDTH
    assert bsz % 8 == 0 and bsz + 1 <= MOD_ROWS
    assert n_lat % GLA_CHUNK == 0 and n_ctx % GLA_CHUNK == 0 and n_lat % GRID_W == 0 and n_lat % NORM_SLAB == 0

    tq_scan = math.gcd(math.gcd(n_lat, n_ctx), 64)
    tq_gla = math.gcd(math.gcd(n_lat, n_ctx), 256)
    tm_all = _row_tile(ltot, 768)
    tm_lat = _row_tile(n_lat, 1024)
    tm_mix = math.gcd(math.gcd(n_lat, n_ctx), 256)

    xs = jnp.concatenate([x, ctx.astype(x.dtype)], axis=1)
    cv = jnp.zeros((MOD_ROWS, d), F32).at[:bsz].set(c).at[bsz].set(c_ctx)
    mod_all = _mod_vectors(cv, w_mod, b_mod).reshape(depth, MOD_ROWS, 6, d)
    cos_t, sin_t = _rotary_tables(n_lat, n_ctx)
    rw_t = router_w.astype(F32).T
    rbias = router_bias.astype(F32).reshape(N_EXPERTS, 1)
    lb_cum = [jnp.cumsum(jax.nn.softmax(hgrn_lb_logits[dr].astype(F32), axis=0), axis=0) for dr in (0, 1)]
    w_in_b, w_out_b = _cast_bf16(w_in), _cast_bf16(w_out)
    n_exp, _, d_ff = moe_w_gate.shape[1:]
    wg_b = _cast_bf16(moe_w_gate.reshape(depth * n_exp, d, d_ff))
    wu_b = _cast_bf16(moe_w_up.reshape(depth * n_exp, d, d_ff))
    wd_b = _cast_bf16(moe_w_down.reshape(depth * n_exp, d_ff, d))

    for layer in range(depth):
        ctx_out = layer < depth - 1
        n_rows = ltot if ctx_out else n_lat
        tm_tok = tm_all if ctx_out else tm_lat
        modv = mod_all[layer]

        tm2, bm = _in_projection(xs, norm_mix_g[layer].reshape(1, d), modv, w_in_b, layer, n_lat, tm_all)
        tm3 = tm2.reshape(ltot, bsz, N_TM_PARTS * gw)

        ys = hs = o_h = o_r = None
        for dr in (0, 1):
            rev = dr == 1
            bw, cw, atab = _s5_tables(s5_lam_re[layer, dr], s5_lam_im[layer, dr], s5_log_dt[layer, dr],
                                      s5_b_re[layer, dr], s5_b_im[layer, dr], s5_c_re[layer, dr], s5_c_im[layer, dr])
            ys = _s5_direction(tm3, bw, cw, atab, ys, rev, n_lat, tq_scan)
            sp = jax.nn.softplus(-lru_lam[layer, dr].astype(F32)).reshape(1, gw)
            hs = _lru_direction(tm3, lru_conv_w[layer].astype(F32), lru_conv_b[layer].astype(F32).reshape(1, gw),
                                _block_diag(lru_wa[layer, dr]).astype(BF16), _block_diag(lru_wx[layer, dr]).astype(BF16),
                                lru_ba[layer, dr].astype(F32).reshape(1, gw), lru_bx[layer, dr].astype(F32).reshape(1, gw),
                                sp, hs, rev, n_lat, tq_scan)
            lb = lb_cum[dr][layer] - lb_cum[dr][0]
            o_h = _hgrn_direction(bm, lb, o_h, rev, n_lat, tq_gla)
            o_r = _ret_direction(bm, cos_t, sin_t, o_r, rev, n_lat, tq_gla)

        mix = _mix_epilogue(tm2, ys.reshape(ltot, bsz * gw), hs.reshape(ltot, bsz * gw), bm, o_h, o_r,
                            s5_d[layer].astype(F32).reshape(1, gw), s5_glu_w[layer].astype(BF16),
                            s5_glu_b[layer].astype(F32).reshape(1, gw), n_rows, tm_mix)
        xs = _out_projection(mix, w_out_b, layer, xs, modv, n_rows, n_lat, tm_tok, 1024)
        xs = _moe_layer(xs, norm_ffn_g[layer].reshape(1, d), modv, rw_t, rbias, wg_b, wu_b, wd_b, layer * n_exp,
                        final_norm_g.reshape(1, d), n_rows, n_lat, tm_tok, tm_mix, 512, not ctx_out)
    return xs
```

```python
import functools
import math

import jax
import jax.numpy as jnp
from jax import lax
from jax.experimental import pallas as pl
from jax.experimental.pallas import tpu as pltpu

F32 = jnp.float32
BF16 = jnp.bfloat16
I32 = jnp.int32

GROUP_WIDTH = 512
N_IN_PARTS = 12
N_TM_PARTS = 3
N_BM_PARTS = N_IN_PARTS - N_TM_PARTS
S5_CH = 16
S5_GROUPS = 32
S5_STATE = 64
S5_LANE_BLOCKS = 4
LRU_HEADS = 8
LRU_CONV = 4
LRU_C = 8.0
GLA_HEADS = 4
GLA_DIM = 128
GLA_CHUNK = 64
GLA_SUB = 16
HGRN_FAST_WINDOW = 32
HGRN_SAFE_LOG_DECAY = 80.0
HGRN_SAFE_QUERY = 1.0e3
HGRN_HEADS_PER_STEP = 2
RET_CHUNK = 256
RET_DECAY_EXP = (5.0, 5.5)
ROPE_BASE = 10000.0
GRID_W = 64
N_EXPERTS = 16
N_EXPERT_GROUPS = 4
EXPERTS_PER_GROUP = 4
D_FF_EXPERT = 1024
EPS = 1e-6
NORM_SLAB = 16
MOD_ROWS = 24

VMEM_LIMIT_BYTES = 56 * 1024 * 1024


def _params(*semantics):
    return pltpu.CompilerParams(dimension_semantics=semantics, vmem_limit_bytes=VMEM_LIMIT_BYTES)


def _dot(a, b):
    return jnp.dot(a, b, preferred_element_type=F32)


def _dot_nt(a, b):
    return lax.dot_general(a, b, (((1,), (1,)), ((), ())), preferred_element_type=F32)


def _dot_tn(a, b):
    return lax.dot_general(a, b, (((0,), (0,)), ((), ())), preferred_element_type=F32)


def _scan_chunk_index(step, n_lat_chunks, n_chunks, rev):
    if rev:
        return n_chunks - 1 - step
    return (step + n_lat_chunks) % n_chunks


def _cast_kernel(x_ref, o_ref):
    o_ref[...] = x_ref[...].astype(o_ref.dtype)


def _cast_bf16(w):
    n, r, c = w.shape
    tc = min(c, 1024)
    spec = pl.BlockSpec((1, r, tc), lambda i, j: (i, 0, j))
    return pl.pallas_call(
        _cast_kernel,
        out_shape=jax.ShapeDtypeStruct(w.shape, BF16),
        grid=(n, c // tc),
        in_specs=[spec],
        out_specs=spec,
        compiler_params=_params("arbitrary", "arbitrary"),
        name="cast_bf16",
    )(w)


def _mod_kernel(cv_ref, w_ref, b_ref, o_ref):
    cv = cv_ref[...]
    s = cv * jax.nn.sigmoid(cv)
    o_ref[0] = _dot(s.astype(BF16), w_ref[0].astype(BF16)) + b_ref[0]


def _mod_vectors(cv, w_mod, b_mod):
    nl, d, n6 = w_mod.shape
    tn = 1024
    return pl.pallas_call(
        _mod_kernel,
        out_shape=jax.ShapeDtypeStruct((nl, MOD_ROWS, n6), F32),
        grid=(nl, n6 // tn),
        in_specs=[pl.BlockSpec((MOD_ROWS, d), lambda l, j: (0, 0)),
                  pl.BlockSpec((1, d, tn), lambda l, j: (l, 0, j)),
                  pl.BlockSpec((1, 1, tn), lambda l, j: (l, 0, j))],
        out_specs=pl.BlockSpec((1, MOD_ROWS, tn), lambda l, j: (l, 0, j)),
        compiler_params=_params("arbitrary", "arbitrary"),
        name="mod_vectors",
    )(cv, w_mod, b_mod.reshape(nl, 1, n6))


def _norm_modulate(x, g, modb_ref, modc_ref, row0, n_lat, shift_idx, scale_idx):
    tm = x.shape[0]
    ms = jnp.mean(x * x, axis=-1, keepdims=True)
    y = x * lax.rsqrt(ms + EPS) * g
    row = row0 + lax.broadcasted_iota(I32, (tm, 1), 0)
    is_ctx = row >= n_lat
    shift = jnp.where(is_ctx, modc_ref[0, shift_idx:shift_idx + 1, :], modb_ref[0, shift_idx:shift_idx + 1, :])
    scale = jnp.where(is_ctx, modc_ref[0, scale_idx:scale_idx + 1, :], modb_ref[0, scale_idx:scale_idx + 1, :])
    return y * (1.0 + scale) + shift


def _inproj_kernel(x_ref, g_ref, modb_ref, modc_ref, w_ref, otm_ref, obm_ref, h_scr, *, tm, n_lat):
    i = pl.program_id(1)
    j = pl.program_id(2)

    @pl.when(j == 0)
    def _():
        def slab(s, carry):
            r0 = pl.multiple_of(s * NORM_SLAB, NORM_SLAB)
            is_ctx = i * tm + r0 >= n_lat
            shift = jnp.where(is_ctx, modc_ref[0, 0:1, :], modb_ref[0, 0:1, :])
            gain = g_ref[...] * (1.0 + jnp.where(is_ctx, modc_ref[0, 1:2, :], modb_ref[0, 1:2, :]))
            x = x_ref[0, pl.ds(r0, NORM_SLAB), :]
            inv = lax.rsqrt(jnp.mean(x * x, axis=-1, keepdims=True) + EPS)
            h_scr[pl.ds(r0, NORM_SLAB), :] = (x * inv * gain + shift).astype(BF16)
            return carry

        lax.fori_loop(0, tm // NORM_SLAB, slab, 0, unroll=2)

    r = _dot(h_scr[...], w_ref[0])

    @pl.when(j == 0)
    def _():
        otm_ref[...] = r

    @pl.when(j > 0)
    def _():
        obm_ref[0] = r


def _in_projection(xs, gain, modv, w_bf16, layer, n_lat, tm):
    bsz, ltot, d = xs.shape
    gw = GROUP_WIDTH
    kern = functools.partial(_inproj_kernel, tm=tm, n_lat=n_lat)
    tn = N_TM_PARTS * gw
    return pl.pallas_call(
        kern,
        out_shape=(jax.ShapeDtypeStruct((ltot, bsz * tn), F32),
                   jax.ShapeDtypeStruct((bsz, ltot, N_BM_PARTS * gw), F32)),
        grid=(bsz, ltot // tm, N_IN_PARTS * gw // tn),
        in_specs=[pl.BlockSpec((1, tm, d), lambda b, i, j: (b, i, 0)),
                  pl.BlockSpec((1, d), lambda b, i, j: (0, 0)),
                  pl.BlockSpec((1, 6, d), lambda b, i, j: (b, 0, 0)),
                  pl.BlockSpec((1, 6, d), lambda b, i, j: (bsz, 0, 0)),
                  pl.BlockSpec((1, d, tn), lambda b, i, j: (layer, 0, j))],
        out_specs=(pl.BlockSpec((tm, tn), lambda b, i, j: (i, b)),
                   pl.BlockSpec((1, tm, tn), lambda b, i, j: (b, i, jnp.maximum(j - 1, 0)))),
        scratch_shapes=[pltpu.VMEM((tm, d), BF16)],
        compiler_params=_params("arbitrary", "arbitrary", "arbitrary"),
        name="in_projection",
    )(xs, gain, modv, modv, w_bf16)


def _s5_kernel(*refs, rev, tq, nb, has_prev):
    if has_prev:
        u_ref, bw_ref, cw_ref, a_ref, prev_ref, y_ref, z_scr, h_scr = refs
    else:
        u_ref, bw_ref, cw_ref, a_ref, y_ref, z_scr, h_scr = refs
        prev_ref = None
    rows = tq * nb
    half = S5_STATE * 8

    @pl.when(pl.program_id(0) == 0)
    def _():
        h_scr[...] = jnp.zeros_like(h_scr)

    for gb in range(S5_LANE_BLOCKS):
        lanes = slice(gb * 128, (gb + 1) * 128)
        ub = u_ref[:, :, lanes].reshape(rows, 128).astype(BF16)
        z_scr[...] = _dot(ub, bw_ref[gb])
        ar = jnp.broadcast_to(a_ref[gb, 0], (nb, half))
        ai = jnp.broadcast_to(a_ref[gb, 1], (nb, half))

        def step(i, carry):
            hr, hi = carry
            t = (tq - 1 - i) if rev else i
            r0 = pl.multiple_of(t * nb, nb)
            zr = z_scr[pl.ds(r0, nb), 0:half]
            zi = z_scr[pl.ds(r0, nb), half:2 * half]
            nr = ar * hr - ai * hi + zr
            ni = ar * hi + ai * hr + zi
            z_scr[pl.ds(r0, nb), 0:half] = nr
            z_scr[pl.ds(r0, nb), half:2 * half] = ni
            return nr, ni

        hr, hi = lax.fori_loop(0, tq, step, (h_scr[gb, 0], h_scr[gb, 1]), unroll=4)
        h_scr[gb, 0] = hr
        h_scr[gb, 1] = hi
        yb = _dot(z_scr[...].astype(BF16), cw_ref[gb]).reshape(tq, nb, 128)
        if has_prev:
            yb = yb + prev_ref[:, :, lanes]
        y_ref[:, :, lanes] = yb


def _s5_direction(tm3, bw, cw, atab, prev, rev, n_lat, tq):
    ltot, nb, _ = tm3.shape
    gw = GROUP_WIDTH
    n_chunks = ltot // tq
    n_lat_chunks = n_lat // tq
    cidx = lambda s: _scan_chunk_index(s, n_lat_chunks, n_chunks, rev)
    has_prev = prev is not None
    in_specs = [pl.BlockSpec((tq, nb, gw), lambda s: (cidx(s), 0, 0)),
                pl.BlockSpec(bw.shape, lambda s: (0, 0, 0)),
                pl.BlockSpec(cw.shape, lambda s: (0, 0, 0)),
                pl.BlockSpec(atab.shape, lambda s: (0, 0, 0, 0))]
    args = [tm3, bw, cw, atab]
    if has_prev:
        in_specs.append(pl.BlockSpec((tq, nb, gw), lambda s: (cidx(s), 0, 0)))
        args.append(prev)
    kern = functools.partial(_s5_kernel, rev=rev, tq=tq, nb=nb, has_prev=has_prev)
    return pl.pallas_call(
        kern,
        out_shape=jax.ShapeDtypeStruct((ltot, nb, gw), F32),
        grid=(n_chunks,),
        in_specs=in_specs,
        out_specs=pl.BlockSpec((tq, nb, gw), lambda s: (cidx(s), 0, 0)),
        scratch_shapes=[pltpu.VMEM((tq * nb, 2 * S5_STATE * 8), F32),
                        pltpu.VMEM((S5_LANE_BLOCKS, 2, nb, S5_STATE * 8), F32)],
        compiler_params=_params("arbitrary"),
        name="s5_rev" if rev else "s5_fwd",
    )(*args)


def _block_diag(blocks):
    n, r, c = blocks.shape
    eye = jnp.eye(n, dtype=blocks.dtype)
    return (eye[:, None, :, None] * blocks[:, :, None, :]).reshape(n * r, n * c)


def _s5_tables(lam_re, lam_im, log_dt, b_re, b_im, c_re, c_im):
    dt = jnp.exp(log_dt.astype(F32))[:, None]
    lr, li = lam_re.astype(F32), lam_im.astype(F32)
    mag = jnp.exp(lr * dt)
    abar_re, abar_im = mag * jnp.cos(li * dt), mag * jnp.sin(li * dt)
    den = lr * lr + li * li
    zr = abar_re - 1.0
    w_re = (zr * lr + abar_im * li) / den
    w_im = (abar_im * lr - zr * li) / den
    bre, bim = b_re.astype(F32), b_im.astype(F32)
    bw_re = w_re[:, None, :] * bre - w_im[:, None, :] * bim
    bw_im = w_re[:, None, :] * bim + w_im[:, None, :] * bre
    gpb = S5_GROUPS // S5_LANE_BLOCKS
    bws, cws, atabs = [], [], []
    for gb in range(S5_LANE_BLOCKS):
        sl = slice(gb * gpb, (gb + 1) * gpb)
        bws.append(jnp.concatenate([_block_diag(bw_re[sl]), _block_diag(bw_im[sl])], axis=1))
        cws.append(jnp.concatenate([_block_diag(c_re[sl].astype(F32)), -_block_diag(c_im[sl].astype(F32))], axis=0))
        atabs.append(jnp.stack([abar_re[sl].reshape(1, -1), abar_im[sl].reshape(1, -1)]))
    return jnp.stack(bws).astype(BF16), jnp.stack(cws).astype(BF16), jnp.stack(atabs)


def _lru_kernel(*refs, rev, tq, nb, n_lat_chunks, n_chunks, has_prev):
    if has_prev:
        x_ref, xp_ref, xn_ref, cw_ref, cb_ref, wa_ref, wx_ref, ba_ref, bx_ref, sp_ref, prev_ref, o_ref, a_scr, b_scr, h_scr = refs
    else:
        x_ref, xp_ref, xn_ref, cw_ref, cb_ref, wa_ref, wx_ref, ba_ref, bx_ref, sp_ref, o_ref, a_scr, b_scr, h_scr = refs
        prev_ref = None
    s = pl.program_id(0)
    gw = GROUP_WIDTH
    rows = tq * nb

    @pl.when(s == 0)
    def _():
        h_scr[...] = jnp.zeros_like(h_scr)

    c = _scan_chunk_index(s, n_lat_chunks, n_chunks, rev)
    first = jnp.logical_or(c == 0, c == n_lat_chunks)
    last = jnp.logical_or(c == n_lat_chunks - 1, c == n_chunks - 1)
    keep_prev = jnp.where(first, 0.0, 1.0)
    keep_next = jnp.where(last, 0.0, 1.0)
    xc = jnp.concatenate([xp_ref[...] * keep_prev, x_ref[...], xn_ref[...] * keep_next], axis=0)
    conv = cb_ref[...] + xc[0:tq] * cw_ref[0:1, :]
    for k in range(1, LRU_CONV):
        conv = conv + xc[k:k + tq] * cw_ref[k:k + 1, :]
    x = conv.reshape(rows, gw)
    xb = x.astype(BF16)
    r = jax.nn.sigmoid(_dot(xb, wa_ref[...]) + ba_ref[...])
    ig = jax.nn.sigmoid(_dot(xb, wx_ref[...]) + bx_ref[...])
    log_a = -LRU_C * r * sp_ref[...]
    a = jnp.exp(log_a)
    a_scr[...] = a
    b_scr[...] = jnp.sqrt(1.0 - a * a) * (ig * x)

    def step(i, h):
        t = (tq - 1 - i) if rev else i
        r0 = pl.multiple_of(t * nb, nb)
        hn = a_scr[pl.ds(r0, nb), :] * h + b_scr[pl.ds(r0, nb), :]
        b_scr[pl.ds(r0, nb), :] = hn
        return hn

    h_scr[...] = lax.fori_loop(0, tq, step, h_scr[...], unroll=4)
    out = b_scr[...].reshape(tq, nb, gw)
    if has_prev:
        out = out + prev_ref[...]
    o_ref[...] = out


def _lru_direction(tm3, conv_w, conv_b, wa_bd, wx_bd, ba, bx, sp, prev, rev, n_lat, tq):
    ltot, nb, _ = tm3.shape
    gw = GROUP_WIDTH
    n_chunks = ltot // tq
    n_lat_chunks = n_lat // tq
    cidx = lambda s: _scan_chunk_index(s, n_lat_chunks, n_chunks, rev)
    has_prev = prev is not None
    full2 = lambda s: (0, 0)
    in_specs = [pl.BlockSpec((tq, nb, gw), lambda s: (cidx(s), 0, 1)),
                pl.BlockSpec((2, nb, gw), lambda s: (jnp.maximum(cidx(s) * (tq // 2) - 1, 0), 0, 1)),
                pl.BlockSpec((1, nb, gw), lambda s: (jnp.minimum((cidx(s) + 1) * tq, ltot - 1), 0, 1)),
                pl.BlockSpec((LRU_CONV, gw), full2),
                pl.BlockSpec((1, gw), full2),
                pl.BlockSpec((gw, gw), full2),
                pl.BlockSpec((gw, gw), full2),
                pl.BlockSpec((1, gw), full2),
                pl.BlockSpec((1, gw), full2),
                pl.BlockSpec((1, gw), full2)]
    args = [tm3, tm3, tm3, conv_w, conv_b, wa_bd, wx_bd, ba, bx, sp]
    if has_prev:
        in_specs.append(pl.BlockSpec((tq, nb, gw), lambda s: (cidx(s), 0, 0)))
        args.append(prev)
    kern = functools.partial(_lru_kernel, rev=rev, tq=tq, nb=nb, n_lat_chunks=n_lat_chunks,
                             n_chunks=n_chunks, has_prev=has_prev)
    return pl.pallas_call(
        kern,
        out_shape=jax.ShapeDtypeStruct((ltot, nb, gw), F32),
        grid=(n_chunks,),
        in_specs=in_specs,
        out_specs=pl.BlockSpec((tq, nb, gw), lambda s: (cidx(s), 0, 0)),
        scratch_shapes=[pltpu.VMEM((tq * nb, gw), F32),
                        pltpu.VMEM((tq * nb, gw), F32),
                        pltpu.VMEM((nb, gw), F32)],
        compiler_params=_params("arbitrary"),
        name="lru_rev" if rev else "lru_fwd",
    )(*args)


def _tri_mask(n, rev):
    t = lax.broadcasted_iota(I32, (n, n), 0)
    s = lax.broadcasted_iota(I32, (n, n), 1)
    return (s >= t) if rev else (s <= t)


def _hgrn_kernel(*refs, rev, tq, has_prev):
    if has_prev:
        q_ref, z_ref, i_ref, lb_ref, tri_ref, ones_ref, prev_ref, o_ref, st_scr, cum_scr, q_scr, k_scr, v_scr = refs
    else:
        q_ref, z_ref, i_ref, lb_ref, tri_ref, ones_ref, o_ref, st_scr, cum_scr, q_scr, k_scr, v_scr = refs
        prev_ref = None
    hps = HGRN_HEADS_PER_STEP
    cs, sb, w = GLA_CHUNK, GLA_SUB, min(HGRN_FAST_WINDOW, tq)
    nsub = cs // sb
    n_ck = tq // cs
    n_w = tq // w

    @pl.when(pl.program_id(2) == 0)
    def _():
        st_scr[...] = jnp.zeros_like(st_scr)

    lb = lb_ref[0]
    z = z_ref[0]
    iv = i_ref[0]
    f = lb + (1.0 - lb) * jax.nn.sigmoid(z)
    k_all = 1.0 - f
    v_all = iv * jax.nn.sigmoid(iv)
    q_all = q_ref[0] * (GLA_DIM ** -0.5)
    logf = jnp.log(f)
    lf_hi = logf.astype(BF16)
    rest = logf - lf_hi.astype(F32)
    lf_mid = rest.astype(BF16)
    lf_lo = (rest - lf_mid.astype(F32)).astype(BF16)
    tri = tri_ref[...]
    cum_all = _dot(tri, lf_hi) + _dot(tri, lf_mid) + _dot(tri, lf_lo)
    for hh in range(hps):
        lanes = slice(hh * GLA_DIM, (hh + 1) * GLA_DIM)
        cum_scr[hh] = cum_all[:, lanes]
        q_scr[hh] = q_all[:, lanes]
        k_scr[hh] = k_all[:, lanes]
        v_scr[hh] = v_all[:, lanes]

    def far_row(c, width):
        return c * width if rev else (c + 1) * width - 1

    def outside_row(c, width, n):
        if rev:
            return (c + 1) * width if c < n - 1 else None
        return c * width - 1 if c > 0 else None

    def store_out(hh, rows, out):
        lanes = slice(hh * GLA_DIM, (hh + 1) * GLA_DIM)
        if has_prev:
            out = out + prev_ref[0, rows, lanes]
        o_ref[0, rows, lanes] = out

    def block_factored(_):
        lane_id = lax.broadcasted_iota(I32, (1, tq), 1)
        for hh in range(hps):
            cumg, q, k = cum_scr[hh], q_scr[hh], k_scr[hh]
            vb = v_scr[hh].astype(BF16)
            st = st_scr[hh]
            ends = [cum_scr[hh, far_row(c, w):far_row(c, w) + 1, :] for c in range(n_w)]
            end_full = jnp.concatenate([jnp.broadcast_to(e, (w, GLA_DIM)) for e in ends], axis=0)
            ks = (k * jnp.exp(end_full - cumg)).astype(BF16)
            pieces, starts, row_lo = [], [], []
            total = 0
            for c in range(n_w):
                rows = slice(0, (c + 1) * w) if rev else slice(c * w, tq)
                scale = jnp.exp(jnp.minimum(cumg[rows] - ends[c], HGRN_SAFE_LOG_DECAY))
                pieces.append((q[rows] * scale).astype(BF16))
                starts.append(total)
                row_lo.append(rows.start)
                total += rows.stop - rows.start
            a_stack = _dot_nt(jnp.concatenate(pieces, axis=0), ks)
            a_rows = []
            for r in range(n_w):
                acc = jnp.zeros((w, tq), F32)
                for c in (range(r, n_w) if rev else range(0, r + 1)):
                    lo = starts[c] + r * w - row_lo[c]
                    keep = jnp.logical_and(lane_id >= c * w, lane_id < (c + 1) * w)
                    if c == r:
                        t_id = r * w + lax.broadcasted_iota(I32, (w, tq), 0)
                        keep = jnp.logical_and(keep, (lane_id >= t_id) if rev else (lane_id <= t_id))
                    acc = jnp.where(keep, a_stack[lo:lo + w, :], acc)
                a_rows.append(acc)
            a = jnp.concatenate(a_rows, axis=0).astype(BF16)
            out = _dot(a, vb) + _dot_nt((q * jnp.exp(cumg)).astype(BF16), st.astype(BF16))
            store_out(hh, slice(None), out)
            last = ends[0] if rev else ends[n_w - 1]
            kd = k * jnp.exp(last - cumg)
            st_scr[hh] = st * jnp.exp(last) + _dot_tn(vb, kd.astype(BF16))
        return 0

    row_id = lax.broadcasted_iota(I32, (sb, GLA_DIM), 0)
    valid = [(row_id <= s) if rev else (row_id >= s) for s in range(sb)]

    def chunk_direct(hh, r0):
        rows = pl.ds(r0, cs)
        q, k, cumg = q_scr[hh, rows, :], k_scr[hh, rows, :], cum_scr[hh, rows, :]
        vb = v_scr[hh, rows, :].astype(BF16)
        if rev:
            base = cum_scr[hh, pl.ds(jnp.minimum(r0 + cs, tq - 1), 1), :]
            base = jnp.where(r0 + cs >= tq, 0.0, base)
        else:
            base = cum_scr[hh, pl.ds(jnp.maximum(r0 - 1, 0), 1), :]
            base = jnp.where(r0 == 0, 0.0, base)
        cum = cumg - base
        last = cum[0:1] if rev else cum[cs - 1:cs]
        st = st_scr[hh]
        o_inter = _dot_nt((q * jnp.exp(cum)).astype(BF16), st.astype(BF16))
        outs = []
        for i in range(nsub):
            blk = slice(i * sb, (i + 1) * sb)
            off = slice((i + 1) * sb, cs) if rev else slice(0, i * sb)
            o_row = outside_row(i, sb, nsub)
            q_i, cum_i = q[blk], cum[blk]
            o_i = o_inter[blk]
            if o_row is not None:
                ref_row = cum[o_row:o_row + 1]
                qt = q_i * jnp.exp(cum_i - ref_row)
                kt = k[off] * jnp.exp(ref_row - cum[off])
                a_off = _dot_nt(qt.astype(BF16), kt.astype(BF16))
                o_i = o_i + _dot(a_off.astype(BF16), vb[off])
            prods = []
            for s in range(sb):
                r = r0 + i * sb + s
                e = jnp.exp(jnp.minimum(cumg[blk] - cum_scr[hh, pl.ds(r, 1), :], 0.0))
                prods.append(jnp.where(valid[s], q_i * e * k_scr[hh, pl.ds(r, 1), :], 0.0))
            sums = _dot(jnp.concatenate(prods, axis=0).astype(BF16), ones_ref[...])
            for s in range(sb):
                r = r0 + i * sb + s
                o_i = o_i + sums[s * sb:(s + 1) * sb] * v_scr[hh, pl.ds(r, 1), :]
            outs.append(o_i)
        store_out(hh, rows, jnp.concatenate(outs, axis=0))
        kd = k * jnp.exp(last - cum)
        st_scr[hh] = st * jnp.exp(last) + _dot_tn(vb, kd.astype(BF16))

    def block_direct(_):
        def chunk(ci, carry):
            c = (n_ck - 1 - ci) if rev else ci
            r0 = pl.multiple_of(c * cs, cs)
            for hh in range(hps):
                chunk_direct(hh, r0)
            return carry

        return lax.fori_loop(0, n_ck, chunk, 0)

    decays = []
    for c in range(n_w):
        far = cum_all[far_row(c, w):far_row(c, w) + 1]
        o_row = outside_row(c, w, n_w)
        decays.append(far if o_row is None else far - cum_all[o_row:o_row + 1])
    safe = jnp.logical_and(jnp.min(jnp.concatenate(decays, axis=0)) > -HGRN_SAFE_LOG_DECAY,
                           jnp.max(jnp.abs(q_all)) < HGRN_SAFE_QUERY)
    lax.cond(safe, block_factored, block_direct, 0)


def _gla_specs(ltot, tq, n_lat, rev):
    n_chunks = ltot // tq
    n_lat_chunks = n_lat // tq
    cidx = lambda s: _scan_chunk_index(s, n_lat_chunks, n_chunks, rev)
    part = lambda p: pl.BlockSpec((1, tq, GLA_DIM), lambda b, h, s: (b, cidx(s), p * GLA_HEADS + h))
    return n_chunks, cidx, part


def _hgrn_direction(bm, lb, prev, rev, n_lat, tq):
    bsz, ltot, _ = bm.shape
    n_chunks, cidx, _ = _gla_specs(ltot, tq, n_lat, rev)
    has_prev = prev is not None
    hps = HGRN_HEADS_PER_STEP
    width = hps * GLA_DIM
    n_hsteps = GLA_HEADS // hps
    part = lambda p: pl.BlockSpec((1, tq, width), lambda b, h, s: (b, cidx(s), p * n_hsteps + h))
    tri = _tri_mask(tq, rev).astype(BF16)
    ones = jnp.ones((GLA_DIM, GLA_DIM), BF16)
    z_part = 2 if rev else 1
    in_specs = [part(0), part(z_part), part(3),
                pl.BlockSpec((1, 1, width), lambda b, h, s: (h, 0, 0)),
                pl.BlockSpec(tri.shape, lambda b, h, s: (0, 0)),
                pl.BlockSpec(ones.shape, lambda b, h, s: (0, 0))]
    args = [bm, bm, bm, lb.reshape(n_hsteps, 1, width), tri, ones]
    out_spec = pl.BlockSpec((1, tq, width), lambda b, h, s: (b, cidx(s), h))
    if has_prev:
        in_specs.append(out_spec)
        args.append(prev)
    kern = functools.partial(_hgrn_kernel, rev=rev, tq=tq, has_prev=has_prev)
    tile = pltpu.VMEM((hps, tq, GLA_DIM), F32)
    return pl.pallas_call(
        kern,
        out_shape=jax.ShapeDtypeStruct((bsz, ltot, GROUP_WIDTH), F32),
        grid=(bsz, n_hsteps, n_chunks),
        in_specs=in_specs,
        out_specs=out_spec,
        scratch_shapes=[pltpu.VMEM((hps, GLA_DIM, GLA_DIM), F32), tile, tile, tile, tile],
        compiler_params=_params("arbitrary", "arbitrary", "arbitrary"),
        name="hgrn_rev" if rev else "hgrn_fwd",
    )(*args)


def _ret_kernel(*refs, rev, tq, has_prev):
    if has_prev:
        q_ref, k_ref, v_ref, cos_ref, sin_ref, eq_ref, ek_ref, dm_ref, el_ref, prev_ref, o_ref, st_scr = refs
    else:
        q_ref, k_ref, v_ref, cos_ref, sin_ref, eq_ref, ek_ref, dm_ref, el_ref, o_ref, st_scr = refs
        prev_ref = None
    cs = min(tq, RET_CHUNK)
    n_ck = tq // cs

    @pl.when(pl.program_id(1) == 0)
    def _():
        st_scr[...] = jnp.zeros_like(st_scr)

    def chunk(ci, carry):
        c = (n_ck - 1 - ci) if rev else ci
        r0 = pl.multiple_of(c * cs, cs)
        rows = pl.ds(r0, cs)
        cos = cos_ref[rows, :]
        sin = sin_ref[rows, :]
        for h in range(GLA_HEADS):
            lanes = slice(h * GLA_DIM, (h + 1) * GLA_DIM)
            q = q_ref[0, rows, lanes]
            k = k_ref[0, rows, lanes]
            vb = v_ref[0, rows, lanes].astype(BF16)
            q = q * cos + pltpu.roll(q, GLA_DIM // 2, 1) * sin
            k = (k * cos + pltpu.roll(k, GLA_DIM // 2, 1) * sin) * (GLA_DIM ** -0.5)
            st = st_scr[h]
            scores = _dot_nt(q.astype(BF16), k.astype(BF16)) * dm_ref[h]
            out = _dot_nt((q * eq_ref[h]).astype(BF16), st.astype(BF16)) + _dot(scores.astype(BF16), vb)
            if has_prev:
                out = out + prev_ref[0, rows, lanes]
            o_ref[0, rows, lanes] = out
            st_scr[h] = st * el_ref[h] + _dot_tn(vb, (k * ek_ref[h]).astype(BF16))
        return carry

    lax.fori_loop(0, n_ck, chunk, 0)


def _ret_tables(rev, cs):
    gamma = jnp.log1p(-jnp.exp2(-(RET_DECAY_EXP[1 if rev else 0] + jnp.arange(GLA_HEADS, dtype=F32))))
    t = jnp.arange(cs, dtype=F32)
    steps = (cs - t) if rev else (t + 1.0)
    cum = gamma[:, None] * steps[None, :]
    last = gamma * cs
    eq = jnp.broadcast_to(jnp.exp(cum)[:, :, None], (GLA_HEADS, cs, GLA_DIM))
    ek = jnp.broadcast_to(jnp.exp(last[:, None] - cum)[:, :, None], (GLA_HEADS, cs, GLA_DIM))
    rel = cum[:, :, None] - cum[:, None, :]
    dm = jnp.where(_tri_mask(cs, rev)[None], jnp.exp(jnp.minimum(rel, 0.0)), 0.0)
    el = jnp.broadcast_to(jnp.exp(last)[:, None, None], (GLA_HEADS, 1, GLA_DIM))
    return eq, ek, dm, el


def _rotary_tables(n_lat, n_ctx):
    rows = n_lat // GRID_W
    row = jnp.repeat(jnp.arange(rows, dtype=F32), GRID_W)
    col = jnp.tile(jnp.arange(GRID_W, dtype=F32), rows)
    quarter = GLA_DIM // 4
    inv_freq = ROPE_BASE ** (-jnp.arange(quarter, dtype=F32) / quarter)
    ang = jnp.concatenate([row[:, None] * inv_freq, col[:, None] * inv_freq], axis=-1)
    cos, sin = jnp.cos(ang), jnp.sin(ang)
    cos_l = jnp.concatenate([cos, cos], axis=-1)
    sin_l = jnp.concatenate([-sin, sin], axis=-1)
    cos_t = jnp.concatenate([cos_l, jnp.ones((n_ctx, GLA_DIM), F32)], axis=0)
    sin_t = jnp.concatenate([sin_l, jnp.zeros((n_ctx, GLA_DIM), F32)], axis=0)
    return cos_t, sin_t


def _ret_direction(bm, cos_t, sin_t, prev, rev, n_lat, tq):
    bsz, ltot, _ = bm.shape
    n_chunks, cidx, _ = _gla_specs(ltot, tq, n_lat, rev)
    has_prev = prev is not None
    eq, ek, dm, el = _ret_tables(rev, min(tq, RET_CHUNK))
    gw = GROUP_WIDTH
    part = lambda p: pl.BlockSpec((1, tq, gw), lambda b, s: (b, cidx(s), p))
    whole = lambda a: pl.BlockSpec(a.shape, lambda b, s: (0, 0, 0))
    pos = pl.BlockSpec((tq, GLA_DIM), lambda b, s: (cidx(s), 0))
    in_specs = [part(5), part(6), part(7), pos, pos, whole(eq), whole(ek), whole(dm), whole(el)]
    args = [bm, bm, bm, cos_t, sin_t, eq, ek, dm, el]
    out_spec = pl.BlockSpec((1, tq, gw), lambda b, s: (b, cidx(s), 0))
    if has_prev:
        in_specs.append(out_spec)
        args.append(prev)
    kern = functools.partial(_ret_kernel, rev=rev, tq=tq, has_prev=has_prev)
    return pl.pallas_call(
        kern,
        out_shape=jax.ShapeDtypeStruct((bsz, ltot, gw), F32),
        grid=(bsz, n_chunks),
        in_specs=in_specs,
        out_specs=out_spec,
        scratch_shapes=[pltpu.VMEM((GLA_HEADS, GLA_DIM, GLA_DIM), F32)],
        compiler_params=_params("arbitrary", "arbitrary"),
        name="ret_rev" if rev else "ret_fwd",
    )(*args)


def _head_rms(o):
    parts = []
    for h in range(GLA_HEADS):
        oh = o[:, h * GLA_DIM:(h + 1) * GLA_DIM]
        parts.append(oh * lax.rsqrt(jnp.mean(oh * oh, axis=-1, keepdims=True) + EPS))
    return jnp.concatenate(parts, axis=-1)


def _mix_kernel(u_ref, lg_ref, ys_ref, hs_ref, hg_ref, rg_ref, oh_ref, or_ref, d_ref, gw_ref, gb_ref, o_ref):
    gw = GROUP_WIDTH
    y = jax.nn.gelu(d_ref[...] * u_ref[...] + ys_ref[...])
    a = y * jax.nn.sigmoid(_dot(y.astype(BF16), gw_ref[...]) + gb_ref[...])
    o_ref[0, :, 0:gw] = a.astype(BF16)
    o_ref[0, :, gw:2 * gw] = (hs_ref[...] * jax.nn.gelu(lg_ref[...])).astype(BF16)
    g = hg_ref[0]
    o_ref[0, :, 2 * gw:3 * gw] = (_head_rms(oh_ref[0]) * (g * jax.nn.sigmoid(g))).astype(BF16)
    g = rg_ref[0]
    o_ref[0, :, 3 * gw:4 * gw] = (_head_rms(or_ref[0]) * (g * jax.nn.sigmoid(g))).astype(BF16)


def _mix_epilogue(tm2, ys2, hs2, bm, o_h, o_r, s5_d, glu_w, glu_b, n_rows, tm):
    bsz, ltot, _ = bm.shape
    gw = GROUP_WIDTH
    tmaj = lambda p, k: pl.BlockSpec((tm, gw), lambda b, i: (i, b * k + p))
    bmaj = lambda p: pl.BlockSpec((1, tm, gw), lambda b, i: (b, i, p))
    full2 = lambda b, i: (0, 0)
    return pl.pallas_call(
        _mix_kernel,
        out_shape=jax.ShapeDtypeStruct((bsz, ltot, 4 * gw), BF16),
        grid=(bsz, n_rows // tm),
        in_specs=[tmaj(0, N_TM_PARTS), tmaj(2, N_TM_PARTS), tmaj(0, 1), tmaj(0, 1),
                  bmaj(4), bmaj(8), bmaj(0), bmaj(0),
                  pl.BlockSpec((1, gw), full2), pl.BlockSpec((gw, gw), full2), pl.BlockSpec((1, gw), full2)],
        out_specs=pl.BlockSpec((1, tm, 4 * gw), lambda b, i: (b, i, 0)),
        compiler_params=_params("arbitrary", "arbitrary"),
        name="mix_epilogue",
    )(tm2, tm2, ys2, hs2, bm, bm, o_h, o_r, s5_d, glu_w, glu_b)


def _gate_rows(modb_ref, modc_ref, idx, row0, tm, n_lat):
    row = row0 + lax.broadcasted_iota(I32, (tm, 1), 0)
    return jnp.where(row >= n_lat, modc_ref[0, idx:idx + 1, :], modb_ref[0, idx:idx + 1, :])


def _outproj_kernel(a_ref, w_ref, x_ref, modb_ref, modc_ref, o_ref, *, tm, n_lat):
    gate = _gate_rows(modb_ref, modc_ref, 2, pl.program_id(1) * tm, tm, n_lat)
    o_ref[0] = x_ref[0] + gate * _dot(a_ref[0], w_ref[0])


def _out_projection(mix, w_bf16, layer, xs, modv, n_rows, n_lat, tm, tn):
    bsz, ltot, d = xs.shape
    k = mix.shape[-1]
    kern = functools.partial(_outproj_kernel, tm=tm, n_lat=n_lat)
    return pl.pallas_call(
        kern,
        out_shape=jax.ShapeDtypeStruct((bsz, ltot, d), F32),
        grid=(bsz, n_rows // tm, d // tn),
        in_specs=[pl.BlockSpec((1, tm, k), lambda b, i, j: (b, i, 0)),
                  pl.BlockSpec((1, k, tn), lambda b, i, j: (layer, 0, j)),
                  pl.BlockSpec((1, tm, tn), lambda b, i, j: (b, i, j)),
                  pl.BlockSpec((1, 6, tn), lambda b, i, j: (b, 0, j)),
                  pl.BlockSpec((1, 6, tn), lambda b, i, j: (bsz, 0, j))],
        out_specs=pl.BlockSpec((1, tm, tn), lambda b, i, j: (b, i, j)),
        compiler_params=_params("arbitrary", "arbitrary", "arbitrary"),
        name="out_projection",
    )(mix, w_bf16, xs, modv, modv)


def _first_max4(v):
    m1 = jnp.maximum(jnp.maximum(v[0], v[1]), jnp.maximum(v[2], v[3]))
    i1 = jnp.where(v[0] == m1, 0, jnp.where(v[1] == m1, 1, jnp.where(v[2] == m1, 2, 3)))
    rest = [jnp.where(i1 == j, -jnp.inf, v[j]) for j in range(4)]
    m2 = jnp.maximum(jnp.maximum(rest[0], rest[1]), jnp.maximum(rest[2], rest[3]))
    i2 = jnp.where(rest[0] == m2, 0, jnp.where(rest[1] == m2, 1, jnp.where(rest[2] == m2, 2, 3)))
    return m1, i1, m2, i2


def _router_kernel(x_ref, g_ref, modb_ref, modc_ref, rw_ref, rb_ref, tri_ref,
                   h_ref, ids_ref, gates_ref, ranks_ref, counts_ref, base_scr, *, tm, n_lat):
    first = jnp.logical_and(pl.program_id(0) == 0, pl.program_id(1) == 0)

    @pl.when(first)
    def _():
        base_scr[...] = jnp.zeros_like(base_scr)

    h = _norm_modulate(x_ref[0], g_ref[...], modb_ref, modc_ref, pl.program_id(1) * tm, n_lat, 3, 4)
    h_ref[0] = h.astype(BF16)
    logits = lax.dot_general(rw_ref[...], h, (((1,), (1,)), ((), ())), preferred_element_type=F32,
                             precision=lax.Precision.HIGHEST)
    scores = jax.nn.sigmoid(logits)
    biased = scores + rb_ref[...]
    tops = []
    for gidx in range(N_EXPERT_GROUPS):
        rows = [biased[gidx * EXPERTS_PER_GROUP + j:gidx * EXPERTS_PER_GROUP + j + 1, :]
                for j in range(EXPERTS_PER_GROUP)]
        tops.append(_first_max4(rows))
    gs = [t[0] + t[2] for t in tops]
    gmax = jnp.maximum(jnp.maximum(gs[0], gs[1]), jnp.maximum(gs[2], gs[3]))
    best = jnp.where(gs[0] == gmax, 0, jnp.where(gs[1] == gmax, 1, jnp.where(gs[2] == gmax, 2, 3)))
    e1 = jnp.zeros_like(best)
    e2 = jnp.zeros_like(best)
    for gidx in range(N_EXPERT_GROUPS):
        e1 = jnp.where(best == gidx, gidx * EXPERTS_PER_GROUP + tops[gidx][1], e1)
        e2 = jnp.where(best == gidx, gidx * EXPERTS_PER_GROUP + tops[gidx][3], e2)
    eid = lax.broadcasted_iota(I32, (N_EXPERTS, tm), 0)
    sel1 = eid == e1
    sel2 = eid == e2
    w1 = jnp.sum(jnp.where(sel1, scores, 0.0), axis=0, keepdims=True)
    w2 = jnp.sum(jnp.where(sel2, scores, 0.0), axis=0, keepdims=True)
    wsum = w1 + w2
    onehot = jnp.where(jnp.logical_or(sel1, sel2), 1.0, 0.0)
    pos = base_scr[...] + _dot(onehot.astype(BF16), tri_ref[...])
    r1 = jnp.sum(jnp.where(sel1, pos, 0.0), axis=0, keepdims=True)
    r2 = jnp.sum(jnp.where(sel2, pos, 0.0), axis=0, keepdims=True)
    ids_ref[0] = jnp.concatenate([e1, e2], axis=0)
    gates_ref[0] = jnp.concatenate([w1 / wsum, w2 / wsum], axis=0)
    ranks_ref[0] = jnp.concatenate([r1, r2], axis=0).astype(I32)
    base = base_scr[...] + jnp.sum(onehot, axis=1, keepdims=True)
    base_scr[...] = base
    counts_ref[...] = jnp.broadcast_to(base, counts_ref.shape)


def _router(xs, gain, modv, rw_t, rbias, n_rows, n_lat, tm):
    bsz, ltot, d = xs.shape
    tri = (lax.broadcasted_iota(I32, (tm, tm), 0) < lax.broadcasted_iota(I32, (tm, tm), 1)).astype(BF16)
    kern = functools.partial(_router_kernel, tm=tm, n_lat=n_lat)
    small = lambda dt: jax.ShapeDtypeStruct((bsz, 2, n_rows), dt)
    small_spec = pl.BlockSpec((1, 2, tm), lambda b, i: (b, 0, i))
    return pl.pallas_call(
        kern,
        out_shape=(jax.ShapeDtypeStruct((bsz, ltot, d), BF16), small(I32), small(F32), small(I32),
                   jax.ShapeDtypeStruct((N_EXPERTS, 128), F32)),
        grid=(bsz, n_rows // tm),
        in_specs=[pl.BlockSpec((1, tm, d), lambda b, i: (b, i, 0)),
                  pl.BlockSpec((1, d), lambda b, i: (0, 0)),
                  pl.BlockSpec((1, 6, d), lambda b, i: (b, 0, 0)),
                  pl.BlockSpec((1, 6, d), lambda b, i: (bsz, 0, 0)),
                  pl.BlockSpec((N_EXPERTS, d), lambda b, i: (0, 0)),
                  pl.BlockSpec((N_EXPERTS, 1), lambda b, i: (0, 0)),
                  pl.BlockSpec((tm, tm), lambda b, i: (0, 0))],
        out_specs=(pl.BlockSpec((1, tm, d), lambda b, i: (b, i, 0)), small_spec, small_spec, small_spec,
                   pl.BlockSpec((N_EXPERTS, 128), lambda b, i: (0, 0))),
        scratch_shapes=[pltpu.VMEM((N_EXPERTS, 1), F32)],
        compiler_params=_params("arbitrary", "arbitrary"),
        name="moe_router",
    )(xs, gain, modv, modv, rw_t, rbias, tri)


def _expert_ffn_kernel(te_ref, nt_ref, x_ref, wg_ref, wu_ref, wd_ref, o_ref):
    live = pl.program_id(0) < nt_ref[0]

    @pl.when(live)
    def _():
        x = x_ref[...]
        g = _dot(x, wg_ref[0])
        u = _dot(x, wu_ref[0])
        he = (g * jax.nn.sigmoid(g)) * u
        o_ref[...] = _dot(he.astype(BF16), wd_ref[0]).astype(o_ref.dtype)

    @pl.when(jnp.logical_not(live))
    def _():
        o_ref[...] = jnp.zeros_like(o_ref)


def _expert_ffn(tile_expert, n_tiles, x_sorted, wg, wu, wd, tm):
    p, d = x_sorted.shape
    f = wg.shape[-1]
    tile = lambda i, te, nt: i
    expert = lambda i, te, nt: te[jnp.minimum(i, nt[0] - 1)]
    grid_spec = pltpu.PrefetchScalarGridSpec(
        num_scalar_prefetch=2,
        grid=(p // tm,),
        in_specs=[pl.BlockSpec((tm, d), lambda i, te, nt: (tile(i, te, nt), 0)),
                  pl.BlockSpec((1, d, f), lambda i, te, nt: (expert(i, te, nt), 0, 0)),
                  pl.BlockSpec((1, d, f), lambda i, te, nt: (expert(i, te, nt), 0, 0)),
                  pl.BlockSpec((1, f, d), lambda i, te, nt: (expert(i, te, nt), 0, 0))],
        out_specs=pl.BlockSpec((tm, d), lambda i, te, nt: (tile(i, te, nt), 0)),
    )
    return pl.pallas_call(
        _expert_ffn_kernel,
        out_shape=jax.ShapeDtypeStruct((p, d), BF16),
        grid_spec=grid_spec,
        compiler_params=_params("arbitrary"),
        name="moe_expert_ffn",
    )(tile_expert, n_tiles, x_sorted, wg, wu, wd)


def _moe_residual_kernel(x_ref, y0_ref, y1_ref, gates_ref, modb_ref, modc_ref, g_ref, o_ref,
                         *, tm, n_lat, final_norm):
    gates = gates_ref[0]
    y = gates[:, 0:1] * y0_ref[0, 0].astype(F32) + gates[:, 1:2] * y1_ref[0, 0].astype(F32)
    gate = _gate_rows(modb_ref, modc_ref, 5, pl.program_id(1) * tm, tm, n_lat)
    x = x_ref[0] + gate * y
    if final_norm:
        x = x * lax.rsqrt(jnp.mean(x * x, axis=-1, keepdims=True) + EPS) * g_ref[...]
    o_ref[0] = x


def _moe_residual(xs, y_pair, gates_t, modv, final_g, n_rows, n_lat, tm, final_norm):
    bsz, ltot, d = xs.shape
    kern = functools.partial(_moe_residual_kernel, tm=tm, n_lat=n_lat, final_norm=final_norm)
    rows_out = n_rows if final_norm else ltot
    return pl.pallas_call(
        kern,
        out_shape=jax.ShapeDtypeStruct((bsz, rows_out, d), F32),
        grid=(bsz, n_rows // tm),
        in_specs=[pl.BlockSpec((1, tm, d), lambda b, i: (b, i, 0)),
                  pl.BlockSpec((1, 1, tm, d), lambda b, i: (0, b, i, 0)),
                  pl.BlockSpec((1, 1, tm, d), lambda b, i: (1, b, i, 0)),
                  pl.BlockSpec((1, tm, 2), lambda b, i: (b, i, 0)),
                  pl.BlockSpec((1, 6, d), lambda b, i: (b, 0, 0)),
                  pl.BlockSpec((1, 6, d), lambda b, i: (bsz, 0, 0)),
                  pl.BlockSpec((1, d), lambda b, i: (0, 0))],
        out_specs=pl.BlockSpec((1, tm, d), lambda b, i: (b, i, 0)),
        compiler_params=_params("arbitrary", "arbitrary"),
        name="moe_residual",
    )(xs, y_pair, y_pair, gates_t, modv, modv, final_g)


def _moe_layer(xs, gain, modv, rw_t, rbias, wg, wu, wd, expert_base, final_g, n_rows, n_lat, tm_tok, tm_res, tm_exp,
               final_norm):
    bsz, ltot, d = xs.shape
    h, ids, gates, ranks, counts = _router(xs, gain, modv, rw_t, rbias, n_rows, n_lat, tm_tok)
    counts = counts[:, 0].astype(I32)
    padded = ((counts + tm_exp - 1) // tm_exp) * tm_exp
    ends = jnp.cumsum(padded)
    offsets = ends - padded
    onehot = ids[..., None] == jnp.arange(N_EXPERTS, dtype=I32)
    dest = jnp.sum(jnp.where(onehot, offsets, 0), axis=-1) + ranks
    n_slots = bsz * 2 * n_rows
    p_rows = ((n_slots + N_EXPERTS * (tm_exp - 1)) + tm_exp - 1) // tm_exp * tm_exp
    tok = (jnp.arange(bsz, dtype=I32)[:, None, None] * ltot + jnp.arange(n_rows, dtype=I32)[None, None, :])
    tok = jnp.broadcast_to(tok, dest.shape)
    fill_slot = jnp.arange(p_rows, dtype=I32) % n_slots
    filler = (fill_slot // (2 * n_rows)) * ltot + fill_slot % n_rows
    src = filler.at[dest.reshape(-1)].set(tok.reshape(-1), mode="promise_in_bounds", unique_indices=True)
    tile_start = jnp.arange(p_rows // tm_exp, dtype=I32) * tm_exp
    tile_expert = jnp.minimum(jnp.sum(tile_start[:, None] >= ends[None, :], axis=1), N_EXPERTS - 1).astype(I32)
    n_tiles = (ends[-1:] // tm_exp).astype(I32)
    x_sorted = h.reshape(bsz * ltot, d).at[src].get(mode="promise_in_bounds")
    y_sorted = _expert_ffn(tile_expert + expert_base, n_tiles, x_sorted, wg, wu, wd, tm_exp)
    y_pair = y_sorted.at[dest.transpose(1, 0, 2).reshape(-1)].get(mode="promise_in_bounds")
    y_pair = y_pair.reshape(2, bsz, n_rows, d)
    return _moe_residual(xs, y_pair, gates.transpose(0, 2, 1), modv, final_g, n_rows, n_lat, tm_res, final_norm)


def _row_tile(n, target):
    t = min(n, target)
    while n % t or t % 8:
        t -= 8
    return t


def kernel(x, c, ctx, c_ctx, norm_mix_g, norm_ffn_g, w_mod, b_mod, w_in, w_out, s5_lam_re, s5_lam_im, s5_log_dt, s5_b_re, s5_b_im, s5_c_re, s5_c_im, s5_d, s5_glu_w, s5_glu_b, lru_conv_w, lru_conv_b, lru_wa, lru_ba, lru_wx, lru_bx, lru_lam, hgrn_lb_logits, router_w, router_bias, moe_w_gate, moe_w_up, moe_w_down, final_norm_g):
    bsz, n_lat, d = x.shape
    n_ctx = ctx.shape[1]
    ltot = n_lat + n_ctx
    depth = w_in.shape[0]
    gw = GROUP_WIDTH
    assert bsz % 8 == 0 and bsz + 1 <= MOD_ROWS
    assert n_lat % GLA_CHUNK == 0 and n_ctx % GLA_CHUNK == 0 and n_lat % GRID_W == 0 and n_lat % NORM_SLAB == 0

    tq_scan = math.gcd(math.gcd(n_lat, n_ctx), 64)
    tq_gla = math.gcd(math.gcd(n_lat, n_ctx), 256)
    tm_all = _row_tile(ltot, 768)
    tm_lat = _row_tile(n_lat, 1024)
    tm_mix = math.gcd(math.gcd(n_lat, n_ctx), 256)

    xs = jnp.concatenate([x, ctx.astype(x.dtype)], axis=1)
    cv = jnp.zeros((MOD_ROWS, d), F32).at[:bsz].set(c).at[bsz].set(c_ctx)
    mod_all = _mod_vectors(cv, w_mod, b_mod).reshape(depth, MOD_ROWS, 6, d)
    cos_t, sin_t = _rotary_tables(n_lat, n_ctx)
    rw_t = router_w.astype(F32).T
    rbias = router_bias.astype(F32).reshape(N_EXPERTS, 1)
    lb_cum = [jnp.cumsum(jax.nn.softmax(hgrn_lb_logits[dr].astype(F32), axis=0), axis=0) for dr in (0, 1)]
    w_in_b, w_out_b = _cast_bf16(w_in), _cast_bf16(w_out)
    n_exp, _, d_ff = moe_w_gate.shape[1:]
    wg_b = _cast_bf16(moe_w_gate.reshape(depth * n_exp, d, d_ff))
    wu_b = _cast_bf16(moe_w_up.reshape(depth * n_exp, d, d_ff))
    wd_b = _cast_bf16(moe_w_down.reshape(depth * n_exp, d_ff, d))

    for layer in range(depth):
        ctx_out = layer < depth - 1
        n_rows = ltot if ctx_out else n_lat
        tm_tok = tm_all if ctx_out else tm_lat
        modv = mod_all[layer]

        tm2, bm = _in_projection(xs, norm_mix_g[layer].reshape(1, d), modv, w_in_b, layer, n_lat, tm_all)
        tm3 = tm2.reshape(ltot, bsz, N_TM_PARTS * gw)

        ys = hs = o_h = o_r = None
        for dr in (0, 1):
            rev = dr == 1
            bw, cw, atab = _s5_tables(s5_lam_re[layer, dr], s5_lam_im[layer, dr], s5_log_dt[layer, dr],
                                      s5_b_re[layer, dr], s5_b_im[layer, dr], s5_c_re[layer, dr], s5_c_im[layer, dr])
            ys = _s5_direction(tm3, bw, cw, atab, ys, rev, n_lat, tq_scan)
            sp = jax.nn.softplus(-lru_lam[layer, dr].astype(F32)).reshape(1, gw)
            hs = _lru_direction(tm3, lru_conv_w[layer].astype(F32), lru_conv_b[layer].astype(F32).reshape(1, gw),
                                _block_diag(lru_wa[layer, dr]).astype(BF16), _block_diag(lru_wx[layer, dr]).astype(BF16),
                                lru_ba[layer, dr].astype(F32).reshape(1, gw), lru_bx[layer, dr].astype(F32).reshape(1, gw),
                                sp, hs, rev, n_lat, tq_scan)
            lb = lb_cum[dr][layer] - lb_cum[dr][0]
            o_h = _hgrn_direction(bm, lb, o_h, rev, n_lat, tq_gla)
            o_r = _ret_direction(bm, cos_t, sin_t, o_r, rev, n_lat, tq_gla)

        mix = _mix_epilogue(tm2, ys.reshape(ltot, bsz * gw), hs.reshape(ltot, bsz * gw), bm, o_h, o_r,
                            s5_d[layer].astype(F32).reshape(1, gw), s5_glu_w[layer].astype(BF16),
                            s5_glu_b[layer].astype(F32).reshape(1, gw), n_rows, tm_mix)
        xs = _out_projection(mix, w_out_b, layer, xs, modv, n_rows, n_lat, tm_tok, 1024)
        xs = _moe_layer(xs, norm_ffn_g[layer].reshape(1, d), modv, rw_t, rbias, wg_b, wu_b, wd_b, layer * n_exp,
                        final_norm_g.reshape(1, d), n_rows, n_lat, tm_tok, tm_mix, 512, not ctx_out)
    return xs
```

```python
import functools
import math

import jax
import jax.numpy as jnp
from jax import lax
from jax.experimental import pallas as pl
from jax.experimental.pallas import tpu as pltpu

F32 = jnp.float32
BF16 = jnp.bfloat16
I32 = jnp.int32

GROUP_WIDTH = 512
N_IN_PARTS = 12
N_TM_PARTS = 3
N_BM_PARTS = N_IN_PARTS - N_TM_PARTS
S5_CH = 16
S5_GROUPS = 32
S5_STATE = 64
S5_LANE_BLOCKS = 4
LRU_HEADS = 8
LRU_CONV = 4
LRU_C = 8.0
GLA_HEADS = 4
GLA_DIM = 128
GLA_CHUNK = 64
GLA_SUB = 16
HGRN_FAST_WINDOW = 32
HGRN_SAFE_LOG_DECAY = 80.0
HGRN_SAFE_QUERY = 1.0e3
HGRN_HEADS_PER_STEP = 4
RET_CHUNK = 256
RET_DECAY_EXP = (5.0, 5.5)
ROPE_BASE = 10000.0
GRID_W = 64
N_EXPERTS = 16
N_EXPERT_GROUPS = 4
EXPERTS_PER_GROUP = 4
D_FF_EXPERT = 1024
EPS = 1e-6
NORM_SLAB = 16
MOD_ROWS = 24

VMEM_LIMIT_BYTES = 56 * 1024 * 1024


def _params(*semantics):
    return pltpu.CompilerParams(dimension_semantics=semantics, vmem_limit_bytes=VMEM_LIMIT_BYTES)


def _dot(a, b):
    return jnp.dot(a, b, preferred_element_type=F32)


def _dot_nt(a, b):
    return lax.dot_general(a, b, (((1,), (1,)), ((), ())), preferred_element_type=F32)


def _dot_tn(a, b):
    return lax.dot_general(a, b, (((0,), (0,)), ((), ())), preferred_element_type=F32)


def _scan_chunk_index(step, n_lat_chunks, n_chunks, rev):
    if rev:
        return n_chunks - 1 - step
    return (step + n_lat_chunks) % n_chunks


def _cast_kernel(x_ref, o_ref):
    o_ref[...] = x_ref[...].astype(o_ref.dtype)


def _cast_bf16(w):
    n, r, c = w.shape
    tc = min(c // 2, 1024)
    spec = pl.BlockSpec((1, r, tc), lambda i, j: (i, 0, j))
    return pl.pallas_call(
        _cast_kernel,
        out_shape=jax.ShapeDtypeStruct(w.shape, BF16),
        grid=(n, c // tc),
        in_specs=[spec],
        out_specs=spec,
        compiler_params=_params("arbitrary", "arbitrary"),
        name="cast_bf16",
    )(w)


def _mod_kernel(cv_ref, w_ref, b_ref, o_ref):
    cv = cv_ref[...]
    s = cv * jax.nn.sigmoid(cv)
    o_ref[0] = _dot(s.astype(BF16), w_ref[0].astype(BF16)) + b_ref[0]


def _mod_vectors(cv, w_mod, b_mod):
    nl, d, n6 = w_mod.shape
    tn = 1024
    return pl.pallas_call(
        _mod_kernel,
        out_shape=jax.ShapeDtypeStruct((nl, MOD_ROWS, n6), F32),
        grid=(nl, n6 // tn),
        in_specs=[pl.BlockSpec((MOD_ROWS, d), lambda l, j: (0, 0)),
                  pl.BlockSpec((1, d, tn), lambda l, j: (l, 0, j)),
                  pl.BlockSpec((1, 1, tn), lambda l, j: (l, 0, j))],
        out_specs=pl.BlockSpec((1, MOD_ROWS, tn), lambda l, j: (l, 0, j)),
        compiler_params=_params("arbitrary", "arbitrary"),
        name="mod_vectors",
    )(cv, w_mod, b_mod.reshape(nl, 1, n6))


def _norm_modulate(x, g, modb_ref, modc_ref, row0, n_lat, shift_idx, scale_idx):
    tm = x.shape[0]
    ms = jnp.mean(x * x, axis=-1, keepdims=True)
    y = x * lax.rsqrt(ms + EPS) * g
    row = row0 + lax.broadcasted_iota(I32, (tm, 1), 0)
    is_ctx = row >= n_lat
    shift = jnp.where(is_ctx, modc_ref[0, shift_idx:shift_idx + 1, :], modb_ref[0, shift_idx:shift_idx + 1, :])
    scale = jnp.where(is_ctx, modc_ref[0, scale_idx:scale_idx + 1, :], modb_ref[0, scale_idx:scale_idx + 1, :])
    return y * (1.0 + scale) + shift


def _inproj_kernel(x_ref, g_ref, modb_ref, modc_ref, w_ref, otm_ref, obm_ref, h_scr, *, tm, n_lat):
    i = pl.program_id(1)
    j = pl.program_id(2)

    @pl.when(j == 0)
    def _():
        def slab(s, carry):
            r0 = pl.multiple_of(s * NORM_SLAB, NORM_SLAB)
            is_ctx = i * tm + r0 >= n_lat
            shift = jnp.where(is_ctx, modc_ref[0, 0:1, :], modb_ref[0, 0:1, :])
            gain = g_ref[...] * (1.0 + jnp.where(is_ctx, modc_ref[0, 1:2, :], modb_ref[0, 1:2, :]))
            x = x_ref[0, pl.ds(r0, NORM_SLAB), :]
            inv = lax.rsqrt(jnp.mean(x * x, axis=-1, keepdims=True) + EPS)
            h_scr[pl.ds(r0, NORM_SLAB), :] = (x * inv * gain + shift).astype(BF16)
            return carry

        lax.fori_loop(0, tm // NORM_SLAB, slab, 0, unroll=2)

    r = _dot(h_scr[...], w_ref[0])

    @pl.when(j == 0)
    def _():
        otm_ref[...] = r

    @pl.when(j > 0)
    def _():
        obm_ref[0] = r


def _in_projection(xs, gain, modv, w_bf16, layer, n_lat, tm):
    bsz, ltot, d = xs.shape
    gw = GROUP_WIDTH
    kern = functools.partial(_inproj_kernel, tm=tm, n_lat=n_lat)
    tn = N_TM_PARTS * gw
    return pl.pallas_call(
        kern,
        out_shape=(jax.ShapeDtypeStruct((ltot, bsz * tn), F32),
                   jax.ShapeDtypeStruct((bsz, ltot, N_BM_PARTS * gw), F32)),
        grid=(bsz, ltot // tm, N_IN_PARTS * gw // tn),
        in_specs=[pl.BlockSpec((1, tm, d), lambda b, i, j: (b, i, 0)),
                  pl.BlockSpec((1, d), lambda b, i, j: (0, 0)),
                  pl.BlockSpec((1, 6, d), lambda b, i, j: (b, 0, 0)),
                  pl.BlockSpec((1, 6, d), lambda b, i, j: (bsz, 0, 0)),
                  pl.BlockSpec((1, d, tn), lambda b, i, j: (layer, 0, j))],
        out_specs=(pl.BlockSpec((tm, tn), lambda b, i, j: (i, b)),
                   pl.BlockSpec((1, tm, tn), lambda b, i, j: (b, i, jnp.maximum(j - 1, 0)))),
        scratch_shapes=[pltpu.VMEM((tm, d), BF16)],
        compiler_params=_params("arbitrary", "arbitrary", "arbitrary"),
        name="in_projection",
    )(xs, gain, modv, modv, w_bf16)


def _s5_kernel(*refs, rev, tq, nb, has_prev):
    if has_prev:
        u_ref, bw_ref, cw_ref, a_ref, prev_ref, y_ref, z_scr, h_scr = refs
    else:
        u_ref, bw_ref, cw_ref, a_ref, y_ref, z_scr, h_scr = refs
        prev_ref = None
    rows = tq * nb
    half = S5_STATE * 8

    @pl.when(pl.program_id(0) == 0)
    def _():
        h_scr[...] = jnp.zeros_like(h_scr)

    for gb in range(S5_LANE_BLOCKS):
        lanes = slice(gb * 128, (gb + 1) * 128)
        ub = u_ref[:, :, lanes].reshape(rows, 128).astype(BF16)
        z_scr[...] = _dot(ub, bw_ref[gb])
        ar = jnp.broadcast_to(a_ref[gb, 0], (nb, half))
        ai = jnp.broadcast_to(a_ref[gb, 1], (nb, half))

        def step(i, carry):
            hr, hi = carry
            t = (tq - 1 - i) if rev else i
            r0 = pl.multiple_of(t * nb, nb)
            zr = z_scr[pl.ds(r0, nb), 0:half]
            zi = z_scr[pl.ds(r0, nb), half:2 * half]
            nr = ar * hr - ai * hi + zr
            ni = ar * hi + ai * hr + zi
            z_scr[pl.ds(r0, nb), 0:half] = nr
            z_scr[pl.ds(r0, nb), half:2 * half] = ni
            return nr, ni

        hr, hi = lax.fori_loop(0, tq, step, (h_scr[gb, 0], h_scr[gb, 1]), unroll=4)
        h_scr[gb, 0] = hr
        h_scr[gb, 1] = hi
        yb = _dot(z_scr[...].astype(BF16), cw_ref[gb]).reshape(tq, nb, 128)
        if has_prev:
            yb = yb + prev_ref[:, :, lanes]
        y_ref[:, :, lanes] = yb


def _s5_direction(tm3, bw, cw, atab, prev, rev, n_lat, tq):
    ltot, nb, _ = tm3.shape
    gw = GROUP_WIDTH
    n_chunks = ltot // tq
    n_lat_chunks = n_lat // tq
    cidx = lambda s: _scan_chunk_index(s, n_lat_chunks, n_chunks, rev)
    has_prev = prev is not None
    in_specs = [pl.BlockSpec((tq, nb, gw), lambda s: (cidx(s), 0, 0)),
                pl.BlockSpec(bw.shape, lambda s: (0, 0, 0)),
                pl.BlockSpec(cw.shape, lambda s: (0, 0, 0)),
                pl.BlockSpec(atab.shape, lambda s: (0, 0, 0, 0))]
    args = [tm3, bw, cw, atab]
    if has_prev:
        in_specs.append(pl.BlockSpec((tq, nb, gw), lambda s: (cidx(s), 0, 0)))
        args.append(prev)
    kern = functools.partial(_s5_kernel, rev=rev, tq=tq, nb=nb, has_prev=has_prev)
    return pl.pallas_call(
        kern,
        out_shape=jax.ShapeDtypeStruct((ltot, nb, gw), F32),
        grid=(n_chunks,),
        in_specs=in_specs,
        out_specs=pl.BlockSpec((tq, nb, gw), lambda s: (cidx(s), 0, 0)),
        scratch_shapes=[pltpu.VMEM((tq * nb, 2 * S5_STATE * 8), F32),
                        pltpu.VMEM((S5_LANE_BLOCKS, 2, nb, S5_STATE * 8), F32)],
        compiler_params=_params("arbitrary"),
        name="s5_rev" if rev else "s5_fwd",
    )(*args)


def _block_diag(blocks):
    n, r, c = blocks.shape
    eye = jnp.eye(n, dtype=blocks.dtype)
    return (eye[:, None, :, None] * blocks[:, :, None, :]).reshape(n * r, n * c)


def _s5_tables(lam_re, lam_im, log_dt, b_re, b_im, c_re, c_im):
    dt = jnp.exp(log_dt.astype(F32))[:, None]
    lr, li = lam_re.astype(F32), lam_im.astype(F32)
    mag = jnp.exp(lr * dt)
    abar_re, abar_im = mag * jnp.cos(li * dt), mag * jnp.sin(li * dt)
    den = lr * lr + li * li
    zr = abar_re - 1.0
    w_re = (zr * lr + abar_im * li) / den
    w_im = (abar_im * lr - zr * li) / den
    bre, bim = b_re.astype(F32), b_im.astype(F32)
    bw_re = w_re[:, None, :] * bre - w_im[:, None, :] * bim
    bw_im = w_re[:, None, :] * bim + w_im[:, None, :] * bre
    gpb = S5_GROUPS // S5_LANE_BLOCKS
    bws, cws, atabs = [], [], []
    for gb in range(S5_LANE_BLOCKS):
        sl = slice(gb * gpb, (gb + 1) * gpb)
        bws.append(jnp.concatenate([_block_diag(bw_re[sl]), _block_diag(bw_im[sl])], axis=1))
        cws.append(jnp.concatenate([_block_diag(c_re[sl].astype(F32)), -_block_diag(c_im[sl].astype(F32))], axis=0))
        atabs.append(jnp.stack([abar_re[sl].reshape(1, -1), abar_im[sl].reshape(1, -1)]))
    return jnp.stack(bws).astype(BF16), jnp.stack(cws).astype(BF16), jnp.stack(atabs)


def _lru_kernel(*refs, rev, tq, nb, n_lat_chunks, n_chunks, has_prev):
    if has_prev:
        x_ref, xp_ref, xn_ref, cw_ref, cb_ref, wa_ref, wx_ref, ba_ref, bx_ref, sp_ref, prev_ref, o_ref, a_scr, b_scr, h_scr = refs
    else:
        x_ref, xp_ref, xn_ref, cw_ref, cb_ref, wa_ref, wx_ref, ba_ref, bx_ref, sp_ref, o_ref, a_scr, b_scr, h_scr = refs
        prev_ref = None
    s = pl.program_id(0)
    gw = GROUP_WIDTH
    rows = tq * nb

    @pl.when(s == 0)
    def _():
        h_scr[...] = jnp.zeros_like(h_scr)

    c = _scan_chunk_index(s, n_lat_chunks, n_chunks, rev)
    first = jnp.logical_or(c == 0, c == n_lat_chunks)
    last = jnp.logical_or(c == n_lat_chunks - 1, c == n_chunks - 1)
    keep_prev = jnp.where(first, 0.0, 1.0)
    keep_next = jnp.where(last, 0.0, 1.0)
    xc = jnp.concatenate([xp_ref[...] * keep_prev, x_ref[...], xn_ref[...] * keep_next], axis=0)
    conv = cb_ref[...] + xc[0:tq] * cw_ref[0:1, :]
    for k in range(1, LRU_CONV):
        conv = conv + xc[k:k + tq] * cw_ref[k:k + 1, :]
    x = conv.reshape(rows, gw)
    xb = x.astype(BF16)
    r = jax.nn.sigmoid(_dot(xb, wa_ref[...]) + ba_ref[...])
    ig = jax.nn.sigmoid(_dot(xb, wx_ref[...]) + bx_ref[...])
    log_a = -LRU_C * r * sp_ref[...]
    a = jnp.exp(log_a)
    a_scr[...] = a
    b_scr[...] = jnp.sqrt(1.0 - a * a) * (ig * x)

    def step(i, h):
        t = (tq - 1 - i) if rev else i
        r0 = pl.multiple_of(t * nb, nb)
        hn = a_scr[pl.ds(r0, nb), :] * h + b_scr[pl.ds(r0, nb), :]
        b_scr[pl.ds(r0, nb), :] = hn
        return hn

    h_scr[...] = lax.fori_loop(0, tq, step, h_scr[...], unroll=4)
    out = b_scr[...].reshape(tq, nb, gw)
    if has_prev:
        out = out + prev_ref[...]
    o_ref[...] = out


def _lru_direction(tm3, conv_w, conv_b, wa_bd, wx_bd, ba, bx, sp, prev, rev, n_lat, tq):
    ltot, nb, _ = tm3.shape
    gw = GROUP_WIDTH
    n_chunks = ltot // tq
    n_lat_chunks = n_lat // tq
    cidx = lambda s: _scan_chunk_index(s, n_lat_chunks, n_chunks, rev)
    has_prev = prev is not None
    full2 = lambda s: (0, 0)
    in_specs = [pl.BlockSpec((tq, nb, gw), lambda s: (cidx(s), 0, 1)),
                pl.BlockSpec((2, nb, gw), lambda s: (jnp.maximum(cidx(s) * (tq // 2) - 1, 0), 0, 1)),
                pl.BlockSpec((1, nb, gw), lambda s: (jnp.minimum((cidx(s) + 1) * tq, ltot - 1), 0, 1)),
                pl.BlockSpec((LRU_CONV, gw), full2),
                pl.BlockSpec((1, gw), full2),
                pl.BlockSpec((gw, gw), full2),
                pl.BlockSpec((gw, gw), full2),
                pl.BlockSpec((1, gw), full2),
                pl.BlockSpec((1, gw), full2),
                pl.BlockSpec((1, gw), full2)]
    args = [tm3, tm3, tm3, conv_w, conv_b, wa_bd, wx_bd, ba, bx, sp]
    if has_prev:
        in_specs.append(pl.BlockSpec((tq, nb, gw), lambda s: (cidx(s), 0, 0)))
        args.append(prev)
    kern = functools.partial(_lru_kernel, rev=rev, tq=tq, nb=nb, n_lat_chunks=n_lat_chunks,
                             n_chunks=n_chunks, has_prev=has_prev)
    return pl.pallas_call(
        kern,
        out_shape=jax.ShapeDtypeStruct((ltot, nb, gw), F32),
        grid=(n_chunks,),
        in_specs=in_specs,
        out_specs=pl.BlockSpec((tq, nb, gw), lambda s: (cidx(s), 0, 0)),
        scratch_shapes=[pltpu.VMEM((tq * nb, gw), F32),
                        pltpu.VMEM((tq * nb, gw), F32),
                        pltpu.VMEM((nb, gw), F32)],
        compiler_params=_params("arbitrary"),
        name="lru_rev" if rev else "lru_fwd",
    )(*args)


def _tri_mask(n, rev):
    t = lax.broadcasted_iota(I32, (n, n), 0)
    s = lax.broadcasted_iota(I32, (n, n), 1)
    return (s >= t) if rev else (s <= t)


def _hgrn_kernel(*refs, rev, tq, has_prev):
    if has_prev:
        q_ref, z_ref, i_ref, lb_ref, tri_ref, ones_ref, prev_ref, o_ref, st_scr, cum_scr, q_scr, k_scr, v_scr = refs
    else:
        q_ref, z_ref, i_ref, lb_ref, tri_ref, ones_ref, o_ref, st_scr, cum_scr, q_scr, k_scr, v_scr = refs
        prev_ref = None
    hps = HGRN_HEADS_PER_STEP
    cs, sb, w = GLA_CHUNK, GLA_SUB, min(HGRN_FAST_WINDOW, tq)
    nsub = cs // sb
    n_ck = tq // cs
    n_w = tq // w

    @pl.when(pl.program_id(2) == 0)
    def _():
        st_scr[...] = jnp.zeros_like(st_scr)

    lb = lb_ref[0]
    z = z_ref[0]
    iv = i_ref[0]
    f = lb + (1.0 - lb) * jax.nn.sigmoid(z)
    k_all = 1.0 - f
    v_all = iv * jax.nn.sigmoid(iv)
    q_all = q_ref[0] * (GLA_DIM ** -0.5)
    logf = jnp.log(f)
    lf_hi = logf.astype(BF16)
    rest = logf - lf_hi.astype(F32)
    lf_mid = rest.astype(BF16)
    lf_lo = (rest - lf_mid.astype(F32)).astype(BF16)
    tri = tri_ref[...]
    cum_all = _dot(tri, lf_hi) + _dot(tri, lf_mid) + _dot(tri, lf_lo)
    for hh in range(hps):
        lanes = slice(hh * GLA_DIM, (hh + 1) * GLA_DIM)
        cum_scr[hh] = cum_all[:, lanes]
        q_scr[hh] = q_all[:, lanes]
        k_scr[hh] = k_all[:, lanes]
        v_scr[hh] = v_all[:, lanes]

    def far_row(c, width):
        return c * width if rev else (c + 1) * width - 1

    def outside_row(c, width, n):
        if rev:
            return (c + 1) * width if c < n - 1 else None
        return c * width - 1 if c > 0 else None

    def store_out(hh, rows, out):
        lanes = slice(hh * GLA_DIM, (hh + 1) * GLA_DIM)
        if has_prev:
            out = out + prev_ref[0, rows, lanes]
        o_ref[0, rows, lanes] = out

    def block_factored(_):
        lane_id = lax.broadcasted_iota(I32, (1, tq), 1)
        for hh in range(hps):
            cumg, q, k = cum_scr[hh], q_scr[hh], k_scr[hh]
            vb = v_scr[hh].astype(BF16)
            st = st_scr[hh]
            ends = [cum_scr[hh, far_row(c, w):far_row(c, w) + 1, :] for c in range(n_w)]
            end_full = jnp.concatenate([jnp.broadcast_to(e, (w, GLA_DIM)) for e in ends], axis=0)
            ks = (k * jnp.exp(end_full - cumg)).astype(BF16)
            pieces, starts, row_lo = [], [], []
            total = 0
            for c in range(n_w):
                rows = slice(0, (c + 1) * w) if rev else slice(c * w, tq)
                scale = jnp.exp(jnp.minimum(cumg[rows] - ends[c], HGRN_SAFE_LOG_DECAY))
                pieces.append((q[rows] * scale).astype(BF16))
                starts.append(total)
                row_lo.append(rows.start)
                total += rows.stop - rows.start
            a_stack = _dot_nt(jnp.concatenate(pieces, axis=0), ks)
            a_rows = []
            for r in range(n_w):
                acc = jnp.zeros((w, tq), F32)
                for c in (range(r, n_w) if rev else range(0, r + 1)):
                    lo = starts[c] + r * w - row_lo[c]
                    keep = jnp.logical_and(lane_id >= c * w, lane_id < (c + 1) * w)
                    if c == r:
                        t_id = r * w + lax.broadcasted_iota(I32, (w, tq), 0)
                        keep = jnp.logical_and(keep, (lane_id >= t_id) if rev else (lane_id <= t_id))
                    acc = jnp.where(keep, a_stack[lo:lo + w, :], acc)
                a_rows.append(acc)
            a = jnp.concatenate(a_rows, axis=0).astype(BF16)
            out = _dot(a, vb) + _dot_nt((q * jnp.exp(cumg)).astype(BF16), st.astype(BF16))
            store_out(hh, slice(None), out)
            last = ends[0] if rev else ends[n_w - 1]
            kd = k * jnp.exp(last - cumg)
            st_scr[hh] = st * jnp.exp(last) + _dot_tn(vb, kd.astype(BF16))
        return 0

    row_id = lax.broadcasted_iota(I32, (sb, GLA_DIM), 0)
    valid = [(row_id <= s) if rev else (row_id >= s) for s in range(sb)]

    def chunk_direct(hh, r0):
        rows = pl.ds(r0, cs)
        q, k, cumg = q_scr[hh, rows, :], k_scr[hh, rows, :], cum_scr[hh, rows, :]
        vb = v_scr[hh, rows, :].astype(BF16)
        if rev:
            base = cum_scr[hh, pl.ds(jnp.minimum(r0 + cs, tq - 1), 1), :]
            base = jnp.where(r0 + cs >= tq, 0.0, base)
        else:
            base = cum_scr[hh, pl.ds(jnp.maximum(r0 - 1, 0), 1), :]
            base = jnp.where(r0 == 0, 0.0, base)
        cum = cumg - base
        last = cum[0:1] if rev else cum[cs - 1:cs]
        st = st_scr[hh]
        o_inter = _dot_nt((q * jnp.exp(cum)).astype(BF16), st.astype(BF16))
        outs = []
        for i in range(nsub):
            blk = slice(i * sb, (i + 1) * sb)
            off = slice((i + 1) * sb, cs) if rev else slice(0, i * sb)
            o_row = outside_row(i, sb, nsub)
            q_i, cum_i = q[blk], cum[blk]
            o_i = o_inter[blk]
            if o_row is not None:
                ref_row = cum[o_row:o_row + 1]
                qt = q_i * jnp.exp(cum_i - ref_row)
                kt = k[off] * jnp.exp(ref_row - cum[off])
                a_off = _dot_nt(qt.astype(BF16), kt.astype(BF16))
                o_i = o_i + _dot(a_off.astype(BF16), vb[off])
            prods = []
            for s in range(sb):
                r = r0 + i * sb + s
                e = jnp.exp(jnp.minimum(cumg[blk] - cum_scr[hh, pl.ds(r, 1), :], 0.0))
                prods.append(jnp.where(valid[s], q_i * e * k_scr[hh, pl.ds(r, 1), :], 0.0))
            sums = _dot(jnp.concatenate(prods, axis=0).astype(BF16), ones_ref[...])
            for s in range(sb):
                r = r0 + i * sb + s
                o_i = o_i + sums[s * sb:(s + 1) * sb] * v_scr[hh, pl.ds(r, 1), :]
            outs.append(o_i)
        store_out(hh, rows, jnp.concatenate(outs, axis=0))
        kd = k * jnp.exp(last - cum)
        st_scr[hh] = st * jnp.exp(last) + _dot_tn(vb, kd.astype(BF16))

    def block_direct(_):
        def chunk(ci, carry):
            c = (n_ck - 1 - ci) if rev else ci
            r0 = pl.multiple_of(c * cs, cs)
            for hh in range(hps):
                chunk_direct(hh, r0)
            return carry

        return lax.fori_loop(0, n_ck, chunk, 0)

    decays = []
    for c in range(n_w):
        far = cum_all[far_row(c, w):far_row(c, w) + 1]
        o_row = outside_row(c, w, n_w)
        decays.append(far if o_row is None else far - cum_all[o_row:o_row + 1])
    safe = jnp.logical_and(jnp.min(jnp.concatenate(decays, axis=0)) > -HGRN_SAFE_LOG_DECAY,
                           jnp.max(jnp.abs(q_all)) < HGRN_SAFE_QUERY)
    lax.cond(safe, block_factored, block_direct, 0)


def _gla_specs(ltot, tq, n_lat, rev):
    n_chunks = ltot // tq
    n_lat_chunks = n_lat // tq
    cidx = lambda s: _scan_chunk_index(s, n_lat_chunks, n_chunks, rev)
    part = lambda p: pl.BlockSpec((1, tq, GLA_DIM), lambda b, h, s: (b, cidx(s), p * GLA_HEADS + h))
    return n_chunks, cidx, part


def _hgrn_direction(bm, lb, prev, rev, n_lat, tq):
    bsz, ltot, _ = bm.shape
    n_chunks, cidx, _ = _gla_specs(ltot, tq, n_lat, rev)
    has_prev = prev is not None
    hps = HGRN_HEADS_PER_STEP
    width = hps * GLA_DIM
    n_hsteps = GLA_HEADS // hps
    part = lambda p: pl.BlockSpec((1, tq, width), lambda b, h, s: (b, cidx(s), p * n_hsteps + h))
    tri = _tri_mask(tq, rev).astype(BF16)
    ones = jnp.ones((GLA_DIM, GLA_DIM), BF16)
    z_part = 2 if rev else 1
    in_specs = [part(0), part(z_part), part(3),
                pl.BlockSpec((1, 1, width), lambda b, h, s: (h, 0, 0)),
                pl.BlockSpec(tri.shape, lambda b, h, s: (0, 0)),
                pl.BlockSpec(ones.shape, lambda b, h, s: (0, 0))]
    args = [bm, bm, bm, lb.reshape(n_hsteps, 1, width), tri, ones]
    out_spec = pl.BlockSpec((1, tq, width), lambda b, h, s: (b, cidx(s), h))
    if has_prev:
        in_specs.append(out_spec)
        args.append(prev)
    kern = functools.partial(_hgrn_kernel, rev=rev, tq=tq, has_prev=has_prev)
    tile = pltpu.VMEM((hps, tq, GLA_DIM), F32)
    return pl.pallas_call(
        kern,
        out_shape=jax.ShapeDtypeStruct((bsz, ltot, GROUP_WIDTH), F32),
        grid=(bsz, n_hsteps, n_chunks),
        in_specs=in_specs,
        out_specs=out_spec,
        scratch_shapes=[pltpu.VMEM((hps, GLA_DIM, GLA_DIM), F32), tile, tile, tile, tile],
        compiler_params=_params("arbitrary", "arbitrary", "arbitrary"),
        name="hgrn_rev" if rev else "hgrn_fwd",
    )(*args)


def _ret_kernel(*refs, rev, tq, has_prev):
    if has_prev:
        q_ref, k_ref, v_ref, cos_ref, sin_ref, eq_ref, ek_ref, dm_ref, el_ref, prev_ref, o_ref, st_scr = refs
    else:
        q_ref, k_ref, v_ref, cos_ref, sin_ref, eq_ref, ek_ref, dm_ref, el_ref, o_ref, st_scr = refs
        prev_ref = None
    cs = min(tq, RET_CHUNK)
    n_ck = tq // cs

    @pl.when(pl.program_id(1) == 0)
    def _():
        st_scr[...] = jnp.zeros_like(st_scr)

    def chunk(ci, carry):
        c = (n_ck - 1 - ci) if rev else ci
        r0 = pl.multiple_of(c * cs, cs)
        rows = pl.ds(r0, cs)
        cos = cos_ref[rows, :]
        sin = sin_ref[rows, :]
        for h in range(GLA_HEADS):
            lanes = slice(h * GLA_DIM, (h + 1) * GLA_DIM)
            q = q_ref[0, rows, lanes]
            k = k_ref[0, rows, lanes]
            vb = v_ref[0, rows, lanes].astype(BF16)
            q = q * cos + pltpu.roll(q, GLA_DIM // 2, 1) * sin
            k = (k * cos + pltpu.roll(k, GLA_DIM // 2, 1) * sin) * (GLA_DIM ** -0.5)
            st = st_scr[h]
            scores = _dot_nt(q.astype(BF16), k.astype(BF16)) * dm_ref[h]
            out = _dot_nt((q * eq_ref[h]).astype(BF16), st.astype(BF16)) + _dot(scores.astype(BF16), vb)
            if has_prev:
                out = out + prev_ref[0, rows, lanes]
            o_ref[0, rows, lanes] = out
            st_scr[h] = st * el_ref[h] + _dot_tn(vb, (k * ek_ref[h]).astype(BF16))
        return carry

    lax.fori_loop(0, n_ck, chunk, 0)


def _ret_tables(rev, cs):
    gamma = jnp.log1p(-jnp.exp2(-(RET_DECAY_EXP[1 if rev else 0] + jnp.arange(GLA_HEADS, dtype=F32))))
    t = jnp.arange(cs, dtype=F32)
    steps = (cs - t) if rev else (t + 1.0)
    cum = gamma[:, None] * steps[None, :]
    last = gamma * cs
    eq = jnp.broadcast_to(jnp.exp(cum)[:, :, None], (GLA_HEADS, cs, GLA_DIM))
    ek = jnp.broadcast_to(jnp.exp(last[:, None] - cum)[:, :, None], (GLA_HEADS, cs, GLA_DIM))
    rel = cum[:, :, None] - cum[:, None, :]
    dm = jnp.where(_tri_mask(cs, rev)[None], jnp.exp(jnp.minimum(rel, 0.0)), 0.0)
    el = jnp.broadcast_to(jnp.exp(last)[:, None, None], (GLA_HEADS, 1, GLA_DIM))
    return eq, ek, dm, el


def _rotary_tables(n_lat, n_ctx):
    rows = n_lat // GRID_W
    row = jnp.repeat(jnp.arange(rows, dtype=F32), GRID_W)
    col = jnp.tile(jnp.arange(GRID_W, dtype=F32), rows)
    quarter = GLA_DIM // 4
    inv_freq = ROPE_BASE ** (-jnp.arange(quarter, dtype=F32) / quarter)
    ang = jnp.concatenate([row[:, None] * inv_freq, col[:, None] * inv_freq], axis=-1)
    cos, sin = jnp.cos(ang), jnp.sin(ang)
    cos_l = jnp.concatenate([cos, cos], axis=-1)
    sin_l = jnp.concatenate([-sin, sin], axis=-1)
    cos_t = jnp.concatenate([cos_l, jnp.ones((n_ctx, GLA_DIM), F32)], axis=0)
    sin_t = jnp.concatenate([sin_l, jnp.zeros((n_ctx, GLA_DIM), F32)], axis=0)
    return cos_t, sin_t


def _ret_direction(bm, cos_t, sin_t, prev, rev, n_lat, tq):
    bsz, ltot, _ = bm.shape
    n_chunks, cidx, _ = _gla_specs(ltot, tq, n_lat, rev)
    has_prev = prev is not None
    eq, ek, dm, el = _ret_tables(rev, min(tq, RET_CHUNK))
    gw = GROUP_WIDTH
    part = lambda p: pl.BlockSpec((1, tq, gw), lambda b, s: (b, cidx(s), p))
    whole = lambda a: pl.BlockSpec(a.shape, lambda b, s: (0, 0, 0))
    pos = pl.BlockSpec((tq, GLA_DIM), lambda b, s: (cidx(s), 0))
    in_specs = [part(5), part(6), part(7), pos, pos, whole(eq), whole(ek), whole(dm), whole(el)]
    args = [bm, bm, bm, cos_t, sin_t, eq, ek, dm, el]
    out_spec = pl.BlockSpec((1, tq, gw), lambda b, s: (b, cidx(s), 0))
    if has_prev:
        in_specs.append(out_spec)
        args.append(prev)
    kern = functools.partial(_ret_kernel, rev=rev, tq=tq, has_prev=has_prev)
    return pl.pallas_call(
        kern,
        out_shape=jax.ShapeDtypeStruct((bsz, ltot, gw), F32),
        grid=(bsz, n_chunks),
        in_specs=in_specs,
        out_specs=out_spec,
        scratch_shapes=[pltpu.VMEM((GLA_HEADS, GLA_DIM, GLA_DIM), F32)],
        compiler_params=_params("arbitrary", "arbitrary"),
        name="ret_rev" if rev else "ret_fwd",
    )(*args)


def _head_rms(o):
    parts = []
    for h in range(GLA_HEADS):
        oh = o[:, h * GLA_DIM:(h + 1) * GLA_DIM]
        parts.append(oh * lax.rsqrt(jnp.mean(oh * oh, axis=-1, keepdims=True) + EPS))
    return jnp.concatenate(parts, axis=-1)


def _mix_kernel(u_ref, lg_ref, ys_ref, hs_ref, hg_ref, rg_ref, oh_ref, or_ref, d_ref, gw_ref, gb_ref, o_ref):
    gw = GROUP_WIDTH
    y = jax.nn.gelu(d_ref[...] * u_ref[...] + ys_ref[...])
    a = y * jax.nn.sigmoid(_dot(y.astype(BF16), gw_ref[...]) + gb_ref[...])
    o_ref[0, :, 0:gw] = a.astype(BF16)
    o_ref[0, :, gw:2 * gw] = (hs_ref[...] * jax.nn.gelu(lg_ref[...])).astype(BF16)
    g = hg_ref[0]
    o_ref[0, :, 2 * gw:3 * gw] = (_head_rms(oh_ref[0]) * (g * jax.nn.sigmoid(g))).astype(BF16)
    g = rg_ref[0]
    o_ref[0, :, 3 * gw:4 * gw] = (_head_rms(or_ref[0]) * (g * jax.nn.sigmoid(g))).astype(BF16)


def _mix_epilogue(tm2, ys2, hs2, bm, o_h, o_r, s5_d, glu_w, glu_b, n_rows, tm):
    bsz, ltot, _ = bm.shape
    gw = GROUP_WIDTH
    tmaj = lambda p, k: pl.BlockSpec((tm, gw), lambda b, i: (i, b * k + p))
    bmaj = lambda p: pl.BlockSpec((1, tm, gw), lambda b, i: (b, i, p))
    full2 = lambda b, i: (0, 0)
    return pl.pallas_call(
        _mix_kernel,
        out_shape=jax.ShapeDtypeStruct((bsz, ltot, 4 * gw), BF16),
        grid=(bsz, n_rows // tm),
        in_specs=[tmaj(0, N_TM_PARTS), tmaj(2, N_TM_PARTS), tmaj(0, 1), tmaj(0, 1),
                  bmaj(4), bmaj(8), bmaj(0), bmaj(0),
                  pl.BlockSpec((1, gw), full2), pl.BlockSpec((gw, gw), full2), pl.BlockSpec((1, gw), full2)],
        out_specs=pl.BlockSpec((1, tm, 4 * gw), lambda b, i: (b, i, 0)),
        compiler_params=_params("arbitrary", "arbitrary"),
        name="mix_epilogue",
    )(tm2, tm2, ys2, hs2, bm, bm, o_h, o_r, s5_d, glu_w, glu_b)


def _gate_rows(modb_ref, modc_ref, idx, row0, tm, n_lat):
    row = row0 + lax.broadcasted_iota(I32, (tm, 1), 0)
    return jnp.where(row >= n_lat, modc_ref[0, idx:idx + 1, :], modb_ref[0, idx:idx + 1, :])


def _outproj_kernel(a_ref, w_ref, x_ref, modb_ref, modc_ref, o_ref, *, tm, n_lat):
    gate = _gate_rows(modb_ref, modc_ref, 2, pl.program_id(1) * tm, tm, n_lat)
    o_ref[0] = x_ref[0] + gate * _dot(a_ref[0], w_ref[0])


def _out_projection(mix, w_bf16, layer, xs, modv, n_rows, n_lat, tm, tn):
    bsz, ltot, d = xs.shape
    k = mix.shape[-1]
    kern = functools.partial(_outproj_kernel, tm=tm, n_lat=n_lat)
    return pl.pallas_call(
        kern,
        out_shape=jax.ShapeDtypeStruct((bsz, ltot, d), F32),
        grid=(bsz, n_rows // tm, d // tn),
        in_specs=[pl.BlockSpec((1, tm, k), lambda b, i, j: (b, i, 0)),
                  pl.BlockSpec((1, k, tn), lambda b, i, j: (layer, 0, j)),
                  pl.BlockSpec((1, tm, tn), lambda b, i, j: (b, i, j)),
                  pl.BlockSpec((1, 6, tn), lambda b, i, j: (b, 0, j)),
                  pl.BlockSpec((1, 6, tn), lambda b, i, j: (bsz, 0, j))],
        out_specs=pl.BlockSpec((1, tm, tn), lambda b, i, j: (b, i, j)),
        compiler_params=_params("arbitrary", "arbitrary", "arbitrary"),
        name="out_projection",
    )(mix, w_bf16, xs, modv, modv)


def _first_max4(v):
    m1 = jnp.maximum(jnp.maximum(v[0], v[1]), jnp.maximum(v[2], v[3]))
    i1 = jnp.where(v[0] == m1, 0, jnp.where(v[1] == m1, 1, jnp.where(v[2] == m1, 2, 3)))
    rest = [jnp.where(i1 == j, -jnp.inf, v[j]) for j in range(4)]
    m2 = jnp.maximum(jnp.maximum(rest[0], rest[1]), jnp.maximum(rest[2], rest[3]))
    i2 = jnp.where(rest[0] == m2, 0, jnp.where(rest[1] == m2, 1, jnp.where(rest[2] == m2, 2, 3)))
    return m1, i1, m2, i2


def _router_kernel(x_ref, g_ref, modb_ref, modc_ref, rw_ref, rb_ref, tri_ref,
                   h_ref, ids_ref, gates_ref, ranks_ref, counts_ref, base_scr, *, tm, n_lat):
    first = jnp.logical_and(pl.program_id(0) == 0, pl.program_id(1) == 0)

    @pl.when(first)
    def _():
        base_scr[...] = jnp.zeros_like(base_scr)

    h = _norm_modulate(x_ref[0], g_ref[...], modb_ref, modc_ref, pl.program_id(1) * tm, n_lat, 3, 4)
    h_ref[0] = h.astype(BF16)
    logits = lax.dot_general(rw_ref[...], h, (((1,), (1,)), ((), ())), preferred_element_type=F32,
                             precision=lax.Precision.HIGHEST)
    scores = jax.nn.sigmoid(logits)
    biased = scores + rb_ref[...]
    tops = []
    for gidx in range(N_EXPERT_GROUPS):
        rows = [biased[gidx * EXPERTS_PER_GROUP + j:gidx * EXPERTS_PER_GROUP + j + 1, :]
                for j in range(EXPERTS_PER_GROUP)]
        tops.append(_first_max4(rows))
    gs = [t[0] + t[2] for t in tops]
    gmax = jnp.maximum(jnp.maximum(gs[0], gs[1]), jnp.maximum(gs[2], gs[3]))
    best = jnp.where(gs[0] == gmax, 0, jnp.where(gs[1] == gmax, 1, jnp.where(gs[2] == gmax, 2, 3)))
    e1 = jnp.zeros_like(best)
    e2 = jnp.zeros_like(best)
    for gidx in range(N_EXPERT_GROUPS):
        e1 = jnp.where(best == gidx, gidx * EXPERTS_PER_GROUP + tops[gidx][1], e1)
        e2 = jnp.where(best == gidx, gidx * EXPERTS_PER_GROUP + tops[gidx][3], e2)
    eid = lax.broadcasted_iota(I32, (N_EXPERTS, tm), 0)
    sel1 = eid == e1
    sel2 = eid == e2
    w1 = jnp.sum(jnp.where(sel1, scores, 0.0), axis=0, keepdims=True)
    w2 = jnp.sum(jnp.where(sel2, scores, 0.0), axis=0, keepdims=True)
    wsum = w1 + w2
    onehot = jnp.where(jnp.logical_or(sel1, sel2), 1.0, 0.0)
    pos = base_scr[...] + _dot(onehot.astype(BF16), tri_ref[...])
    r1 = jnp.sum(jnp.where(sel1, pos, 0.0), axis=0, keepdims=True)
    r2 = jnp.sum(jnp.where(sel2, pos, 0.0), axis=0, keepdims=True)
    ids_ref[0] = jnp.concatenate([e1, e2], axis=0)
    gates_ref[0] = jnp.concatenate([w1 / wsum, w2 / wsum], axis=0)
    ranks_ref[0] = jnp.concatenate([r1, r2], axis=0).astype(I32)
    base = base_scr[...] + jnp.sum(onehot, axis=1, keepdims=True)
    base_scr[...] = base
    counts_ref[...] = jnp.broadcast_to(base, counts_ref.shape)


def _router(xs, gain, modv, rw_t, rbias, n_rows, n_lat, tm):
    bsz, ltot, d = xs.shape
    tri = (lax.broadcasted_iota(I32, (tm, tm), 0) < lax.broadcasted_iota(I32, (tm, tm), 1)).astype(BF16)
    kern = functools.partial(_router_kernel, tm=tm, n_lat=n_lat)
    small = lambda dt: jax.ShapeDtypeStruct((bsz, 2, n_rows), dt)
    small_spec = pl.BlockSpec((1, 2, tm), lambda b, i: (b, 0, i))
    return pl.pallas_call(
        kern,
        out_shape=(jax.ShapeDtypeStruct((bsz, ltot, d), BF16), small(I32), small(F32), small(I32),
                   jax.ShapeDtypeStruct((N_EXPERTS, 128), F32)),
        grid=(bsz, n_rows // tm),
        in_specs=[pl.BlockSpec((1, tm, d), lambda b, i: (b, i, 0)),
                  pl.BlockSpec((1, d), lambda b, i: (0, 0)),
                  pl.BlockSpec((1, 6, d), lambda b, i: (b, 0, 0)),
                  pl.BlockSpec((1, 6, d), lambda b, i: (bsz, 0, 0)),
                  pl.BlockSpec((N_EXPERTS, d), lambda b, i: (0, 0)),
                  pl.BlockSpec((N_EXPERTS, 1), lambda b, i: (0, 0)),
                  pl.BlockSpec((tm, tm), lambda b, i: (0, 0))],
        out_specs=(pl.BlockSpec((1, tm, d), lambda b, i: (b, i, 0)), small_spec, small_spec, small_spec,
                   pl.BlockSpec((N_EXPERTS, 128), lambda b, i: (0, 0))),
        scratch_shapes=[pltpu.VMEM((N_EXPERTS, 1), F32)],
        compiler_params=_params("arbitrary", "arbitrary"),
        name="moe_router",
    )(xs, gain, modv, modv, rw_t, rbias, tri)


def _expert_ffn_kernel(te_ref, nt_ref, x_ref, wg_ref, wu_ref, wd_ref, o_ref):
    live = pl.program_id(0) < nt_ref[0]

    @pl.when(live)
    def _():
        x = x_ref[...]
        g = _dot(x, wg_ref[0])
        u = _dot(x, wu_ref[0])
        he = (g * jax.nn.sigmoid(g)) * u
        o_ref[...] = _dot(he.astype(BF16), wd_ref[0]).astype(o_ref.dtype)

    @pl.when(jnp.logical_not(live))
    def _():
        o_ref[...] = jnp.zeros_like(o_ref)


def _expert_ffn(tile_expert, n_tiles, x_sorted, wg, wu, wd, tm):
    p, d = x_sorted.shape
    f = wg.shape[-1]
    tile = lambda i, te, nt: i
    expert = lambda i, te, nt: te[jnp.minimum(i, nt[0] - 1)]
    grid_spec = pltpu.PrefetchScalarGridSpec(
        num_scalar_prefetch=2,
        grid=(p // tm,),
        in_specs=[pl.BlockSpec((tm, d), lambda i, te, nt: (tile(i, te, nt), 0)),
                  pl.BlockSpec((1, d, f), lambda i, te, nt: (expert(i, te, nt), 0, 0)),
                  pl.BlockSpec((1, d, f), lambda i, te, nt: (expert(i, te, nt), 0, 0)),
                  pl.BlockSpec((1, f, d), lambda i, te, nt: (expert(i, te, nt), 0, 0))],
        out_specs=pl.BlockSpec((tm, d), lambda i, te, nt: (tile(i, te, nt), 0)),
    )
    return pl.pallas_call(
        _expert_ffn_kernel,
        out_shape=jax.ShapeDtypeStruct((p, d), BF16),
        grid_spec=grid_spec,
        compiler_params=_params("arbitrary"),
        name="moe_expert_ffn",
    )(tile_expert, n_tiles, x_sorted, wg, wu, wd)


def _moe_residual_kernel(x_ref, y0_ref, y1_ref, gates_ref, modb_ref, modc_ref, g_ref, o_ref,
                         *, tm, n_lat, final_norm):
    gates = gates_ref[0]
    y = gates[:, 0:1] * y0_ref[0, 0].astype(F32) + gates[:, 1:2] * y1_ref[0, 0].astype(F32)
    gate = _gate_rows(modb_ref, modc_ref, 5, pl.program_id(1) * tm, tm, n_lat)
    x = x_ref[0] + gate * y
    if final_norm:
        x = x * lax.rsqrt(jnp.mean(x * x, axis=-1, keepdims=True) + EPS) * g_ref[...]
    o_ref[0] = x


def _moe_residual(xs, y_pair, gates_t, modv, final_g, n_rows, n_lat, tm, final_norm):
    bsz, ltot, d = xs.shape
    kern = functools.partial(_moe_residual_kernel, tm=tm, n_lat=n_lat, final_norm=final_norm)
    rows_out = n_rows if final_norm else ltot
    return pl.pallas_call(
        kern,
        out_shape=jax.ShapeDtypeStruct((bsz, rows_out, d), F32),
        grid=(bsz, n_rows // tm),
        in_specs=[pl.BlockSpec((1, tm, d), lambda b, i: (b, i, 0)),
                  pl.BlockSpec((1, 1, tm, d), lambda b, i: (0, b, i, 0)),
                  pl.BlockSpec((1, 1, tm, d), lambda b, i: (1, b, i, 0)),
                  pl.BlockSpec((1, tm, 2), lambda b, i: (b, i, 0)),
                  pl.BlockSpec((1, 6, d), lambda b, i: (b, 0, 0)),
                  pl.BlockSpec((1, 6, d), lambda b, i: (bsz, 0, 0)),
                  pl.BlockSpec((1, d), lambda b, i: (0, 0))],
        out_specs=pl.BlockSpec((1, tm, d), lambda b, i: (b, i, 0)),
        compiler_params=_params("arbitrary", "arbitrary"),
        name="moe_residual",
    )(xs, y_pair, y_pair, gates_t, modv, modv, final_g)


def _moe_layer(xs, gain, modv, rw_t, rbias, wg, wu, wd, expert_base, final_g, n_rows, n_lat, tm_tok, tm_res, tm_exp,
               final_norm):
    bsz, ltot, d = xs.shape
    h, ids, gates, ranks, counts = _router(xs, gain, modv, rw_t, rbias, n_rows, n_lat, tm_tok)
    counts = counts[:, 0].astype(I32)
    padded = ((counts + tm_exp - 1) // tm_exp) * tm_exp
    ends = jnp.cumsum(padded)
    offsets = ends - padded
    onehot = ids[..., None] == jnp.arange(N_EXPERTS, dtype=I32)
    dest = jnp.sum(jnp.where(onehot, offsets, 0), axis=-1) + ranks
    n_slots = bsz * 2 * n_rows
    p_rows = ((n_slots + N_EXPERTS * (tm_exp - 1)) + tm_exp - 1) // tm_exp * tm_exp
    tok = (jnp.arange(bsz, dtype=I32)[:, None, None] * ltot + jnp.arange(n_rows, dtype=I32)[None, None, :])
    tok = jnp.broadcast_to(tok, dest.shape)
    fill_slot = jnp.arange(p_rows, dtype=I32) % n_slots
    filler = (fill_slot // (2 * n_rows)) * ltot + fill_slot % n_rows
    src = filler.at[dest.reshape(-1)].set(tok.reshape(-1), mode="promise_in_bounds", unique_indices=True)
    tile_start = jnp.arange(p_rows // tm_exp, dtype=I32) * tm_exp
    tile_expert = jnp.minimum(jnp.sum(tile_start[:, None] >= ends[None, :], axis=1), N_EXPERTS - 1).astype(I32)
    n_tiles = (ends[-1:] // tm_exp).astype(I32)
    x_sorted = h.reshape(bsz * ltot, d).at[src].get(mode="promise_in_bounds")
    y_sorted = _expert_ffn(tile_expert + expert_base, n_tiles, x_sorted, wg, wu, wd, tm_exp)
    y_pair = y_sorted.at[dest.transpose(1, 0, 2).reshape(-1)].get(mode="promise_in_bounds")
    y_pair = y_pair.reshape(2, bsz, n_rows, d)
    return _moe_residual(xs, y_pair, gates.transpose(0, 2, 1), modv, final_g, n_rows, n_lat, tm_res, final_norm)


def _row_tile(n, target):
    t = min(n, target)
    while n % t or t % 8:
        t -= 8
    return t


def kernel(x, c, ctx, c_ctx, norm_mix_g, norm_ffn_g, w_mod, b_mod, w_in, w_out, s5_lam_re, s5_lam_im, s5_log_dt, s5_b_re, s5_b_im, s5_c_re, s5_c_im, s5_d, s5_glu_w, s5_glu_b, lru_conv_w, lru_conv_b, lru_wa, lru_ba, lru_wx, lru_bx, lru_lam, hgrn_lb_logits, router_w, router_bias, moe_w_gate, moe_w_up, moe_w_down, final_norm_g):
    bsz, n_lat, d = x.shape
    n_ctx = ctx.shape[1]
    ltot = n_lat + n_ctx
    depth = w_in.shape[0]
    gw = GROUP_WIDTH
    assert bsz % 8 == 0 and bsz + 1 <= MOD_ROWS
    assert n_lat % GLA_CHUNK == 0 and n_ctx % GLA_CHUNK == 0 and n_lat % GRID_W == 0 and n_lat % NORM_SLAB == 0

    tq_scan = math.gcd(math.gcd(n_lat, n_ctx), 64)
    tq_gla = math.gcd(math.gcd(n_lat, n_ctx), 256)
    tm_all = _row_tile(ltot, 768)
    tm_lat = _row_tile(n_lat, 1024)
    tm_mix = math.gcd(math.gcd(n_lat, n_ctx), 256)

    xs = jnp.concatenate([x, ctx.astype(x.dtype)], axis=1)
    cv = jnp.zeros((MOD_ROWS, d), F32).at[:bsz].set(c).at[bsz].set(c_ctx)
    mod_all = _mod_vectors(cv, w_mod, b_mod).reshape(depth, MOD_ROWS, 6, d)
    cos_t, sin_t = _rotary_tables(n_lat, n_ctx)
    rw_t = router_w.astype(F32).T
    rbias = router_bias.astype(F32).reshape(N_EXPERTS, 1)
    lb_cum = [jnp.cumsum(jax.nn.softmax(hgrn_lb_logits[dr].astype(F32), axis=0), axis=0) for dr in (0, 1)]
    w_in_b, w_out_b = _cast_bf16(w_in), _cast_bf16(w_out)
    n_exp, _, d_ff = moe_w_gate.shape[1:]
    wg_b = _cast_bf16(moe_w_gate.reshape(depth * n_exp, d, d_ff))
    wu_b = _cast_bf16(moe_w_up.reshape(depth * n_exp, d, d_ff))
    wd_b = _cast_bf16(moe_w_down.reshape(depth * n_exp, d_ff, d))

    for layer in range(depth):
        ctx_out = layer < depth - 1
        n_rows = ltot if ctx_out else n_lat
        tm_tok = tm_all if ctx_out else tm_lat
        modv = mod_all[layer]

        tm2, bm = _in_projection(xs, norm_mix_g[layer].reshape(1, d), modv, w_in_b, layer, n_lat, tm_all)
        tm3 = tm2.reshape(ltot, bsz, N_TM_PARTS * gw)

        ys = hs = o_h = o_r = None
        for dr in (0, 1):
            rev = dr == 1
            bw, cw, atab = _s5_tables(s5_lam_re[layer, dr], s5_lam_im[layer, dr], s5_log_dt[layer, dr],
                                      s5_b_re[layer, dr], s5_b_im[layer, dr], s5_c_re[layer, dr], s5_c_im[layer, dr])
            ys = _s5_direction(tm3, bw, cw, atab, ys, rev, n_lat, tq_scan)
            sp = jax.nn.softplus(-lru_lam[layer, dr].astype(F32)).reshape(1, gw)
            hs = _lru_direction(tm3, lru_conv_w[layer].astype(F32), lru_conv_b[layer].astype(F32).reshape(1, gw),
                                _block_diag(lru_wa[layer, dr]).astype(BF16), _block_diag(lru_wx[layer, dr]).astype(BF16),
                                lru_ba[layer, dr].astype(F32).reshape(1, gw), lru_bx[layer, dr].astype(F32).reshape(1, gw),
                                sp, hs, rev, n_lat, tq_scan)
            lb = lb_cum[dr][layer] - lb_cum[dr][0]
            o_h = _hgrn_direction(bm, lb, o_h, rev, n_lat, tq_gla)
            o_r = _ret_direction(bm, cos_t, sin_t, o_r, rev, n_lat, tq_gla)

        mix = _mix_epilogue(tm2, ys.reshape(ltot, bsz * gw), hs.reshape(ltot, bsz * gw), bm, o_h, o_r,
                            s5_d[layer].astype(F32).reshape(1, gw), s5_glu_w[layer].astype(BF16),
                            s5_glu_b[layer].astype(F32).reshape(1, gw), n_rows, tm_mix)
        xs = _out_projection(mix, w_out_b, layer, xs, modv, n_rows, n_lat, tm_tok, 1024)
        xs = _moe_layer(xs, norm_ffn_g[layer].reshape(1, d), modv, rw_t, rbias, wg_b, wu_b, wd_b, layer * n_exp,
                        final_norm_g.reshape(1, d), n_rows, n_lat, tm_tok, tm_mix, 512, not ctx_out)
    return xs
```

```python
import functools
import math

import jax
import jax.numpy as jnp
from jax import lax
from jax.experimental import pallas as pl
from jax.experimental.pallas import tpu as pltpu

F32 = jnp.float32
BF16 = jnp.bfloat16
I32 = jnp.int32

GROUP_WIDTH = 512
N_IN_PARTS = 12
N_TM_PARTS = 3
N_BM_PARTS = N_IN_PARTS - N_TM_PARTS
S5_CH = 16
S5_GROUPS = 32
S5_STATE = 64
S5_LANE_BLOCKS = 4
LRU_HEADS = 8
LRU_CONV = 4
LRU_C = 8.0
GLA_HEADS = 4
GLA_DIM = 128
GLA_CHUNK = 64
GLA_SUB = 16
HGRN_FAST_WINDOW = 32
HGRN_SAFE_LOG_DECAY = 80.0
HGRN_SAFE_QUERY = 1.0e3
HGRN_HEADS_PER_STEP = 4
RET_CHUNK = 256
RET_DECAY_EXP = (5.0, 5.5)
ROPE_BASE = 10000.0
GRID_W = 64
N_EXPERTS = 16
N_EXPERT_GROUPS = 4
EXPERTS_PER_GROUP = 4
D_FF_EXPERT = 1024
EPS = 1e-6
NORM_SLAB = 16
MOD_ROWS = 24

VMEM_LIMIT_BYTES = 56 * 1024 * 1024


def _params(*semantics):
    return pltpu.CompilerParams(dimension_semantics=semantics, vmem_limit_bytes=VMEM_LIMIT_BYTES)


def _dot(a, b):
    return jnp.dot(a, b, preferred_element_type=F32)


def _dot_nt(a, b):
    return lax.dot_general(a, b, (((1,), (1,)), ((), ())), preferred_element_type=F32)


def _dot_tn(a, b):
    return lax.dot_general(a, b, (((0,), (0,)), ((), ())), preferred_element_type=F32)


def _scan_chunk_index(step, n_lat_chunks, n_chunks, rev):
    if rev:
        return n_chunks - 1 - step
    return (step + n_lat_chunks) % n_chunks


def _cast_kernel(x_ref, o_ref):
    o_ref[...] = x_ref[...].astype(o_ref.dtype)


def _cast_bf16(w):
    n, r, c = w.shape
    tc = min(c, 1024)
    spec = pl.BlockSpec((1, r, tc), lambda i, j: (i, 0, j))
    return pl.pallas_call(
        _cast_kernel,
        out_shape=jax.ShapeDtypeStruct(w.shape, BF16),
        grid=(n, c // tc),
        in_specs=[spec],
        out_specs=spec,
        compiler_params=_params("arbitrary", "arbitrary"),
        name="cast_bf16",
    )(w)


def _mod_kernel(cv_ref, w_ref, b_ref, o_ref):
    cv = cv_ref[...]
    s = cv * jax.nn.sigmoid(cv)
    o_ref[0] = _dot(s.astype(BF16), w_ref[0].astype(BF16)) + b_ref[0]


def _mod_vectors(cv, w_mod, b_mod):
    nl, d, n6 = w_mod.shape
    tn = 1024
    return pl.pallas_call(
        _mod_kernel,
        out_shape=jax.ShapeDtypeStruct((nl, MOD_ROWS, n6), F32),
        grid=(nl, n6 // tn),
        in_specs=[pl.BlockSpec((MOD_ROWS, d), lambda l, j: (0, 0)),
                  pl.BlockSpec((1, d, tn), lambda l, j: (l, 0, j)),
                  pl.BlockSpec((1, 1, tn), lambda l, j: (l, 0, j))],
        out_specs=pl.BlockSpec((1, MOD_ROWS, tn), lambda l, j: (l, 0, j)),
        compiler_params=_params("arbitrary", "arbitrary"),
        name="mod_vectors",
    )(cv, w_mod, b_mod.reshape(nl, 1, n6))


def _norm_modulate(x, g, modb_ref, modc_ref, row0, n_lat, shift_idx, scale_idx):
    tm = x.shape[0]
    ms = jnp.mean(x * x, axis=-1, keepdims=True)
    y = x * lax.rsqrt(ms + EPS) * g
    row = row0 + lax.broadcasted_iota(I32, (tm, 1), 0)
    is_ctx = row >= n_lat
    shift = jnp.where(is_ctx, modc_ref[0, shift_idx:shift_idx + 1, :], modb_ref[0, shift_idx:shift_idx + 1, :])
    scale = jnp.where(is_ctx, modc_ref[0, scale_idx:scale_idx + 1, :], modb_ref[0, scale_idx:scale_idx + 1, :])
    return y * (1.0 + scale) + shift


def _inproj_kernel(x_ref, g_ref, modb_ref, modc_ref, w_ref, otm_ref, obm_ref, h_scr, aff_scr, *, tm, n_lat):
    i = pl.program_id(1)
    j = pl.program_id(2)

    @pl.when(j == 0)
    def _():
        d = x_ref.shape[-1]
        for kind, mod_ref in enumerate((modb_ref, modc_ref)):
            aff_scr[kind, 0] = jnp.broadcast_to(g_ref[...] * (1.0 + mod_ref[0, 1:2, :]), (NORM_SLAB, d))
            aff_scr[kind, 1] = jnp.broadcast_to(mod_ref[0, 0:1, :], (NORM_SLAB, d))

        def slab(s, carry):
            r0 = pl.multiple_of(s * NORM_SLAB, NORM_SLAB)
            kind = (i * tm + r0 >= n_lat).astype(I32)
            x = x_ref[0, pl.ds(r0, NORM_SLAB), :]
            inv = lax.rsqrt(jnp.mean(x * x, axis=-1, keepdims=True) + EPS)
            h_scr[pl.ds(r0, NORM_SLAB), :] = (x * inv * aff_scr[kind, 0] + aff_scr[kind, 1]).astype(BF16)
            return carry

        lax.fori_loop(0, tm // NORM_SLAB, slab, 0, unroll=8)

    r = _dot(h_scr[...], w_ref[0])

    @pl.when(j == 0)
    def _():
        otm_ref[...] = r

    @pl.when(j > 0)
    def _():
        obm_ref[0] = r


def _in_projection(xs, gain, modv, w_bf16, layer, n_lat, tm):
    bsz, ltot, d = xs.shape
    gw = GROUP_WIDTH
    kern = functools.partial(_inproj_kernel, tm=tm, n_lat=n_lat)
    tn = N_TM_PARTS * gw
    return pl.pallas_call(
        kern,
        out_shape=(jax.ShapeDtypeStruct((ltot, bsz * tn), F32),
                   jax.ShapeDtypeStruct((bsz, ltot, N_BM_PARTS * gw), F32)),
        grid=(bsz, ltot // tm, N_IN_PARTS * gw // tn),
        in_specs=[pl.BlockSpec((1, tm, d), lambda b, i, j: (b, i, 0)),
                  pl.BlockSpec((1, d), lambda b, i, j: (0, 0)),
                  pl.BlockSpec((1, 6, d), lambda b, i, j: (b, 0, 0)),
                  pl.BlockSpec((1, 6, d), lambda b, i, j: (bsz, 0, 0)),
                  pl.BlockSpec((1, d, tn), lambda b, i, j: (layer, 0, j))],
        out_specs=(pl.BlockSpec((tm, tn), lambda b, i, j: (i, b)),
                   pl.BlockSpec((1, tm, tn), lambda b, i, j: (b, i, jnp.maximum(j - 1, 0)))),
        scratch_shapes=[pltpu.VMEM((tm, d), BF16), pltpu.VMEM((2, 2, NORM_SLAB, d), F32)],
        compiler_params=_params("arbitrary", "arbitrary", "arbitrary"),
        name="in_projection",
    )(xs, gain, modv, modv, w_bf16)


def _s5_kernel(*refs, rev, tq, nb, has_prev):
    if has_prev:
        u_ref, bw_ref, cw_ref, a_ref, prev_ref, y_ref, z_scr, h_scr = refs
    else:
        u_ref, bw_ref, cw_ref, a_ref, y_ref, z_scr, h_scr = refs
        prev_ref = None
    rows = tq * nb
    half = S5_STATE * 8

    @pl.when(pl.program_id(0) == 0)
    def _():
        h_scr[...] = jnp.zeros_like(h_scr)

    for gb in range(S5_LANE_BLOCKS):
        lanes = slice(gb * 128, (gb + 1) * 128)
        ub = u_ref[:, :, lanes].reshape(rows, 128).astype(BF16)
        z_scr[...] = _dot(ub, bw_ref[gb])
        ar = jnp.broadcast_to(a_ref[gb, 0], (nb, half))
        ai = jnp.broadcast_to(a_ref[gb, 1], (nb, half))

        def step(i, carry):
            hr, hi = carry
            t = (tq - 1 - i) if rev else i
            r0 = pl.multiple_of(t * nb, nb)
            zr = z_scr[pl.ds(r0, nb), 0:half]
            zi = z_scr[pl.ds(r0, nb), half:2 * half]
            nr = ar * hr - ai * hi + zr
            ni = ar * hi + ai * hr + zi
            z_scr[pl.ds(r0, nb), 0:half] = nr
            z_scr[pl.ds(r0, nb), half:2 * half] = ni
            return nr, ni

        hr, hi = lax.fori_loop(0, tq, step, (h_scr[gb, 0], h_scr[gb, 1]), unroll=4)
        h_scr[gb, 0] = hr
        h_scr[gb, 1] = hi
        yb = _dot(z_scr[...].astype(BF16), cw_ref[gb]).reshape(tq, nb, 128)
        if has_prev:
            yb = yb + prev_ref[:, :, lanes]
        y_ref[:, :, lanes] = yb


def _s5_direction(tm3, bw, cw, atab, prev, rev, n_lat, tq):
    ltot, nb, _ = tm3.shape
    gw = GROUP_WIDTH
    n_chunks = ltot // tq
    n_lat_chunks = n_lat // tq
    cidx = lambda s: _scan_chunk_index(s, n_lat_chunks, n_chunks, rev)
    has_prev = prev is not None
    in_specs = [pl.BlockSpec((tq, nb, gw), lambda s: (cidx(s), 0, 0)),
                pl.BlockSpec(bw.shape, lambda s: (0, 0, 0)),
                pl.BlockSpec(cw.shape, lambda s: (0, 0, 0)),
                pl.BlockSpec(atab.shape, lambda s: (0, 0, 0, 0))]
    args = [tm3, bw, cw, atab]
    if has_prev:
        in_specs.append(pl.BlockSpec((tq, nb, gw), lambda s: (cidx(s), 0, 0)))
        args.append(prev)
    kern = functools.partial(_s5_kernel, rev=rev, tq=tq, nb=nb, has_prev=has_prev)
    return pl.pallas_call(
        kern,
        out_shape=jax.ShapeDtypeStruct((ltot, nb, gw), F32),
        grid=(n_chunks,),
        in_specs=in_specs,
        out_specs=pl.BlockSpec((tq, nb, gw), lambda s: (cidx(s), 0, 0)),
        scratch_shapes=[pltpu.VMEM((tq * nb, 2 * S5_STATE * 8), F32),
                        pltpu.VMEM((S5_LANE_BLOCKS, 2, nb, S5_STATE * 8), F32)],
        compiler_params=_params("arbitrary"),
        name="s5_rev" if rev else "s5_fwd",
    )(*args)


def _block_diag(blocks):
    n, r, c = blocks.shape
    eye = jnp.eye(n, dtype=blocks.dtype)
    return (eye[:, None, :, None] * blocks[:, :, None, :]).reshape(n * r, n * c)


def _s5_tables(lam_re, lam_im, log_dt, b_re, b_im, c_re, c_im):
    dt = jnp.exp(log_dt.astype(F32))[:, None]
    lr, li = lam_re.astype(F32), lam_im.astype(F32)
    mag = jnp.exp(lr * dt)
    abar_re, abar_im = mag * jnp.cos(li * dt), mag * jnp.sin(li * dt)
    den = lr * lr + li * li
    zr = abar_re - 1.0
    w_re = (zr * lr + abar_im * li) / den
    w_im = (abar_im * lr - zr * li) / den
    bre, bim = b_re.astype(F32), b_im.astype(F32)
    bw_re = w_re[:, None, :] * bre - w_im[:, None, :] * bim
    bw_im = w_re[:, None, :] * bim + w_im[:, None, :] * bre
    gpb = S5_GROUPS // S5_LANE_BLOCKS
    bws, cws, atabs = [], [], []
    for gb in range(S5_LANE_BLOCKS):
        sl = slice(gb * gpb, (gb + 1) * gpb)
        bws.append(jnp.concatenate([_block_diag(bw_re[sl]), _block_diag(bw_im[sl])], axis=1))
        cws.append(jnp.concatenate([_block_diag(c_re[sl].astype(F32)), -_block_diag(c_im[sl].astype(F32))], axis=0))
        atabs.append(jnp.stack([abar_re[sl].reshape(1, -1), abar_im[sl].reshape(1, -1)]))
    return jnp.stack(bws).astype(BF16), jnp.stack(cws).astype(BF16), jnp.stack(atabs)


def _lru_kernel(*refs, rev, tq, nb, n_lat_chunks, n_chunks, has_prev):
    if has_prev:
        x_ref, xp_ref, xn_ref, cw_ref, cb_ref, wa_ref, wx_ref, ba_ref, bx_ref, sp_ref, prev_ref, o_ref, a_scr, b_scr, h_scr = refs
    else:
        x_ref, xp_ref, xn_ref, cw_ref, cb_ref, wa_ref, wx_ref, ba_ref, bx_ref, sp_ref, o_ref, a_scr, b_scr, h_scr = refs
        prev_ref = None
    s = pl.program_id(0)
    gw = GROUP_WIDTH
    rows = tq * nb

    @pl.when(s == 0)
    def _():
        h_scr[...] = jnp.zeros_like(h_scr)

    c = _scan_chunk_index(s, n_lat_chunks, n_chunks, rev)
    first = jnp.logical_or(c == 0, c == n_lat_chunks)
    last = jnp.logical_or(c == n_lat_chunks - 1, c == n_chunks - 1)
    keep_prev = jnp.where(first, 0.0, 1.0)
    keep_next = jnp.where(last, 0.0, 1.0)
    xc = jnp.concatenate([xp_ref[...] * keep_prev, x_ref[...], xn_ref[...] * keep_next], axis=0)
    conv = cb_ref[...] + xc[0:tq] * cw_ref[0:1, :]
    for k in range(1, LRU_CONV):
        conv = conv + xc[k:k + tq] * cw_ref[k:k + 1, :]
    x = conv.reshape(rows, gw)
    xb = x.astype(BF16)
    r = jax.nn.sigmoid(_dot(xb, wa_ref[...]) + ba_ref[...])
    ig = jax.nn.sigmoid(_dot(xb, wx_ref[...]) + bx_ref[...])
    log_a = -LRU_C * r * sp_ref[...]
    a = jnp.exp(log_a)
    a_scr[...] = a
    b_scr[...] = jnp.sqrt(1.0 - a * a) * (ig * x)

    def step(i, h):
        t = (tq - 1 - i) if rev else i
        r0 = pl.multiple_of(t * nb, nb)
        hn = a_scr[pl.ds(r0, nb), :] * h + b_scr[pl.ds(r0, nb), :]
        b_scr[pl.ds(r0, nb), :] = hn
        return hn

    h_scr[...] = lax.fori_loop(0, tq, step, h_scr[...], unroll=4)
    out = b_scr[...].reshape(tq, nb, gw)
    if has_prev:
        out = out + prev_ref[...]
    o_ref[...] = out


def _lru_direction(tm3, conv_w, conv_b, wa_bd, wx_bd, ba, bx, sp, prev, rev, n_lat, tq):
    ltot, nb, _ = tm3.shape
    gw = GROUP_WIDTH
    n_chunks = ltot // tq
    n_lat_chunks = n_lat // tq
    cidx = lambda s: _scan_chunk_index(s, n_lat_chunks, n_chunks, rev)
    has_prev = prev is not None
    full2 = lambda s: (0, 0)
    in_specs = [pl.BlockSpec((tq, nb, gw), lambda s: (cidx(s), 0, 1)),
                pl.BlockSpec((2, nb, gw), lambda s: (jnp.maximum(cidx(s) * (tq // 2) - 1, 0), 0, 1)),
                pl.BlockSpec((1, nb, gw), lambda s: (jnp.minimum((cidx(s) + 1) * tq, ltot - 1), 0, 1)),
                pl.BlockSpec((LRU_CONV, gw), full2),
                pl.BlockSpec((1, gw), full2),
                pl.BlockSpec((gw, gw), full2),
                pl.BlockSpec((gw, gw), full2),
                pl.BlockSpec((1, gw), full2),
                pl.BlockSpec((1, gw), full2),
                pl.BlockSpec((1, gw), full2)]
    args = [tm3, tm3, tm3, conv_w, conv_b, wa_bd, wx_bd, ba, bx, sp]
    if has_prev:
        in_specs.append(pl.BlockSpec((tq, nb, gw), lambda s: (cidx(s), 0, 0)))
        args.append(prev)
    kern = functools.partial(_lru_kernel, rev=rev, tq=tq, nb=nb, n_lat_chunks=n_lat_chunks,
                             n_chunks=n_chunks, has_prev=has_prev)
    return pl.pallas_call(
        kern,
        out_shape=jax.ShapeDtypeStruct((ltot, nb, gw), F32),
        grid=(n_chunks,),
        in_specs=in_specs,
        out_specs=pl.BlockSpec((tq, nb, gw), lambda s: (cidx(s), 0, 0)),
        scratch_shapes=[pltpu.VMEM((tq * nb, gw), F32),
                        pltpu.VMEM((tq * nb, gw), F32),
                        pltpu.VMEM((nb, gw), F32)],
        compiler_params=_params("arbitrary"),
        name="lru_rev" if rev else "lru_fwd",
    )(*args)


def _tri_mask(n, rev):
    t = lax.broadcasted_iota(I32, (n, n), 0)
    s = lax.broadcasted_iota(I32, (n, n), 1)
    return (s >= t) if rev else (s <= t)


def _hgrn_kernel(*refs, rev, tq, has_prev):
    if has_prev:
        q_ref, z_ref, i_ref, lb_ref, tri_ref, ones_ref, prev_ref, o_ref, st_scr, cum_scr, q_scr, k_scr, v_scr = refs
    else:
        q_ref, z_ref, i_ref, lb_ref, tri_ref, ones_ref, o_ref, st_scr, cum_scr, q_scr, k_scr, v_scr = refs
        prev_ref = None
    hps = HGRN_HEADS_PER_STEP
    cs, sb, w = GLA_CHUNK, GLA_SUB, min(HGRN_FAST_WINDOW, tq)
    nsub = cs // sb
    n_ck = tq // cs
    n_w = tq // w

    @pl.when(pl.program_id(2) == 0)
    def _():
        st_scr[...] = jnp.zeros_like(st_scr)

    lb = lb_ref[0]
    z = z_ref[0]
    iv = i_ref[0]
    f = lb + (1.0 - lb) * jax.nn.sigmoid(z)
    k_all = 1.0 - f
    v_all = iv * jax.nn.sigmoid(iv)
    q_all = q_ref[0] * (GLA_DIM ** -0.5)
    logf = jnp.log(f)
    lf_hi = logf.astype(BF16)
    rest = logf - lf_hi.astype(F32)
    lf_mid = rest.astype(BF16)
    lf_lo = (rest - lf_mid.astype(F32)).astype(BF16)
    tri = tri_ref[...]
    cum_all = _dot(tri, lf_hi) + _dot(tri, lf_mid) + _dot(tri, lf_lo)
    for hh in range(hps):
        lanes = slice(hh * GLA_DIM, (hh + 1) * GLA_DIM)
        cum_scr[hh] = cum_all[:, lanes]
        q_scr[hh] = q_all[:, lanes]
        k_scr[hh] = k_all[:, lanes]
        v_scr[hh] = v_all[:, lanes]

    def far_row(c, width):
        return c * width if rev else (c + 1) * width - 1

    def outside_row(c, width, n):
        if rev:
            return (c + 1) * width if c < n - 1 else None
        return c * width - 1 if c > 0 else None

    def store_out(hh, rows, out):
        lanes = slice(hh * GLA_DIM, (hh + 1) * GLA_DIM)
        if has_prev:
            out = out + prev_ref[0, rows, lanes]
        o_ref[0, rows, lanes] = out

    def block_factored(_):
        lane_id = lax.broadcasted_iota(I32, (1, tq), 1)
        for hh in range(hps):
            cumg, q, k = cum_scr[hh], q_scr[hh], k_scr[hh]
            vb = v_scr[hh].astype(BF16)
            st = st_scr[hh]
            ends = [cum_scr[hh, far_row(c, w):far_row(c, w) + 1, :] for c in range(n_w)]
            end_full = jnp.concatenate([jnp.broadcast_to(e, (w, GLA_DIM)) for e in ends], axis=0)
            ks = (k * jnp.exp(end_full - cumg)).astype(BF16)
            pieces, starts, row_lo = [], [], []
            total = 0
            for c in range(n_w):
                rows = slice(0, (c + 1) * w) if rev else slice(c * w, tq)
                scale = jnp.exp(jnp.minimum(cumg[rows] - ends[c], HGRN_SAFE_LOG_DECAY))
                pieces.append((q[rows] * scale).astype(BF16))
                starts.append(total)
                row_lo.append(rows.start)
                total += rows.stop - rows.start
            a_stack = _dot_nt(jnp.concatenate(pieces, axis=0), ks)
            a_rows = []
            for r in range(n_w):
                acc = jnp.zeros((w, tq), F32)
                for c in (range(r, n_w) if rev else range(0, r + 1)):
                    lo = starts[c] + r * w - row_lo[c]
                    keep = jnp.logical_and(lane_id >= c * w, lane_id < (c + 1) * w)
                    if c == r:
                        t_id = r * w + lax.broadcasted_iota(I32, (w, tq), 0)
                        keep = jnp.logical_and(keep, (lane_id >= t_id) if rev else (lane_id <= t_id))
                    acc = jnp.where(keep, a_stack[lo:lo + w, :], acc)
                a_rows.append(acc)
            a = jnp.concatenate(a_rows, axis=0).astype(BF16)
            out = _dot(a, vb) + _dot_nt((q * jnp.exp(cumg)).astype(BF16), st.astype(BF16))
            store_out(hh, slice(None), out)
            last = ends[0] if rev else ends[n_w - 1]
            kd = k * jnp.exp(last - cumg)
            st_scr[hh] = st * jnp.exp(last) + _dot_tn(vb, kd.astype(BF16))
        return 0

    row_id = lax.broadcasted_iota(I32, (sb, GLA_DIM), 0)
    valid = [(row_id <= s) if rev else (row_id >= s) for s in range(sb)]

    def chunk_direct(hh, r0):
        rows = pl.ds(r0, cs)
        q, k, cumg = q_scr[hh, rows, :], k_scr[hh, rows, :], cum_scr[hh, rows, :]
        vb = v_scr[hh, rows, :].astype(BF16)
        if rev:
            base = cum_scr[hh, pl.ds(jnp.minimum(r0 + cs, tq - 1), 1), :]
            base = jnp.where(r0 + cs >= tq, 0.0, base)
        else:
            base = cum_scr[hh, pl.ds(jnp.maximum(r0 - 1, 0), 1), :]
            base = jnp.where(r0 == 0, 0.0, base)
        cum = cumg - base
        last = cum[0:1] if rev else cum[cs - 1:cs]
        st = st_scr[hh]
        o_inter = _dot_nt((q * jnp.exp(cum)).astype(BF16), st.astype(BF16))
        outs = []
        for i in range(nsub):
            blk = slice(i * sb, (i + 1) * sb)
            off = slice((i + 1) * sb, cs) if rev else slice(0, i * sb)
            o_row = outside_row(i, sb, nsub)
            q_i, cum_i = q[blk], cum[blk]
            o_i = o_inter[blk]
            if o_row is not None:
                ref_row = cum[o_row:o_row + 1]
                qt = q_i * jnp.exp(cum_i - ref_row)
                kt = k[off] * jnp.exp(ref_row - cum[off])
                a_off = _dot_nt(qt.astype(BF16), kt.astype(BF16))
                o_i = o_i + _dot(a_off.astype(BF16), vb[off])
            prods = []
            for s in range(sb):
                r = r0 + i * sb + s
                e = jnp.exp(jnp.minimum(cumg[blk] - cum_scr[hh, pl.ds(r, 1), :], 0.0))
                prods.append(jnp.where(valid[s], q_i * e * k_scr[hh, pl.ds(r, 1), :], 0.0))
            sums = _dot(jnp.concatenate(prods, axis=0).astype(BF16), ones_ref[...])
            for s in range(sb):
                r = r0 + i * sb + s
                o_i = o_i + sums[s * sb:(s + 1) * sb] * v_scr[hh, pl.ds(r, 1), :]
            outs.append(o_i)
        store_out(hh, rows, jnp.concatenate(outs, axis=0))
        kd = k * jnp.exp(last - cum)
        st_scr[hh] = st * jnp.exp(last) + _dot_tn(vb, kd.astype(BF16))

    def block_direct(_):
        def chunk(ci, carry):
            c = (n_ck - 1 - ci) if rev else ci
            r0 = pl.multiple_of(c * cs, cs)
            for hh in range(hps):
                chunk_direct(hh, r0)
            return carry

        return lax.fori_loop(0, n_ck, chunk, 0)

    decays = []
    for c in range(n_w):
        far = cum_all[far_row(c, w):far_row(c, w) + 1]
        o_row = outside_row(c, w, n_w)
        decays.append(far if o_row is None else far - cum_all[o_row:o_row + 1])
    safe = jnp.logical_and(jnp.min(jnp.concatenate(decays, axis=0)) > -HGRN_SAFE_LOG_DECAY,
                           jnp.max(jnp.abs(q_all)) < HGRN_SAFE_QUERY)
    lax.cond(safe, block_factored, block_direct, 0)


def _gla_specs(ltot, tq, n_lat, rev):
    n_chunks = ltot // tq
    n_lat_chunks = n_lat // tq
    cidx = lambda s: _scan_chunk_index(s, n_lat_chunks, n_chunks, rev)
    part = lambda p: pl.BlockSpec((1, tq, GLA_DIM), lambda b, h, s: (b, cidx(s), p * GLA_HEADS + h))
    return n_chunks, cidx, part


def _hgrn_direction(bm, lb, prev, rev, n_lat, tq):
    bsz, ltot, _ = bm.shape
    n_chunks, cidx, _ = _gla_specs(ltot, tq, n_lat, rev)
    has_prev = prev is not None
    hps = HGRN_HEADS_PER_STEP
    width = hps * GLA_DIM
    n_hsteps = GLA_HEADS // hps
    part = lambda p: pl.BlockSpec((1, tq, width), lambda b, h, s: (b, cidx(s), p * n_hsteps + h))
    tri = _tri_mask(tq, rev).astype(BF16)
    ones = jnp.ones((GLA_DIM, GLA_DIM), BF16)
    z_part = 2 if rev else 1
    in_specs = [part(0), part(z_part), part(3),
                pl.BlockSpec((1, 1, width), lambda b, h, s: (h, 0, 0)),
                pl.BlockSpec(tri.shape, lambda b, h, s: (0, 0)),
                pl.BlockSpec(ones.shape, lambda b, h, s: (0, 0))]
    args = [bm, bm, bm, lb.reshape(n_hsteps, 1, width), tri, ones]
    out_spec = pl.BlockSpec((1, tq, width), lambda b, h, s: (b, cidx(s), h))
    if has_prev:
        in_specs.append(out_spec)
        args.append(prev)
    kern = functools.partial(_hgrn_kernel, rev=rev, tq=tq, has_prev=has_prev)
    tile = pltpu.VMEM((hps, tq, GLA_DIM), F32)
    return pl.pallas_call(
        kern,
        out_shape=jax.ShapeDtypeStruct((bsz, ltot, GROUP_WIDTH), F32),
        grid=(bsz, n_hsteps, n_chunks),
        in_specs=in_specs,
        out_specs=out_spec,
        scratch_shapes=[pltpu.VMEM((hps, GLA_DIM, GLA_DIM), F32), tile, tile, tile, tile],
        compiler_params=_params("arbitrary", "arbitrary", "arbitrary"),
        name="hgrn_rev" if rev else "hgrn_fwd",
    )(*args)


def _ret_kernel(*refs, rev, tq, has_prev):
    if has_prev:
        q_ref, k_ref, v_ref, cos_ref, sin_ref, eq_ref, ek_ref, dm_ref, el_ref, prev_ref, o_ref, st_scr = refs
    else:
        q_ref, k_ref, v_ref, cos_ref, sin_ref, eq_ref, ek_ref, dm_ref, el_ref, o_ref, st_scr = refs
        prev_ref = None
    cs = min(tq, RET_CHUNK)
    n_ck = tq // cs

    @pl.when(pl.program_id(1) == 0)
    def _():
        st_scr[...] = jnp.zeros_like(st_scr)

    def chunk(ci, carry):
        c = (n_ck - 1 - ci) if rev else ci
        r0 = pl.multiple_of(c * cs, cs)
        rows = pl.ds(r0, cs)
        cos = cos_ref[rows, :]
        sin = sin_ref[rows, :]
        for h in range(GLA_HEADS):
            lanes = slice(h * GLA_DIM, (h + 1) * GLA_DIM)
            q = q_ref[0, rows, lanes]
            k = k_ref[0, rows, lanes]
            vb = v_ref[0, rows, lanes].astype(BF16)
            q = q * cos + pltpu.roll(q, GLA_DIM // 2, 1) * sin
            k = (k * cos + pltpu.roll(k, GLA_DIM // 2, 1) * sin) * (GLA_DIM ** -0.5)
            st = st_scr[h]
            scores = _dot_nt(q.astype(BF16), k.astype(BF16)) * dm_ref[h]
            out = _dot_nt((q * eq_ref[h]).astype(BF16), st.astype(BF16)) + _dot(scores.astype(BF16), vb)
            if has_prev:
                out = out + prev_ref[0, rows, lanes]
            o_ref[0, rows, lanes] = out
            st_scr[h] = st * el_ref[h] + _dot_tn(vb, (k * ek_ref[h]).astype(BF16))
        return carry

    lax.fori_loop(0, n_ck, chunk, 0)


def _ret_tables(rev, cs):
    gamma = jnp.log1p(-jnp.exp2(-(RET_DECAY_EXP[1 if rev else 0] + jnp.arange(GLA_HEADS, dtype=F32))))
    t = jnp.arange(cs, dtype=F32)
    steps = (cs - t) if rev else (t + 1.0)
    cum = gamma[:, None] * steps[None, :]
    last = gamma * cs
    eq = jnp.broadcast_to(jnp.exp(cum)[:, :, None], (GLA_HEADS, cs, GLA_DIM))
    ek = jnp.broadcast_to(jnp.exp(last[:, None] - cum)[:, :, None], (GLA_HEADS, cs, GLA_DIM))
    rel = cum[:, :, None] - cum[:, None, :]
    dm = jnp.where(_tri_mask(cs, rev)[None], jnp.exp(jnp.minimum(rel, 0.0)), 0.0)
    el = jnp.broadcast_to(jnp.exp(last)[:, None, None], (GLA_HEADS, 1, GLA_DIM))
    return eq, ek, dm, el


def _rotary_tables(n_lat, n_ctx):
    rows = n_lat // GRID_W
    row = jnp.repeat(jnp.arange(rows, dtype=F32), GRID_W)
    col = jnp.tile(jnp.arange(GRID_W, dtype=F32), rows)
    quarter = GLA_DIM // 4
    inv_freq = ROPE_BASE ** (-jnp.arange(quarter, dtype=F32) / quarter)
    ang = jnp.concatenate([row[:, None] * inv_freq, col[:, None] * inv_freq], axis=-1)
    cos, sin = jnp.cos(ang), jnp.sin(ang)
    cos_l = jnp.concatenate([cos, cos], axis=-1)
    sin_l = jnp.concatenate([-sin, sin], axis=-1)
    cos_t = jnp.concatenate([cos_l, jnp.ones((n_ctx, GLA_DIM), F32)], axis=0)
    sin_t = jnp.concatenate([sin_l, jnp.zeros((n_ctx, GLA_DIM), F32)], axis=0)
    return cos_t, sin_t


def _ret_direction(bm, cos_t, sin_t, prev, rev, n_lat, tq):
    bsz, ltot, _ = bm.shape
    n_chunks, cidx, _ = _gla_specs(ltot, tq, n_lat, rev)
    has_prev = prev is not None
    eq, ek, dm, el = _ret_tables(rev, min(tq, RET_CHUNK))
    gw = GROUP_WIDTH
    part = lambda p: pl.BlockSpec((1, tq, gw), lambda b, s: (b, cidx(s), p))
    whole = lambda a: pl.BlockSpec(a.shape, lambda b, s: (0, 0, 0))
    pos = pl.BlockSpec((tq, GLA_DIM), lambda b, s: (cidx(s), 0))
    in_specs = [part(5), part(6), part(7), pos, pos, whole(eq), whole(ek), whole(dm), whole(el)]
    args = [bm, bm, bm, cos_t, sin_t, eq, ek, dm, el]
    out_spec = pl.BlockSpec((1, tq, gw), lambda b, s: (b, cidx(s), 0))
    if has_prev:
        in_specs.append(out_spec)
        args.append(prev)
    kern = functools.partial(_ret_kernel, rev=rev, tq=tq, has_prev=has_prev)
    return pl.pallas_call(
        kern,
        out_shape=jax.ShapeDtypeStruct((bsz, ltot, gw), F32),
        grid=(bsz, n_chunks),
        in_specs=in_specs,
        out_specs=out_spec,
        scratch_shapes=[pltpu.VMEM((GLA_HEADS, GLA_DIM, GLA_DIM), F32)],
        compiler_params=_params("arbitrary", "arbitrary"),
        name="ret_rev" if rev else "ret_fwd",
    )(*args)


def _head_rms(o):
    parts = []
    for h in range(GLA_HEADS):
        oh = o[:, h * GLA_DIM:(h + 1) * GLA_DIM]
        parts.append(oh * lax.rsqrt(jnp.mean(oh * oh, axis=-1, keepdims=True) + EPS))
    return jnp.concatenate(parts, axis=-1)


def _mix_kernel(u_ref, lg_ref, ys_ref, hs_ref, hg_ref, rg_ref, oh_ref, or_ref, d_ref, gw_ref, gb_ref, o_ref):
    gw = GROUP_WIDTH
    y = jax.nn.gelu(d_ref[...] * u_ref[...] + ys_ref[...])
    a = y * jax.nn.sigmoid(_dot(y.astype(BF16), gw_ref[...]) + gb_ref[...])
    o_ref[0, :, 0:gw] = a.astype(BF16)
    o_ref[0, :, gw:2 * gw] = (hs_ref[...] * jax.nn.gelu(lg_ref[...])).astype(BF16)
    g = hg_ref[0]
    o_ref[0, :, 2 * gw:3 * gw] = (_head_rms(oh_ref[0]) * (g * jax.nn.sigmoid(g))).astype(BF16)
    g = rg_ref[0]
    o_ref[0, :, 3 * gw:4 * gw] = (_head_rms(or_ref[0]) * (g * jax.nn.sigmoid(g))).astype(BF16)


def _mix_epilogue(tm2, ys2, hs2, bm, o_h, o_r, s5_d, glu_w, glu_b, n_rows, tm):
    bsz, ltot, _ = bm.shape
    gw = GROUP_WIDTH
    tmaj = lambda p, k: pl.BlockSpec((tm, gw), lambda b, i: (i, b * k + p))
    bmaj = lambda p: pl.BlockSpec((1, tm, gw), lambda b, i: (b, i, p))
    full2 = lambda b, i: (0, 0)
    return pl.pallas_call(
        _mix_kernel,
        out_shape=jax.ShapeDtypeStruct((bsz, ltot, 4 * gw), BF16),
        grid=(bsz, n_rows // tm),
        in_specs=[tmaj(0, N_TM_PARTS), tmaj(2, N_TM_PARTS), tmaj(0, 1), tmaj(0, 1),
                  bmaj(4), bmaj(8), bmaj(0), bmaj(0),
                  pl.BlockSpec((1, gw), full2), pl.BlockSpec((gw, gw), full2), pl.BlockSpec((1, gw), full2)],
        out_specs=pl.BlockSpec((1, tm, 4 * gw), lambda b, i: (b, i, 0)),
        compiler_params=_params("arbitrary", "arbitrary"),
        name="mix_epilogue",
    )(tm2, tm2, ys2, hs2, bm, bm, o_h, o_r, s5_d, glu_w, glu_b)


def _gate_rows(modb_ref, modc_ref, idx, row0, tm, n_lat):
    row = row0 + lax.broadcasted_iota(I32, (tm, 1), 0)
    return jnp.where(row >= n_lat, modc_ref[0, idx:idx + 1, :], modb_ref[0, idx:idx + 1, :])


def _outproj_kernel(a_ref, w_ref, x_ref, modb_ref, modc_ref, o_ref, *, tm, n_lat):
    gate = _gate_rows(modb_ref, modc_ref, 2, pl.program_id(1) * tm, tm, n_lat)
    o_ref[0] = x_ref[0] + gate * _dot(a_ref[0], w_ref[0])


def _out_projection(mix, w_bf16, layer, xs, modv, n_rows, n_lat, tm, tn):
    bsz, ltot, d = xs.shape
    k = mix.shape[-1]
    kern = functools.partial(_outproj_kernel, tm=tm, n_lat=n_lat)
    return pl.pallas_call(
        kern,
        out_shape=jax.ShapeDtypeStruct((bsz, ltot, d), F32),
        grid=(bsz, n_rows // tm, d // tn),
        in_specs=[pl.BlockSpec((1, tm, k), lambda b, i, j: (b, i, 0)),
                  pl.BlockSpec((1, k, tn), lambda b, i, j: (layer, 0, j)),
                  pl.BlockSpec((1, tm, tn), lambda b, i, j: (b, i, j)),
                  pl.BlockSpec((1, 6, tn), lambda b, i, j: (b, 0, j)),
                  pl.BlockSpec((1, 6, tn), lambda b, i, j: (bsz, 0, j))],
        out_specs=pl.BlockSpec((1, tm, tn), lambda b, i, j: (b, i, j)),
        compiler_params=_params("arbitrary", "arbitrary", "arbitrary"),
        name="out_projection",
    )(mix, w_bf16, xs, modv, modv)


def _first_max4(v):
    m1 = jnp.maximum(jnp.maximum(v[0], v[1]), jnp.maximum(v[2], v[3]))
    i1 = jnp.where(v[0] == m1, 0, jnp.where(v[1] == m1, 1, jnp.where(v[2] == m1, 2, 3)))
    rest = [jnp.where(i1 == j, -jnp.inf, v[j]) for j in range(4)]
    m2 = jnp.maximum(jnp.maximum(rest[0], rest[1]), jnp.maximum(rest[2], rest[3]))
    i2 = jnp.where(rest[0] == m2, 0, jnp.where(rest[1] == m2, 1, jnp.where(rest[2] == m2, 2, 3)))
    return m1, i1, m2, i2


def _router_kernel(x_ref, g_ref, modb_ref, modc_ref, rw_ref, rb_ref, tri_ref,
                   h_ref, ids_ref, gates_ref, ranks_ref, counts_ref, base_scr, *, tm, n_lat):
    first = jnp.logical_and(pl.program_id(0) == 0, pl.program_id(1) == 0)

    @pl.when(first)
    def _():
        base_scr[...] = jnp.zeros_like(base_scr)

    h = _norm_modulate(x_ref[0], g_ref[...], modb_ref, modc_ref, pl.program_id(1) * tm, n_lat, 3, 4)
    h_ref[0] = h.astype(BF16)
    logits = lax.dot_general(rw_ref[...], h, (((1,), (1,)), ((), ())), preferred_element_type=F32,
                             precision=lax.Precision.HIGHEST)
    scores = jax.nn.sigmoid(logits)
    biased = scores + rb_ref[...]
    tops = []
    for gidx in range(N_EXPERT_GROUPS):
        rows = [biased[gidx * EXPERTS_PER_GROUP + j:gidx * EXPERTS_PER_GROUP + j + 1, :]
                for j in range(EXPERTS_PER_GROUP)]
        tops.append(_first_max4(rows))
    gs = [t[0] + t[2] for t in tops]
    gmax = jnp.maximum(jnp.maximum(gs[0], gs[1]), jnp.maximum(gs[2], gs[3]))
    best = jnp.where(gs[0] == gmax, 0, jnp.where(gs[1] == gmax, 1, jnp.where(gs[2] == gmax, 2, 3)))
    e1 = jnp.zeros_like(best)
    e2 = jnp.zeros_like(best)
    for gidx in range(N_EXPERT_GROUPS):
        e1 = jnp.where(best == gidx, gidx * EXPERTS_PER_GROUP + tops[gidx][1], e1)
        e2 = jnp.where(best == gidx, gidx * EXPERTS_PER_GROUP + tops[gidx][3], e2)
    eid = lax.broadcasted_iota(I32, (N_EXPERTS, tm), 0)
    sel1 = eid == e1
    sel2 = eid == e2
    w1 = jnp.sum(jnp.where(sel1, scores, 0.0), axis=0, keepdims=True)
    w2 = jnp.sum(jnp.where(sel2, scores, 0.0), axis=0, keepdims=True)
    wsum = w1 + w2
    onehot = jnp.where(jnp.logical_or(sel1, sel2), 1.0, 0.0)
    pos = base_scr[...] + _dot(onehot.astype(BF16), tri_ref[...])
    r1 = jnp.sum(jnp.where(sel1, pos, 0.0), axis=0, keepdims=True)
    r2 = jnp.sum(jnp.where(sel2, pos, 0.0), axis=0, keepdims=True)
    ids_ref[0] = jnp.concatenate([e1, e2], axis=0)
    gates_ref[0] = jnp.concatenate([w1 / wsum, w2 / wsum], axis=0)
    ranks_ref[0] = jnp.concatenate([r1, r2], axis=0).astype(I32)
    base = base_scr[...] + jnp.sum(onehot, axis=1, keepdims=True)
    base_scr[...] = base
    counts_ref[...] = jnp.broadcast_to(base, counts_ref.shape)


def _router(xs, gain, modv, rw_t, rbias, n_rows, n_lat, tm):
    bsz, ltot, d = xs.shape
    tri = (lax.broadcasted_iota(I32, (tm, tm), 0) < lax.broadcasted_iota(I32, (tm, tm), 1)).astype(BF16)
    kern = functools.partial(_router_kernel, tm=tm, n_lat=n_lat)
    small = lambda dt: jax.ShapeDtypeStruct((bsz, 2, n_rows), dt)
    small_spec = pl.BlockSpec((1, 2, tm), lambda b, i: (b, 0, i))
    return pl.pallas_call(
        kern,
        out_shape=(jax.ShapeDtypeStruct((bsz, ltot, d), BF16), small(I32), small(F32), small(I32),
                   jax.ShapeDtypeStruct((N_EXPERTS, 128), F32)),
        grid=(bsz, n_rows // tm),
        in_specs=[pl.BlockSpec((1, tm, d), lambda b, i: (b, i, 0)),
                  pl.BlockSpec((1, d), lambda b, i: (0, 0)),
                  pl.BlockSpec((1, 6, d), lambda b, i: (b, 0, 0)),
                  pl.BlockSpec((1, 6, d), lambda b, i: (bsz, 0, 0)),
                  pl.BlockSpec((N_EXPERTS, d), lambda b, i: (0, 0)),
                  pl.BlockSpec((N_EXPERTS, 1), lambda b, i: (0, 0)),
                  pl.BlockSpec((tm, tm), lambda b, i: (0, 0))],
        out_specs=(pl.BlockSpec((1, tm, d), lambda b, i: (b, i, 0)), small_spec, small_spec, small_spec,
                   pl.BlockSpec((N_EXPERTS, 128), lambda b, i: (0, 0))),
        scratch_shapes=[pltpu.VMEM((N_EXPERTS, 1), F32)],
        compiler_params=_params("arbitrary", "arbitrary"),
        name="moe_router",
    )(xs, gain, modv, modv, rw_t, rbias, tri)


def _expert_ffn_kernel(te_ref, nt_ref, x_ref, wg_ref, wu_ref, wd_ref, o_ref):
    live = pl.program_id(0) < nt_ref[0]

    @pl.when(live)
    def _():
        x = x_ref[...]
        g = _dot(x, wg_ref[0])
        u = _dot(x, wu_ref[0])
        he = (g * jax.nn.sigmoid(g)) * u
        o_ref[...] = _dot(he.astype(BF16), wd_ref[0]).astype(o_ref.dtype)

    @pl.when(jnp.logical_not(live))
    def _():
        o_ref[...] = jnp.zeros_like(o_ref)


def _expert_ffn(tile_expert, n_tiles, x_sorted, wg, wu, wd, tm):
    p, d = x_sorted.shape
    f = wg.shape[-1]
    tile = lambda i, te, nt: i
    expert = lambda i, te, nt: te[jnp.minimum(i, nt[0] - 1)]
    grid_spec = pltpu.PrefetchScalarGridSpec(
        num_scalar_prefetch=2,
        grid=(p // tm,),
        in_specs=[pl.BlockSpec((tm, d), lambda i, te, nt: (tile(i, te, nt), 0)),
                  pl.BlockSpec((1, d, f), lambda i, te, nt: (expert(i, te, nt), 0, 0)),
                  pl.BlockSpec((1, d, f), lambda i, te, nt: (expert(i, te, nt), 0, 0)),
                  pl.BlockSpec((1, f, d), lambda i, te, nt: (expert(i, te, nt), 0, 0))],
        out_specs=pl.BlockSpec((tm, d), lambda i, te, nt: (tile(i, te, nt), 0)),
    )
    return pl.pallas_call(
        _expert_ffn_kernel,
        out_shape=jax.ShapeDtypeStruct((p, d), BF16),
        grid_spec=grid_spec,
        compiler_params=_params("arbitrary"),
        name="moe_expert_ffn",
    )(tile_expert, n_tiles, x_sorted, wg, wu, wd)


def _moe_residual_kernel(x_ref, y0_ref, y1_ref, gates_ref, modb_ref, modc_ref, g_ref, o_ref,
                         *, tm, n_lat, final_norm):
    gates = gates_ref[0]
    y = gates[:, 0:1] * y0_ref[0, 0].astype(F32) + gates[:, 1:2] * y1_ref[0, 0].astype(F32)
    gate = _gate_rows(modb_ref, modc_ref, 5, pl.program_id(1) * tm, tm, n_lat)
    x = x_ref[0] + gate * y
    if final_norm:
        x = x * lax.rsqrt(jnp.mean(x * x, axis=-1, keepdims=True) + EPS) * g_ref[...]
    o_ref[0] = x


def _moe_residual(xs, y_pair, gates_t, modv, final_g, n_rows, n_lat, tm, final_norm):
    bsz, ltot, d = xs.shape
    kern = functools.partial(_moe_residual_kernel, tm=tm, n_lat=n_lat, final_norm=final_norm)
    rows_out = n_rows if final_norm else ltot
    return pl.pallas_call(
        kern,
        out_shape=jax.ShapeDtypeStruct((bsz, rows_out, d), F32),
        grid=(bsz, n_rows // tm),
        in_specs=[pl.BlockSpec((1, tm, d), lambda b, i: (b, i, 0)),
                  pl.BlockSpec((1, 1, tm, d), lambda b, i: (0, b, i, 0)),
                  pl.BlockSpec((1, 1, tm, d), lambda b, i: (1, b, i, 0)),
                  pl.BlockSpec((1, tm, 2), lambda b, i: (b, i, 0)),
                  pl.BlockSpec((1, 6, d), lambda b, i: (b, 0, 0)),
                  pl.BlockSpec((1, 6, d), lambda b, i: (bsz, 0, 0)),
                  pl.BlockSpec((1, d), lambda b, i: (0, 0))],
        out_specs=pl.BlockSpec((1, tm, d), lambda b, i: (b, i, 0)),
        compiler_params=_params("arbitrary", "arbitrary"),
        name="moe_residual",
    )(xs, y_pair, y_pair, gates_t, modv, modv, final_g)


def _moe_layer(xs, gain, modv, rw_t, rbias, wg, wu, wd, expert_base, final_g, n_rows, n_lat, tm_tok, tm_res, tm_exp,
               final_norm):
    bsz, ltot, d = xs.shape
    h, ids, gates, ranks, counts = _router(xs, gain, modv, rw_t, rbias, n_rows, n_lat, tm_tok)
    counts = counts[:, 0].astype(I32)
    padded = ((counts + tm_exp - 1) // tm_exp) * tm_exp
    ends = jnp.cumsum(padded)
    offsets = ends - padded
    onehot = ids[..., None] == jnp.arange(N_EXPERTS, dtype=I32)
    dest = jnp.sum(jnp.where(onehot, offsets, 0), axis=-1) + ranks
    n_slots = bsz * 2 * n_rows
    p_rows = ((n_slots + N_EXPERTS * (tm_exp - 1)) + tm_exp - 1) // tm_exp * tm_exp
    tok = (jnp.arange(bsz, dtype=I32)[:, None, None] * ltot + jnp.arange(n_rows, dtype=I32)[None, None, :])
    tok = jnp.broadcast_to(tok, dest.shape)
    fill_slot = jnp.arange(p_rows, dtype=I32) % n_slots
    filler = (fill_slot // (2 * n_rows)) * ltot + fill_slot % n_rows
    src = filler.at[dest.reshape(-1)].set(tok.reshape(-1), mode="promise_in_bounds", unique_indices=True)
    tile_start = jnp.arange(p_rows // tm_exp, dtype=I32) * tm_exp
    tile_expert = jnp.minimum(jnp.sum(tile_start[:, None] >= ends[None, :], axis=1), N_EXPERTS - 1).astype(I32)
    n_tiles = (ends[-1:] // tm_exp).astype(I32)
    x_sorted = h.reshape(bsz * ltot, d).at[src].get(mode="promise_in_bounds")
    y_sorted = _expert_ffn(tile_expert + expert_base, n_tiles, x_sorted, wg, wu, wd, tm_exp)
    y_pair = y_sorted.at[dest.transpose(1, 0, 2).reshape(-1)].get(mode="promise_in_bounds")
    y_pair = y_pair.reshape(2, bsz, n_rows, d)
    return _moe_residual(xs, y_pair, gates.transpose(0, 2, 1), modv, final_g, n_rows, n_lat, tm_res, final_norm)


def _row_tile(n, target):
    t = min(n, target)
    while n % t or t % 8:
        t -= 8
    return t


def kernel(x, c, ctx, c_ctx, norm_mix_g, norm_ffn_g, w_mod, b_mod, w_in, w_out, s5_lam_re, s5_lam_im, s5_log_dt, s5_b_re, s5_b_im, s5_c_re, s5_c_im, s5_d, s5_glu_w, s5_glu_b, lru_conv_w, lru_conv_b, lru_wa, lru_ba, lru_wx, lru_bx, lru_lam, hgrn_lb_logits, router_w, router_bias, moe_w_gate, moe_w_up, moe_w_down, final_norm_g):
    bsz, n_lat, d = x.shape
    n_ctx = ctx.shape[1]
    ltot = n_lat + n_ctx
    depth = w_in.shape[0]
    gw = GROUP_WIDTH
    assert bsz % 8 == 0 and bsz + 1 <= MOD_ROWS
    assert n_lat % GLA_CHUNK == 0 and n_ctx % GLA_CHUNK == 0 and n_lat % GRID_W == 0 and n_lat % NORM_SLAB == 0

    tq_scan = math.gcd(math.gcd(n_lat, n_ctx), 64)
    tq_gla = math.gcd(math.gcd(n_lat, n_ctx), 256)
    tm_all = _row_tile(ltot, 768)
    tm_lat = _row_tile(n_lat, 1024)
    tm_mix = math.gcd(math.gcd(n_lat, n_ctx), 256)

    xs = jnp.concatenate([x, ctx.astype(x.dtype)], axis=1)
    cv = jnp.zeros((MOD_ROWS, d), F32).at[:bsz].set(c).at[bsz].set(c_ctx)
    mod_all = _mod_vectors(cv, w_mod, b_mod).reshape(depth, MOD_ROWS, 6, d)
    cos_t, sin_t = _rotary_tables(n_lat, n_ctx)
    rw_t = router_w.astype(F32).T
    rbias = router_bias.astype(F32).reshape(N_EXPERTS, 1)
    lb_cum = [jnp.cumsum(jax.nn.softmax(hgrn_lb_logits[dr].astype(F32), axis=0), axis=0) for dr in (0, 1)]
    w_in_b, w_out_b = _cast_bf16(w_in), _cast_bf16(w_out)
    n_exp, _, d_ff = moe_w_gate.shape[1:]
    wg_b = _cast_bf16(moe_w_gate.reshape(depth * n_exp, d, d_ff))
    wu_b = _cast_bf16(moe_w_up.reshape(depth * n_exp, d, d_ff))
    wd_b = _cast_bf16(moe_w_down.reshape(depth * n_exp, d_ff, d))

    for layer in range(depth):
        ctx_out = layer < depth - 1
        n_rows = ltot if ctx_out else n_lat
        tm_tok = tm_all if ctx_out else tm_lat
        modv = mod_all[layer]

        tm2, bm = _in_projection(xs, norm_mix_g[layer].reshape(1, d), modv, w_in_b, layer, n_lat, tm_all)
        tm3 = tm2.reshape(ltot, bsz, N_TM_PARTS * gw)

        ys = hs = o_h = o_r = None
        for dr in (0, 1):
            rev = dr == 1
            bw, cw, atab = _s5_tables(s5_lam_re[layer, dr], s5_lam_im[layer, dr], s5_log_dt[layer, dr],
                                      s5_b_re[layer, dr], s5_b_im[layer, dr], s5_c_re[layer, dr], s5_c_im[layer, dr])
            ys = _s5_direction(tm3, bw, cw, atab, ys, rev, n_lat, tq_scan)
            sp = jax.nn.softplus(-lru_lam[layer, dr].astype(F32)).reshape(1, gw)
            hs = _lru_direction(tm3, lru_conv_w[layer].astype(F32), lru_conv_b[layer].astype(F32).reshape(1, gw),
                                _block_diag(lru_wa[layer, dr]).astype(BF16), _block_diag(lru_wx[layer, dr]).astype(BF16),
                                lru_ba[layer, dr].astype(F32).reshape(1, gw), lru_bx[layer, dr].astype(F32).reshape(1, gw),
                                sp, hs, rev, n_lat, tq_scan)
            lb = lb_cum[dr][layer] - lb_cum[dr][0]
            o_h = _hgrn_direction(bm, lb, o_h, rev, n_lat, tq_gla)
            o_r = _ret_direction(bm, cos_t, sin_t, o_r, rev, n_lat, tq_gla)

        mix = _mix_epilogue(tm2, ys.reshape(ltot, bsz * gw), hs.reshape(ltot, bsz * gw), bm, o_h, o_r,
                            s5_d[layer].astype(F32).reshape(1, gw), s5_glu_w[layer].astype(BF16),
                            s5_glu_b[layer].astype(F32).reshape(1, gw), n_rows, tm_mix)
        xs = _out_projection(mix, w_out_b, layer, xs, modv, n_rows, n_lat, tm_tok, 1024)
        xs = _moe_layer(xs, norm_ffn_g[layer].reshape(1, d), modv, rw_t, rbias, wg_b, wu_b, wd_b, layer * n_exp,
                        final_norm_g.reshape(1, d), n_rows, n_lat, tm_tok, tm_mix, 512, not ctx_out)
    return xs
```

```python
import functools
import math

import jax
import jax.numpy as jnp
from jax import lax
from jax.experimental import pallas as pl
from jax.experimental.pallas import tpu as pltpu

F32 = jnp.float32
BF16 = jnp.bfloat16
I32 = jnp.int32

GROUP_WIDTH = 512
N_IN_PARTS = 12
N_TM_PARTS = 3
N_BM_PARTS = N_IN_PARTS - N_TM_PARTS
S5_CH = 16
S5_GROUPS = 32
S5_STATE = 64
S5_LANE_BLOCKS = 4
LRU_HEADS = 8
LRU_CONV = 4
LRU_C = 8.0
GLA_HEADS = 4
GLA_DIM = 128
GLA_CHUNK = 64
GLA_SUB = 16
HGRN_FAST_WINDOW = 32
HGRN_SAFE_LOG_DECAY = 80.0
HGRN_SAFE_QUERY = 1.0e3
HGRN_HEADS_PER_STEP = 4
RET_CHUNK = 256
RET_DECAY_EXP = (5.0, 5.5)
ROPE_BASE = 10000.0
GRID_W = 64
N_EXPERTS = 16
N_EXPERT_GROUPS = 4
EXPERTS_PER_GROUP = 4
D_FF_EXPERT = 1024
EPS = 1e-6
NORM_SLAB = 16
MOD_ROWS = 24

VMEM_LIMIT_BYTES = 56 * 1024 * 1024


def _params(*semantics):
    return pltpu.CompilerParams(dimension_semantics=semantics, vmem_limit_bytes=VMEM_LIMIT_BYTES)


def _dot(a, b):
    return jnp.dot(a, b, preferred_element_type=F32)


def _dot_nt(a, b):
    return lax.dot_general(a, b, (((1,), (1,)), ((), ())), preferred_element_type=F32)


def _dot_tn(a, b):
    return lax.dot_general(a, b, (((0,), (0,)), ((), ())), preferred_element_type=F32)


def _scan_chunk_index(step, n_lat_chunks, n_chunks, rev):
    if rev:
        return n_chunks - 1 - step
    return (step + n_lat_chunks) % n_chunks


def _cast_kernel(x_ref, o_ref):
    o_ref[...] = x_ref[...].astype(o_ref.dtype)


def _cast_bf16(w):
    n, r, c = w.shape
    tc = min(c, 1024)
    spec = pl.BlockSpec((1, r, tc), lambda i, j: (i, 0, j))
    return pl.pallas_call(
        _cast_kernel,
        out_shape=jax.ShapeDtypeStruct(w.shape, BF16),
        grid=(n, c // tc),
        in_specs=[spec],
        out_specs=spec,
        compiler_params=_params("arbitrary", "arbitrary"),
        name="cast_bf16",
    )(w)


def _mod_kernel(cv_ref, w_ref, b_ref, o_ref):
    cv = cv_ref[...]
    s = cv * jax.nn.sigmoid(cv)
    o_ref[0] = _dot(s.astype(BF16), w_ref[0].astype(BF16)) + b_ref[0]


def _mod_vectors(cv, w_mod, b_mod):
    nl, d, n6 = w_mod.shape
    tn = 1024
    return pl.pallas_call(
        _mod_kernel,
        out_shape=jax.ShapeDtypeStruct((nl, MOD_ROWS, n6), F32),
        grid=(nl, n6 // tn),
        in_specs=[pl.BlockSpec((MOD_ROWS, d), lambda l, j: (0, 0)),
                  pl.BlockSpec((1, d, tn), lambda l, j: (l, 0, j)),
                  pl.BlockSpec((1, 1, tn), lambda l, j: (l, 0, j))],
        out_specs=pl.BlockSpec((1, MOD_ROWS, tn), lambda l, j: (l, 0, j)),
        compiler_params=_params("arbitrary", "arbitrary"),
        name="mod_vectors",
    )(cv, w_mod, b_mod.reshape(nl, 1, n6))


def _norm_modulate(x, g, modb_ref, modc_ref, row0, n_lat, shift_idx, scale_idx):
    tm = x.shape[0]
    ms = jnp.mean(x * x, axis=-1, keepdims=True)
    y = x * lax.rsqrt(ms + EPS) * g
    row = row0 + lax.broadcasted_iota(I32, (tm, 1), 0)
    is_ctx = row >= n_lat
    shift = jnp.where(is_ctx, modc_ref[0, shift_idx:shift_idx + 1, :], modb_ref[0, shift_idx:shift_idx + 1, :])
    scale = jnp.where(is_ctx, modc_ref[0, scale_idx:scale_idx + 1, :], modb_ref[0, scale_idx:scale_idx + 1, :])
    return y * (1.0 + scale) + shift


def _inproj_kernel(x_ref, g_ref, modb_ref, modc_ref, w_ref, otm_ref, obm_ref, h_scr, aff_scr, *, tm, n_lat):
    i = pl.program_id(1)
    j = pl.program_id(2)

    @pl.when(j == 0)
    def _():
        d = x_ref.shape[-1]
        for kind, mod_ref in enumerate((modb_ref, modc_ref)):
            aff_scr[kind, 0] = jnp.broadcast_to(g_ref[...] * (1.0 + mod_ref[0, 1:2, :]), (NORM_SLAB, d))
            aff_scr[kind, 1] = jnp.broadcast_to(mod_ref[0, 0:1, :], (NORM_SLAB, d))

        def slab(s, carry):
            r0 = pl.multiple_of(s * NORM_SLAB, NORM_SLAB)
            kind = (i * tm + r0 >= n_lat).astype(I32)
            x = x_ref[0, pl.ds(r0, NORM_SLAB), :]
            inv = lax.rsqrt(jnp.mean(x * x, axis=-1, keepdims=True) + EPS)
            h_scr[pl.ds(r0, NORM_SLAB), :] = (x * inv * aff_scr[kind, 0] + aff_scr[kind, 1]).astype(BF16)
            return carry

        lax.fori_loop(0, tm // NORM_SLAB, slab, 0, unroll=8)

    r = _dot(h_scr[...], w_ref[0])

    @pl.when(j == 0)
    def _():
        otm_ref[...] = r

    @pl.when(j > 0)
    def _():
        obm_ref[0] = r


def _in_projection(xs, gain, modv, w_bf16, layer, n_lat, tm):
    bsz, ltot, d = xs.shape
    gw = GROUP_WIDTH
    kern = functools.partial(_inproj_kernel, tm=tm, n_lat=n_lat)
    tn = N_TM_PARTS * gw
    return pl.pallas_call(
        kern,
        out_shape=(jax.ShapeDtypeStruct((ltot, bsz * tn), F32),
                   jax.ShapeDtypeStruct((bsz, ltot, N_BM_PARTS * gw), F32)),
        grid=(bsz, ltot // tm, N_IN_PARTS * gw // tn),
        in_specs=[pl.BlockSpec((1, tm, d), lambda b, i, j: (b, i, 0)),
                  pl.BlockSpec((1, d), lambda b, i, j: (0, 0)),
                  pl.BlockSpec((1, 6, d), lambda b, i, j: (b, 0, 0)),
                  pl.BlockSpec((1, 6, d), lambda b, i, j: (bsz, 0, 0)),
                  pl.BlockSpec((1, d, tn), lambda b, i, j: (layer, 0, j))],
        out_specs=(pl.BlockSpec((tm, tn), lambda b, i, j: (i, b)),
                   pl.BlockSpec((1, tm, tn), lambda b, i, j: (b, i, jnp.maximum(j - 1, 0)))),
        scratch_shapes=[pltpu.VMEM((tm, d), BF16), pltpu.VMEM((2, 2, NORM_SLAB, d), F32)],
        compiler_params=_params("arbitrary", "arbitrary", "arbitrary"),
        name="in_projection",
    )(xs, gain, modv, modv, w_bf16)


def _s5_kernel(*refs, rev, tq, nb, has_prev):
    if has_prev:
        u_ref, bw_ref, cw_ref, a_ref, prev_ref, y_ref, z_scr, h_scr = refs
    else:
        u_ref, bw_ref, cw_ref, a_ref, y_ref, z_scr, h_scr = refs
        prev_ref = None
    rows = tq * nb
    half = S5_STATE * 8

    @pl.when(pl.program_id(0) == 0)
    def _():
        h_scr[...] = jnp.zeros_like(h_scr)

    for gb in range(S5_LANE_BLOCKS):
        lanes = slice(gb * 128, (gb + 1) * 128)
        ub = u_ref[:, :, lanes].reshape(rows, 128).astype(BF16)
        z_scr[...] = _dot(ub, bw_ref[gb])
        ar = jnp.broadcast_to(a_ref[gb, 0], (nb, half))
        ai = jnp.broadcast_to(a_ref[gb, 1], (nb, half))

        def step(i, carry):
            hr, hi = carry
            t = (tq - 1 - i) if rev else i
            r0 = pl.multiple_of(t * nb, nb)
            zr = z_scr[pl.ds(r0, nb), 0:half]
            zi = z_scr[pl.ds(r0, nb), half:2 * half]
            nr = ar * hr - ai * hi + zr
            ni = ar * hi + ai * hr + zi
            z_scr[pl.ds(r0, nb), 0:half] = nr
            z_scr[pl.ds(r0, nb), half:2 * half] = ni
            return nr, ni

        hr, hi = lax.fori_loop(0, tq, step, (h_scr[gb, 0], h_scr[gb, 1]), unroll=4)
        h_scr[gb, 0] = hr
        h_scr[gb, 1] = hi
        yb = _dot(z_scr[...].astype(BF16), cw_ref[gb]).reshape(tq, nb, 128)
        if has_prev:
            yb = yb + prev_ref[:, :, lanes]
        y_ref[:, :, lanes] = yb


def _s5_direction(tm3, bw, cw, atab, prev, rev, n_lat, tq):
    ltot, nb, _ = tm3.shape
    gw = GROUP_WIDTH
    n_chunks = ltot // tq
    n_lat_chunks = n_lat // tq
    cidx = lambda s: _scan_chunk_index(s, n_lat_chunks, n_chunks, rev)
    has_prev = prev is not None
    in_specs = [pl.BlockSpec((tq, nb, gw), lambda s: (cidx(s), 0, 0)),
                pl.BlockSpec(bw.shape, lambda s: (0, 0, 0)),
                pl.BlockSpec(cw.shape, lambda s: (0, 0, 0)),
                pl.BlockSpec(atab.shape, lambda s: (0, 0, 0, 0))]
    args = [tm3, bw, cw, atab]
    if has_prev:
        in_specs.append(pl.BlockSpec((tq, nb, gw), lambda s: (cidx(s), 0, 0)))
        args.append(prev)
    kern = functools.partial(_s5_kernel, rev=rev, tq=tq, nb=nb, has_prev=has_prev)
    return pl.pallas_call(
        kern,
        out_shape=jax.ShapeDtypeStruct((ltot, nb, gw), F32),
        grid=(n_chunks,),
        in_specs=in_specs,
        out_specs=pl.BlockSpec((tq, nb, gw), lambda s: (cidx(s), 0, 0)),
        scratch_shapes=[pltpu.VMEM((tq * nb, 2 * S5_STATE * 8), F32),
                        pltpu.VMEM((S5_LANE_BLOCKS, 2, nb, S5_STATE * 8), F32)],
        compiler_params=_params("arbitrary"),
        name="s5_rev" if rev else "s5_fwd",
    )(*args)


def _block_diag(blocks):
    n, r, c = blocks.shape
    eye = jnp.eye(n, dtype=blocks.dtype)
    return (eye[:, None, :, None] * blocks[:, :, None, :]).reshape(n * r, n * c)


def _s5_tables(lam_re, lam_im, log_dt, b_re, b_im, c_re, c_im):
    dt = jnp.exp(log_dt.astype(F32))[:, None]
    lr, li = lam_re.astype(F32), lam_im.astype(F32)
    mag = jnp.exp(lr * dt)
    abar_re, abar_im = mag * jnp.cos(li * dt), mag * jnp.sin(li * dt)
    den = lr * lr + li * li
    zr = abar_re - 1.0
    w_re = (zr * lr + abar_im * li) / den
    w_im = (abar_im * lr - zr * li) / den
    bre, bim = b_re.astype(F32), b_im.astype(F32)
    bw_re = w_re[:, None, :] * bre - w_im[:, None, :] * bim
    bw_im = w_re[:, None, :] * bim + w_im[:, None, :] * bre
    gpb = S5_GROUPS // S5_LANE_BLOCKS
    bws, cws, atabs = [], [], []
    for gb in range(S5_LANE_BLOCKS):
        sl = slice(gb * gpb, (gb + 1) * gpb)
        bws.append(jnp.concatenate([_block_diag(bw_re[sl]), _block_diag(bw_im[sl])], axis=1))
        cws.append(jnp.concatenate([_block_diag(c_re[sl].astype(F32)), -_block_diag(c_im[sl].astype(F32))], axis=0))
        atabs.append(jnp.stack([abar_re[sl].reshape(1, -1), abar_im[sl].reshape(1, -1)]))
    return jnp.stack(bws).astype(BF16), jnp.stack(cws).astype(BF16), jnp.stack(atabs)


def _lru_kernel(*refs, rev, tq, nb, n_lat_chunks, n_chunks, has_prev):
    if has_prev:
        x_ref, xp_ref, xn_ref, cw_ref, cb_ref, wa_ref, wx_ref, ba_ref, bx_ref, sp_ref, prev_ref, o_ref, a_scr, b_scr, h_scr = refs
    else:
        x_ref, xp_ref, xn_ref, cw_ref, cb_ref, wa_ref, wx_ref, ba_ref, bx_ref, sp_ref, o_ref, a_scr, b_scr, h_scr = refs
        prev_ref = None
    s = pl.program_id(0)
    gw = GROUP_WIDTH
    rows = tq * nb

    @pl.when(s == 0)
    def _():
        h_scr[...] = jnp.zeros_like(h_scr)

    c = _scan_chunk_index(s, n_lat_chunks, n_chunks, rev)
    first = jnp.logical_or(c == 0, c == n_lat_chunks)
    last = jnp.logical_or(c == n_lat_chunks - 1, c == n_chunks - 1)
    keep_prev = jnp.where(first, 0.0, 1.0)
    keep_next = jnp.where(last, 0.0, 1.0)
    xc = jnp.concatenate([xp_ref[...] * keep_prev, x_ref[...], xn_ref[...] * keep_next], axis=0)
    conv = cb_ref[...] + xc[0:tq] * cw_ref[0:1, :]
    for k in range(1, LRU_CONV):
        conv = conv + xc[k:k + tq] * cw_ref[k:k + 1, :]
    x = conv.reshape(rows, gw)
    xb = x.astype(BF16)
    r = jax.nn.sigmoid(_dot(xb, wa_ref[...]) + ba_ref[...])
    ig = jax.nn.sigmoid(_dot(xb, wx_ref[...]) + bx_ref[...])
    log_a = -LRU_C * r * sp_ref[...]
    a = jnp.exp(log_a)
    a_scr[...] = a
    b_scr[...] = jnp.sqrt(1.0 - a * a) * (ig * x)

    def step(i, h):
        t = (tq - 1 - i) if rev else i
        r0 = pl.multiple_of(t * nb, nb)
        hn = a_scr[pl.ds(r0, nb), :] * h + b_scr[pl.ds(r0, nb), :]
        b_scr[pl.ds(r0, nb), :] = hn
        return hn

    h_scr[...] = lax.fori_loop(0, tq, step, h_scr[...], unroll=4)
    out = b_scr[...].reshape(tq, nb, gw)
    if has_prev:
        out = out + prev_ref[...]
    o_ref[...] = out


def _lru_direction(tm3, conv_w, conv_b, wa_bd, wx_bd, ba, bx, sp, prev, rev, n_lat, tq):
    ltot, nb, _ = tm3.shape
    gw = GROUP_WIDTH
    n_chunks = ltot // tq
    n_lat_chunks = n_lat // tq
    cidx = lambda s: _scan_chunk_index(s, n_lat_chunks, n_chunks, rev)
    has_prev = prev is not None
    full2 = lambda s: (0, 0)
    in_specs = [pl.BlockSpec((tq, nb, gw), lambda s: (cidx(s), 0, 1)),
                pl.BlockSpec((2, nb, gw), lambda s: (jnp.maximum(cidx(s) * (tq // 2) - 1, 0), 0, 1)),
                pl.BlockSpec((1, nb, gw), lambda s: (jnp.minimum((cidx(s) + 1) * tq, ltot - 1), 0, 1)),
                pl.BlockSpec((LRU_CONV, gw), full2),
                pl.BlockSpec((1, gw), full2),
                pl.BlockSpec((gw, gw), full2),
                pl.BlockSpec((gw, gw), full2),
                pl.BlockSpec((1, gw), full2),
                pl.BlockSpec((1, gw), full2),
                pl.BlockSpec((1, gw), full2)]
    args = [tm3, tm3, tm3, conv_w, conv_b, wa_bd, wx_bd, ba, bx, sp]
    if has_prev:
        in_specs.append(pl.BlockSpec((tq, nb, gw), lambda s: (cidx(s), 0, 0)))
        args.append(prev)
    kern = functools.partial(_lru_kernel, rev=rev, tq=tq, nb=nb, n_lat_chunks=n_lat_chunks,
                             n_chunks=n_chunks, has_prev=has_prev)
    return pl.pallas_call(
        kern,
        out_shape=jax.ShapeDtypeStruct((ltot, nb, gw), F32),
        grid=(n_chunks,),
        in_specs=in_specs,
        out_specs=pl.BlockSpec((tq, nb, gw), lambda s: (cidx(s), 0, 0)),
        scratch_shapes=[pltpu.VMEM((tq * nb, gw), F32),
                        pltpu.VMEM((tq * nb, gw), F32),
                        pltpu.VMEM((nb, gw), F32)],
        compiler_params=_params("arbitrary"),
        name="lru_rev" if rev else "lru_fwd",
    )(*args)


def _tri_mask(n, rev):
    t = lax.broadcasted_iota(I32, (n, n), 0)
    s = lax.broadcasted_iota(I32, (n, n), 1)
    return (s >= t) if rev else (s <= t)


def _hgrn_kernel(*refs, rev, tq, has_prev):
    if has_prev:
        q_ref, z_ref, i_ref, lb_ref, tri_ref, ones_ref, prev_ref, o_ref, st_scr, cum_scr, q_scr, k_scr, v_scr = refs
    else:
        q_ref, z_ref, i_ref, lb_ref, tri_ref, ones_ref, o_ref, st_scr, cum_scr, q_scr, k_scr, v_scr = refs
        prev_ref = None
    hps = HGRN_HEADS_PER_STEP
    cs, sb, w = GLA_CHUNK, GLA_SUB, min(HGRN_FAST_WINDOW, tq)
    nsub = cs // sb
    n_ck = tq // cs
    n_w = tq // w

    @pl.when(pl.program_id(2) == 0)
    def _():
        st_scr[...] = jnp.zeros_like(st_scr)

    lb = lb_ref[0]
    z = z_ref[0]
    iv = i_ref[0]
    f = lb + (1.0 - lb) * jax.nn.sigmoid(z)
    k_all = 1.0 - f
    v_all = iv * jax.nn.sigmoid(iv)
    q_all = q_ref[0] * (GLA_DIM ** -0.5)
    logf = jnp.log(f)
    lf_hi = logf.astype(BF16)
    rest = logf - lf_hi.astype(F32)
    lf_mid = rest.astype(BF16)
    lf_lo = (rest - lf_mid.astype(F32)).astype(BF16)
    tri = tri_ref[...]
    cum_all = _dot(tri, lf_hi) + _dot(tri, lf_mid) + _dot(tri, lf_lo)
    for hh in range(hps):
        lanes = slice(hh * GLA_DIM, (hh + 1) * GLA_DIM)
        cum_scr[hh] = cum_all[:, lanes]
        q_scr[hh] = q_all[:, lanes]
        k_scr[hh] = k_all[:, lanes]
        v_scr[hh] = v_all[:, lanes]

    def far_row(c, width):
        return c * width if rev else (c + 1) * width - 1

    def outside_row(c, width, n):
        if rev:
            return (c + 1) * width if c < n - 1 else None
        return c * width - 1 if c > 0 else None

    def store_out(hh, rows, out):
        lanes = slice(hh * GLA_DIM, (hh + 1) * GLA_DIM)
        if has_prev:
            out = out + prev_ref[0, rows, lanes]
        o_ref[0, rows, lanes] = out

    def block_factored(_):
        lane_id = lax.broadcasted_iota(I32, (1, tq), 1)
        for hh in range(hps):
            cumg, q, k = cum_scr[hh], q_scr[hh], k_scr[hh]
            vb = v_scr[hh].astype(BF16)
            st = st_scr[hh]
            ends = [cum_scr[hh, far_row(c, w):far_row(c, w) + 1, :] for c in range(n_w)]
            end_full = jnp.concatenate([jnp.broadcast_to(e, (w, GLA_DIM)) for e in ends], axis=0)
            ks = (k * jnp.exp(end_full - cumg)).astype(BF16)
            pieces, starts, row_lo = [], [], []
            total = 0
            for c in range(n_w):
                rows = slice(0, (c + 1) * w) if rev else slice(c * w, tq)
                scale = jnp.exp(jnp.minimum(cumg[rows] - ends[c], HGRN_SAFE_LOG_DECAY))
                pieces.append((q[rows] * scale).astype(BF16))
                starts.append(total)
                row_lo.append(rows.start)
                total += rows.stop - rows.start
            a_stack = _dot_nt(jnp.concatenate(pieces, axis=0), ks)
            a_rows = []
            for r in range(n_w):
                acc = jnp.zeros((w, tq), F32)
                for c in (range(r, n_w) if rev else range(0, r + 1)):
                    lo = starts[c] + r * w - row_lo[c]
                    keep = jnp.logical_and(lane_id >= c * w, lane_id < (c + 1) * w)
                    if c == r:
                        t_id = r * w + lax.broadcasted_iota(I32, (w, tq), 0)
                        keep = jnp.logical_and(keep, (lane_id >= t_id) if rev else (lane_id <= t_id))
                    acc = jnp.where(keep, a_stack[lo:lo + w, :], acc)
                a_rows.append(acc)
            a = jnp.concatenate(a_rows, axis=0).astype(BF16)
            out = _dot(a, vb) + _dot_nt((q * jnp.exp(cumg)).astype(BF16), st.astype(BF16))
            store_out(hh, slice(None), out)
            last = ends[0] if rev else ends[n_w - 1]
            kd = k * jnp.exp(last - cumg)
            st_scr[hh] = st * jnp.exp(last) + _dot_tn(vb, kd.astype(BF16))
        return 0

    row_id = lax.broadcasted_iota(I32, (sb, GLA_DIM), 0)
    valid = [(row_id <= s) if rev else (row_id >= s) for s in range(sb)]

    def chunk_direct(hh, r0):
        rows = pl.ds(r0, cs)
        q, k, cumg = q_scr[hh, rows, :], k_scr[hh, rows, :], cum_scr[hh, rows, :]
        vb = v_scr[hh, rows, :].astype(BF16)
        if rev:
            base = cum_scr[hh, pl.ds(jnp.minimum(r0 + cs, tq - 1), 1), :]
            base = jnp.where(r0 + cs >= tq, 0.0, base)
        else:
            base = cum_scr[hh, pl.ds(jnp.maximum(r0 - 1, 0), 1), :]
            base = jnp.where(r0 == 0, 0.0, base)
        cum = cumg - base
        last = cum[0:1] if rev else cum[cs - 1:cs]
        st = st_scr[hh]
        o_inter = _dot_nt((q * jnp.exp(cum)).astype(BF16), st.astype(BF16))
        outs = []
        for i in range(nsub):
            blk = slice(i * sb, (i + 1) * sb)
            off = slice((i + 1) * sb, cs) if rev else slice(0, i * sb)
            o_row = outside_row(i, sb, nsub)
            q_i, cum_i = q[blk], cum[blk]
            o_i = o_inter[blk]
            if o_row is not None:
                ref_row = cum[o_row:o_row + 1]
                qt = q_i * jnp.exp(cum_i - ref_row)
                kt = k[off] * jnp.exp(ref_row - cum[off])
                a_off = _dot_nt(qt.astype(BF16), kt.astype(BF16))
                o_i = o_i + _dot(a_off.astype(BF16), vb[off])
            prods = []
            for s in range(sb):
                r = r0 + i * sb + s
                e = jnp.exp(jnp.minimum(cumg[blk] - cum_scr[hh, pl.ds(r, 1), :], 0.0))
                prods.append(jnp.where(valid[s], q_i * e * k_scr[hh, pl.ds(r, 1), :], 0.0))
            sums = _dot(jnp.concatenate(prods, axis=0).astype(BF16), ones_ref[...])
            for s in range(sb):
                r = r0 + i * sb + s
                o_i = o_i + sums[s * sb:(s + 1) * sb] * v_scr[hh, pl.ds(r, 1), :]
            outs.append(o_i)
        store_out(hh, rows, jnp.concatenate(outs, axis=0))
        kd = k * jnp.exp(last - cum)
        st_scr[hh] = st * jnp.exp(last) + _dot_tn(vb, kd.astype(BF16))

    def block_direct(_):
        def chunk(ci, carry):
            c = (n_ck - 1 - ci) if rev else ci
            r0 = pl.multiple_of(c * cs, cs)
            for hh in range(hps):
                chunk_direct(hh, r0)
            return carry

        return lax.fori_loop(0, n_ck, chunk, 0)

    decays = []
    for c in range(n_w):
        far = cum_all[far_row(c, w):far_row(c, w) + 1]
        o_row = outside_row(c, w, n_w)
        decays.append(far if o_row is None else far - cum_all[o_row:o_row + 1])
    safe = jnp.logical_and(jnp.min(jnp.concatenate(decays, axis=0)) > -HGRN_SAFE_LOG_DECAY,
                           jnp.max(jnp.abs(q_all)) < HGRN_SAFE_QUERY)
    lax.cond(safe, block_factored, block_direct, 0)


def _gla_specs(ltot, tq, n_lat, rev):
    n_chunks = ltot // tq
    n_lat_chunks = n_lat // tq
    cidx = lambda s: _scan_chunk_index(s, n_lat_chunks, n_chunks, rev)
    part = lambda p: pl.BlockSpec((1, tq, GLA_DIM), lambda b, h, s: (b, cidx(s), p * GLA_HEADS + h))
    return n_chunks, cidx, part


def _hgrn_direction(bm, lb, prev, rev, n_lat, tq):
    bsz, ltot, _ = bm.shape
    n_chunks, cidx, _ = _gla_specs(ltot, tq, n_lat, rev)
    has_prev = prev is not None
    hps = HGRN_HEADS_PER_STEP
    width = hps * GLA_DIM
    n_hsteps = GLA_HEADS // hps
    part = lambda p: pl.BlockSpec((1, tq, width), lambda b, h, s: (b, cidx(s), p * n_hsteps + h))
    tri = _tri_mask(tq, rev).astype(BF16)
    ones = jnp.ones((GLA_DIM, GLA_DIM), BF16)
    z_part = 2 if rev else 1
    in_specs = [part(0), part(z_part), part(3),
                pl.BlockSpec((1, 1, width), lambda b, h, s: (h, 0, 0)),
                pl.BlockSpec(tri.shape, lambda b, h, s: (0, 0)),
                pl.BlockSpec(ones.shape, lambda b, h, s: (0, 0))]
    args = [bm, bm, bm, lb.reshape(n_hsteps, 1, width), tri, ones]
    out_spec = pl.BlockSpec((1, tq, width), lambda b, h, s: (b, cidx(s), h))
    if has_prev:
        in_specs.append(out_spec)
        args.append(prev)
    kern = functools.partial(_hgrn_kernel, rev=rev, tq=tq, has_prev=has_prev)
    tile = pltpu.VMEM((hps, tq, GLA_DIM), F32)
    return pl.pallas_call(
        kern,
        out_shape=jax.ShapeDtypeStruct((bsz, ltot, GROUP_WIDTH), F32),
        grid=(bsz, n_hsteps, n_chunks),
        in_specs=in_specs,
        out_specs=out_spec,
        scratch_shapes=[pltpu.VMEM((hps, GLA_DIM, GLA_DIM), F32), tile, tile, tile, tile],
        compiler_params=_params("arbitrary", "arbitrary", "arbitrary"),
        name="hgrn_rev" if rev else "hgrn_fwd",
    )(*args)


def _ret_kernel(*refs, rev, tq, has_prev):
    if has_prev:
        q_ref, k_ref, v_ref, cos_ref, sin_ref, eq_ref, ek_ref, dm_ref, el_ref, prev_ref, o_ref, st_scr = refs
    else:
        q_ref, k_ref, v_ref, cos_ref, sin_ref, eq_ref, ek_ref, dm_ref, el_ref, o_ref, st_scr = refs
        prev_ref = None
    cs = min(tq, RET_CHUNK)
    n_ck = tq // cs

    @pl.when(pl.program_id(1) == 0)
    def _():
        st_scr[...] = jnp.zeros_like(st_scr)

    def chunk(ci, carry):
        c = (n_ck - 1 - ci) if rev else ci
        r0 = pl.multiple_of(c * cs, cs)
        rows = pl.ds(r0, cs)
        cos = cos_ref[rows, :]
        sin = sin_ref[rows, :]
        for h in range(GLA_HEADS):
            lanes = slice(h * GLA_DIM, (h + 1) * GLA_DIM)
            q = q_ref[0, rows, lanes]
            k = k_ref[0, rows, lanes]
            vb = v_ref[0, rows, lanes].astype(BF16)
            q = q * cos + pltpu.roll(q, GLA_DIM // 2, 1) * sin
            k = (k * cos + pltpu.roll(k, GLA_DIM // 2, 1) * sin) * (GLA_DIM ** -0.5)
            st = st_scr[h]
            scores = _dot_nt(q.astype(BF16), k.astype(BF16)) * dm_ref[h]
            out = _dot_nt((q * eq_ref[h]).astype(BF16), st.astype(BF16)) + _dot(scores.astype(BF16), vb)
            if has_prev:
                out = out + prev_ref[0, rows, lanes]
            o_ref[0, rows, lanes] = out
            st_scr[h] = st * el_ref[h] + _dot_tn(vb, (k * ek_ref[h]).astype(BF16))
        return carry

    lax.fori_loop(0, n_ck, chunk, 0)


def _ret_tables(rev, cs):
    gamma = jnp.log1p(-jnp.exp2(-(RET_DECAY_EXP[1 if rev else 0] + jnp.arange(GLA_HEADS, dtype=F32))))
    t = jnp.arange(cs, dtype=F32)
    steps = (cs - t) if rev else (t + 1.0)
    cum = gamma[:, None] * steps[None, :]
    last = gamma * cs
    eq = jnp.broadcast_to(jnp.exp(cum)[:, :, None], (GLA_HEADS, cs, GLA_DIM))
    ek = jnp.broadcast_to(jnp.exp(last[:, None] - cum)[:, :, None], (GLA_HEADS, cs, GLA_DIM))
    rel = cum[:, :, None] - cum[:, None, :]
    dm = jnp.where(_tri_mask(cs, rev)[None], jnp.exp(jnp.minimum(rel, 0.0)), 0.0)
    el = jnp.broadcast_to(jnp.exp(last)[:, None, None], (GLA_HEADS, 1, GLA_DIM))
    return eq, ek, dm, el


def _rotary_tables(n_lat, n_ctx):
    rows = n_lat // GRID_W
    row = jnp.repeat(jnp.arange(rows, dtype=F32), GRID_W)
    col = jnp.tile(jnp.arange(GRID_W, dtype=F32), rows)
    quarter = GLA_DIM // 4
    inv_freq = ROPE_BASE ** (-jnp.arange(quarter, dtype=F32) / quarter)
    ang = jnp.concatenate([row[:, None] * inv_freq, col[:, None] * inv_freq], axis=-1)
    cos, sin = jnp.cos(ang), jnp.sin(ang)
    cos_l = jnp.concatenate([cos, cos], axis=-1)
    sin_l = jnp.concatenate([-sin, sin], axis=-1)
    cos_t = jnp.concatenate([cos_l, jnp.ones((n_ctx, GLA_DIM), F32)], axis=0)
    sin_t = jnp.concatenate([sin_l, jnp.zeros((n_ctx, GLA_DIM), F32)], axis=0)
    return cos_t, sin_t


def _ret_direction(bm, cos_t, sin_t, prev, rev, n_lat, tq):
    bsz, ltot, _ = bm.shape
    n_chunks, cidx, _ = _gla_specs(ltot, tq, n_lat, rev)
    has_prev = prev is not None
    eq, ek, dm, el = _ret_tables(rev, min(tq, RET_CHUNK))
    gw = GROUP_WIDTH
    part = lambda p: pl.BlockSpec((1, tq, gw), lambda b, s: (b, cidx(s), p))
    whole = lambda a: pl.BlockSpec(a.shape, lambda b, s: (0, 0, 0))
    pos = pl.BlockSpec((tq, GLA_DIM), lambda b, s: (cidx(s), 0))
    in_specs = [part(5), part(6), part(7), pos, pos, whole(eq), whole(ek), whole(dm), whole(el)]
    args = [bm, bm, bm, cos_t, sin_t, eq, ek, dm, el]
    out_spec = pl.BlockSpec((1, tq, gw), lambda b, s: (b, cidx(s), 0))
    if has_prev:
        in_specs.append(out_spec)
        args.append(prev)
    kern = functools.partial(_ret_kernel, rev=rev, tq=tq, has_prev=has_prev)
    return pl.pallas_call(
        kern,
        out_shape=jax.ShapeDtypeStruct((bsz, ltot, gw), F32),
        grid=(bsz, n_chunks),
        in_specs=in_specs,
        out_specs=out_spec,
        scratch_shapes=[pltpu.VMEM((GLA_HEADS, GLA_DIM, GLA_DIM), F32)],
        compiler_params=_params("arbitrary", "arbitrary"),
        name="ret_rev" if rev else "ret_fwd",
    )(*args)


def _head_rms(o):
    parts = []
    for h in range(GLA_HEADS):
        oh = o[:, h * GLA_DIM:(h + 1) * GLA_DIM]
        parts.append(oh * lax.rsqrt(jnp.mean(oh * oh, axis=-1, keepdims=True) + EPS))
    return jnp.concatenate(parts, axis=-1)


def _mix_kernel(u_ref, lg_ref, ys_ref, hs_ref, hg_ref, rg_ref, oh_ref, or_ref, d_ref, gw_ref, gb_ref, o_ref):
    gw = GROUP_WIDTH
    y = jax.nn.gelu(d_ref[...] * u_ref[...] + ys_ref[...])
    a = y * jax.nn.sigmoid(_dot(y.astype(BF16), gw_ref[...]) + gb_ref[...])
    o_ref[0, :, 0:gw] = a.astype(BF16)
    o_ref[0, :, gw:2 * gw] = (hs_ref[...] * jax.nn.gelu(lg_ref[...])).astype(BF16)
    g = hg_ref[0]
    o_ref[0, :, 2 * gw:3 * gw] = (_head_rms(oh_ref[0]) * (g * jax.nn.sigmoid(g))).astype(BF16)
    g = rg_ref[0]
    o_ref[0, :, 3 * gw:4 * gw] = (_head_rms(or_ref[0]) * (g * jax.nn.sigmoid(g))).astype(BF16)


def _mix_epilogue(tm2, ys2, hs2, bm, o_h, o_r, s5_d, glu_w, glu_b, n_rows, tm):
    bsz, ltot, _ = bm.shape
    gw = GROUP_WIDTH
    tmaj = lambda p, k: pl.BlockSpec((tm, gw), lambda b, i: (i, b * k + p))
    bmaj = lambda p: pl.BlockSpec((1, tm, gw), lambda b, i: (b, i, p))
    full2 = lambda b, i: (0, 0)
    return pl.pallas_call(
        _mix_kernel,
        out_shape=jax.ShapeDtypeStruct((bsz, ltot, 4 * gw), BF16),
        grid=(bsz, n_rows // tm),
        in_specs=[tmaj(0, N_TM_PARTS), tmaj(2, N_TM_PARTS), tmaj(0, 1), tmaj(0, 1),
                  bmaj(4), bmaj(8), bmaj(0), bmaj(0),
                  pl.BlockSpec((1, gw), full2), pl.BlockSpec((gw, gw), full2), pl.BlockSpec((1, gw), full2)],
        out_specs=pl.BlockSpec((1, tm, 4 * gw), lambda b, i: (b, i, 0)),
        compiler_params=_params("arbitrary", "arbitrary"),
        name="mix_epilogue",
    )(tm2, tm2, ys2, hs2, bm, bm, o_h, o_r, s5_d, glu_w, glu_b)


def _gate_rows(modb_ref, modc_ref, idx, row0, tm, n_lat):
    row = row0 + lax.broadcasted_iota(I32, (tm, 1), 0)
    return jnp.where(row >= n_lat, modc_ref[0, idx:idx + 1, :], modb_ref[0, idx:idx + 1, :])


def _outproj_kernel(a_ref, w_ref, x_ref, modb_ref, modc_ref, o_ref, *, tm, n_lat):
    gate = _gate_rows(modb_ref, modc_ref, 2, pl.program_id(1) * tm, tm, n_lat)
    o_ref[0] = x_ref[0] + gate * _dot(a_ref[0], w_ref[0])


def _out_projection(mix, w_bf16, layer, xs, modv, n_rows, n_lat, tm, tn):
    bsz, ltot, d = xs.shape
    k = mix.shape[-1]
    kern = functools.partial(_outproj_kernel, tm=tm, n_lat=n_lat)
    return pl.pallas_call(
        kern,
        out_shape=jax.ShapeDtypeStruct((bsz, ltot, d), F32),
        grid=(bsz, n_rows // tm, d // tn),
        in_specs=[pl.BlockSpec((1, tm, k), lambda b, i, j: (b, i, 0)),
                  pl.BlockSpec((1, k, tn), lambda b, i, j: (layer, 0, j)),
                  pl.BlockSpec((1, tm, tn), lambda b, i, j: (b, i, j)),
                  pl.BlockSpec((1, 6, tn), lambda b, i, j: (b, 0, j)),
                  pl.BlockSpec((1, 6, tn), lambda b, i, j: (bsz, 0, j))],
        out_specs=pl.BlockSpec((1, tm, tn), lambda b, i, j: (b, i, j)),
        compiler_params=_params("arbitrary", "arbitrary", "arbitrary"),
        name="out_projection",
    )(mix, w_bf16, xs, modv, modv)


def _first_max4(v):
    m1 = jnp.maximum(jnp.maximum(v[0], v[1]), jnp.maximum(v[2], v[3]))
    i1 = jnp.where(v[0] == m1, 0, jnp.where(v[1] == m1, 1, jnp.where(v[2] == m1, 2, 3)))
    rest = [jnp.where(i1 == j, -jnp.inf, v[j]) for j in range(4)]
    m2 = jnp.maximum(jnp.maximum(rest[0], rest[1]), jnp.maximum(rest[2], rest[3]))
    i2 = jnp.where(rest[0] == m2, 0, jnp.where(rest[1] == m2, 1, jnp.where(rest[2] == m2, 2, 3)))
    return m1, i1, m2, i2


def _router_kernel(x_ref, g_ref, modb_ref, modc_ref, rw_ref, rb_ref, tri_ref,
                   h_ref, ids_ref, gates_ref, ranks_ref, counts_ref, base_scr, *, tm, n_lat):
    first = jnp.logical_and(pl.program_id(0) == 0, pl.program_id(1) == 0)

    @pl.when(first)
    def _():
        base_scr[...] = jnp.zeros_like(base_scr)

    h = _norm_modulate(x_ref[0], g_ref[...], modb_ref, modc_ref, pl.program_id(1) * tm, n_lat, 3, 4)
    h_ref[0] = h.astype(BF16)
    logits = lax.dot_general(rw_ref[...], h, (((1,), (1,)), ((), ())), preferred_element_type=F32,
                             precision=lax.Precision.HIGHEST)
    scores = jax.nn.sigmoid(logits)
    biased = scores + rb_ref[...]
    tops = []
    for gidx in range(N_EXPERT_GROUPS):
        rows = [biased[gidx * EXPERTS_PER_GROUP + j:gidx * EXPERTS_PER_GROUP + j + 1, :]
                for j in range(EXPERTS_PER_GROUP)]
        tops.append(_first_max4(rows))
    gs = [t[0] + t[2] for t in tops]
    gmax = jnp.maximum(jnp.maximum(gs[0], gs[1]), jnp.maximum(gs[2], gs[3]))
    best = jnp.where(gs[0] == gmax, 0, jnp.where(gs[1] == gmax, 1, jnp.where(gs[2] == gmax, 2, 3)))
    e1 = jnp.zeros_like(best)
    e2 = jnp.zeros_like(best)
    for gidx in range(N_EXPERT_GROUPS):
        e1 = jnp.where(best == gidx, gidx * EXPERTS_PER_GROUP + tops[gidx][1], e1)
        e2 = jnp.where(best == gidx, gidx * EXPERTS_PER_GROUP + tops[gidx][3], e2)
    eid = lax.broadcasted_iota(I32, (N_EXPERTS, tm), 0)
    sel1 = eid == e1
    sel2 = eid == e2
    w1 = jnp.sum(jnp.where(sel1, scores, 0.0), axis=0, keepdims=True)
    w2 = jnp.sum(jnp.where(sel2, scores, 0.0), axis=0, keepdims=True)
    wsum = w1 + w2
    onehot = jnp.where(jnp.logical_or(sel1, sel2), 1.0, 0.0)
    pos = base_scr[...] + _dot(onehot.astype(BF16), tri_ref[...])
    r1 = jnp.sum(jnp.where(sel1, pos, 0.0), axis=0, keepdims=True)
    r2 = jnp.sum(jnp.where(sel2, pos, 0.0), axis=0, keepdims=True)
    ids_ref[0] = jnp.concatenate([e1, e2], axis=0)
    gates_ref[0] = jnp.concatenate([w1 / wsum, w2 / wsum], axis=0)
    ranks_ref[0] = jnp.concatenate([r1, r2], axis=0).astype(I32)
    base = base_scr[...] + jnp.sum(onehot, axis=1, keepdims=True)
    base_scr[...] = base
    counts_ref[...] = jnp.broadcast_to(base, counts_ref.shape)


def _router(xs, gain, modv, rw_t, rbias, n_rows, n_lat, tm):
    bsz, ltot, d = xs.shape
    tri = (lax.broadcasted_iota(I32, (tm, tm), 0) < lax.broadcasted_iota(I32, (tm, tm), 1)).astype(BF16)
    kern = functools.partial(_router_kernel, tm=tm, n_lat=n_lat)
    small = lambda dt: jax.ShapeDtypeStruct((bsz, 2, n_rows), dt)
    small_spec = pl.BlockSpec((1, 2, tm), lambda b, i: (b, 0, i))
    return pl.pallas_call(
        kern,
        out_shape=(jax.ShapeDtypeStruct((bsz, ltot, d), BF16), small(I32), small(F32), small(I32),
                   jax.ShapeDtypeStruct((N_EXPERTS, 128), F32)),
        grid=(bsz, n_rows // tm),
        in_specs=[pl.BlockSpec((1, tm, d), lambda b, i: (b, i, 0)),
                  pl.BlockSpec((1, d), lambda b, i: (0, 0)),
                  pl.BlockSpec((1, 6, d), lambda b, i: (b, 0, 0)),
                  pl.BlockSpec((1, 6, d), lambda b, i: (bsz, 0, 0)),
                  pl.BlockSpec((N_EXPERTS, d), lambda b, i: (0, 0)),
                  pl.BlockSpec((N_EXPERTS, 1), lambda b, i: (0, 0)),
                  pl.BlockSpec((tm, tm), lambda b, i: (0, 0))],
        out_specs=(pl.BlockSpec((1, tm, d), lambda b, i: (b, i, 0)), small_spec, small_spec, small_spec,
                   pl.BlockSpec((N_EXPERTS, 128), lambda b, i: (0, 0))),
        scratch_shapes=[pltpu.VMEM((N_EXPERTS, 1), F32)],
        compiler_params=_params("arbitrary", "arbitrary"),
        name="moe_router",
    )(xs, gain, modv, modv, rw_t, rbias, tri)


def _expert_ffn_kernel(te_ref, nt_ref, x_ref, wg_ref, wu_ref, wd_ref, o_ref):
    live = pl.program_id(0) < nt_ref[0]

    @pl.when(live)
    def _():
        x = x_ref[...]
        g = _dot(x, wg_ref[0])
        u = _dot(x, wu_ref[0])
        he = (g * jax.nn.sigmoid(g)) * u
        o_ref[...] = _dot(he.astype(BF16), wd_ref[0]).astype(o_ref.dtype)

    @pl.when(jnp.logical_not(live))
    def _():
        o_ref[...] = jnp.zeros_like(o_ref)


def _expert_ffn(tile_expert, n_tiles, x_sorted, wg, wu, wd, tm):
    p, d = x_sorted.shape
    f = wg.shape[-1]
    tile = lambda i, te, nt: i
    expert = lambda i, te, nt: te[jnp.minimum(i, nt[0] - 1)]
    grid_spec = pltpu.PrefetchScalarGridSpec(
        num_scalar_prefetch=2,
        grid=(p // tm,),
        in_specs=[pl.BlockSpec((tm, d), lambda i, te, nt: (tile(i, te, nt), 0)),
                  pl.BlockSpec((1, d, f), lambda i, te, nt: (expert(i, te, nt), 0, 0)),
                  pl.BlockSpec((1, d, f), lambda i, te, nt: (expert(i, te, nt), 0, 0)),
                  pl.BlockSpec((1, f, d), lambda i, te, nt: (expert(i, te, nt), 0, 0))],
        out_specs=pl.BlockSpec((tm, d), lambda i, te, nt: (tile(i, te, nt), 0)),
    )
    return pl.pallas_call(
        _expert_ffn_kernel,
        out_shape=jax.ShapeDtypeStruct((p, d), BF16),
        grid_spec=grid_spec,
        compiler_params=_params("arbitrary"),
        name="moe_expert_ffn",
    )(tile_expert, n_tiles, x_sorted, wg, wu, wd)


def _moe_residual_kernel(x_ref, y0_ref, y1_ref, gates_ref, modb_ref, modc_ref, g_ref, o_ref,
                         *, tm, n_lat, final_norm):
    gates = gates_ref[0]
    y = gates[:, 0:1] * y0_ref[0, 0].astype(F32) + gates[:, 1:2] * y1_ref[0, 0].astype(F32)
    gate = _gate_rows(modb_ref, modc_ref, 5, pl.program_id(1) * tm, tm, n_lat)
    x = x_ref[0] + gate * y
    if final_norm:
        x = x * lax.rsqrt(jnp.mean(x * x, axis=-1, keepdims=True) + EPS) * g_ref[...]
    o_ref[0] = x


def _moe_residual(xs, y_pair, gates_t, modv, final_g, n_rows, n_lat, tm, final_norm):
    bsz, ltot, d = xs.shape
    kern = functools.partial(_moe_residual_kernel, tm=tm, n_lat=n_lat, final_norm=final_norm)
    rows_out = n_rows if final_norm else ltot
    return pl.pallas_call(
        kern,
        out_shape=jax.ShapeDtypeStruct((bsz, rows_out, d), F32),
        grid=(bsz, n_rows // tm),
        in_specs=[pl.BlockSpec((1, tm, d), lambda b, i: (b, i, 0)),
                  pl.BlockSpec((1, 1, tm, d), lambda b, i: (0, b, i, 0)),
                  pl.BlockSpec((1, 1, tm, d), lambda b, i: (1, b, i, 0)),
                  pl.BlockSpec((1, tm, 2), lambda b, i: (b, i, 0)),
                  pl.BlockSpec((1, 6, d), lambda b, i: (b, 0, 0)),
                  pl.BlockSpec((1, 6, d), lambda b, i: (bsz, 0, 0)),
                  pl.BlockSpec((1, d), lambda b, i: (0, 0))],
        out_specs=pl.BlockSpec((1, tm, d), lambda b, i: (b, i, 0)),
        compiler_params=_params("arbitrary", "arbitrary"),
        name="moe_residual",
    )(xs, y_pair, y_pair, gates_t, modv, modv, final_g)


def _moe_layer(xs, gain, modv, rw_t, rbias, wg, wu, wd, expert_base, final_g, n_rows, n_lat, tm_tok, tm_res, tm_exp,
               final_norm):
    bsz, ltot, d = xs.shape
    h, ids, gates, ranks, counts = _router(xs, gain, modv, rw_t, rbias, n_rows, n_lat, tm_tok)
    counts = counts[:, 0].astype(I32)
    padded = ((counts + tm_exp - 1) // tm_exp) * tm_exp
    ends = jnp.cumsum(padded)
    offsets = ends - padded
    onehot = ids[..., None] == jnp.arange(N_EXPERTS, dtype=I32)
    dest = jnp.sum(jnp.where(onehot, offsets, 0), axis=-1) + ranks
    n_slots = bsz * 2 * n_rows
    p_rows = ((n_slots + N_EXPERTS * (tm_exp - 1)) + tm_exp - 1) // tm_exp * tm_exp
    tok = (jnp.arange(bsz, dtype=I32)[:, None, None] * ltot + jnp.arange(n_rows, dtype=I32)[None, None, :])
    tok = jnp.broadcast_to(tok, dest.shape)
    fill_slot = jnp.arange(p_rows, dtype=I32) % n_slots
    filler = (fill_slot // (2 * n_rows)) * ltot + fill_slot % n_rows
    src = filler.at[dest.reshape(-1)].set(tok.reshape(-1), mode="promise_in_bounds", unique_indices=True)
    tile_start = jnp.arange(p_rows // tm_exp, dtype=I32) * tm_exp
    tile_expert = jnp.minimum(jnp.sum(tile_start[:, None] >= ends[None, :], axis=1), N_EXPERTS - 1).astype(I32)
    n_tiles = (ends[-1:] // tm_exp).astype(I32)
    x_sorted = h.reshape(bsz * ltot, d).at[src].get(mode="promise_in_bounds")
    y_sorted = _expert_ffn(tile_expert + expert_base, n_tiles, x_sorted, wg, wu, wd, tm_exp)
    y_pair = y_sorted.at[dest.transpose(1, 0, 2).reshape(-1)].get(mode="promise_in_bounds")
    y_pair = y_pair.reshape(2, bsz, n_rows, d)
    return _moe_residual(xs, y_pair, gates.transpose(0, 2, 1), modv, final_g, n_rows, n_lat, tm_res, final_norm)


def _row_tile(n, target):
    t = min(n, target)
    while n % t or t % 8:
        t -= 8
    return t


def kernel(x, c, ctx, c_ctx, norm_mix_g, norm_ffn_g, w_mod, b_mod, w_in, w_out, s5_lam_re, s5_lam_im, s5_log_dt, s5_b_re, s5_b_im, s5_c_re, s5_c_im, s5_d, s5_glu_w, s5_glu_b, lru_conv_w, lru_conv_b, lru_wa, lru_ba, lru_wx, lru_bx, lru_lam, hgrn_lb_logits, router_w, router_bias, moe_w_gate, moe_w_up, moe_w_down, final_norm_g):
    bsz, n_lat, d = x.shape
    n_ctx = ctx.shape[1]
    ltot = n_lat + n_ctx
    depth = w_in.shape[0]
    gw = GROUP_WIDTH
    assert bsz % 8 == 0 and bsz + 1 <= MOD_ROWS
    assert n_lat % GLA_CHUNK == 0 and n_ctx % GLA_CHUNK == 0 and n_lat % GRID_W == 0 and n_lat % NORM_SLAB == 0

    tq_scan = math.gcd(math.gcd(n_lat, n_ctx), 64)
    tq_gla = math.gcd(math.gcd(n_lat, n_ctx), 256)
    tm_all = _row_tile(ltot, 768)
    tm_lat = _row_tile(n_lat, 1024)

    xs = jnp.concatenate([x, ctx.astype(x.dtype)], axis=1)
    cv = jnp.zeros((MOD_ROWS, d), F32).at[:bsz].set(c).at[bsz].set(c_ctx)
    mod_all = _mod_vectors(cv, w_mod, b_mod).reshape(depth, MOD_ROWS, 6, d)
    cos_t, sin_t = _rotary_tables(n_lat, n_ctx)
    rw_t = router_w.astype(F32).T
    rbias = router_bias.astype(F32).reshape(N_EXPERTS, 1)
    lb_cum = [jnp.cumsum(jax.nn.softmax(hgrn_lb_logits[dr].astype(F32), axis=0), axis=0) for dr in (0, 1)]
    w_in_b, w_out_b = _cast_bf16(w_in), _cast_bf16(w_out)
    n_exp, _, d_ff = moe_w_gate.shape[1:]
    wg_b = _cast_bf16(moe_w_gate.reshape(depth * n_exp, d, d_ff))
    wu_b = _cast_bf16(moe_w_up.reshape(depth * n_exp, d, d_ff))
    wd_b = _cast_bf16(moe_w_down.reshape(depth * n_exp, d_ff, d))

    for layer in range(depth):
        ctx_out = layer < depth - 1
        n_rows = ltot if ctx_out else n_lat
        tm_tok = tm_all if ctx_out else tm_lat
        tm_mix = _row_tile(n_rows, 512)
        modv = mod_all[layer]

        tm2, bm = _in_projection(xs, norm_mix_g[layer].reshape(1, d), modv, w_in_b, layer, n_lat, tm_all)
        tm3 = tm2.reshape(ltot, bsz, N_TM_PARTS * gw)

        ys = hs = o_h = o_r = None
        for dr in (0, 1):
            rev = dr == 1
            bw, cw, atab = _s5_tables(s5_lam_re[layer, dr], s5_lam_im[layer, dr], s5_log_dt[layer, dr],
                                      s5_b_re[layer, dr], s5_b_im[layer, dr], s5_c_re[layer, dr], s5_c_im[layer, dr])
            ys = _s5_direction(tm3, bw, cw, atab, ys, rev, n_lat, tq_scan)
            sp = jax.nn.softplus(-lru_lam[layer, dr].astype(F32)).reshape(1, gw)
            hs = _lru_direction(tm3, lru_conv_w[layer].astype(F32), lru_conv_b[layer].astype(F32).reshape(1, gw),
                                _block_diag(lru_wa[layer, dr]).astype(BF16), _block_diag(lru_wx[layer, dr]).astype(BF16),
                                lru_ba[layer, dr].astype(F32).reshape(1, gw), lru_bx[layer, dr].astype(F32).reshape(1, gw),
                                sp, hs, rev, n_lat, tq_scan)
            lb = lb_cum[dr][layer] - lb_cum[dr][0]
            o_h = _hgrn_direction(bm, lb, o_h, rev, n_lat, tq_gla)
            o_r = _ret_direction(bm, cos_t, sin_t, o_r, rev, n_lat, tq_gla)

        mix = _mix_epilogue(tm2, ys.reshape(ltot, bsz * gw), hs.reshape(ltot, bsz * gw), bm, o_h, o_r,
                            s5_d[layer].astype(F32).reshape(1, gw), s5_glu_w[layer].astype(BF16),
                            s5_glu_b[layer].astype(F32).reshape(1, gw), n_rows, tm_mix)
        xs = _out_projection(mix, w_out_b, layer, xs, modv, n_rows, n_lat, tm_tok, 1024)
        xs = _moe_layer(xs, norm_ffn_g[layer].reshape(1, d), modv, rw_t, rbias, wg_b, wu_b, wd_b, layer * n_exp,
                        final_norm_g.reshape(1, d), n_rows, n_lat, tm_tok, tm_mix, 512, not ctx_out)
    return xs
```

```python
import functools
import math

import jax
import jax.numpy as jnp
from jax import lax
from jax.experimental import pallas as pl
from jax.experimental.pallas import tpu as pltpu

F32 = jnp.float32
BF16 = jnp.bfloat16
I32 = jnp.int32

GROUP_WIDTH = 512
N_IN_PARTS = 12
N_TM_PARTS = 3
N_BM_PARTS = N_IN_PARTS - N_TM_PARTS
S5_CH = 16
S5_GROUPS = 32
S5_STATE = 64
S5_LANE_BLOCKS = 4
LRU_HEADS = 8
LRU_CONV = 4
LRU_C = 8.0
GLA_HEADS = 4
GLA_DIM = 128
GLA_CHUNK = 64
GLA_SUB = 16
HGRN_FAST_WINDOW = 32
HGRN_SAFE_LOG_DECAY = 80.0
HGRN_SAFE_QUERY = 1.0e3
HGRN_HEADS_PER_STEP = 4
RET_CHUNK = 256
RET_DECAY_EXP = (5.0, 5.5)
ROPE_BASE = 10000.0
GRID_W = 64
N_EXPERTS = 16
N_EXPERT_GROUPS = 4
EXPERTS_PER_GROUP = 4
D_FF_EXPERT = 1024
EPS = 1e-6
NORM_SLAB = 16
MOD_ROWS = 24

VMEM_LIMIT_BYTES = 56 * 1024 * 1024


def _params(*semantics):
    return pltpu.CompilerParams(dimension_semantics=semantics, vmem_limit_bytes=VMEM_LIMIT_BYTES)


def _dot(a, b):
    return jnp.dot(a, b, preferred_element_type=F32)


def _dot_nt(a, b):
    return lax.dot_general(a, b, (((1,), (1,)), ((), ())), preferred_element_type=F32)


def _dot_tn(a, b):
    return lax.dot_general(a, b, (((0,), (0,)), ((), ())), preferred_element_type=F32)


def _scan_chunk_index(step, n_lat_chunks, n_chunks, rev):
    if rev:
        return n_chunks - 1 - step
    return (step + n_lat_chunks) % n_chunks


def _cast_kernel(x_ref, o_ref):
    o_ref[...] = x_ref[...].astype(o_ref.dtype)


def _cast_bf16(w, start=0, count=None):
    n, r, c = w.shape
    count = n if count is None else count
    tc = min(c, 1024)
    return pl.pallas_call(
        _cast_kernel,
        out_shape=jax.ShapeDtypeStruct((count, r, c), BF16),
        grid=(count, c // tc),
        in_specs=[pl.BlockSpec((1, r, tc), lambda i, j: (start + i, 0, j))],
        out_specs=pl.BlockSpec((1, r, tc), lambda i, j: (i, 0, j)),
        compiler_params=_params("arbitrary", "arbitrary"),
        name="cast_bf16",
    )(w)


def _mod_kernel(cv_ref, w_ref, b_ref, o_ref):
    cv = cv_ref[...]
    s = cv * jax.nn.sigmoid(cv)
    o_ref[0] = _dot(s.astype(BF16), w_ref[0].astype(BF16)) + b_ref[0]


def _mod_vectors(cv, w_mod, b_mod):
    nl, d, n6 = w_mod.shape
    tn = 1024
    return pl.pallas_call(
        _mod_kernel,
        out_shape=jax.ShapeDtypeStruct((nl, MOD_ROWS, n6), F32),
        grid=(nl, n6 // tn),
        in_specs=[pl.BlockSpec((MOD_ROWS, d), lambda l, j: (0, 0)),
                  pl.BlockSpec((1, d, tn), lambda l, j: (l, 0, j)),
                  pl.BlockSpec((1, 1, tn), lambda l, j: (l, 0, j))],
        out_specs=pl.BlockSpec((1, MOD_ROWS, tn), lambda l, j: (l, 0, j)),
        compiler_params=_params("arbitrary", "arbitrary"),
        name="mod_vectors",
    )(cv, w_mod, b_mod.reshape(nl, 1, n6))


def _norm_modulate(x, g, modb_ref, modc_ref, row0, n_lat, shift_idx, scale_idx):
    tm = x.shape[0]
    ms = jnp.mean(x * x, axis=-1, keepdims=True)
    y = x * lax.rsqrt(ms + EPS) * g
    row = row0 + lax.broadcasted_iota(I32, (tm, 1), 0)
    is_ctx = row >= n_lat
    shift = jnp.where(is_ctx, modc_ref[0, shift_idx:shift_idx + 1, :], modb_ref[0, shift_idx:shift_idx + 1, :])
    scale = jnp.where(is_ctx, modc_ref[0, scale_idx:scale_idx + 1, :], modb_ref[0, scale_idx:scale_idx + 1, :])
    return y * (1.0 + scale) + shift


def _inproj_kernel(x_ref, g_ref, modb_ref, modc_ref, w_ref, otm_ref, obm_ref, h_scr, aff_scr, *, tm, n_lat):
    i = pl.program_id(1)
    j = pl.program_id(2)

    @pl.when(j == 0)
    def _():
        d = x_ref.shape[-1]
        for kind, mod_ref in enumerate((modb_ref, modc_ref)):
            aff_scr[kind, 0] = jnp.broadcast_to(g_ref[...] * (1.0 + mod_ref[0, 1:2, :]), (NORM_SLAB, d))
            aff_scr[kind, 1] = jnp.broadcast_to(mod_ref[0, 0:1, :], (NORM_SLAB, d))

        def slab(s, carry):
            r0 = pl.multiple_of(s * NORM_SLAB, NORM_SLAB)
            kind = (i * tm + r0 >= n_lat).astype(I32)
            x = x_ref[0, pl.ds(r0, NORM_SLAB), :]
            inv = lax.rsqrt(jnp.mean(x * x, axis=-1, keepdims=True) + EPS)
            h_scr[pl.ds(r0, NORM_SLAB), :] = (x * inv * aff_scr[kind, 0] + aff_scr[kind, 1]).astype(BF16)
            return carry

        lax.fori_loop(0, tm // NORM_SLAB, slab, 0, unroll=8)

    r = _dot(h_scr[...], w_ref[0])

    @pl.when(j == 0)
    def _():
        otm_ref[...] = r

    @pl.when(j > 0)
    def _():
        obm_ref[0] = r


def _in_projection(xs, gain, modv, w_bf16, layer, n_lat, tm):
    bsz, ltot, d = xs.shape
    gw = GROUP_WIDTH
    kern = functools.partial(_inproj_kernel, tm=tm, n_lat=n_lat)
    tn = N_TM_PARTS * gw
    return pl.pallas_call(
        kern,
        out_shape=(jax.ShapeDtypeStruct((ltot, bsz * tn), F32),
                   jax.ShapeDtypeStruct((bsz, ltot, N_BM_PARTS * gw), F32)),
        grid=(bsz, ltot // tm, N_IN_PARTS * gw // tn),
        in_specs=[pl.BlockSpec((1, tm, d), lambda b, i, j: (b, i, 0)),
                  pl.BlockSpec((1, d), lambda b, i, j: (0, 0)),
                  pl.BlockSpec((1, 6, d), lambda b, i, j: (b, 0, 0)),
                  pl.BlockSpec((1, 6, d), lambda b, i, j: (bsz, 0, 0)),
                  pl.BlockSpec((1, d, tn), lambda b, i, j: (layer, 0, j))],
        out_specs=(pl.BlockSpec((tm, tn), lambda b, i, j: (i, b)),
                   pl.BlockSpec((1, tm, tn), lambda b, i, j: (b, i, jnp.maximum(j - 1, 0)))),
        scratch_shapes=[pltpu.VMEM((tm, d), BF16), pltpu.VMEM((2, 2, NORM_SLAB, d), F32)],
        compiler_params=_params("arbitrary", "arbitrary", "arbitrary"),
        name="in_projection",
    )(xs, gain, modv, modv, w_bf16)


def _s5_kernel(*refs, rev, tq, nb, has_prev):
    if has_prev:
        u_ref, bw_ref, cw_ref, a_ref, prev_ref, y_ref, z_scr, h_scr = refs
    else:
        u_ref, bw_ref, cw_ref, a_ref, y_ref, z_scr, h_scr = refs
        prev_ref = None
    rows = tq * nb
    half = S5_STATE * 8

    @pl.when(pl.program_id(0) == 0)
    def _():
        h_scr[...] = jnp.zeros_like(h_scr)

    for gb in range(S5_LANE_BLOCKS):
        lanes = slice(gb * 128, (gb + 1) * 128)
        ub = u_ref[:, :, lanes].reshape(rows, 128).astype(BF16)
        z_scr[...] = _dot(ub, bw_ref[gb])
        ar = jnp.broadcast_to(a_ref[gb, 0], (nb, half))
        ai = jnp.broadcast_to(a_ref[gb, 1], (nb, half))

        def step(i, carry):
            hr, hi = carry
            t = (tq - 1 - i) if rev else i
            r0 = pl.multiple_of(t * nb, nb)
            zr = z_scr[pl.ds(r0, nb), 0:half]
            zi = z_scr[pl.ds(r0, nb), half:2 * half]
            nr = ar * hr - ai * hi + zr
            ni = ar * hi + ai * hr + zi
            z_scr[pl.ds(r0, nb), 0:half] = nr
            z_scr[pl.ds(r0, nb), half:2 * half] = ni
            return nr, ni

        hr, hi = lax.fori_loop(0, tq, step, (h_scr[gb, 0], h_scr[gb, 1]), unroll=4)
        h_scr[gb, 0] = hr
        h_scr[gb, 1] = hi
        yb = _dot(z_scr[...].astype(BF16), cw_ref[gb]).reshape(tq, nb, 128)
        if has_prev:
            yb = yb + prev_ref[:, :, lanes]
        y_ref[:, :, lanes] = yb


def _s5_direction(tm3, bw, cw, atab, prev, rev, n_lat, tq):
    ltot, nb, _ = tm3.shape
    gw = GROUP_WIDTH
    n_chunks = ltot // tq
    n_lat_chunks = n_lat // tq
    cidx = lambda s: _scan_chunk_index(s, n_lat_chunks, n_chunks, rev)
    has_prev = prev is not None
    in_specs = [pl.BlockSpec((tq, nb, gw), lambda s: (cidx(s), 0, 0)),
                pl.BlockSpec(bw.shape, lambda s: (0, 0, 0)),
                pl.BlockSpec(cw.shape, lambda s: (0, 0, 0)),
                pl.BlockSpec(atab.shape, lambda s: (0, 0, 0, 0))]
    args = [tm3, bw, cw, atab]
    if has_prev:
        in_specs.append(pl.BlockSpec((tq, nb, gw), lambda s: (cidx(s), 0, 0)))
        args.append(prev)
    kern = functools.partial(_s5_kernel, rev=rev, tq=tq, nb=nb, has_prev=has_prev)
    return pl.pallas_call(
        kern,
        out_shape=jax.ShapeDtypeStruct((ltot, nb, gw), F32),
        grid=(n_chunks,),
        in_specs=in_specs,
        out_specs=pl.BlockSpec((tq, nb, gw), lambda s: (cidx(s), 0, 0)),
        scratch_shapes=[pltpu.VMEM((tq * nb, 2 * S5_STATE * 8), F32),
                        pltpu.VMEM((S5_LANE_BLOCKS, 2, nb, S5_STATE * 8), F32)],
        compiler_params=_params("arbitrary"),
        name="s5_rev" if rev else "s5_fwd",
    )(*args)


def _block_diag(blocks):
    n, r, c = blocks.shape
    eye = jnp.eye(n, dtype=blocks.dtype)
    return (eye[:, None, :, None] * blocks[:, :, None, :]).reshape(n * r, n * c)


def _s5_tables(lam_re, lam_im, log_dt, b_re, b_im, c_re, c_im):
    dt = jnp.exp(log_dt.astype(F32))[:, None]
    lr, li = lam_re.astype(F32), lam_im.astype(F32)
    mag = jnp.exp(lr * dt)
    abar_re, abar_im = mag * jnp.cos(li * dt), mag * jnp.sin(li * dt)
    den = lr * lr + li * li
    zr = abar_re - 1.0
    w_re = (zr * lr + abar_im * li) / den
    w_im = (abar_im * lr - zr * li) / den
    bre, bim = b_re.astype(F32), b_im.astype(F32)
    bw_re = w_re[:, None, :] * bre - w_im[:, None, :] * bim
    bw_im = w_re[:, None, :] * bim + w_im[:, None, :] * bre
    gpb = S5_GROUPS // S5_LANE_BLOCKS
    bws, cws, atabs = [], [], []
    for gb in range(S5_LANE_BLOCKS):
        sl = slice(gb * gpb, (gb + 1) * gpb)
        bws.append(jnp.concatenate([_block_diag(bw_re[sl]), _block_diag(bw_im[sl])], axis=1))
        cws.append(jnp.concatenate([_block_diag(c_re[sl].astype(F32)), -_block_diag(c_im[sl].astype(F32))], axis=0))
        atabs.append(jnp.stack([abar_re[sl].reshape(1, -1), abar_im[sl].reshape(1, -1)]))
    return jnp.stack(bws).astype(BF16), jnp.stack(cws).astype(BF16), jnp.stack(atabs)


def _lru_kernel(*refs, rev, tq, nb, n_lat_chunks, n_chunks, has_prev):
    if has_prev:
        x_ref, xp_ref, xn_ref, cw_ref, cb_ref, wa_ref, wx_ref, ba_ref, bx_ref, sp_ref, prev_ref, o_ref, a_scr, b_scr, h_scr = refs
    else:
        x_ref, xp_ref, xn_ref, cw_ref, cb_ref, wa_ref, wx_ref, ba_ref, bx_ref, sp_ref, o_ref, a_scr, b_scr, h_scr = refs
        prev_ref = None
    s = pl.program_id(0)
    gw = GROUP_WIDTH
    rows = tq * nb

    @pl.when(s == 0)
    def _():
        h_scr[...] = jnp.zeros_like(h_scr)

    c = _scan_chunk_index(s, n_lat_chunks, n_chunks, rev)
    first = jnp.logical_or(c == 0, c == n_lat_chunks)
    last = jnp.logical_or(c == n_lat_chunks - 1, c == n_chunks - 1)
    keep_prev = jnp.where(first, 0.0, 1.0)
    keep_next = jnp.where(last, 0.0, 1.0)
    xc = jnp.concatenate([xp_ref[...] * keep_prev, x_ref[...], xn_ref[...] * keep_next], axis=0)
    conv = cb_ref[...] + xc[0:tq] * cw_ref[0:1, :]
    for k in range(1, LRU_CONV):
        conv = conv + xc[k:k + tq] * cw_ref[k:k + 1, :]
    x = conv.reshape(rows, gw)
    xb = x.astype(BF16)
    r = jax.nn.sigmoid(_dot(xb, wa_ref[...]) + ba_ref[...])
    ig = jax.nn.sigmoid(_dot(xb, wx_ref[...]) + bx_ref[...])
    log_a = -LRU_C * r * sp_ref[...]
    a = jnp.exp(log_a)
    a_scr[...] = a
    b_scr[...] = jnp.sqrt(1.0 - a * a) * (ig * x)

    def step(i, h):
        t = (tq - 1 - i) if rev else i
        r0 = pl.multiple_of(t * nb, nb)
        hn = a_scr[pl.ds(r0, nb), :] * h + b_scr[pl.ds(r0, nb), :]
        b_scr[pl.ds(r0, nb), :] = hn
        return hn

    h_scr[...] = lax.fori_loop(0, tq, step, h_scr[...], unroll=4)
    out = b_scr[...].reshape(tq, nb, gw)
    if has_prev:
        out = out + prev_ref[...]
    o_ref[...] = out


def _lru_direction(tm3, conv_w, conv_b, wa_bd, wx_bd, ba, bx, sp, prev, rev, n_lat, tq):
    ltot, nb, _ = tm3.shape
    gw = GROUP_WIDTH
    n_chunks = ltot // tq
    n_lat_chunks = n_lat // tq
    cidx = lambda s: _scan_chunk_index(s, n_lat_chunks, n_chunks, rev)
    has_prev = prev is not None
    full2 = lambda s: (0, 0)
    in_specs = [pl.BlockSpec((tq, nb, gw), lambda s: (cidx(s), 0, 1)),
                pl.BlockSpec((2, nb, gw), lambda s: (jnp.maximum(cidx(s) * (tq // 2) - 1, 0), 0, 1)),
                pl.BlockSpec((1, nb, gw), lambda s: (jnp.minimum((cidx(s) + 1) * tq, ltot - 1), 0, 1)),
                pl.BlockSpec((LRU_CONV, gw), full2),
                pl.BlockSpec((1, gw), full2),
                pl.BlockSpec((gw, gw), full2),
                pl.BlockSpec((gw, gw), full2),
                pl.BlockSpec((1, gw), full2),
                pl.BlockSpec((1, gw), full2),
                pl.BlockSpec((1, gw), full2)]
    args = [tm3, tm3, tm3, conv_w, conv_b, wa_bd, wx_bd, ba, bx, sp]
    if has_prev:
        in_specs.append(pl.BlockSpec((tq, nb, gw), lambda s: (cidx(s), 0, 0)))
        args.append(prev)
    kern = functools.partial(_lru_kernel, rev=rev, tq=tq, nb=nb, n_lat_chunks=n_lat_chunks,
                             n_chunks=n_chunks, has_prev=has_prev)
    return pl.pallas_call(
        kern,
        out_shape=jax.ShapeDtypeStruct((ltot, nb, gw), F32),
        grid=(n_chunks,),
        in_specs=in_specs,
        out_specs=pl.BlockSpec((tq, nb, gw), lambda s: (cidx(s), 0, 0)),
        scratch_shapes=[pltpu.VMEM((tq * nb, gw), F32),
                        pltpu.VMEM((tq * nb, gw), F32),
                        pltpu.VMEM((nb, gw), F32)],
        compiler_params=_params("arbitrary"),
        name="lru_rev" if rev else "lru_fwd",
    )(*args)


def _tri_mask(n, rev):
    t = lax.broadcasted_iota(I32, (n, n), 0)
    s = lax.broadcasted_iota(I32, (n, n), 1)
    return (s >= t) if rev else (s <= t)


def _hgrn_kernel(*refs, rev, tq, has_prev):
    if has_prev:
        q_ref, z_ref, i_ref, lb_ref, tri_ref, ones_ref, prev_ref, o_ref, st_scr, cum_scr, q_scr, k_scr, v_scr = refs
    else:
        q_ref, z_ref, i_ref, lb_ref, tri_ref, ones_ref, o_ref, st_scr, cum_scr, q_scr, k_scr, v_scr = refs
        prev_ref = None
    hps = HGRN_HEADS_PER_STEP
    cs, sb, w = GLA_CHUNK, GLA_SUB, min(HGRN_FAST_WINDOW, tq)
    nsub = cs // sb
    n_ck = tq // cs
    n_w = tq // w

    @pl.when(pl.program_id(2) == 0)
    def _():
        st_scr[...] = jnp.zeros_like(st_scr)

    lb = lb_ref[0]
    z = z_ref[0]
    iv = i_ref[0]
    f = lb + (1.0 - lb) * jax.nn.sigmoid(z)
    k_all = 1.0 - f
    v_all = iv * jax.nn.sigmoid(iv)
    q_all = q_ref[0] * (GLA_DIM ** -0.5)
    logf = jnp.log(f)
    lf_hi = logf.astype(BF16)
    rest = logf - lf_hi.astype(F32)
    lf_mid = rest.astype(BF16)
    lf_lo = (rest - lf_mid.astype(F32)).astype(BF16)
    tri = tri_ref[...]
    cum_all = _dot(tri, lf_hi) + _dot(tri, lf_mid) + _dot(tri, lf_lo)
    for hh in range(hps):
        lanes = slice(hh * GLA_DIM, (hh + 1) * GLA_DIM)
        cum_scr[hh] = cum_all[:, lanes]
        q_scr[hh] = q_all[:, lanes]
        k_scr[hh] = k_all[:, lanes]
        v_scr[hh] = v_all[:, lanes]

    def far_row(c, width):
        return c * width if rev else (c + 1) * width - 1

    def outside_row(c, width, n):
        if rev:
            return (c + 1) * width if c < n - 1 else None
        return c * width - 1 if c > 0 else None

    def store_out(hh, rows, out):
        lanes = slice(hh * GLA_DIM, (hh + 1) * GLA_DIM)
        if has_prev:
            out = out + prev_ref[0, rows, lanes]
        o_ref[0, rows, lanes] = out

    def block_factored(_):
        lane_id = lax.broadcasted_iota(I32, (1, tq), 1)
        for hh in range(hps):
            cumg, q, k = cum_scr[hh], q_scr[hh], k_scr[hh]
            vb = v_scr[hh].astype(BF16)
            st = st_scr[hh]
            ends = [cum_scr[hh, far_row(c, w):far_row(c, w) + 1, :] for c in range(n_w)]
            end_full = jnp.concatenate([jnp.broadcast_to(e, (w, GLA_DIM)) for e in ends], axis=0)
            ks = (k * jnp.exp(end_full - cumg)).astype(BF16)
            pieces, starts, row_lo = [], [], []
            total = 0
            for c in range(n_w):
                rows = slice(0, (c + 1) * w) if rev else slice(c * w, tq)
                scale = jnp.exp(jnp.minimum(cumg[rows] - ends[c], HGRN_SAFE_LOG_DECAY))
                pieces.append((q[rows] * scale).astype(BF16))
                starts.append(total)
                row_lo.append(rows.start)
                total += rows.stop - rows.start
            a_stack = _dot_nt(jnp.concatenate(pieces, axis=0), ks)
            a_rows = []
            for r in range(n_w):
                acc = jnp.zeros((w, tq), F32)
                for c in (range(r, n_w) if rev else range(0, r + 1)):
                    lo = starts[c] + r * w - row_lo[c]
                    keep = jnp.logical_and(lane_id >= c * w, lane_id < (c + 1) * w)
                    if c == r:
                        t_id = r * w + lax.broadcasted_iota(I32, (w, tq), 0)
                        keep = jnp.logical_and(keep, (lane_id >= t_id) if rev else (lane_id <= t_id))
                    acc = jnp.where(keep, a_stack[lo:lo + w, :], acc)
                a_rows.append(acc)
            a = jnp.concatenate(a_rows, axis=0).astype(BF16)
            out = _dot(a, vb) + _dot_nt((q * jnp.exp(cumg)).astype(BF16), st.astype(BF16))
            store_out(hh, slice(None), out)
            last = ends[0] if rev else ends[n_w - 1]
            kd = k * jnp.exp(last - cumg)
            st_scr[hh] = st * jnp.exp(last) + _dot_tn(vb, kd.astype(BF16))
        return 0

    row_id = lax.broadcasted_iota(I32, (sb, GLA_DIM), 0)
    valid = [(row_id <= s) if rev else (row_id >= s) for s in range(sb)]

    def chunk_direct(hh, r0):
        rows = pl.ds(r0, cs)
        q, k, cumg = q_scr[hh, rows, :], k_scr[hh, rows, :], cum_scr[hh, rows, :]
        vb = v_scr[hh, rows, :].astype(BF16)
        if rev:
            base = cum_scr[hh, pl.ds(jnp.minimum(r0 + cs, tq - 1), 1), :]
            base = jnp.where(r0 + cs >= tq, 0.0, base)
        else:
            base = cum_scr[hh, pl.ds(jnp.maximum(r0 - 1, 0), 1), :]
            base = jnp.where(r0 == 0, 0.0, base)
        cum = cumg - base
        last = cum[0:1] if rev else cum[cs - 1:cs]
        st = st_scr[hh]
        o_inter = _dot_nt((q * jnp.exp(cum)).astype(BF16), st.astype(BF16))
        outs = []
        for i in range(nsub):
            blk = slice(i * sb, (i + 1) * sb)
            off = slice((i + 1) * sb, cs) if rev else slice(0, i * sb)
            o_row = outside_row(i, sb, nsub)
            q_i, cum_i = q[blk], cum[blk]
            o_i = o_inter[blk]
            if o_row is not None:
                ref_row = cum[o_row:o_row + 1]
                qt = q_i * jnp.exp(cum_i - ref_row)
                kt = k[off] * jnp.exp(ref_row - cum[off])
                a_off = _dot_nt(qt.astype(BF16), kt.astype(BF16))
                o_i = o_i + _dot(a_off.astype(BF16), vb[off])
            prods = []
            for s in range(sb):
                r = r0 + i * sb + s
                e = jnp.exp(jnp.minimum(cumg[blk] - cum_scr[hh, pl.ds(r, 1), :], 0.0))
                prods.append(jnp.where(valid[s], q_i * e * k_scr[hh, pl.ds(r, 1), :], 0.0))
            sums = _dot(jnp.concatenate(prods, axis=0).astype(BF16), ones_ref[...])
            for s in range(sb):
                r = r0 + i * sb + s
                o_i = o_i + sums[s * sb:(s + 1) * sb] * v_scr[hh, pl.ds(r, 1), :]
            outs.append(o_i)
        store_out(hh, rows, jnp.concatenate(outs, axis=0))
        kd = k * jnp.exp(last - cum)
        st_scr[hh] = st * jnp.exp(last) + _dot_tn(vb, kd.astype(BF16))

    def block_direct(_):
        def chunk(ci, carry):
            c = (n_ck - 1 - ci) if rev else ci
            r0 = pl.multiple_of(c * cs, cs)
            for hh in range(hps):
                chunk_direct(hh, r0)
            return carry

        return lax.fori_loop(0, n_ck, chunk, 0)

    decays = []
    for c in range(n_w):
        far = cum_all[far_row(c, w):far_row(c, w) + 1]
        o_row = outside_row(c, w, n_w)
        decays.append(far if o_row is None else far - cum_all[o_row:o_row + 1])
    safe = jnp.logical_and(jnp.min(jnp.concatenate(decays, axis=0)) > -HGRN_SAFE_LOG_DECAY,
                           jnp.max(jnp.abs(q_all)) < HGRN_SAFE_QUERY)
    lax.cond(safe, block_factored, block_direct, 0)


def _gla_specs(ltot, tq, n_lat, rev):
    n_chunks = ltot // tq
    n_lat_chunks = n_lat // tq
    cidx = lambda s: _scan_chunk_index(s, n_lat_chunks, n_chunks, rev)
    part = lambda p: pl.BlockSpec((1, tq, GLA_DIM), lambda b, h, s: (b, cidx(s), p * GLA_HEADS + h))
    return n_chunks, cidx, part


def _hgrn_direction(bm, lb, prev, rev, n_lat, tq):
    bsz, ltot, _ = bm.shape
    n_chunks, cidx, _ = _gla_specs(ltot, tq, n_lat, rev)
    has_prev = prev is not None
    hps = HGRN_HEADS_PER_STEP
    width = hps * GLA_DIM
    n_hsteps = GLA_HEADS // hps
    part = lambda p: pl.BlockSpec((1, tq, width), lambda b, h, s: (b, cidx(s), p * n_hsteps + h))
    tri = _tri_mask(tq, rev).astype(BF16)
    ones = jnp.ones((GLA_DIM, GLA_DIM), BF16)
    z_part = 2 if rev else 1
    in_specs = [part(0), part(z_part), part(3),
                pl.BlockSpec((1, 1, width), lambda b, h, s: (h, 0, 0)),
                pl.BlockSpec(tri.shape, lambda b, h, s: (0, 0)),
                pl.BlockSpec(ones.shape, lambda b, h, s: (0, 0))]
    args = [bm, bm, bm, lb.reshape(n_hsteps, 1, width), tri, ones]
    out_spec = pl.BlockSpec((1, tq, width), lambda b, h, s: (b, cidx(s), h))
    if has_prev:
        in_specs.append(out_spec)
        args.append(prev)
    kern = functools.partial(_hgrn_kernel, rev=rev, tq=tq, has_prev=has_prev)
    tile = pltpu.VMEM((hps, tq, GLA_DIM), F32)
    return pl.pallas_call(
        kern,
        out_shape=jax.ShapeDtypeStruct((bsz, ltot, GROUP_WIDTH), F32),
        grid=(bsz, n_hsteps, n_chunks),
        in_specs=in_specs,
        out_specs=out_spec,
        scratch_shapes=[pltpu.VMEM((hps, GLA_DIM, GLA_DIM), F32), tile, tile, tile, tile],
        compiler_params=_params("arbitrary", "arbitrary", "arbitrary"),
        name="hgrn_rev" if rev else "hgrn_fwd",
    )(*args)


def _ret_kernel(*refs, rev, tq, has_prev):
    if has_prev:
        q_ref, k_ref, v_ref, cos_ref, sin_ref, eq_ref, ek_ref, dm_ref, el_ref, prev_ref, o_ref, st_scr = refs
    else:
        q_ref, k_ref, v_ref, cos_ref, sin_ref, eq_ref, ek_ref, dm_ref, el_ref, o_ref, st_scr = refs
        prev_ref = None
    cs = min(tq, RET_CHUNK)
    n_ck = tq // cs

    @pl.when(pl.program_id(1) == 0)
    def _():
        st_scr[...] = jnp.zeros_like(st_scr)

    def chunk(ci, carry):
        c = (n_ck - 1 - ci) if rev else ci
        r0 = pl.multiple_of(c * cs, cs)
        rows = pl.ds(r0, cs)
        cos = cos_ref[rows, :]
        sin = sin_ref[rows, :]
        for h in range(GLA_HEADS):
            lanes = slice(h * GLA_DIM, (h + 1) * GLA_DIM)
            q = q_ref[0, rows, lanes]
            k = k_ref[0, rows, lanes]
            vb = v_ref[0, rows, lanes].astype(BF16)
            q = q * cos + pltpu.roll(q, GLA_DIM // 2, 1) * sin
            k = (k * cos + pltpu.roll(k, GLA_DIM // 2, 1) * sin) * (GLA_DIM ** -0.5)
            st = st_scr[h]
            scores = _dot_nt(q.astype(BF16), k.astype(BF16)) * dm_ref[h]
            out = _dot_nt((q * eq_ref[h]).astype(BF16), st.astype(BF16)) + _dot(scores.astype(BF16), vb)
            if has_prev:
                out = out + prev_ref[0, rows, lanes]
            o_ref[0, rows, lanes] = out
            st_scr[h] = st * el_ref[h] + _dot_tn(vb, (k * ek_ref[h]).astype(BF16))
        return carry

    lax.fori_loop(0, n_ck, chunk, 0)


def _ret_tables(rev, cs):
    gamma = jnp.log1p(-jnp.exp2(-(RET_DECAY_EXP[1 if rev else 0] + jnp.arange(GLA_HEADS, dtype=F32))))
    t = jnp.arange(cs, dtype=F32)
    steps = (cs - t) if rev else (t + 1.0)
    cum = gamma[:, None] * steps[None, :]
    last = gamma * cs
    eq = jnp.broadcast_to(jnp.exp(cum)[:, :, None], (GLA_HEADS, cs, GLA_DIM))
    ek = jnp.broadcast_to(jnp.exp(last[:, None] - cum)[:, :, None], (GLA_HEADS, cs, GLA_DIM))
    rel = cum[:, :, None] - cum[:, None, :]
    dm = jnp.where(_tri_mask(cs, rev)[None], jnp.exp(jnp.minimum(rel, 0.0)), 0.0)
    el = jnp.broadcast_to(jnp.exp(last)[:, None, None], (GLA_HEADS, 1, GLA_DIM))
    return eq, ek, dm, el


def _rotary_tables(n_lat, n_ctx):
    rows = n_lat // GRID_W
    row = jnp.repeat(jnp.arange(rows, dtype=F32), GRID_W)
    col = jnp.tile(jnp.arange(GRID_W, dtype=F32), rows)
    quarter = GLA_DIM // 4
    inv_freq = ROPE_BASE ** (-jnp.arange(quarter, dtype=F32) / quarter)
    ang = jnp.concatenate([row[:, None] * inv_freq, col[:, None] * inv_freq], axis=-1)
    cos, sin = jnp.cos(ang), jnp.sin(ang)
    cos_l = jnp.concatenate([cos, cos], axis=-1)
    sin_l = jnp.concatenate([-sin, sin], axis=-1)
    cos_t = jnp.concatenate([cos_l, jnp.ones((n_ctx, GLA_DIM), F32)], axis=0)
    sin_t = jnp.concatenate([sin_l, jnp.zeros((n_ctx, GLA_DIM), F32)], axis=0)
    return cos_t, sin_t


def _ret_direction(bm, cos_t, sin_t, prev, rev, n_lat, tq):
    bsz, ltot, _ = bm.shape
    n_chunks, cidx, _ = _gla_specs(ltot, tq, n_lat, rev)
    has_prev = prev is not None
    eq, ek, dm, el = _ret_tables(rev, min(tq, RET_CHUNK))
    gw = GROUP_WIDTH
    part = lambda p: pl.BlockSpec((1, tq, gw), lambda b, s: (b, cidx(s), p))
    whole = lambda a: pl.BlockSpec(a.shape, lambda b, s: (0, 0, 0))
    pos = pl.BlockSpec((tq, GLA_DIM), lambda b, s: (cidx(s), 0))
    in_specs = [part(5), part(6), part(7), pos, pos, whole(eq), whole(ek), whole(dm), whole(el)]
    args = [bm, bm, bm, cos_t, sin_t, eq, ek, dm, el]
    out_spec = pl.BlockSpec((1, tq, gw), lambda b, s: (b, cidx(s), 0))
    if has_prev:
        in_specs.append(out_spec)
        args.append(prev)
    kern = functools.partial(_ret_kernel, rev=rev, tq=tq, has_prev=has_prev)
    return pl.pallas_call(
        kern,
        out_shape=jax.ShapeDtypeStruct((bsz, ltot, gw), F32),
        grid=(bsz, n_chunks),
        in_specs=in_specs,
        out_specs=out_spec,
        scratch_shapes=[pltpu.VMEM((GLA_HEADS, GLA_DIM, GLA_DIM), F32)],
        compiler_params=_params("arbitrary", "arbitrary"),
        name="ret_rev" if rev else "ret_fwd",
    )(*args)


def _head_rms(o):
    parts = []
    for h in range(GLA_HEADS):
        oh = o[:, h * GLA_DIM:(h + 1) * GLA_DIM]
        parts.append(oh * lax.rsqrt(jnp.mean(oh * oh, axis=-1, keepdims=True) + EPS))
    return jnp.concatenate(parts, axis=-1)


def _mix_kernel(u_ref, lg_ref, ys_ref, hs_ref, hg_ref, rg_ref, oh_ref, or_ref, d_ref, gw_ref, gb_ref, o_ref):
    gw = GROUP_WIDTH
    y = jax.nn.gelu(d_ref[...] * u_ref[...] + ys_ref[...])
    a = y * jax.nn.sigmoid(_dot(y.astype(BF16), gw_ref[...]) + gb_ref[...])
    o_ref[0, :, 0:gw] = a.astype(BF16)
    o_ref[0, :, gw:2 * gw] = (hs_ref[...] * jax.nn.gelu(lg_ref[...])).astype(BF16)
    g = hg_ref[0]
    o_ref[0, :, 2 * gw:3 * gw] = (_head_rms(oh_ref[0]) * (g * jax.nn.sigmoid(g))).astype(BF16)
    g = rg_ref[0]
    o_ref[0, :, 3 * gw:4 * gw] = (_head_rms(or_ref[0]) * (g * jax.nn.sigmoid(g))).astype(BF16)


def _mix_epilogue(tm2, ys2, hs2, bm, o_h, o_r, s5_d, glu_w, glu_b, n_rows, tm):
    bsz, ltot, _ = bm.shape
    gw = GROUP_WIDTH
    tmaj = lambda p, k: pl.BlockSpec((tm, gw), lambda b, i: (i, b * k + p))
    bmaj = lambda p: pl.BlockSpec((1, tm, gw), lambda b, i: (b, i, p))
    full2 = lambda b, i: (0, 0)
    return pl.pallas_call(
        _mix_kernel,
        out_shape=jax.ShapeDtypeStruct((bsz, ltot, 4 * gw), BF16),
        grid=(bsz, n_rows // tm),
        in_specs=[tmaj(0, N_TM_PARTS), tmaj(2, N_TM_PARTS), tmaj(0, 1), tmaj(0, 1),
                  bmaj(4), bmaj(8), bmaj(0), bmaj(0),
                  pl.BlockSpec((1, gw), full2), pl.BlockSpec((gw, gw), full2), pl.BlockSpec((1, gw), full2)],
        out_specs=pl.BlockSpec((1, tm, 4 * gw), lambda b, i: (b, i, 0)),
        compiler_params=_params("arbitrary", "arbitrary"),
        name="mix_epilogue",
    )(tm2, tm2, ys2, hs2, bm, bm, o_h, o_r, s5_d, glu_w, glu_b)


def _gate_rows(modb_ref, modc_ref, idx, row0, tm, n_lat):
    row = row0 + lax.broadcasted_iota(I32, (tm, 1), 0)
    return jnp.where(row >= n_lat, modc_ref[0, idx:idx + 1, :], modb_ref[0, idx:idx + 1, :])


def _outproj_kernel(a_ref, w_ref, x_ref, modb_ref, modc_ref, o_ref, *, tm, n_lat):
    gate = _gate_rows(modb_ref, modc_ref, 2, pl.program_id(1) * tm, tm, n_lat)
    o_ref[0] = x_ref[0] + gate * _dot(a_ref[0], w_ref[0])


def _out_projection(mix, w_bf16, layer, xs, modv, n_rows, n_lat, tm, tn):
    bsz, ltot, d = xs.shape
    k = mix.shape[-1]
    kern = functools.partial(_outproj_kernel, tm=tm, n_lat=n_lat)
    return pl.pallas_call(
        kern,
        out_shape=jax.ShapeDtypeStruct((bsz, ltot, d), F32),
        grid=(bsz, n_rows // tm, d // tn),
        in_specs=[pl.BlockSpec((1, tm, k), lambda b, i, j: (b, i, 0)),
                  pl.BlockSpec((1, k, tn), lambda b, i, j: (layer, 0, j)),
                  pl.BlockSpec((1, tm, tn), lambda b, i, j: (b, i, j)),
                  pl.BlockSpec((1, 6, tn), lambda b, i, j: (b, 0, j)),
                  pl.BlockSpec((1, 6, tn), lambda b, i, j: (bsz, 0, j))],
        out_specs=pl.BlockSpec((1, tm, tn), lambda b, i, j: (b, i, j)),
        compiler_params=_params("arbitrary", "arbitrary", "arbitrary"),
        name="out_projection",
    )(mix, w_bf16, xs, modv, modv)


def _first_max4(v):
    m1 = jnp.maximum(jnp.maximum(v[0], v[1]), jnp.maximum(v[2], v[3]))
    i1 = jnp.where(v[0] == m1, 0, jnp.where(v[1] == m1, 1, jnp.where(v[2] == m1, 2, 3)))
    rest = [jnp.where(i1 == j, -jnp.inf, v[j]) for j in range(4)]
    m2 = jnp.maximum(jnp.maximum(rest[0], rest[1]), jnp.maximum(rest[2], rest[3]))
    i2 = jnp.where(rest[0] == m2, 0, jnp.where(rest[1] == m2, 1, jnp.where(rest[2] == m2, 2, 3)))
    return m1, i1, m2, i2


def _router_kernel(x_ref, g_ref, modb_ref, modc_ref, rw_ref, rb_ref, tri_ref,
                   h_ref, ids_ref, gates_ref, ranks_ref, counts_ref, base_scr, *, tm, n_lat):
    first = jnp.logical_and(pl.program_id(0) == 0, pl.program_id(1) == 0)

    @pl.when(first)
    def _():
        base_scr[...] = jnp.zeros_like(base_scr)

    h = _norm_modulate(x_ref[0], g_ref[...], modb_ref, modc_ref, pl.program_id(1) * tm, n_lat, 3, 4)
    h_ref[0] = h.astype(BF16)
    logits = lax.dot_general(rw_ref[...], h, (((1,), (1,)), ((), ())), preferred_element_type=F32,
                             precision=lax.Precision.HIGHEST)
    scores = jax.nn.sigmoid(logits)
    biased = scores + rb_ref[...]
    tops = []
    for gidx in range(N_EXPERT_GROUPS):
        rows = [biased[gidx * EXPERTS_PER_GROUP + j:gidx * EXPERTS_PER_GROUP + j + 1, :]
                for j in range(EXPERTS_PER_GROUP)]
        tops.append(_first_max4(rows))
    gs = [t[0] + t[2] for t in tops]
    gmax = jnp.maximum(jnp.maximum(gs[0], gs[1]), jnp.maximum(gs[2], gs[3]))
    best = jnp.where(gs[0] == gmax, 0, jnp.where(gs[1] == gmax, 1, jnp.where(gs[2] == gmax, 2, 3)))
    e1 = jnp.zeros_like(best)
    e2 = jnp.zeros_like(best)
    for gidx in range(N_EXPERT_GROUPS):
        e1 = jnp.where(best == gidx, gidx * EXPERTS_PER_GROUP + tops[gidx][1], e1)
        e2 = jnp.where(best == gidx, gidx * EXPERTS_PER_GROUP + tops[gidx][3], e2)
    eid = lax.broadcasted_iota(I32, (N_EXPERTS, tm), 0)
    sel1 = eid == e1
    sel2 = eid == e2
    w1 = jnp.sum(jnp.where(sel1, scores, 0.0), axis=0, keepdims=True)
    w2 = jnp.sum(jnp.where(sel2, scores, 0.0), axis=0, keepdims=True)
    wsum = w1 + w2
    onehot = jnp.where(jnp.logical_or(sel1, sel2), 1.0, 0.0)
    pos = base_scr[...] + _dot(onehot.astype(BF16), tri_ref[...])
    r1 = jnp.sum(jnp.where(sel1, pos, 0.0), axis=0, keepdims=True)
    r2 = jnp.sum(jnp.where(sel2, pos, 0.0), axis=0, keepdims=True)
    ids_ref[0] = jnp.concatenate([e1, e2], axis=0)
    gates_ref[0] = jnp.concatenate([w1 / wsum, w2 / wsum], axis=0)
    ranks_ref[0] = jnp.concatenate([r1, r2], axis=0).astype(I32)
    base = base_scr[...] + jnp.sum(onehot, axis=1, keepdims=True)
    base_scr[...] = base
    counts_ref[...] = jnp.broadcast_to(base, counts_ref.shape)


def _router(xs, gain, modv, rw_t, rbias, n_rows, n_lat, tm):
    bsz, ltot, d = xs.shape
    tri = (lax.broadcasted_iota(I32, (tm, tm), 0) < lax.broadcasted_iota(I32, (tm, tm), 1)).astype(BF16)
    kern = functools.partial(_router_kernel, tm=tm, n_lat=n_lat)
    small = lambda dt: jax.ShapeDtypeStruct((bsz, 2, n_rows), dt)
    small_spec = pl.BlockSpec((1, 2, tm), lambda b, i: (b, 0, i))
    return pl.pallas_call(
        kern,
        out_shape=(jax.ShapeDtypeStruct((bsz, ltot, d), BF16), small(I32), small(F32), small(I32),
                   jax.ShapeDtypeStruct((N_EXPERTS, 128), F32)),
        grid=(bsz, n_rows // tm),
        in_specs=[pl.BlockSpec((1, tm, d), lambda b, i: (b, i, 0)),
                  pl.BlockSpec((1, d), lambda b, i: (0, 0)),
                  pl.BlockSpec((1, 6, d), lambda b, i: (b, 0, 0)),
                  pl.BlockSpec((1, 6, d), lambda b, i: (bsz, 0, 0)),
                  pl.BlockSpec((N_EXPERTS, d), lambda b, i: (0, 0)),
                  pl.BlockSpec((N_EXPERTS, 1), lambda b, i: (0, 0)),
                  pl.BlockSpec((tm, tm), lambda b, i: (0, 0))],
        out_specs=(pl.BlockSpec((1, tm, d), lambda b, i: (b, i, 0)), small_spec, small_spec, small_spec,
                   pl.BlockSpec((N_EXPERTS, 128), lambda b, i: (0, 0))),
        scratch_shapes=[pltpu.VMEM((N_EXPERTS, 1), F32)],
        compiler_params=_params("arbitrary", "arbitrary"),
        name="moe_router",
    )(xs, gain, modv, modv, rw_t, rbias, tri)


def _expert_ffn_kernel(te_ref, nt_ref, x_ref, wg_ref, wu_ref, wd_ref, o_ref):
    live = pl.program_id(0) < nt_ref[0]

    @pl.when(live)
    def _():
        x = x_ref[...]
        g = _dot(x, wg_ref[0])
        u = _dot(x, wu_ref[0])
        he = (g * jax.nn.sigmoid(g)) * u
        o_ref[...] = _dot(he.astype(BF16), wd_ref[0]).astype(o_ref.dtype)

    @pl.when(jnp.logical_not(live))
    def _():
        o_ref[...] = jnp.zeros_like(o_ref)


def _expert_ffn(tile_expert, n_tiles, x_sorted, wg, wu, wd, tm):
    p, d = x_sorted.shape
    f = wg.shape[-1]
    tile = lambda i, te, nt: i
    expert = lambda i, te, nt: te[jnp.minimum(i, nt[0] - 1)]
    grid_spec = pltpu.PrefetchScalarGridSpec(
        num_scalar_prefetch=2,
        grid=(p // tm,),
        in_specs=[pl.BlockSpec((tm, d), lambda i, te, nt: (tile(i, te, nt), 0)),
                  pl.BlockSpec((1, d, f), lambda i, te, nt: (expert(i, te, nt), 0, 0)),
                  pl.BlockSpec((1, d, f), lambda i, te, nt: (expert(i, te, nt), 0, 0)),
                  pl.BlockSpec((1, f, d), lambda i, te, nt: (expert(i, te, nt), 0, 0))],
        out_specs=pl.BlockSpec((tm, d), lambda i, te, nt: (tile(i, te, nt), 0)),
    )
    return pl.pallas_call(
        _expert_ffn_kernel,
        out_shape=jax.ShapeDtypeStruct((p, d), BF16),
        grid_spec=grid_spec,
        compiler_params=_params("arbitrary"),
        name="moe_expert_ffn",
    )(tile_expert, n_tiles, x_sorted, wg, wu, wd)


def _moe_residual_kernel(x_ref, y0_ref, y1_ref, gates_ref, modb_ref, modc_ref, g_ref, o_ref,
                         *, tm, n_lat, final_norm):
    gates = gates_ref[0]
    y = gates[:, 0:1] * y0_ref[0, 0].astype(F32) + gates[:, 1:2] * y1_ref[0, 0].astype(F32)
    gate = _gate_rows(modb_ref, modc_ref, 5, pl.program_id(1) * tm, tm, n_lat)
    x = x_ref[0] + gate * y
    if final_norm:
        x = x * lax.rsqrt(jnp.mean(x * x, axis=-1, keepdims=True) + EPS) * g_ref[...]
    o_ref[0] = x


def _moe_residual(xs, y_pair, gates_t, modv, final_g, n_rows, n_lat, tm, final_norm):
    bsz, ltot, d = xs.shape
    kern = functools.partial(_moe_residual_kernel, tm=tm, n_lat=n_lat, final_norm=final_norm)
    rows_out = n_rows if final_norm else ltot
    return pl.pallas_call(
        kern,
        out_shape=jax.ShapeDtypeStruct((bsz, rows_out, d), F32),
        grid=(bsz, n_rows // tm),
        in_specs=[pl.BlockSpec((1, tm, d), lambda b, i: (b, i, 0)),
                  pl.BlockSpec((1, 1, tm, d), lambda b, i: (0, b, i, 0)),
                  pl.BlockSpec((1, 1, tm, d), lambda b, i: (1, b, i, 0)),
                  pl.BlockSpec((1, tm, 2), lambda b, i: (b, i, 0)),
                  pl.BlockSpec((1, 6, d), lambda b, i: (b, 0, 0)),
                  pl.BlockSpec((1, 6, d), lambda b, i: (bsz, 0, 0)),
                  pl.BlockSpec((1, d), lambda b, i: (0, 0))],
        out_specs=pl.BlockSpec((1, tm, d), lambda b, i: (b, i, 0)),
        compiler_params=_params("arbitrary", "arbitrary"),
        name="moe_residual",
    )(xs, y_pair, y_pair, gates_t, modv, modv, final_g)


def _moe_layer(xs, gain, modv, rw_t, rbias, wg, wu, wd, expert_base, final_g, n_rows, n_lat, tm_tok, tm_res, tm_exp,
               final_norm):
    bsz, ltot, d = xs.shape
    h, ids, gates, ranks, counts = _router(xs, gain, modv, rw_t, rbias, n_rows, n_lat, tm_tok)
    counts = counts[:, 0].astype(I32)
    padded = ((counts + tm_exp - 1) // tm_exp) * tm_exp
    ends = jnp.cumsum(padded)
    offsets = ends - padded
    onehot = ids[..., None] == jnp.arange(N_EXPERTS, dtype=I32)
    dest = jnp.sum(jnp.where(onehot, offsets, 0), axis=-1) + ranks
    n_slots = bsz * 2 * n_rows
    p_rows = ((n_slots + N_EXPERTS * (tm_exp - 1)) + tm_exp - 1) // tm_exp * tm_exp
    tok = (jnp.arange(bsz, dtype=I32)[:, None, None] * ltot + jnp.arange(n_rows, dtype=I32)[None, None, :])
    tok = jnp.broadcast_to(tok, dest.shape)
    fill_slot = jnp.arange(p_rows, dtype=I32) % n_slots
    filler = (fill_slot // (2 * n_rows)) * ltot + fill_slot % n_rows
    src = filler.at[dest.reshape(-1)].set(tok.reshape(-1), mode="promise_in_bounds", unique_indices=True)
    tile_start = jnp.arange(p_rows // tm_exp, dtype=I32) * tm_exp
    tile_expert = jnp.minimum(jnp.sum(tile_start[:, None] >= ends[None, :], axis=1), N_EXPERTS - 1).astype(I32)
    n_tiles = (ends[-1:] // tm_exp).astype(I32)
    x_sorted = h.reshape(bsz * ltot, d).at[src].get(mode="promise_in_bounds")
    y_sorted = _expert_ffn(tile_expert + expert_base, n_tiles, x_sorted, wg, wu, wd, tm_exp)
    y_pair = y_sorted.at[dest.transpose(1, 0, 2).reshape(-1)].get(mode="promise_in_bounds")
    y_pair = y_pair.reshape(2, bsz, n_rows, d)
    return _moe_residual(xs, y_pair, gates.transpose(0, 2, 1), modv, final_g, n_rows, n_lat, tm_res, final_norm)


def _row_tile(n, target):
    t = min(n, target)
    while n % t or t % 8:
        t -= 8
    return t


def kernel(x, c, ctx, c_ctx, norm_mix_g, norm_ffn_g, w_mod, b_mod, w_in, w_out, s5_lam_re, s5_lam_im, s5_log_dt, s5_b_re, s5_b_im, s5_c_re, s5_c_im, s5_d, s5_glu_w, s5_glu_b, lru_conv_w, lru_conv_b, lru_wa, lru_ba, lru_wx, lru_bx, lru_lam, hgrn_lb_logits, router_w, router_bias, moe_w_gate, moe_w_up, moe_w_down, final_norm_g):
    bsz, n_lat, d = x.shape
    n_ctx = ctx.shape[1]
    ltot = n_lat + n_ctx
    depth = w_in.shape[0]
    gw = GROUP_WIDTH
    assert bsz % 8 == 0 and bsz + 1 <= MOD_ROWS
    assert n_lat % GLA_CHUNK == 0 and n_ctx % GLA_CHUNK == 0 and n_lat % GRID_W == 0 and n_lat % NORM_SLAB == 0

    tq_scan = math.gcd(math.gcd(n_lat, n_ctx), 64)
    tq_gla = math.gcd(math.gcd(n_lat, n_ctx), 256)
    tm_all = _row_tile(ltot, 768)
    tm_lat = _row_tile(n_lat, 1024)

    xs = jnp.concatenate([x, ctx.astype(x.dtype)], axis=1)
    cv = jnp.zeros((MOD_ROWS, d), F32).at[:bsz].set(c).at[bsz].set(c_ctx)
    mod_all = _mod_vectors(cv, w_mod, b_mod).reshape(depth, MOD_ROWS, 6, d)
    cos_t, sin_t = _rotary_tables(n_lat, n_ctx)
    rw_t = router_w.astype(F32).T
    rbias = router_bias.astype(F32).reshape(N_EXPERTS, 1)
    lb_cum = [jnp.cumsum(jax.nn.softmax(hgrn_lb_logits[dr].astype(F32), axis=0), axis=0) for dr in (0, 1)]
    w_in_b, w_out_b = _cast_bf16(w_in), _cast_bf16(w_out)
    n_exp, _, d_ff = moe_w_gate.shape[1:]
    moe_flat = (moe_w_gate.reshape(depth * n_exp, d, d_ff), moe_w_up.reshape(depth * n_exp, d, d_ff),
                moe_w_down.reshape(depth * n_exp, d_ff, d))

    for layer in range(depth):
        ctx_out = layer < depth - 1
        n_rows = ltot if ctx_out else n_lat
        tm_tok = tm_all if ctx_out else tm_lat
        tm_mix = _row_tile(n_rows, 512)
        modv = mod_all[layer]

        tm2, bm = _in_projection(xs, norm_mix_g[layer].reshape(1, d), modv, w_in_b, layer, n_lat, tm_all)
        tm3 = tm2.reshape(ltot, bsz, N_TM_PARTS * gw)

        ys = hs = o_h = o_r = None
        for dr in (0, 1):
            rev = dr == 1
            bw, cw, atab = _s5_tables(s5_lam_re[layer, dr], s5_lam_im[layer, dr], s5_log_dt[layer, dr],
                                      s5_b_re[layer, dr], s5_b_im[layer, dr], s5_c_re[layer, dr], s5_c_im[layer, dr])
            ys = _s5_direction(tm3, bw, cw, atab, ys, rev, n_lat, tq_scan)
            sp = jax.nn.softplus(-lru_lam[layer, dr].astype(F32)).reshape(1, gw)
            hs = _lru_direction(tm3, lru_conv_w[layer].astype(F32), lru_conv_b[layer].astype(F32).reshape(1, gw),
                                _block_diag(lru_wa[layer, dr]).astype(BF16), _block_diag(lru_wx[layer, dr]).astype(BF16),
                                lru_ba[layer, dr].astype(F32).reshape(1, gw), lru_bx[layer, dr].astype(F32).reshape(1, gw),
                                sp, hs, rev, n_lat, tq_scan)
            lb = lb_cum[dr][layer] - lb_cum[dr][0]
            o_h = _hgrn_direction(bm, lb, o_h, rev, n_lat, tq_gla)
            o_r = _ret_direction(bm, cos_t, sin_t, o_r, rev, n_lat, tq_gla)

        mix = _mix_epilogue(tm2, ys.reshape(ltot, bsz * gw), hs.reshape(ltot, bsz * gw), bm, o_h, o_r,
                            s5_d[layer].astype(F32).reshape(1, gw), s5_glu_w[layer].astype(BF16),
                            s5_glu_b[layer].astype(F32).reshape(1, gw), n_rows, tm_mix)
        xs = _out_projection(mix, w_out_b, layer, xs, modv, n_rows, n_lat, tm_tok, 1024)
        wg_b, wu_b, wd_b = (_cast_bf16(w, layer * n_exp, n_exp) for w in moe_flat)
        xs = _moe_layer(xs, norm_ffn_g[layer].reshape(1, d), modv, rw_t, rbias, wg_b, wu_b, wd_b, 0,
                        final_norm_g.reshape(1, d), n_rows, n_lat, tm_tok, tm_mix, 512, not ctx_out)
    return xs
```
